```python
import jax
import jax.numpy as jnp
from jax import lax
import numpy as np

D_MODEL = 1024
BATCH = 4
SEQ = 4096
DEPTH = 2
DEC_BATCH = 32
DEC_SEQ = 4
PAST_LEN = 8192
PAGE_SIZE = 128

F32 = jnp.float32
HEAD_DIM = 64
RW_HEADS = 4
RW_WIDTH = RW_HEADS * HEAD_DIM
RW_DECAY_LORA = 64
RW_A_LORA = 64
RW_GATE_LORA = 128
RW_COLS = 3 * RW_WIDTH + RW_DECAY_LORA + RW_A_LORA + RW_GATE_LORA
RW_GN_EPS = 64e-5
POOL_WINDOWS = (2, 4, 8, 16)
POOL_GROUPS = 4
POOL_GROUP_DIM = 64
POOL_WIDTH = POOL_GROUPS * POOL_GROUP_DIM
POOL_BUF = 15
RET_HEADS = 4
RET_WIDTH = RET_HEADS * HEAD_DIM
RET_CHUNK = 128
ROPE_BASE = 10000.0
DIL_PATTERNS = ((128, 1), (512, 4), (2048, 16))
DIL_GROUPS = 3
DIL_HEADS = 4
DIL_WIDTH = DIL_HEADS * HEAD_DIM
DIL_BLOCK = 128
N_BRANCH = 4
BRANCH_WIDTH = 256
COL_POOL = RW_COLS
COL_RET = COL_POOL + POOL_WIDTH
COL_DIL = COL_RET + 4 * RET_WIDTH
COL_GATE = COL_DIL + 3 * DIL_GROUPS * DIL_WIDTH
IN_COLS = COL_GATE + N_BRANCH * D_MODEL
D_FF = 2816
N_EXPERTS = 8
TOP_K = 2
D_FF_EXPERT = 3584
N_DENSE = (DEPTH + 1) // 2
N_MOE = DEPTH // 2
DN_ALPHA = (2 * DEPTH) ** 0.25
DN_BETA = (8 * DEPTH) ** -0.25
LN_EPS = 1e-5

kernel_name = 'hybrid_rwkv7_pool_retnet_dilated_step'


def layer_norm(x, g, b):
    xf = x.astype(F32)
    mu = xf.mean(-1, keepdims=True)
    var = jnp.square(xf - mu).mean(-1, keepdims=True)
    return ((xf - mu) * lax.rsqrt(var + LN_EPS) * g.astype(F32) + b.astype(F32)).astype(x.dtype)


def head_norm(x, g, b, eps):
    H, N = x.shape[-2:]
    mu = x.mean(-1, keepdims=True)
    var = jnp.square(x - mu).mean(-1, keepdims=True)
    return (x - mu) * lax.rsqrt(var + eps) * g.reshape(H, N).astype(F32) + b.reshape(H, N).astype(F32)


def rotary(x, pos):
    half = HEAD_DIM // 2
    inv = ROPE_BASE ** (-jnp.arange(half, dtype=F32) / half)
    ang = pos.astype(F32)[:, None] * inv[None, :]
    cos, sin = jnp.cos(ang)[None, :, None, :], jnp.sin(ang)[None, :, None, :]
    x1, x2 = x[..., :half], x[..., half:]
    return jnp.concatenate([x1 * cos - x2 * sin, x2 * cos + x1 * sin], axis=-1)


def alibi_slopes(n):
    return 2.0 ** (-8.0 * jnp.arange(1, n + 1, dtype=F32) / n)


def rwkv7_branch(p, p_prev, wkv0, mu, w0, w2, a0, a2, g2, k_k, k_a, r_k, gn_g, gn_b):
    B, T, _ = p.shape
    shifted = jnp.concatenate([p_prev[:, None, :], p[:, :-1]], axis=1)
    u = p + (shifted - p) * mu
    o1, o2, o3 = RW_WIDTH, 2 * RW_WIDTH, 3 * RW_WIDTH
    o4 = o3 + RW_DECAY_LORA
    o5 = o4 + RW_A_LORA
    r, k, v = u[..., :o1], u[..., o1:o2], u[..., o2:o3]
    wd, ad, gd = u[..., o3:o4], u[..., o4:o5], u[..., o5:]
    w_pre = (w0 + jnp.tanh(wd) @ w2).astype(F32)
    log_decay = -jnp.exp(-jax.nn.softplus(-w_pre) - 0.5)
    a = jax.nn.sigmoid((a0 + ad @ a2).astype(F32))
    g = jax.nn.sigmoid(gd) @ g2
    heads = lambda t: t.astype(F32).reshape(B, T, RW_HEADS, HEAD_DIM)
    kk = heads(k.astype(F32) * k_k.astype(F32))
    kk = kk * lax.rsqrt(jnp.maximum(jnp.sum(kk * kk, -1, keepdims=True), 1e-24))
    k_mod = heads(k.astype(F32) * (1.0 + (a - 1.0) * k_a.astype(F32)))
    r_h, v_h, a_h = heads(r), heads(v), heads(a)
    decay = jnp.exp(heads(log_decay))
    tm = lambda t: jnp.swapaxes(t, 0, 1)

    def step(S, inp):
        r_t, w_t, k_t, v_t, a_t, b_t = inp
        sa = jnp.einsum('bhvk,bhk->bhv', S, a_t)
        S = S * w_t[:, :, None, :] + sa[..., None] * b_t[:, :, None, :] + v_t[..., None] * k_t[:, :, None, :]
        return S, jnp.einsum('bhvk,bhk->bhv', S, r_t)

    S_T, o = lax.scan(step, wkv0.astype(F32), (tm(r_h), tm(decay), tm(k_mod), tm(v_h), tm(-kk), tm(kk * a_h)))
    o = head_norm(jnp.swapaxes(o, 0, 1), gn_g, gn_b, RW_GN_EPS)
    o = o + jnp.sum(r_h * k_mod * r_k.astype(F32), -1, keepdims=True) * v_h
    return (o.reshape(B, T, RW_WIDTH) * g.astype(F32)).astype(p.dtype), S_T.astype(wkv0.dtype)


def pool_branch(u, buf, pos, w_grp, scale):
    B, T, C = u.shape
    ext = jnp.concatenate([buf, u], axis=1).astype(F32)
    cs = jnp.pad(jnp.cumsum(ext, axis=1), ((0, 0), (1, 0), (0, 0)))
    end = cs[:, POOL_BUF + 1:]
    means = []
    for g, win in enumerate(POOL_WINDOWS):
        sl = slice(g * POOL_GROUP_DIM, (g + 1) * POOL_GROUP_DIM)
        start = cs[:, POOL_BUF + 1 - win: POOL_BUF + 1 - win + T, sl]
        cnt = jnp.minimum(win, pos + 1).astype(F32)[None, :, None]
        means.append((end[..., sl] - start) / cnt)
    pooled = jnp.concatenate(means, axis=-1) - ext[:, POOL_BUF:]
    mixed = jnp.einsum('btgc,gcd->btgd', pooled.reshape(B, T, POOL_GROUPS, POOL_GROUP_DIM), w_grp.astype(F32))
    out = mixed.reshape(B, T, C) * scale.astype(F32)
    return out.astype(u.dtype), ext[:, -POOL_BUF:].astype(buf.dtype)


def chunk_retention(q, k, v, S0, chunk):
    B, T, H, N = q.shape
    nc = T // chunk
    log_g = jnp.log(1.0 - 2.0 ** (-5.0 - jnp.arange(H, dtype=F32)))
    idx = jnp.arange(chunk)
    rel = idx[:, None] - idx[None, :]
    dmask = jnp.where(rel >= 0, jnp.exp(log_g[:, None, None] * jnp.maximum(rel, 0)), 0.0)
    qdec = jnp.exp(log_g[None, :] * (idx[:, None] + 1))[None, :, :, None]
    kdec = jnp.exp(log_g[None, :] * (chunk - 1 - idx[:, None]))[None, :, :, None]
    cdec = jnp.exp(log_g * chunk)[None, :, None, None]
    split = lambda t: jnp.swapaxes(t.reshape(B, nc, chunk, H, N), 0, 1)

    def step(S, inp):
        qc, kc, vc = inp
        inner = jnp.einsum('bqhn,bkhn->bhqk', qc, kc) * dmask
        o = jnp.einsum('bhqk,bkhn->bqhn', inner, vc) + jnp.einsum('bqhn,bhnm->bqhm', qc * qdec, S)
        S = S * cdec + jnp.einsum('bkhn,bkhm->bhnm', kc * kdec, vc)
        return S, o

    S_T, o = lax.scan(step, S0, (split(q), split(k), split(v)))
    return jnp.swapaxes(o, 0, 1).reshape(B, T, H, N), S_T


def retention_branch(p, S0, pos, gn_g, gn_b, chunk):
    B, T, _ = p.shape
    q, k, v, g = [p[..., i * RET_WIDTH:(i + 1) * RET_WIDTH].astype(F32).reshape(B, T, RET_HEADS, HEAD_DIM) for i in range(4)]
    q = rotary(q, pos)
    k = rotary(k, pos) * HEAD_DIM ** -0.5
    o, S_T = chunk_retention(q, k, v, S0.astype(F32), chunk)
    o = jax.nn.silu(g) * head_norm(o, gn_g, gn_b, LN_EPS)
    return o.reshape(B, T, RET_WIDTH).astype(p.dtype), S_T.astype(S0.dtype)


def dilated_attn_prompt(q, k, v, win, dil, slopes):
    B, S, H, N = q.shape
    n_steps = win // dil
    span = dil * DIL_BLOCK
    Lp = -(-S // span) * span
    nb = Lp // span

    def to_streams(x):
        x = jnp.pad(x.astype(F32), ((0, 0), (0, Lp - S), (0, 0), (0, 0)))
        x = x.reshape(B, Lp // dil, dil, H, N).transpose(0, 2, 1, 3, 4)
        return x.reshape(B, dil, nb, DIL_BLOCK, H, N)

    def band(x):
        xp = jnp.pad(x, ((0, 0), (0, 0), (1, 0), (0, 0), (0, 0), (0, 0)))
        return jnp.concatenate([xp[:, :, :-1], xp[:, :, 1:]], axis=3)

    qs = to_streams(q)
    kb, vb = band(to_streams(k)), band(to_streams(v))
    qi = jnp.arange(DIL_BLOCK)[:, None]
    ki = jnp.arange(2 * DIL_BLOCK)[None, :]
    steps = DIL_BLOCK + qi - ki
    valid = (steps >= 0) & (steps <= n_steps)
    valid = valid[None] & ((jnp.arange(nb)[:, None, None] > 0) | (ki[None] >= DIL_BLOCK))
    scores = jnp.einsum('brcqhn,brckhn->brchqk', qs, kb) * HEAD_DIM ** -0.5
    scores = scores - slopes[:, None, None] * (dil * steps).astype(F32)[None]
    scores = jnp.where(valid[:, None], scores, -jnp.inf)
    lse = jax.nn.logsumexp(scores, axis=-1)
    o = jnp.einsum('brchqk,brckhn->brcqhn', jnp.exp(scores - lse[..., None]), vb)
    o = o.reshape(B, dil, Lp // dil, H, N).transpose(0, 2, 1, 3, 4).reshape(B, Lp, H, N)[:, :S]
    lse = lse.transpose(0, 1, 2, 4, 3).reshape(B, dil, Lp // dil, H).transpose(0, 2, 1, 3).reshape(B, Lp, H)[:, :S]
    return o, lse


def dilated_attn_step(q, kcat, vcat, win, dil, slopes):
    B, T = q.shape[:2]
    L = kcat.shape[1] - T
    i = jnp.arange(win // dil + 1)
    idx = L + jnp.arange(T)[:, None] - i[None, :] * dil
    valid = idx >= 0
    idxc = jnp.maximum(idx, 0)
    kg = kcat[:, idxc].astype(F32)
    vg = vcat[:, idxc].astype(F32)
    scores = jnp.einsum('bthn,btjhn->bthj', q.astype(F32), kg) * HEAD_DIM ** -0.5
    scores = scores - slopes[:, None] * (i * dil).astype(F32)[None, :]
    scores = jnp.where(valid[None, :, None, :], scores, -jnp.inf)
    lse = jax.nn.logsumexp(scores, axis=-1)
    o = jnp.einsum('bthj,btjhn->bthn', jnp.exp(scores - lse[..., None]), vg)
    return o, lse


def dilated_branch(p, kv_bufs):
    B, T, _ = p.shape
    qkv = p.reshape(B, T, DIL_GROUPS, 3, DIL_HEADS, HEAD_DIM)
    slopes = alibi_slopes(DIL_GROUPS * DIL_HEADS).reshape(DIL_GROUPS, DIL_HEADS)
    outs, lses, kv_new = [], [], []
    for g, (win, dil) in enumerate(DIL_PATTERNS):
        q, k, v = qkv[:, :, g, 0], qkv[:, :, g, 1], qkv[:, :, g, 2]
        if kv_bufs is None:
            o, lse = dilated_attn_prompt(q, k, v, win, dil, slopes[g])
            keep = min(win, T)
            kv_new.append(jnp.stack([k[:, T - keep:], v[:, T - keep:]], axis=2))
        else:
            buf = kv_bufs[g]
            kcat = jnp.concatenate([buf[:, :, 0], k], axis=1)
            vcat = jnp.concatenate([buf[:, :, 1], v], axis=1)
            o, lse = dilated_attn_step(q, kcat, vcat, win, dil, slopes[g])
            kv_new.append(jnp.stack([k, v], axis=2))
        outs.append(o)
        lses.append(lse)
    wts = jax.nn.softmax(jnp.stack(lses), axis=0)
    o = jnp.einsum('gbth,gbthn->bthn', wts, jnp.stack(outs))
    return o.reshape(B, T, DIL_WIDTH).astype(p.dtype), kv_new


def token_mix(h, pos0, rw_prev, wkv0, pool_buf, ret0, kv_bufs, lw):
    (w_in, rw_mu, rw_w0, rw_w2, rw_a0, rw_a2, rw_g2, rw_kk, rw_ka, rw_rk, rw_gn_g, rw_gn_b,
     pool_w, pool_scale, ret_gn_g, ret_gn_b, w_branch, w_out) = lw
    B, T, _ = h.shape
    pos = pos0 + jnp.arange(T, dtype=jnp.int32)
    proj = h @ w_in
    p_rw = proj[..., :COL_POOL]
    o_a, wkv_new = rwkv7_branch(p_rw, rw_prev, wkv0, rw_mu, rw_w0, rw_w2, rw_a0, rw_a2, rw_g2,
                                rw_kk, rw_ka, rw_rk, rw_gn_g, rw_gn_b)
    o_b, pool_new = pool_branch(proj[..., COL_POOL:COL_RET], pool_buf, pos, pool_w, pool_scale)
    chunk = RET_CHUNK if T % RET_CHUNK == 0 else T
    o_c, ret_new = retention_branch(proj[..., COL_RET:COL_DIL], ret0, pos, ret_gn_g, ret_gn_b, chunk)
    o_d, kv_new = dilated_branch(proj[..., COL_DIL:COL_GATE], kv_bufs)
    branches = jnp.stack([o_a, o_b, o_c, o_d], axis=2)
    up = jnp.einsum('btnc,ncd->btnd', branches, w_branch)
    gates = jax.nn.sigmoid(proj[..., COL_GATE:].reshape(B, T, N_BRANCH, D_MODEL))
    y = jnp.einsum('btnd,btnd->btd', gates, up) @ w_out
    new_state = (wkv_new, p_rw[:, -1], pool_new, ret_new, kv_new[0], kv_new[1], kv_new[2])
    return y, new_state


def swiglu(x, wg, wu, wd):
    return (jax.nn.silu(x @ wg) * (x @ wu)) @ wd


def moe_ffn(h, router, wg, wu, wd):
    B, T, D = h.shape
    xf = h.reshape(B * T, D)
    logits = (xf @ router).astype(F32)
    top_v, top_i = lax.top_k(logits, TOP_K)
    top_w = jax.nn.softmax(top_v, axis=-1)
    gate = jnp.einsum('nk,nke->ne', top_w, jax.nn.one_hot(top_i, N_EXPERTS, dtype=F32)).astype(h.dtype)
    out = jnp.zeros_like(xf)
    for e in range(N_EXPERTS):
        out = out + gate[:, e:e + 1] * swiglu(xf, wg[e], wu[e], wd[e])
    return out.reshape(B, T, D)


def setup_inputs(seed: int = 0) -> dict:
    key = jax.random.key(seed)
    ks = iter(jax.random.split(key, 64))

    def nrm(shape, scale):
        return jax.random.normal(next(ks), shape, F32) * scale

    def unif(shape, lo, hi):
        return jax.random.uniform(next(ks), shape, F32, lo, hi)

    L, D = DEPTH, D_MODEL
    return {
        'x_prompt': nrm((BATCH, SEQ, D), 1.0),
        'x_sample': nrm((DEC_BATCH, DEC_SEQ, D), 1.0),
        'state_wkv': nrm((L, DEC_BATCH, RW_HEADS, HEAD_DIM, HEAD_DIM), 0.3),
        'state_shift': nrm((L, DEC_BATCH, RW_COLS), 1.0),
        'state_pool': nrm((L, DEC_BATCH, POOL_BUF, POOL_WIDTH), 1.0),
        'state_ret': nrm((L, DEC_BATCH, RET_HEADS, HEAD_DIM, HEAD_DIM), 0.3),
        'cache_kv_w128': nrm((L, DEC_BATCH, min(DIL_PATTERNS[0][0], PAST_LEN), 2, DIL_HEADS, HEAD_DIM), 1.0),
        'cache_kv_w512': nrm((L, DEC_BATCH, min(DIL_PATTERNS[1][0], PAST_LEN), 2, DIL_HEADS, HEAD_DIM), 1.0),
        'cache_kv_w2048': nrm((L, DEC_BATCH, min(DIL_PATTERNS[2][0], PAST_LEN), 2, DIL_HEADS, HEAD_DIM), 1.0),
        'w_in': nrm((L, D, IN_COLS), D ** -0.5),
        'rw_mu': unif((L, RW_COLS), 0.0, 1.0),
        'rw_w0': unif((L, RW_WIDTH), -4.0, 1.0),
        'rw_w2': nrm((L, RW_DECAY_LORA, RW_WIDTH), 0.1 * RW_DECAY_LORA ** -0.5),
        'rw_a0': nrm((L, RW_WIDTH), 0.1),
        'rw_a2': nrm((L, RW_A_LORA, RW_WIDTH), 0.1 * RW_A_LORA ** -0.5),
        'rw_g2': nrm((L, RW_GATE_LORA, RW_WIDTH), RW_GATE_LORA ** -0.5),
        'rw_kk': 0.85 + nrm((L, RW_WIDTH), 0.05),
        'rw_ka': 1.0 + nrm((L, RW_WIDTH), 0.05),
        'rw_rk': nrm((L, RW_HEADS, HEAD_DIM), 0.1),
        'rw_gn_g': 1.0 + nrm((L, RW_WIDTH), 0.05),
        'rw_gn_b': nrm((L, RW_WIDTH), 0.02),
        'pool_w': nrm((L, POOL_GROUPS, POOL_GROUP_DIM, POOL_GROUP_DIM), POOL_GROUP_DIM ** -0.5),
        'pool_scale': 1.0 + nrm((L, POOL_WIDTH), 0.05),
        'ret_gn_g': 1.0 + nrm((L, RET_WIDTH), 0.05),
        'ret_gn_b': nrm((L, RET_WIDTH), 0.02),
        'w_branch': nrm((L, N_BRANCH, BRANCH_WIDTH, D), DN_BETA * BRANCH_WIDTH ** -0.5),
        'w_out': nrm((L, D, D), DN_BETA * D ** -0.5),
        'ln_g': 1.0 + nrm((L, 2, D), 0.05),
        'ln_b': nrm((L, 2, D), 0.02),
        'ffn_w_gate': nrm((N_DENSE, D, D_FF), D ** -0.5),
        'ffn_w_up': nrm((N_DENSE, D, D_FF), D ** -0.5),
        'ffn_w_down': nrm((N_DENSE, D_FF, D), DN_BETA * D_FF ** -0.5),
        'moe_router': nrm((N_MOE, D, N_EXPERTS), D ** -0.5),
        'moe_w_gate': nrm((N_MOE, N_EXPERTS, D, D_FF_EXPERT), D ** -0.5),
        'moe_w_up': nrm((N_MOE, N_EXPERTS, D, D_FF_EXPERT), D ** -0.5),
        'moe_w_down': nrm((N_MOE, N_EXPERTS, D_FF_EXPERT, D), DN_BETA * D_FF_EXPERT ** -0.5),
    }


def reference(x_prompt, x_sample, state_wkv, state_shift, state_pool, state_ret,
              cache_kv_w128, cache_kv_w512, cache_kv_w2048,
              w_in, rw_mu, rw_w0, rw_w2, rw_a0, rw_a2, rw_g2, rw_kk, rw_ka, rw_rk, rw_gn_g, rw_gn_b,
              pool_w, pool_scale, ret_gn_g, ret_gn_b, w_branch, w_out, ln_g, ln_b,
              ffn_w_gate, ffn_w_up, ffn_w_down, moe_router, moe_w_gate, moe_w_up, moe_w_down):
    hp, hs = x_prompt, x_sample
    Bp = hp.shape[0]
    new_p = [[] for _ in range(7)]
    new_s = [[] for _ in range(7)]
    for l in range(DEPTH):
        lw = (w_in[l], rw_mu[l], rw_w0[l], rw_w2[l], rw_a0[l], rw_a2[l], rw_g2[l], rw_kk[l], rw_ka[l],
              rw_rk[l], rw_gn_g[l], rw_gn_b[l], pool_w[l], pool_scale[l], ret_gn_g[l], ret_gn_b[l],
              w_branch[l], w_out[l])
        mp, st_p = token_mix(
            hp, 0,
            jnp.zeros((Bp, RW_COLS), hp.dtype),
            jnp.zeros((Bp, RW_HEADS, HEAD_DIM, HEAD_DIM), hp.dtype),
            jnp.zeros((Bp, POOL_BUF, POOL_WIDTH), hp.dtype),
            jnp.zeros((Bp, RET_HEADS, HEAD_DIM, HEAD_DIM), hp.dtype),
            None, lw)
        ms, st_s = token_mix(
            hs, PAST_LEN, state_shift[l], state_wkv[l], state_pool[l], state_ret[l],
            (cache_kv_w128[l], cache_kv_w512[l], cache_kv_w2048[l]), lw)
        hp = layer_norm(DN_ALPHA * hp + mp, ln_g[l, 0], ln_b[l, 0])
        hs = layer_norm(DN_ALPHA * hs + ms, ln_g[l, 0], ln_b[l, 0])
        j = l // 2
        if l % 2 == 0:
            fp = swiglu(hp, ffn_w_gate[j], ffn_w_up[j], ffn_w_down[j])
            fs = swiglu(hs, ffn_w_gate[j], ffn_w_up[j], ffn_w_down[j])
        else:
            fp = moe_ffn(hp, moe_router[j], moe_w_gate[j], moe_w_up[j], moe_w_down[j])
            fs = moe_ffn(hs, moe_router[j], moe_w_gate[j], moe_w_up[j], moe_w_down[j])
        hp = layer_norm(DN_ALPHA * hp + fp, ln_g[l, 1], ln_b[l, 1])
        hs = layer_norm(DN_ALPHA * hs + fs, ln_g[l, 1], ln_b[l, 1])
        for i in range(7):
            new_p[i].append(st_p[i])
            new_s[i].append(st_s[i])
    wkv_p, shift_p, pool_p, ret_p = jnp.stack(new_p[0]), jnp.stack(new_p[1]), jnp.stack(new_p[2]), jnp.stack(new_p[3])
    kv128_p, kv512_p, kv2048_p = jnp.stack(new_p[4]), jnp.stack(new_p[5]), jnp.stack(new_p[6])
    wkv_s, shift_s, pool_s, ret_s = jnp.stack(new_s[0]), jnp.stack(new_s[1]), jnp.stack(new_s[2]), jnp.stack(new_s[3])
    kv128_s, kv512_s, kv2048_s = jnp.stack(new_s[4]), jnp.stack(new_s[5]), jnp.stack(new_s[6])
    return (hp, hs, wkv_p, shift_p, pool_p, ret_p, kv128_p, kv512_p, kv2048_p,
            wkv_s, shift_s, pool_s, ret_s, kv128_s, kv512_s, kv2048_s)
```

```python
import functools
import math

import jax
import jax.numpy as jnp
from jax import lax
from jax.experimental import pallas as pl
from jax.experimental.pallas import tpu as pltpu

F32 = jnp.float32
BF16 = jnp.bfloat16

D_MODEL = 1024
DEPTH = 2
HEAD_DIM = 64
N_HEADS = 4
WIDTH = N_HEADS * HEAD_DIM
RW_COLS = 1024
RW_GN_EPS = 64e-5
POOL_WINDOWS = (2, 4, 8, 16)
POOL_BUF = 15
RET_CHUNK = 128
ROPE_BASE = 10000.0
DIL_PATTERNS = ((128, 1), (512, 4), (2048, 16))
DIL_GROUPS = 3
DIL_BLOCK = 128
DIL_COLS = 3 * DIL_GROUPS * WIDTH
N_BRANCH = 4
COL_POOL = RW_COLS
COL_RET = COL_POOL + WIDTH
COL_DIL = COL_RET + 4 * WIDTH
COL_GATE = COL_DIL + DIL_COLS
N_EXPERTS = 8
DN_ALPHA = (2 * DEPTH) ** 0.25
LN_EPS = 1e-5
RW_CHUNK = 64
NEG_BIG = -1e30

NN = (((1,), (0,)), ((), ()))
NT = (((1,), (1,)), ((), ()))
TN = (((0,), (0,)), ((), ()))

VMEM_LIMIT = 56 * 1024 * 1024


def _split(x, n):
    if x.dtype == BF16:
        return [x]
    parts, rem = [], x
    for i in range(n):
        p = rem.astype(BF16)
        parts.append(p)
        if i + 1 < n:
            rem = rem - p.astype(F32)
    return parts


def _mm(a, b, dims=NN, pa=1, pb=1):
    a_parts, b_parts = _split(a, pa), _split(b, pb)
    depth = max(len(a_parts), len(b_parts))
    acc = None
    for i, ai in enumerate(a_parts):
        for j, bj in enumerate(b_parts):
            if i + j < depth:
                t = lax.dot_general(ai, bj, dims, preferred_element_type=F32)
                acc = t if acc is None else acc + t
    return acc


def _iota(shape, axis):
    return lax.broadcasted_iota(jnp.int32, shape, axis)


def _div(x, d):
    assert d & (d - 1) == 0
    return x >> (d.bit_length() - 1)


def _mod(x, d):
    assert d & (d - 1) == 0
    return x & (d - 1)


def _per_head(head, values):
    out = jnp.full(head.shape, values[-1], F32)
    for h in range(len(values) - 2, -1, -1):
        out = jnp.where(head == h, values[h], out)
    return out


def _head_mask(rows_per_head, n_rows):
    return _div(_iota((n_rows, WIDTH), 0), rows_per_head) == _div(_iota((n_rows, WIDTH), 1), HEAD_DIM)


def _stack_heads(x, mask):
    return jnp.where(mask, jnp.concatenate([x] * N_HEADS, axis=0), 0.0)


def _unstack_heads(x_st, mask, c):
    x_st = jnp.where(mask, x_st, 0.0)
    out = x_st[0:c]
    for h in range(1, N_HEADS):
        out = out + x_st[h * c:(h + 1) * c]
    return out


def _ones_bd():
    return _same_head().astype(BF16)


def _same_head():
    return _div(_iota((WIDTH, WIDTH), 0), HEAD_DIM) == _div(_iota((WIDTH, WIDTH), 1), HEAD_DIM)


def _head_norm(x, ones_bd, g, b, eps):
    mu = _mm(x, ones_bd, pa=2) * (1.0 / HEAD_DIM)
    d = x - mu
    var = _mm(d * d, ones_bd, pa=2) * (1.0 / HEAD_DIM)
    return d * lax.rsqrt(var + eps) * g + b


def _layer_norm(x, g, b):
    mu = jnp.mean(x, axis=-1, keepdims=True)
    d = x - mu
    var = jnp.mean(d * d, axis=-1, keepdims=True)
    return d * lax.rsqrt(var + LN_EPS) * g + b


def _load_rows(p_ref, buf_ref, t_in, c):
    if t_in == c:
        return p_ref[0]
    buf_ref[...] = jnp.zeros_like(buf_ref)
    buf_ref[0:t_in, :] = p_ref[0]
    return buf_ref[...]


def _resident(shape):
    nd = len(shape)
    return pl.BlockSpec(shape, lambda *_: (0,) * nd, pipeline_mode=pl.Buffered(1))


def _proj_kernel(x_ref, w_ref, rw_ref, pool_ref, ret_ref, dil_ref):
    xb = x_ref[...].astype(BF16)
    for ref, lo, hi in ((rw_ref, 0, COL_POOL), (pool_ref, COL_POOL, COL_RET),
                        (ret_ref, COL_RET, COL_DIL), (dil_ref, COL_DIL, COL_GATE)):
        for s in range(lo, hi, 512):
            e = min(s + 512, hi)
            ref[:, s - lo:e - lo] = _mm(xb, w_ref[:, s:e])


def _project(x, w_mix, tm):
    m = x.shape[0]
    widths = (COL_POOL, WIDTH, 4 * WIDTH, DIL_COLS)
    return pl.pallas_call(
        _proj_kernel,
        grid=(m // tm,),
        in_specs=[pl.BlockSpec((tm, D_MODEL), lambda i: (i, 0)), _resident((D_MODEL, COL_GATE))],
        out_specs=[pl.BlockSpec((tm, w), lambda i: (i, 0)) for w in widths],
        out_shape=[jax.ShapeDtypeStruct((m, w), F32) for w in widths],
        compiler_params=pltpu.CompilerParams(dimension_semantics=("parallel",), vmem_limit_bytes=VMEM_LIMIT),
        name="proj",
    )(x, w_mix)


def _rwkv_kernel(p_ref, prev_ref, s0_ref, mu_ref, w0_ref, w2_ref, a0_ref, a2_ref, g2_ref, kk_ref, ka_ref,
                 rk_ref, gng_ref, gnb_ref, o_ref, st_ref, s_scr, prev_scr, buf_scr, *, c, t_in):
    ci = pl.program_id(1)

    @pl.when(ci == 0)
    def _():
        s_scr[...] = s0_ref[0]
        prev_scr[0:1, :] = prev_ref[0]

    p = _load_rows(p_ref, buf_scr, t_in, c)
    row = _iota((c, RW_COLS), 0)
    shifted = jnp.where(row == 0, prev_scr[0:1, :], pltpu.roll(p, 1, 0))
    prev_scr[0:1, :] = p[c - 1:c, :]
    u = p + (shifted - p) * mu_ref[...]
    r, k, v, ul = u[:, 0:WIDTH], u[:, WIDTH:2 * WIDTH], u[:, 2 * WIDTH:3 * WIDTH], u[:, 3 * WIDTH:]

    w_pre = w0_ref[...] + _mm(jnp.tanh(ul), w2_ref[...])
    logw = -math.exp(-0.5) * jax.nn.sigmoid(w_pre)
    a = jax.nn.sigmoid(a0_ref[...] + _mm(ul, a2_ref[...]))
    g = _mm(jax.nn.sigmoid(ul), g2_ref[...])

    ones_bd = _ones_bd()
    kk = k * kk_ref[...]
    kk = kk * lax.rsqrt(jnp.maximum(_mm(kk * kk, ones_bd, pa=2), 1e-24))
    k_mod = k * (1.0 + (a - 1.0) * ka_ref[...])
    a_vec, b_vec = -kk, kk * a
    if t_in < c:
        live = _iota((c, WIDTH), 0) < t_in
        logw = jnp.where(live, logw, 0.0)
        a_vec, b_vec = jnp.where(live, a_vec, 0.0), jnp.where(live, b_vec, 0.0)
        k_mod, v = jnp.where(live, k_mod, 0.0), jnp.where(live, v, 0.0)

    tri = (_iota((c, c), 0) >= _iota((c, c), 1)).astype(BF16)
    cum = _mm(tri, logw, pb=3)
    cum_end = cum[c - 1:c, :]
    e_pos, e_neg, e_end = jnp.exp(cum), jnp.exp(-cum), jnp.exp(cum_end - cum)
    a_t = a_vec * jnp.exp(cum - logw)
    r_t = r * e_pos
    b_t, k_t = b_vec * e_neg, k_mod * e_neg
    b_e, k_e = b_vec * e_end, k_mod * e_end

    n = N_HEADS * c
    hm = _head_mask(c, n)
    a_st, r_st = _stack_heads(a_t, hm), _stack_heads(r_t, hm)
    b_st, v_st = _stack_heads(b_t, hm), _stack_heads(v, hm)
    bi, bj = _iota((n, n), 0), _iota((n, n), 1)
    same = _div(bi, c) == _div(bj, c)
    strict_bd = same & (_mod(bi, c) > _mod(bj, c))
    incl_bd = same & (_mod(bi, c) >= _mod(bj, c))
    ti, tj = _mod(_iota((n, c), 0), c), _iota((n, c), 1)
    strict_st, incl_st = ti > tj, ti >= tj

    a_ab = jnp.where(strict_bd, _mm(a_st, b_st, NT, 2, 2), 0.0)
    inv = jnp.where(bi == bj, 1.0, 0.0) + a_ab
    pw = a_ab
    for _ in range(int(math.log2(c)) - 1):
        pw = _mm(pw, pw, NN, 2, 2)
        inv = inv + _mm(inv, pw, NN, 2, 2)
    a_ak = jnp.where(strict_st, _mm(a_st, k_t, NT, 2, 2), 0.0)
    z_st = jnp.where(hm, _mm(a_ak, v, NN, 2, 2), 0.0)
    w_st = _mm(inv, a_st, NN, 2, 2)
    u0_st = _mm(inv, z_st, NN, 2, 2)

    s0 = s_scr[...]
    u_st = _mm(w_st, s0, NT, 2, 2) + u0_st
    lhs = jnp.concatenate([u_st, v_st], axis=0)
    rhs = jnp.concatenate([_stack_heads(b_e, hm), _stack_heads(k_e, hm)], axis=0)
    s1 = s0 * jnp.exp(cum_end) + _mm(lhs, rhs, TN, 2, 2)
    s_scr[...] = s1

    a_rb = jnp.where(incl_bd, _mm(r_st, b_st, NT, 2, 2), 0.0)
    a_rk = jnp.where(incl_st, _mm(r_st, k_t, NT, 2, 2), 0.0)
    o_st = _mm(r_st, s0, NT, 2, 2) + _mm(a_rb, u_st, NN, 2, 2) + _mm(a_rk, v, NN, 2, 2)
    o = _unstack_heads(o_st, hm, c)

    o = _head_norm(o, ones_bd, gng_ref[...], gnb_ref[...], RW_GN_EPS)
    o = o + _mm(r * k_mod * rk_ref[...], ones_bd, pa=2) * v
    o_ref[0] = (o * g)[0:t_in]

    @pl.when(ci == pl.num_programs(1) - 1)
    def _():
        st_ref[0] = s1


def _block_diag_heads(s):
    b = s.shape[0]
    eye = jnp.eye(N_HEADS, dtype=s.dtype)
    return jnp.einsum('bhij,hg->bhigj', s, eye).reshape(b, WIDTH, WIDTH)


def _diag_heads(s_bd):
    b = s_bd.shape[0]
    s = s_bd.reshape(b, N_HEADS, HEAD_DIM, N_HEADS, HEAD_DIM)
    return jnp.stack([s[:, h, :, h, :] for h in range(N_HEADS)], axis=1)


def _rwkv_branch(p, p_prev, wkv0, lw):
    b, t, _ = p.shape
    c = RW_CHUNK
    t_in = c if t % c == 0 else t
    nc = t // c if t % c == 0 else 1
    vec = lambda x: x.reshape(1, -1)
    pad_rows = lambda w, lo: jnp.zeros((WIDTH, WIDTH), F32).at[lo:lo + w.shape[0]].set(w).astype(BF16)
    params = [vec(lw['rw_mu']), vec(lw['rw_w0']), pad_rows(lw['rw_w2'], 0), vec(lw['rw_a0']),
              pad_rows(lw['rw_a2'], 64), pad_rows(lw['rw_g2'], 128), vec(lw['rw_kk']), vec(lw['rw_ka']),
              vec(lw['rw_rk']), vec(lw['rw_gn_g']), vec(lw['rw_gn_b'])]
    o, st = pl.pallas_call(
        functools.partial(_rwkv_kernel, c=c, t_in=t_in),
        grid=(b, nc),
        in_specs=[pl.BlockSpec((1, t_in, RW_COLS), lambda i, j: (i, j, 0)),
                  pl.BlockSpec((1, 1, RW_COLS), lambda i, j: (i, 0, 0)),
                  pl.BlockSpec((1, WIDTH, WIDTH), lambda i, j: (i, 0, 0))]
                 + [_resident(x.shape) for x in params],
        out_specs=[pl.BlockSpec((1, t_in, WIDTH), lambda i, j: (i, j, 0)),
                   pl.BlockSpec((1, WIDTH, WIDTH), lambda i, j: (i, 0, 0))],
        out_shape=[jax.ShapeDtypeStruct((b, t, WIDTH), F32), jax.ShapeDtypeStruct((b, WIDTH, WIDTH), F32)],
        scratch_shapes=[pltpu.VMEM((WIDTH, WIDTH), F32), pltpu.VMEM((8, RW_COLS), F32),
                        pltpu.VMEM((c, RW_COLS), F32)],
        compiler_params=pltpu.CompilerParams(dimension_semantics=("parallel", "arbitrary"),
                                             vmem_limit_bytes=VMEM_LIMIT),
        name="rwkv7",
    )(p, p_prev.reshape(b, 1, RW_COLS), _block_diag_heads(wkv0), *params)
    return o, _diag_heads(st)


def _pool_kernel(u_ref, buf_ref, w_ref, scale_ref, o_ref, ext_scr, *, c, t_in, pos0):
    ci = pl.program_id(1)

    @pl.when(ci == 0)
    def _():
        ext_scr[0:16, :] = buf_ref[0]

    if t_in < c:
        ext_scr[16:16 + c, :] = jnp.zeros((c, WIDTH), F32)
    ext_scr[16:16 + t_in, :] = u_ref[0]
    x = ext_scr[16:16 + c, :]
    sums, acc, off = [], x, 1
    for win in POOL_WINDOWS:
        while off < win:
            acc = acc + ext_scr[16 - off:16 - off + c, :]
            off += 1
        sums.append(acc)
    pos = pos0 + ci * c + _iota((c, WIDTH), 0)
    grp = _div(_iota((c, WIDTH), 1), HEAD_DIM)
    mean = jnp.zeros((c, WIDTH), F32)
    for gi, win in enumerate(POOL_WINDOWS):
        cnt = jnp.minimum(win, pos + 1).astype(F32)
        mean = jnp.where(grp == gi, sums[gi] / cnt, mean)
    mixed = _mm(mean - x, w_ref[...]) * scale_ref[...]
    o_ref[0] = mixed[0:t_in]
    ext_scr[0:16, :] = ext_scr[c:c + 16, :]


def _pool_branch(u, buf, pos0, lw):
    b, t, _ = u.shape
    c = 512 if t % 512 == 0 else 8
    t_in = c if t % c == 0 else t
    nc = t // c if t % c == 0 else 1
    w_bd = jnp.einsum('gcd,gh->gchd', lw['pool_w'], jnp.eye(N_HEADS, dtype=F32)).reshape(WIDTH, WIDTH).astype(BF16)
    buf16 = jnp.pad(buf, ((0, 0), (1, 0), (0, 0)))
    return pl.pallas_call(
        functools.partial(_pool_kernel, c=c, t_in=t_in, pos0=pos0),
        grid=(b, nc),
        in_specs=[pl.BlockSpec((1, t_in, WIDTH), lambda i, j: (i, j, 0)),
                  pl.BlockSpec((1, 16, WIDTH), lambda i, j: (i, 0, 0)),
                  _resident((WIDTH, WIDTH)), _resident((1, WIDTH))],
        out_specs=pl.BlockSpec((1, t_in, WIDTH), lambda i, j: (i, j, 0)),
        out_shape=jax.ShapeDtypeStruct((b, t, WIDTH), F32),
        scratch_shapes=[pltpu.VMEM((16 + c, WIDTH), F32)],
        compiler_params=pltpu.CompilerParams(dimension_semantics=("parallel", "arbitrary")),
        name="pool",
    )(u, buf16, w_bd, lw['pool_scale'].reshape(1, WIDTH))


def _rot_half(x):
    first = _mod(_iota(x.shape, 1), HEAD_DIM) < (HEAD_DIM // 2)
    return jnp.where(first, pltpu.roll(x, WIDTH - HEAD_DIM // 2, 1), pltpu.roll(x, HEAD_DIM // 2, 1))


RET_LOG_DECAY = tuple(math.log(1.0 - 2.0 ** (-5.0 - h)) for h in range(N_HEADS))
ALIBI_SLOPES = tuple(2.0 ** (-8.0 * (i + 1) / (DIL_GROUPS * N_HEADS)) for i in range(DIL_GROUPS * N_HEADS))


def _ret_kernel(p_ref, cos_ref, sin_ref, s0_ref, gng_ref, gnb_ref, o_ref, st_ref, s_scr, buf_scr,
                *, c, t_in):
    ci = pl.program_id(1)

    @pl.when(ci == 0)
    def _():
        s_scr[...] = s0_ref[0]

    p = _load_rows(p_ref, buf_scr, t_in, c)
    q, k, v, g = (p[:, i * WIDTH:(i + 1) * WIDTH] for i in range(4))
    cos, sin = cos_ref[...], sin_ref[...]
    q = q * cos + _rot_half(q) * sin
    k = (k * cos + _rot_half(k) * sin) * HEAD_DIM ** -0.5
    if t_in < c:
        live = _iota((c, WIDTH), 0) < t_in
        k, v = jnp.where(live, k, 0.0), jnp.where(live, v, 0.0)

    lg = _per_head(_div(_iota((1, WIDTH), 1), HEAD_DIM), RET_LOG_DECAY)
    idx = _iota((c, WIDTH), 0).astype(F32)
    q_dec = q * jnp.exp(lg * (idx + 1.0))
    k_dec = k * jnp.exp(lg * (t_in - 1.0 - idx))
    n = N_HEADS * c
    hm = _head_mask(c, n)
    q_st = _stack_heads(q, hm)
    rel = _mod(_iota((n, c), 0), c) - _iota((n, c), 1)
    lg_rows = _per_head(_div(_iota((n, c), 0), c), RET_LOG_DECAY)
    dmask = jnp.where(rel >= 0, jnp.exp(lg_rows * jnp.maximum(rel, 0).astype(F32)), 0.0)
    inner = _mm(q_st, k, NT) * dmask
    s0 = s_scr[...]
    o = _unstack_heads(_mm(inner, v), hm, c) + _mm(q_dec, s0)
    s1 = s0 * jnp.exp(lg * float(t_in)) + jnp.where(_same_head(), _mm(k_dec, v, TN), 0.0)
    s_scr[...] = s1

    o = jax.nn.silu(g) * _head_norm(o, _ones_bd(), gng_ref[...], gnb_ref[...], LN_EPS)
    o_ref[0] = o[0:t_in]

    @pl.when(ci == pl.num_programs(1) - 1)
    def _():
        st_ref[0] = s1


def _rope_tables(pos0, t, rows):
    half = HEAD_DIM // 2
    inv = ROPE_BASE ** (-jnp.arange(half, dtype=F32) / half)
    ang = (pos0 + jnp.arange(t, dtype=jnp.int32)).astype(F32)[:, None] * inv[None, :]
    cos = jnp.tile(jnp.cos(ang), (1, 2 * N_HEADS))
    sin = jnp.tile(jnp.concatenate([-jnp.sin(ang), jnp.sin(ang)], axis=1), (1, N_HEADS))
    pad = ((0, rows - t), (0, 0))
    return jnp.pad(cos, pad), jnp.pad(sin, pad)


def _ret_branch(p, s0, pos0, lw):
    b, t, _ = p.shape
    c = RET_CHUNK
    t_in = c if t % c == 0 else t
    nc = t // c if t % c == 0 else 1
    cos, sin = _rope_tables(pos0, t, nc * c)
    o, st = pl.pallas_call(
        functools.partial(_ret_kernel, c=c, t_in=t_in),
        grid=(b, nc),
        in_specs=[pl.BlockSpec((1, t_in, 4 * WIDTH), lambda i, j: (i, j, 0)),
                  pl.BlockSpec((c, WIDTH), lambda i, j: (j, 0)),
                  pl.BlockSpec((c, WIDTH), lambda i, j: (j, 0)),
                  pl.BlockSpec((1, WIDTH, WIDTH), lambda i, j: (i, 0, 0)),
                  _resident((1, WIDTH)), _resident((1, WIDTH))],
        out_specs=[pl.BlockSpec((1, t_in, WIDTH), lambda i, j: (i, j, 0)),
                   pl.BlockSpec((1, WIDTH, WIDTH), lambda i, j: (i, 0, 0))],
        out_shape=[jax.ShapeDtypeStruct((b, t, WIDTH), F32), jax.ShapeDtypeStruct((b, WIDTH, WIDTH), F32)],
        scratch_shapes=[pltpu.VMEM((WIDTH, WIDTH), F32), pltpu.VMEM((c, 4 * WIDTH), F32)],
        compiler_params=pltpu.CompilerParams(dimension_semantics=("parallel", "arbitrary")),
        name="retention",
    )(p, cos, sin, _block_diag_heads(s0), lw['ret_gn_g'].reshape(1, WIDTH), lw['ret_gn_b'].reshape(1, WIDTH))
    return o, _diag_heads(st)


def _alibi_slope_rows(group, rows_per_head):
    head = _div(_iota((N_HEADS * rows_per_head, 1), 0), rows_per_head)
    return _per_head(head, ALIBI_SLOPES[group * N_HEADS:(group + 1) * N_HEADS])


def _dil_prompt_kernel(q_ref, kc_ref, kp_ref, vc_ref, vp_ref, o_ref, lse_ref, *, group, dil):
    blk = DIL_BLOCK
    ci = pl.program_id(2)
    n = N_HEADS * blk
    hm = _head_mask(blk, n)
    q_st = _stack_heads(q_ref[0], hm)
    kcat = jnp.concatenate([kp_ref[0], kc_ref[0]], axis=0)
    vcat = jnp.concatenate([vp_ref[0], vc_ref[0]], axis=0)
    s = _mm(q_st, kcat, NT) * HEAD_DIM ** -0.5
    ki = _iota((n, 2 * blk), 1)
    steps = blk + _mod(_iota((n, 2 * blk), 0), blk) - ki
    valid = (steps >= 0) & (steps <= blk) & ((ci > 0) | (ki >= blk))
    s = s - _alibi_slope_rows(group, blk) * (dil * steps).astype(F32)
    s = jnp.where(valid, s, NEG_BIG)
    m = jnp.max(s, axis=1, keepdims=True)
    e = jnp.exp(s - m)
    l = jnp.sum(e, axis=1, keepdims=True)
    o_st = _mm(e, vcat) / l
    lse = m + jnp.log(l)
    o_ref[0] = _unstack_heads(o_st, hm, blk)
    lse_ref[0] = _unstack_heads(jnp.broadcast_to(lse, (n, WIDTH)), hm, blk)


def _dil_prompt_group(pd, group):
    b, s, _ = pd.shape
    win, dil = DIL_PATTERNS[group]
    assert win // dil == DIL_BLOCK and s % (dil * DIL_BLOCK) == 0
    j = s // dil
    nb = j // DIL_BLOCK
    lanes = DIL_COLS // WIDTH
    pv = pd.reshape(b, j, dil * DIL_COLS)
    base = group * 3

    def spec(col, prev):
        if prev:
            return pl.BlockSpec((1, DIL_BLOCK, WIDTH), lambda i, r, c: (i, jnp.maximum(c - 1, 0), r * lanes + base + col))
        return pl.BlockSpec((1, DIL_BLOCK, WIDTH), lambda i, r, c: (i, c, r * lanes + base + col))

    out_spec = pl.BlockSpec((1, DIL_BLOCK, WIDTH), lambda i, r, c: (i, c, r))
    o, lse = pl.pallas_call(
        functools.partial(_dil_prompt_kernel, group=group, dil=dil),
        grid=(b, dil, nb),
        in_specs=[spec(0, False), spec(1, False), spec(1, True), spec(2, False), spec(2, True)],
        out_specs=[out_spec, out_spec],
        out_shape=[jax.ShapeDtypeStruct((b, j, dil * WIDTH), F32)] * 2,
        compiler_params=pltpu.CompilerParams(dimension_semantics=("parallel", "parallel", "arbitrary")),
        name=f"dil_prompt_g{group}",
    )(pv, pv, pv, pv, pv)
    return o.reshape(b, s, WIDTH), lse.reshape(b, s, WIDTH)


def _dil_combine_kernel(o0, o1, o2, l0, l1, l2, out_ref):
    ls = [l0[...], l1[...], l2[...]]
    m = jnp.maximum(jnp.maximum(ls[0], ls[1]), ls[2])
    es = [jnp.exp(x - m) for x in ls]
    out_ref[...] = (es[0] * o0[...] + es[1] * o1[...] + es[2] * o2[...]) / (es[0] + es[1] + es[2])


def _dil_prompt(pd):
    b, s, _ = pd.shape
    outs = [_dil_prompt_group(pd, g) for g in range(DIL_GROUPS)]
    flat = [x[0].reshape(b * s, WIDTH) for x in outs] + [x[1].reshape(b * s, WIDTH) for x in outs]
    tm = 1024
    spec = pl.BlockSpec((tm, WIDTH), lambda i: (i, 0))
    o = pl.pallas_call(
        _dil_combine_kernel, grid=(b * s // tm,), in_specs=[spec] * 6, out_specs=spec,
        out_shape=jax.ShapeDtypeStruct((b * s, WIDTH), F32),
        compiler_params=pltpu.CompilerParams(dimension_semantics=("parallel",)),
        name="dil_combine",
    )(*flat)
    return o.reshape(b, s, WIDTH)


def _dil_step_kernel(pd_ref, c0_ref, c1_ref, c2_ref, o_ref, buf_scr, *, t, tp):
    buf_scr[...] = jnp.zeros_like(buf_scr)
    buf_scr[0:t, :] = pd_ref[0]
    pd = buf_scr[...]
    n = N_HEADS * tp
    hm = _head_mask(tp, n)
    qt = _mod(_iota((n, 1), 0), tp)
    outs, lses = [], []
    for g, cache_ref in enumerate((c0_ref, c1_ref, c2_ref)):
        win, dil = DIL_PATTERNS[g]
        length = cache_ref.shape[1]
        q = pd[:, g * 3 * WIDTH:g * 3 * WIDTH + WIDTH]
        k_new = pd[:, g * 3 * WIDTH + WIDTH:g * 3 * WIDTH + 2 * WIDTH]
        v_new = pd[:, g * 3 * WIDTH + 2 * WIDTH:g * 3 * WIDTH + 3 * WIDTH]
        k_old, v_old = cache_ref[0, :, 0:WIDTH], cache_ref[0, :, WIDTH:2 * WIDTH]
        q_st = _stack_heads(q, hm)
        slope = _alibi_slope_rows(g, tp)
        d_old = length + qt - _iota((n, length), 1)
        ok_old = (_mod(d_old, dil) == 0) & (d_old <= win)
        s_old = _mm(q_st, k_old, NT) * HEAD_DIM ** -0.5 - slope * d_old.astype(F32)
        s_old = jnp.where(ok_old, s_old, NEG_BIG)
        d_new = qt - _iota((n, tp), 1)
        ok_new = (d_new >= 0) & (_mod(d_new, dil) == 0)
        s_new = _mm(q_st, k_new, NT) * HEAD_DIM ** -0.5 - slope * d_new.astype(F32)
        s_new = jnp.where(ok_new, s_new, NEG_BIG)
        m = jnp.maximum(jnp.max(s_old, axis=1, keepdims=True), jnp.max(s_new, axis=1, keepdims=True))
        e_old, e_new = jnp.exp(s_old - m), jnp.exp(s_new - m)
        l = jnp.sum(e_old, axis=1, keepdims=True) + jnp.sum(e_new, axis=1, keepdims=True)
        outs.append((_mm(e_old, v_old) + _mm(e_new, v_new)) / l)
        lses.append(m + jnp.log(l))
    m = jnp.maximum(jnp.maximum(lses[0], lses[1]), lses[2])
    es = [jnp.exp(x - m) for x in lses]
    o_st = (es[0] * outs[0] + es[1] * outs[1] + es[2] * outs[2]) / (es[0] + es[1] + es[2])
    o_ref[0] = _unstack_heads(o_st, hm, tp)[0:t]


def _dil_step(pd, caches):
    b, t, _ = pd.shape
    tp = 8
    flat = [c.reshape(b, c.shape[1], 2 * WIDTH) for c in caches]
    return pl.pallas_call(
        functools.partial(_dil_step_kernel, t=t, tp=tp),
        grid=(b,),
        in_specs=[pl.BlockSpec((1, t, DIL_COLS), lambda i: (i, 0, 0))]
                 + [pl.BlockSpec((1, c.shape[1], 2 * WIDTH), lambda i: (i, 0, 0)) for c in flat],
        out_specs=pl.BlockSpec((1, t, WIDTH), lambda i: (i, 0, 0)),
        out_shape=jax.ShapeDtypeStruct((b, t, WIDTH), F32),
        scratch_shapes=[pltpu.VMEM((tp, DIL_COLS), F32)],
        compiler_params=pltpu.CompilerParams(dimension_semantics=("parallel",), vmem_limit_bytes=VMEM_LIMIT),
        name="dil_step",
    )(pd, *flat)


def _merge_kernel(h_ref, oa_ref, ob_ref, oc_ref, od_ref, wg_ref, wb_ref, wo_ref, g_ref, b_ref, out_ref):
    h = h_ref[...]
    hb = h.astype(BF16)
    z = None
    for n, o_ref in enumerate((oa_ref, ob_ref, oc_ref, od_ref)):
        gate = jax.nn.sigmoid(_mm(hb, wg_ref[:, n * D_MODEL:(n + 1) * D_MODEL]))
        term = gate * _mm(o_ref[...], wb_ref[n])
        z = term if z is None else z + term
    y = _mm(z, wo_ref[...])
    out_ref[...] = _layer_norm(DN_ALPHA * h + y, g_ref[...], b_ref[...])


def _merge(h, branches, w_gate, w_branch, w_out, ln_g, ln_b, tm):
    m = h.shape[0]
    row = lambda w: pl.BlockSpec((tm, w), lambda i: (i, 0))
    return pl.pallas_call(
        _merge_kernel,
        grid=(m // tm,),
        in_specs=[row(D_MODEL)] + [row(WIDTH)] * 4
                 + [_resident(w_gate.shape), _resident(w_branch.shape), _resident(w_out.shape),
                    _resident((1, D_MODEL)), _resident((1, D_MODEL))],
        out_specs=row(D_MODEL),
        out_shape=jax.ShapeDtypeStruct((m, D_MODEL), F32),
        compiler_params=pltpu.CompilerParams(dimension_semantics=("parallel",), vmem_limit_bytes=VMEM_LIMIT),
        name="merge_ln",
    )(h, *branches, w_gate, w_branch, w_out, ln_g.reshape(1, -1), ln_b.reshape(1, -1))


def _ffn_kernel(x_ref, wg_ref, wu_ref, wd_ref, g_ref, b_ref, out_ref):
    x = x_ref[...]
    xb = x.astype(BF16)
    act = jax.nn.silu(_mm(xb, wg_ref[...])) * _mm(xb, wu_ref[...])
    out_ref[...] = _layer_norm(DN_ALPHA * x + _mm(act, wd_ref[...]), g_ref[...], b_ref[...])


def _ffn(x, wg, wu, wd, ln_g, ln_b, tm):
    m = x.shape[0]
    row = pl.BlockSpec((tm, D_MODEL), lambda i: (i, 0))
    return pl.pallas_call(
        _ffn_kernel,
        grid=(m // tm,),
        in_specs=[row, _resident(wg.shape), _resident(wu.shape), _resident(wd.shape),
                  _resident((1, D_MODEL)), _resident((1, D_MODEL))],
        out_specs=row,
        out_shape=jax.ShapeDtypeStruct((m, D_MODEL), F32),
        compiler_params=pltpu.CompilerParams(dimension_semantics=("parallel",), vmem_limit_bytes=VMEM_LIMIT),
        name="ffn_ln",
    )(x, wg, wu, wd, ln_g.reshape(1, -1), ln_b.reshape(1, -1))


def _moe_kernel(x_ref, r_ref, wg_ref, wu_ref, wd_ref, g_ref, b_ref, out_ref, xb_scr, gate_scr, acc_scr):
    e, f = pl.program_id(1), pl.program_id(2)
    lane = _iota(gate_scr.shape, 1)

    @pl.when((e == 0) & (f == 0))
    def _():
        x = x_ref[...]
        xb_scr[...] = x.astype(BF16)
        logits = jnp.where(lane < N_EXPERTS, _mm(x, r_ref[...], NN, 2, 2), NEG_BIG)
        m1 = jnp.max(logits, axis=1, keepdims=True)
        lane_f = lane.astype(F32)
        i1 = jnp.min(jnp.where(logits == m1, lane_f, 128.0), axis=1, keepdims=True)
        rest = jnp.where(lane_f == i1, NEG_BIG, logits)
        m2 = jnp.max(rest, axis=1, keepdims=True)
        i2 = jnp.min(jnp.where(rest == m2, lane_f, 128.0), axis=1, keepdims=True)
        e2 = jnp.exp(m2 - m1)
        gate_scr[...] = jnp.where(lane_f == i1, 1.0 / (1.0 + e2), 0.0) + jnp.where(lane_f == i2, e2 / (1.0 + e2), 0.0)
        acc_scr[...] = jnp.zeros_like(acc_scr)

    xb = xb_scr[...]
    gate = jnp.sum(jnp.where(lane == e, gate_scr[...], 0.0), axis=1, keepdims=True)
    act = jax.nn.silu(_mm(xb, wg_ref[0])) * _mm(xb, wu_ref[0])
    acc_scr[...] += gate * _mm(act, wd_ref[0])

    @pl.when((e == pl.num_programs(1) - 1) & (f == pl.num_programs(2) - 1))
    def _():
        out_ref[...] = _layer_norm(DN_ALPHA * x_ref[...] + acc_scr[...], g_ref[...], b_ref[...])


def _moe(x, router, wg, wu, wd, ln_g, ln_b, tm):
    m = x.shape[0]
    dff = wg.shape[2]
    tf = dff // 2
    r_pad = jnp.zeros((D_MODEL, 128), F32).at[:, :N_EXPERTS].set(router)
    row = pl.BlockSpec((tm, D_MODEL), lambda i, e, f: (i, 0))
    return pl.pallas_call(
        _moe_kernel,
        grid=(m // tm, N_EXPERTS, dff // tf),
        in_specs=[row, _resident((D_MODEL, 128)),
                  pl.BlockSpec((1, D_MODEL, tf), lambda i, e, f: (e, 0, f)),
                  pl.BlockSpec((1, D_MODEL, tf), lambda i, e, f: (e, 0, f)),
                  pl.BlockSpec((1, tf, D_MODEL), lambda i, e, f: (e, f, 0)),
                  _resident((1, D_MODEL)), _resident((1, D_MODEL))],
        out_specs=row,
        out_shape=jax.ShapeDtypeStruct((m, D_MODEL), F32),
        scratch_shapes=[pltpu.VMEM((tm, D_MODEL), BF16), pltpu.VMEM((tm, 128), F32), pltpu.VMEM((tm, D_MODEL), F32)],
        compiler_params=pltpu.CompilerParams(dimension_semantics=("parallel", "arbitrary", "arbitrary"),
                                             vmem_limit_bytes=VMEM_LIMIT),
        name="moe_ln",
    )(x, r_pad, wg, wu, wd, ln_g.reshape(1, -1), ln_b.reshape(1, -1))


def _token_mix(h, pos0, rw_prev, wkv0, pool_buf, ret0, kv_bufs, lw, w_mix, w_gate, w_branch, w_out, ln_g, ln_b):
    b, t, _ = h.shape
    m = b * t
    tm = 512 if m % 512 == 0 else m
    hf = h.reshape(m, D_MODEL)
    p_rw, p_pool, p_ret, p_dil = _project(hf, w_mix, tm)
    p_rw, p_pool = p_rw.reshape(b, t, -1), p_pool.reshape(b, t, -1)
    p_ret, p_dil = p_ret.reshape(b, t, -1), p_dil.reshape(b, t, -1)
    o_a, wkv_new = _rwkv_branch(p_rw, rw_prev, wkv0, lw)
    o_b = _pool_branch(p_pool, pool_buf, pos0, lw)
    pool_new = jnp.concatenate([pool_buf, p_pool], axis=1)[:, -POOL_BUF:]
    o_c, ret_new = _ret_branch(p_ret, ret0, pos0, lw)
    kv_rows = p_dil.reshape(b, t, DIL_GROUPS, 3, N_HEADS, HEAD_DIM)[:, :, :, 1:]
    if kv_bufs is None:
        o_d = _dil_prompt(p_dil)
        kv_new = [kv_rows[:, t - min(win, t):, g] for g, (win, _) in enumerate(DIL_PATTERNS)]
    else:
        o_d = _dil_step(p_dil, kv_bufs)
        kv_new = [kv_rows[:, :, g] for g in range(DIL_GROUPS)]
    branches = [x.reshape(m, WIDTH) for x in (o_a, o_b, o_c, o_d)]
    x1 = _merge(hf, branches, w_gate, w_branch, w_out, ln_g, ln_b, tm)
    return x1, (wkv_new, p_rw[:, -1], pool_new, ret_new, kv_new[0], kv_new[1], kv_new[2])


def kernel(x_prompt, x_sample, state_wkv, state_shift, state_pool, state_ret, cache_kv_w128, cache_kv_w512, cache_kv_w2048, w_in, rw_mu, rw_w0, rw_w2, rw_a0, rw_a2, rw_g2, rw_kk, rw_ka, rw_rk, rw_gn_g, rw_gn_b, pool_w, pool_scale, ret_gn_g, ret_gn_b, w_branch, w_out, ln_g, ln_b, ffn_w_gate, ffn_w_up, ffn_w_down, moe_router, moe_w_gate, moe_w_up, moe_w_down):
    hp, hs = x_prompt, x_sample
    bp, tp, _ = hp.shape
    bs, ts, _ = hs.shape
    names = ('rw_mu', 'rw_w0', 'rw_w2', 'rw_a0', 'rw_a2', 'rw_g2', 'rw_kk', 'rw_ka', 'rw_rk', 'rw_gn_g',
             'rw_gn_b', 'pool_w', 'pool_scale', 'ret_gn_g', 'ret_gn_b')
    stacked = (rw_mu, rw_w0, rw_w2, rw_a0, rw_a2, rw_g2, rw_kk, rw_ka, rw_rk, rw_gn_g, rw_gn_b, pool_w,
               pool_scale, ret_gn_g, ret_gn_b)
    new_p = [[] for _ in range(7)]
    new_s = [[] for _ in range(7)]
    zeros = lambda *shape: jnp.zeros(shape, F32)
    for l in range(DEPTH):
        lw = {k: v[l] for k, v in zip(names, stacked)}
        w_mix = w_in[l, :, :COL_GATE].astype(BF16)
        w_gate = w_in[l, :, COL_GATE:].astype(BF16)
        wb, wo = w_branch[l].astype(BF16), w_out[l].astype(BF16)
        xp, st_p = _token_mix(hp, 0, zeros(bp, RW_COLS), zeros(bp, N_HEADS, HEAD_DIM, HEAD_DIM),
                              zeros(bp, POOL_BUF, WIDTH), zeros(bp, N_HEADS, HEAD_DIM, HEAD_DIM), None,
                              lw, w_mix, w_gate, wb, wo, ln_g[l, 0], ln_b[l, 0])
        xs, st_s = _token_mix(hs, 8192, state_shift[l], state_wkv[l], state_pool[l], state_ret[l],
                              (cache_kv_w128[l], cache_kv_w512[l], cache_kv_w2048[l]),
                              lw, w_mix, w_gate, wb, wo, ln_g[l, 0], ln_b[l, 0])
        j = l // 2
        if l % 2 == 0:
            ws = [w[j].astype(BF16) for w in (ffn_w_gate, ffn_w_up, ffn_w_down)]
            xp = _ffn(xp, *ws, ln_g[l, 1], ln_b[l, 1], 256)
            xs = _ffn(xs, *ws, ln_g[l, 1], ln_b[l, 1], xs.shape[0])
        else:
            ws = [w[j].astype(BF16) for w in (moe_w_gate, moe_w_up, moe_w_down)]
            xp = _moe(xp, moe_router[j], *ws, ln_g[l, 1], ln_b[l, 1], 512)
            xs = _moe(xs, moe_router[j], *ws, ln_g[l, 1], ln_b[l, 1], xs.shape[0])
        hp, hs = xp.reshape(bp, tp, D_MODEL), xs.reshape(bs, ts, D_MODEL)
        for i in range(7):
            new_p[i].append(st_p[i])
            new_s[i].append(st_s[i])
    outs_p = [jnp.stack(x) for x in new_p]
    outs_s = [jnp.stack(x) for x in new_s]
    return (hp, hs, *outs_p, *outs_s)
```

```python
import functools
import math

import jax
import jax.numpy as jnp
from jax import lax
from jax.experimental import pallas as pl
from jax.experimental.pallas import tpu as pltpu

F32 = jnp.float32
BF16 = jnp.bfloat16

D_MODEL = 1024
DEPTH = 2
HEAD_DIM = 64
N_HEADS = 4
WIDTH = N_HEADS * HEAD_DIM
RW_COLS = 1024
RW_GN_EPS = 64e-5
POOL_WINDOWS = (2, 4, 8, 16)
POOL_BUF = 15
RET_CHUNK = 128
ROPE_BASE = 10000.0
DIL_PATTERNS = ((128, 1), (512, 4), (2048, 16))
DIL_GROUPS = 3
DIL_BLOCK = 128
DIL_SPAN = 2048
DIL_COLS = 3 * DIL_GROUPS * WIDTH
N_BRANCH = 4
COL_POOL = RW_COLS
COL_RET = COL_POOL + WIDTH
COL_DIL = COL_RET + 4 * WIDTH
COL_GATE = COL_DIL + DIL_COLS
N_EXPERTS = 8
DN_ALPHA = (2 * DEPTH) ** 0.25
LN_EPS = 1e-5
RW_CHUNK = 64
RW_P = 1
RW_BATCH = 4
NEG_BIG = -1e30

NN = (((1,), (0,)), ((), ()))
NT = (((1,), (1,)), ((), ()))
TN = (((0,), (0,)), ((), ()))

VMEM_LIMIT = 56 * 1024 * 1024


def _split(x, n):
    if x.dtype == BF16:
        return [x]
    parts, rem = [], x
    for i in range(n):
        p = rem.astype(BF16)
        parts.append(p)
        if i + 1 < n:
            rem = rem - p.astype(F32)
    return parts


def _mm(a, b, dims=NN, pa=1, pb=1):
    a_parts, b_parts = _split(a, pa), _split(b, pb)
    depth = max(len(a_parts), len(b_parts))
    acc = None
    for i, ai in enumerate(a_parts):
        for j, bj in enumerate(b_parts):
            if i + j < depth:
                t = lax.dot_general(ai, bj, dims, preferred_element_type=F32)
                acc = t if acc is None else acc + t
    return acc


def _iota(shape, axis):
    return lax.broadcasted_iota(jnp.int32, shape, axis)


def _div(x, d):
    assert d & (d - 1) == 0
    return x >> (d.bit_length() - 1)


def _mod(x, d):
    assert d & (d - 1) == 0
    return x & (d - 1)


def _per_head(head, values):
    out = jnp.full(head.shape, values[-1], F32)
    for h in range(len(values) - 2, -1, -1):
        out = jnp.where(head == h, values[h], out)
    return out


def _head_mask(rows_per_head, n_rows):
    return _div(_iota((n_rows, WIDTH), 0), rows_per_head) == _div(_iota((n_rows, WIDTH), 1), HEAD_DIM)


def _stack_heads(x, mask):
    return jnp.where(mask, jnp.concatenate([x] * N_HEADS, axis=0), 0.0)


def _unstack_heads(x_st, mask, c):
    x_st = jnp.where(mask, x_st, 0.0)
    out = x_st[0:c]
    for h in range(1, N_HEADS):
        out = out + x_st[h * c:(h + 1) * c]
    return out


def _ones_bd():
    return _same_head().astype(BF16)


def _same_head():
    return _div(_iota((WIDTH, WIDTH), 0), HEAD_DIM) == _div(_iota((WIDTH, WIDTH), 1), HEAD_DIM)


def _head_norm(x, ones_bd, g, b, eps):
    mu = _mm(x, ones_bd, pa=2) * (1.0 / HEAD_DIM)
    d = x - mu
    var = _mm(d * d, ones_bd, pa=2) * (1.0 / HEAD_DIM)
    return d * lax.rsqrt(var + eps) * g + b


def _layer_norm(x, g, b):
    mu = jnp.mean(x, axis=-1, keepdims=True)
    d = x - mu
    var = jnp.mean(d * d, axis=-1, keepdims=True)
    return d * lax.rsqrt(var + LN_EPS) * g + b


def _load_rows(p_ref, buf_ref, t_in, c):
    if t_in == c:
        return p_ref[0]
    buf_ref[...] = jnp.zeros_like(buf_ref)
    buf_ref[0:t_in, :] = p_ref[0]
    return buf_ref[...]


def _resident(shape):
    nd = len(shape)
    return pl.BlockSpec(shape, lambda *_: (0,) * nd, pipeline_mode=pl.Buffered(1))


def _proj_kernel(x_ref, w_ref, rw_ref, pool_ref, ret_ref, dil_ref):
    xb = x_ref[...].astype(BF16)
    for ref, lo, hi in ((rw_ref, 0, COL_POOL), (pool_ref, COL_POOL, COL_RET),
                        (ret_ref, COL_RET, COL_DIL), (dil_ref, COL_DIL, COL_GATE)):
        for s in range(lo, hi, 512):
            e = min(s + 512, hi)
            ref[:, s - lo:e - lo] = _mm(xb, w_ref[:, s:e])


def _project(x, w_mix, tm):
    m = x.shape[0]
    widths = (COL_POOL, WIDTH, 4 * WIDTH, DIL_COLS)
    return pl.pallas_call(
        _proj_kernel,
        grid=(m // tm,),
        in_specs=[pl.BlockSpec((tm, D_MODEL), lambda i: (i, 0)), _resident((D_MODEL, COL_GATE))],
        out_specs=[pl.BlockSpec((tm, w), lambda i: (i, 0)) for w in widths],
        out_shape=[jax.ShapeDtypeStruct((m, w), F32) for w in widths],
        compiler_params=pltpu.CompilerParams(dimension_semantics=("parallel",), vmem_limit_bytes=VMEM_LIMIT),
        name="proj",
    )(x, w_mix)


def _rwkv_kernel(p_ref, prev_ref, s0_ref, *rest, c, t_in, nb):
    st_ref, s_scr, prev_scr = rest[-4], rest[-3], rest[-2]
    ci = pl.program_id(1)

    @pl.when(ci == 0)
    def _():
        s_scr[...] = s0_ref[...]
        for bi in range(nb):
            prev_scr[bi, 0:1, :] = prev_ref[bi]

    for bi in range(nb):
        _rwkv_chunk(bi, p_ref, *rest, c=c, t_in=t_in)

    @pl.when(ci == pl.num_programs(1) - 1)
    def _():
        st_ref[...] = s_scr[...]


def _rwkv_chunk(bi, p_ref, mu_ref, w0_ref, w2_ref, a0_ref, a2_ref, g2_ref, kk_ref, ka_ref,
                rk_ref, gng_ref, gnb_ref, o_ref, st_ref, s_scr, prev_scr, buf_scr, *, c, t_in):
    if t_in == c:
        p = p_ref[bi]
    else:
        buf_scr[bi] = jnp.zeros((c, RW_COLS), F32)
        buf_scr[bi, 0:t_in, :] = p_ref[bi]
        p = buf_scr[bi]
    row = _iota((c, RW_COLS), 0)
    shifted = jnp.where(row == 0, prev_scr[bi, 0:1, :], pltpu.roll(p, 1, 0))
    prev_scr[bi, 0:1, :] = p[c - 1:c, :]
    u = p + (shifted - p) * mu_ref[...]
    r, k, v, ul = u[:, 0:WIDTH], u[:, WIDTH:2 * WIDTH], u[:, 2 * WIDTH:3 * WIDTH], u[:, 3 * WIDTH:]

    w_pre = w0_ref[...] + _mm(jnp.tanh(ul), w2_ref[...])
    logw = -math.exp(-0.5) * jax.nn.sigmoid(w_pre)
    a = jax.nn.sigmoid(a0_ref[...] + _mm(ul, a2_ref[...]))
    g = _mm(jax.nn.sigmoid(ul), g2_ref[...])

    ones_bd = _ones_bd()
    kk = k * kk_ref[...]
    kk = kk * lax.rsqrt(jnp.maximum(_mm(kk * kk, ones_bd, pa=2), 1e-24))
    k_mod = k * (1.0 + (a - 1.0) * ka_ref[...])
    a_vec, b_vec = -kk, kk * a
    if t_in < c:
        live = _iota((c, WIDTH), 0) < t_in
        logw = jnp.where(live, logw, 0.0)
        a_vec, b_vec = jnp.where(live, a_vec, 0.0), jnp.where(live, b_vec, 0.0)
        k_mod, v = jnp.where(live, k_mod, 0.0), jnp.where(live, v, 0.0)

    tri = (_iota((c, c), 0) >= _iota((c, c), 1)).astype(BF16)
    cum = _mm(tri, logw, pb=3)
    cum_end = cum[c - 1:c, :]
    e_pos, e_neg, e_end = jnp.exp(cum), jnp.exp(-cum), jnp.exp(cum_end - cum)
    a_t = a_vec * jnp.exp(cum - logw)
    r_t = r * e_pos
    b_t, k_t = b_vec * e_neg, k_mod * e_neg
    b_e, k_e = b_vec * e_end, k_mod * e_end

    n = N_HEADS * c
    hm = _head_mask(c, n)
    a_st, r_st = _stack_heads(a_t, hm), _stack_heads(r_t, hm)
    b_st, v_st = _stack_heads(b_t, hm), _stack_heads(v, hm)
    ri, rj = _iota((n, n), 0), _iota((n, n), 1)
    same = _div(ri, c) == _div(rj, c)
    strict_bd = same & (_mod(ri, c) > _mod(rj, c))
    incl_bd = same & (_mod(ri, c) >= _mod(rj, c))
    ti, tj = _mod(_iota((n, c), 0), c), _iota((n, c), 1)
    strict_st, incl_st = ti > tj, ti >= tj

    mm = functools.partial(_mm, pa=RW_P, pb=RW_P)
    a_ab = jnp.where(strict_bd, mm(a_st, b_st, NT), 0.0)
    inv = jnp.where(ri == rj, 1.0, 0.0) + a_ab
    pw = a_ab
    for _ in range(int(math.log2(c)) - 1):
        pw = mm(pw, pw)
        inv = inv + mm(inv, pw)
    a_ak = jnp.where(strict_st, mm(a_st, k_t, NT), 0.0)
    z_st = jnp.where(hm, mm(a_ak, v), 0.0)
    wu = mm(inv, jnp.concatenate([a_st, z_st], axis=1))
    w_st, u0_st = wu[:, 0:WIDTH], wu[:, WIDTH:]

    s0 = s_scr[bi]
    u_st = mm(w_st, s0, NT) + u0_st
    lhs = jnp.concatenate([u_st, v_st], axis=0)
    rhs = jnp.concatenate([_stack_heads(b_e, hm), _stack_heads(k_e, hm)], axis=0)
    s_scr[bi] = s0 * jnp.exp(cum_end) + mm(lhs, rhs, TN)

    a_rb = jnp.where(incl_bd, mm(r_st, b_st, NT), 0.0)
    a_rk = jnp.where(incl_st, mm(r_st, k_t, NT), 0.0)
    o_st = mm(r_st, s0, NT) + mm(a_rb, u_st) + mm(a_rk, v)
    o = _unstack_heads(o_st, hm, c)

    o = _head_norm(o, ones_bd, gng_ref[...], gnb_ref[...], RW_GN_EPS)
    o = o + _mm(r * k_mod * rk_ref[...], ones_bd, pa=2) * v
    o_ref[bi] = (o * g)[0:t_in]


def _block_diag_heads(s):
    b = s.shape[0]
    eye = jnp.eye(N_HEADS, dtype=s.dtype)
    return jnp.einsum('bhij,hg->bhigj', s, eye).reshape(b, WIDTH, WIDTH)


def _diag_heads(s_bd):
    b = s_bd.shape[0]
    s = s_bd.reshape(b, N_HEADS, HEAD_DIM, N_HEADS, HEAD_DIM)
    return jnp.stack([s[:, h, :, h, :] for h in range(N_HEADS)], axis=1)


def _rwkv_branch(p, p_prev, wkv0, lw):
    b, t, _ = p.shape
    c = RW_CHUNK
    t_in = c if t % c == 0 else t
    nc = t // c if t % c == 0 else 1
    vec = lambda x: x.reshape(1, -1)
    pad_rows = lambda w, lo: jnp.zeros((WIDTH, WIDTH), F32).at[lo:lo + w.shape[0]].set(w).astype(BF16)
    params = [vec(lw['rw_mu']), vec(lw['rw_w0']), pad_rows(lw['rw_w2'], 0), vec(lw['rw_a0']),
              pad_rows(lw['rw_a2'], 64), pad_rows(lw['rw_g2'], 128), vec(lw['rw_kk']), vec(lw['rw_ka']),
              vec(lw['rw_rk']), vec(lw['rw_gn_g']), vec(lw['rw_gn_b'])]
    nb = RW_BATCH
    assert b % nb == 0
    o, st = pl.pallas_call(
        functools.partial(_rwkv_kernel, c=c, t_in=t_in, nb=nb),
        grid=(b // nb, nc),
        in_specs=[pl.BlockSpec((nb, t_in, RW_COLS), lambda i, j: (i, j, 0)),
                  pl.BlockSpec((nb, 1, RW_COLS), lambda i, j: (i, 0, 0)),
                  pl.BlockSpec((nb, WIDTH, WIDTH), lambda i, j: (i, 0, 0))]
                 + [_resident(x.shape) for x in params],
        out_specs=[pl.BlockSpec((nb, t_in, WIDTH), lambda i, j: (i, j, 0)),
                   pl.BlockSpec((nb, WIDTH, WIDTH), lambda i, j: (i, 0, 0))],
        out_shape=[jax.ShapeDtypeStruct((b, t, WIDTH), F32), jax.ShapeDtypeStruct((b, WIDTH, WIDTH), F32)],
        scratch_shapes=[pltpu.VMEM((nb, WIDTH, WIDTH), F32), pltpu.VMEM((nb, 8, RW_COLS), F32),
                        pltpu.VMEM((nb, c, RW_COLS), F32)],
        compiler_params=pltpu.CompilerParams(dimension_semantics=("parallel", "arbitrary"),
                                             vmem_limit_bytes=VMEM_LIMIT),
        name="rwkv7",
    )(p, p_prev.reshape(b, 1, RW_COLS), _block_diag_heads(wkv0), *params)
    return o, _diag_heads(st)


def _pool_kernel(u_ref, buf_ref, w_ref, scale_ref, o_ref, ext_scr, *, c, t_in, pos0):
    ci = pl.program_id(1)

    @pl.when(ci == 0)
    def _():
        ext_scr[0:16, :] = buf_ref[0]

    if t_in < c:
        ext_scr[16:16 + c, :] = jnp.zeros((c, WIDTH), F32)
    ext_scr[16:16 + t_in, :] = u_ref[0]
    x = ext_scr[16:16 + c, :]
    sums, acc, off = [], x, 1
    for win in POOL_WINDOWS:
        while off < win:
            acc = acc + ext_scr[16 - off:16 - off + c, :]
            off += 1
        sums.append(acc)
    pos = pos0 + ci * c + _iota((c, WIDTH), 0)
    grp = _div(_iota((c, WIDTH), 1), HEAD_DIM)
    mean = jnp.zeros((c, WIDTH), F32)
    for gi, win in enumerate(POOL_WINDOWS):
        cnt = jnp.minimum(win, pos + 1).astype(F32)
        mean = jnp.where(grp == gi, sums[gi] / cnt, mean)
    mixed = _mm(mean - x, w_ref[...]) * scale_ref[...]
    o_ref[0] = mixed[0:t_in]
    ext_scr[0:16, :] = ext_scr[c:c + 16, :]


def _pool_branch(u, buf, pos0, lw):
    b, t, _ = u.shape
    c = 512 if t % 512 == 0 else 8
    t_in = c if t % c == 0 else t
    nc = t // c if t % c == 0 else 1
    w_bd = jnp.einsum('gcd,gh->gchd', lw['pool_w'], jnp.eye(N_HEADS, dtype=F32)).reshape(WIDTH, WIDTH).astype(BF16)
    buf16 = jnp.pad(buf, ((0, 0), (1, 0), (0, 0)))
    return pl.pallas_call(
        functools.partial(_pool_kernel, c=c, t_in=t_in, pos0=pos0),
        grid=(b, nc),
        in_specs=[pl.BlockSpec((1, t_in, WIDTH), lambda i, j: (i, j, 0)),
                  pl.BlockSpec((1, 16, WIDTH), lambda i, j: (i, 0, 0)),
                  _resident((WIDTH, WIDTH)), _resident((1, WIDTH))],
        out_specs=pl.BlockSpec((1, t_in, WIDTH), lambda i, j: (i, j, 0)),
        out_shape=jax.ShapeDtypeStruct((b, t, WIDTH), F32),
        scratch_shapes=[pltpu.VMEM((16 + c, WIDTH), F32)],
        compiler_params=pltpu.CompilerParams(dimension_semantics=("parallel", "arbitrary")),
        name="pool",
    )(u, buf16, w_bd, lw['pool_scale'].reshape(1, WIDTH))


def _rot_half(x):
    first = _mod(_iota(x.shape, 1), HEAD_DIM) < (HEAD_DIM // 2)
    return jnp.where(first, pltpu.roll(x, WIDTH - HEAD_DIM // 2, 1), pltpu.roll(x, HEAD_DIM // 2, 1))


RET_LOG_DECAY = tuple(math.log(1.0 - 2.0 ** (-5.0 - h)) for h in range(N_HEADS))
ALIBI_SLOPES = tuple(2.0 ** (-8.0 * (i + 1) / (DIL_GROUPS * N_HEADS)) for i in range(DIL_GROUPS * N_HEADS))


def _ret_kernel(p_ref, cos_ref, sin_ref, s0_ref, gng_ref, gnb_ref, o_ref, st_ref, s_scr, buf_scr,
                *, c, t_in):
    ci = pl.program_id(1)

    @pl.when(ci == 0)
    def _():
        s_scr[...] = s0_ref[0]

    p = _load_rows(p_ref, buf_scr, t_in, c)
    q, k, v, g = (p[:, i * WIDTH:(i + 1) * WIDTH] for i in range(4))
    cos, sin = cos_ref[...], sin_ref[...]
    q = q * cos + _rot_half(q) * sin
    k = (k * cos + _rot_half(k) * sin) * HEAD_DIM ** -0.5
    if t_in < c:
        live = _iota((c, WIDTH), 0) < t_in
        k, v = jnp.where(live, k, 0.0), jnp.where(live, v, 0.0)

    lg = _per_head(_div(_iota((1, WIDTH), 1), HEAD_DIM), RET_LOG_DECAY)
    idx = _iota((c, WIDTH), 0).astype(F32)
    q_dec = q * jnp.exp(lg * (idx + 1.0))
    k_dec = k * jnp.exp(lg * (t_in - 1.0 - idx))
    n = N_HEADS * c
    hm = _head_mask(c, n)
    q_st = _stack_heads(q, hm)
    rel = _mod(_iota((n, c), 0), c) - _iota((n, c), 1)
    lg_rows = _per_head(_div(_iota((n, c), 0), c), RET_LOG_DECAY)
    dmask = jnp.where(rel >= 0, jnp.exp(lg_rows * jnp.maximum(rel, 0).astype(F32)), 0.0)
    inner = _mm(q_st, k, NT) * dmask
    s0 = s_scr[...]
    o = _unstack_heads(_mm(inner, v), hm, c) + _mm(q_dec, s0)
    s1 = s0 * jnp.exp(lg * float(t_in)) + jnp.where(_same_head(), _mm(k_dec, v, TN), 0.0)
    s_scr[...] = s1

    o = jax.nn.silu(g) * _head_norm(o, _ones_bd(), gng_ref[...], gnb_ref[...], LN_EPS)
    o_ref[0] = o[0:t_in]

    @pl.when(ci == pl.num_programs(1) - 1)
    def _():
        st_ref[0] = s1


def _rope_tables(pos0, t, rows):
    half = HEAD_DIM // 2
    inv = ROPE_BASE ** (-jnp.arange(half, dtype=F32) / half)
    ang = (pos0 + jnp.arange(t, dtype=jnp.int32)).astype(F32)[:, None] * inv[None, :]
    cos = jnp.tile(jnp.cos(ang), (1, 2 * N_HEADS))
    sin = jnp.tile(jnp.concatenate([-jnp.sin(ang), jnp.sin(ang)], axis=1), (1, N_HEADS))
    pad = ((0, rows - t), (0, 0))
    return jnp.pad(cos, pad), jnp.pad(sin, pad)


def _ret_branch(p, s0, pos0, lw):
    b, t, _ = p.shape
    c = RET_CHUNK
    t_in = c if t % c == 0 else t
    nc = t // c if t % c == 0 else 1
    cos, sin = _rope_tables(pos0, t, nc * c)
    o, st = pl.pallas_call(
        functools.partial(_ret_kernel, c=c, t_in=t_in),
        grid=(b, nc),
        in_specs=[pl.BlockSpec((1, t_in, 4 * WIDTH), lambda i, j: (i, j, 0)),
                  pl.BlockSpec((c, WIDTH), lambda i, j: (j, 0)),
                  pl.BlockSpec((c, WIDTH), lambda i, j: (j, 0)),
                  pl.BlockSpec((1, WIDTH, WIDTH), lambda i, j: (i, 0, 0)),
                  _resident((1, WIDTH)), _resident((1, WIDTH))],
        out_specs=[pl.BlockSpec((1, t_in, WIDTH), lambda i, j: (i, j, 0)),
                   pl.BlockSpec((1, WIDTH, WIDTH), lambda i, j: (i, 0, 0))],
        out_shape=[jax.ShapeDtypeStruct((b, t, WIDTH), F32), jax.ShapeDtypeStruct((b, WIDTH, WIDTH), F32)],
        scratch_shapes=[pltpu.VMEM((WIDTH, WIDTH), F32), pltpu.VMEM((c, 4 * WIDTH), F32)],
        compiler_params=pltpu.CompilerParams(dimension_semantics=("parallel", "arbitrary")),
        name="retention",
    )(p, cos, sin, _block_diag_heads(s0), lw['ret_gn_g'].reshape(1, WIDTH), lw['ret_gn_b'].reshape(1, WIDTH))
    return o, _diag_heads(st)


def _alibi_slope_rows(group, rows_per_head):
    head = _div(_iota((N_HEADS * rows_per_head, 1), 0), rows_per_head)
    return _per_head(head, ALIBI_SLOPES[group * N_HEADS:(group + 1) * N_HEADS])


def _dil_prompt_kernel(*refs, group, dil, span):
    q_refs, kc_refs, kp_refs, vc_refs, vp_refs = (refs[2 * i:2 * i + 2] for i in range(5))
    o_ref, lse_ref, k_scr, v_scr, o_scr, lse_scr = refs[10:]
    blk = DIL_BLOCK
    tail = blk * dil
    si = pl.program_id(1)
    for half in range(2):
        k_scr[half, 0:tail, :] = kp_refs[half][0]
        k_scr[half, tail:tail + span, :] = kc_refs[half][0]
        v_scr[half, 0:tail, :] = vp_refs[half][0]
        v_scr[half, tail:tail + span, :] = vc_refs[half][0]
    both = lambda ref3, rows: jnp.concatenate([ref3[0, rows, :], ref3[1, rows, :]], axis=1)
    n = N_HEADS * blk
    hm = _head_mask(blk, n)
    ki = _iota((n, 2 * blk), 1)
    steps = blk + _mod(_iota((n, 2 * blk), 0), blk) - ki
    band = (steps >= 0) & (steps <= blk)
    bias = _alibi_slope_rows(group, blk) * (dil * steps).astype(F32)
    for cc in range(span // tail):
        valid = band & ((si > 0) | (ki >= blk)) if cc == 0 else band
        for r in range(dil):
            rows_q = pl.ds(cc * tail + r, blk, stride=dil) if dil > 1 else pl.ds(cc * tail, blk)
            rows_kv = pl.ds(cc * tail + r, 2 * blk, stride=dil) if dil > 1 else pl.ds(cc * tail, 2 * blk)
            q = jnp.concatenate([q_refs[0][0, rows_q, :], q_refs[1][0, rows_q, :]], axis=1)
            q_st = _stack_heads(q, hm)
            s = _mm(q_st, both(k_scr, rows_kv), NT) * HEAD_DIM ** -0.5 - bias
            s = jnp.where(valid, s, NEG_BIG)
            m = jnp.max(s, axis=1, keepdims=True)
            e = jnp.exp(s - m)
            l = jnp.sum(e, axis=1, keepdims=True)
            o_st = _mm(e, both(v_scr, rows_kv)) / l
            lse = m + jnp.log(l)
            o = _unstack_heads(o_st, hm, blk)
            lse = _unstack_heads(jnp.broadcast_to(lse, (n, WIDTH)), hm, blk)
            for half in range(2):
                o_scr[half, rows_q, :] = o[:, half * 128:(half + 1) * 128]
                lse_scr[half, rows_q, :] = lse[:, half * 128:(half + 1) * 128]
    o_ref[0] = jnp.concatenate([o_scr[0], o_scr[1]], axis=1)
    lse_ref[0] = jnp.concatenate([lse_scr[0], lse_scr[1]], axis=1)


def _dil_prompt_group(pd, group):
    b, s, _ = pd.shape
    win, dil = DIL_PATTERNS[group]
    span = DIL_SPAN
    tail = DIL_BLOCK * dil
    assert win // dil == DIL_BLOCK and span % tail == 0 and s % span == 0
    base = group * 3

    def cur(col):
        return [pl.BlockSpec((1, span, 128), lambda i, j, h=h: (i, j, 2 * (base + col) + h)) for h in range(2)]

    def prev(col):
        return [pl.BlockSpec((1, tail, 128),
                             lambda i, j, h=h: (i, jnp.maximum(j * (span // tail) - 1, 0), 2 * (base + col) + h))
                for h in range(2)]

    out_spec = pl.BlockSpec((1, span, WIDTH), lambda i, j: (i, j, 0))
    return pl.pallas_call(
        functools.partial(_dil_prompt_kernel, group=group, dil=dil, span=span),
        grid=(b, s // span),
        in_specs=cur(0) + cur(1) + prev(1) + cur(2) + prev(2),
        out_specs=[out_spec, out_spec],
        out_shape=[jax.ShapeDtypeStruct((b, s, WIDTH), F32)] * 2,
        scratch_shapes=[pltpu.VMEM((2, tail + span, 128), F32)] * 2 + [pltpu.VMEM((2, span, 128), F32)] * 2,
        compiler_params=pltpu.CompilerParams(dimension_semantics=("parallel", "arbitrary"),
                                             vmem_limit_bytes=VMEM_LIMIT),
        name=f"dil_prompt_g{group}",
    )(*([pd] * 10))


def _dil_combine_kernel(o0, o1, o2, l0, l1, l2, out_ref):
    ls = [l0[...], l1[...], l2[...]]
    m = jnp.maximum(jnp.maximum(ls[0], ls[1]), ls[2])
    es = [jnp.exp(x - m) for x in ls]
    out_ref[...] = (es[0] * o0[...] + es[1] * o1[...] + es[2] * o2[...]) / (es[0] + es[1] + es[2])


def _dil_prompt(pd):
    b, s, _ = pd.shape
    outs = [_dil_prompt_group(pd, g) for g in range(DIL_GROUPS)]
    flat = [x[0].reshape(b * s, WIDTH) for x in outs] + [x[1].reshape(b * s, WIDTH) for x in outs]
    tm = 1024
    spec = pl.BlockSpec((tm, WIDTH), lambda i: (i, 0))
    o = pl.pallas_call(
        _dil_combine_kernel, grid=(b * s // tm,), in_specs=[spec] * 6, out_specs=spec,
        out_shape=jax.ShapeDtypeStruct((b * s, WIDTH), F32),
        compiler_params=pltpu.CompilerParams(dimension_semantics=("parallel",)),
        name="dil_combine",
    )(*flat)
    return o.reshape(b, s, WIDTH)


def _dil_step_kernel(pd_ref, c0_ref, c1_ref, c2_ref, o_ref, buf_scr, *, t, tp):
    buf_scr[...] = jnp.zeros_like(buf_scr)
    buf_scr[0:t, :] = pd_ref[0]
    pd = buf_scr[...]
    n = N_HEADS * tp
    hm = _head_mask(tp, n)
    qt = _mod(_iota((n, 1), 0), tp)
    outs, lses = [], []
    for g, cache_ref in enumerate((c0_ref, c1_ref, c2_ref)):
        win, dil = DIL_PATTERNS[g]
        length = cache_ref.shape[1]
        q = pd[:, g * 3 * WIDTH:g * 3 * WIDTH + WIDTH]
        k_new = pd[:, g * 3 * WIDTH + WIDTH:g * 3 * WIDTH + 2 * WIDTH]
        v_new = pd[:, g * 3 * WIDTH + 2 * WIDTH:g * 3 * WIDTH + 3 * WIDTH]
        k_old, v_old = cache_ref[0, :, 0:WIDTH], cache_ref[0, :, WIDTH:2 * WIDTH]
        q_st = _stack_heads(q, hm)
        slope = _alibi_slope_rows(g, tp)
        d_old = length + qt - _iota((n, length), 1)
        ok_old = (_mod(d_old, dil) == 0) & (d_old <= win)
        s_old = _mm(q_st, k_old, NT) * HEAD_DIM ** -0.5 - slope * d_old.astype(F32)
        s_old = jnp.where(ok_old, s_old, NEG_BIG)
        d_new = qt - _iota((n, tp), 1)
        ok_new = (d_new >= 0) & (_mod(d_new, dil) == 0)
        s_new = _mm(q_st, k_new, NT) * HEAD_DIM ** -0.5 - slope * d_new.astype(F32)
        s_new = jnp.where(ok_new, s_new, NEG_BIG)
        m = jnp.maximum(jnp.max(s_old, axis=1, keepdims=True), jnp.max(s_new, axis=1, keepdims=True))
        e_old, e_new = jnp.exp(s_old - m), jnp.exp(s_new - m)
        l = jnp.sum(e_old, axis=1, keepdims=True) + jnp.sum(e_new, axis=1, keepdims=True)
        outs.append((_mm(e_old, v_old) + _mm(e_new, v_new)) / l)
        lses.append(m + jnp.log(l))
    m = jnp.maximum(jnp.maximum(lses[0], lses[1]), lses[2])
    es = [jnp.exp(x - m) for x in lses]
    o_st = (es[0] * outs[0] + es[1] * outs[1] + es[2] * outs[2]) / (es[0] + es[1] + es[2])
    o_ref[0] = _unstack_heads(o_st, hm, tp)[0:t]


def _dil_step(pd, caches):
    b, t, _ = pd.shape
    tp = 8
    flat = [c.reshape(b, c.shape[1], 2 * WIDTH) for c in caches]
    return pl.pallas_call(
        functools.partial(_dil_step_kernel, t=t, tp=tp),
        grid=(b,),
        in_specs=[pl.BlockSpec((1, t, DIL_COLS), lambda i: (i, 0, 0))]
                 + [pl.BlockSpec((1, c.shape[1], 2 * WIDTH), lambda i: (i, 0, 0)) for c in flat],
        out_specs=pl.BlockSpec((1, t, WIDTH), lambda i: (i, 0, 0)),
        out_shape=jax.ShapeDtypeStruct((b, t, WIDTH), F32),
        scratch_shapes=[pltpu.VMEM((tp, DIL_COLS), F32)],
        compiler_params=pltpu.CompilerParams(dimension_semantics=("parallel",), vmem_limit_bytes=VMEM_LIMIT),
        name="dil_step",
    )(pd, *flat)


def _merge_kernel(h_ref, oa_ref, ob_ref, oc_ref, od_ref, wg_ref, wb_ref, wo_ref, g_ref, b_ref, out_ref):
    h = h_ref[...]
    hb = h.astype(BF16)
    z = None
    for n, o_ref in enumerate((oa_ref, ob_ref, oc_ref, od_ref)):
        gate = jax.nn.sigmoid(_mm(hb, wg_ref[:, n * D_MODEL:(n + 1) * D_MODEL]))
        term = gate * _mm(o_ref[...], wb_ref[n])
        z = term if z is None else z + term
    y = _mm(z, wo_ref[...])
    out_ref[...] = _layer_norm(DN_ALPHA * h + y, g_ref[...], b_ref[...])


def _merge(h, branches, w_gate, w_branch, w_out, ln_g, ln_b, tm):
    m = h.shape[0]
    row = lambda w: pl.BlockSpec((tm, w), lambda i: (i, 0))
    return pl.pallas_call(
        _merge_kernel,
        grid=(m // tm,),
        in_specs=[row(D_MODEL)] + [row(WIDTH)] * 4
                 + [_resident(w_gate.shape), _resident(w_branch.shape), _resident(w_out.shape),
                    _resident((1, D_MODEL)), _resident((1, D_MODEL))],
        out_specs=row(D_MODEL),
        out_shape=jax.ShapeDtypeStruct((m, D_MODEL), F32),
        compiler_params=pltpu.CompilerParams(dimension_semantics=("parallel",), vmem_limit_bytes=VMEM_LIMIT),
        name="merge_ln",
    )(h, *branches, w_gate, w_branch, w_out, ln_g.reshape(1, -1), ln_b.reshape(1, -1))


def _ffn_kernel(x_ref, wg_ref, wu_ref, wd_ref, g_ref, b_ref, out_ref):
    x = x_ref[...]
    xb = x.astype(BF16)
    act = jax.nn.silu(_mm(xb, wg_ref[...])) * _mm(xb, wu_ref[...])
    out_ref[...] = _layer_norm(DN_ALPHA * x + _mm(act, wd_ref[...]), g_ref[...], b_ref[...])


def _ffn(x, wg, wu, wd, ln_g, ln_b, tm):
    m = x.shape[0]
    row = pl.BlockSpec((tm, D_MODEL), lambda i: (i, 0))
    return pl.pallas_call(
        _ffn_kernel,
        grid=(m // tm,),
        in_specs=[row, _resident(wg.shape), _resident(wu.shape), _resident(wd.shape),
                  _resident((1, D_MODEL)), _resident((1, D_MODEL))],
        out_specs=row,
        out_shape=jax.ShapeDtypeStruct((m, D_MODEL), F32),
        compiler_params=pltpu.CompilerParams(dimension_semantics=("parallel",), vmem_limit_bytes=VMEM_LIMIT),
        name="ffn_ln",
    )(x, wg, wu, wd, ln_g.reshape(1, -1), ln_b.reshape(1, -1))


def _moe_kernel(x_ref, r_ref, wg_ref, wu_ref, wd_ref, g_ref, b_ref, out_ref, xb_scr, gate_scr, acc_scr):
    e, f = pl.program_id(1), pl.program_id(2)
    lane = _iota(gate_scr.shape, 1)

    @pl.when((e == 0) & (f == 0))
    def _():
        x = x_ref[...]
        xb_scr[...] = x.astype(BF16)
        logits = jnp.where(lane < N_EXPERTS, _mm(x, r_ref[...], NN, 2, 2), NEG_BIG)
        m1 = jnp.max(logits, axis=1, keepdims=True)
        lane_f = lane.astype(F32)
        i1 = jnp.min(jnp.where(logits == m1, lane_f, 128.0), axis=1, keepdims=True)
        rest = jnp.where(lane_f == i1, NEG_BIG, logits)
        m2 = jnp.max(rest, axis=1, keepdims=True)
        i2 = jnp.min(jnp.where(rest == m2, lane_f, 128.0), axis=1, keepdims=True)
        e2 = jnp.exp(m2 - m1)
        gate_scr[...] = jnp.where(lane_f == i1, 1.0 / (1.0 + e2), 0.0) + jnp.where(lane_f == i2, e2 / (1.0 + e2), 0.0)
        acc_scr[...] = jnp.zeros_like(acc_scr)

    xb = xb_scr[...]
    gate = jnp.sum(jnp.where(lane == e, gate_scr[...], 0.0), axis=1, keepdims=True)
    act = jax.nn.silu(_mm(xb, wg_ref[0])) * _mm(xb, wu_ref[0])
    acc_scr[...] += gate * _mm(act, wd_ref[0])

    @pl.when((e == pl.num_programs(1) - 1) & (f == pl.num_programs(2) - 1))
    def _():
        out_ref[...] = _layer_norm(DN_ALPHA * x_ref[...] + acc_scr[...], g_ref[...], b_ref[...])


def _moe(x, router, wg, wu, wd, ln_g, ln_b, tm):
    m = x.shape[0]
    dff = wg.shape[2]
    tf = dff // 2
    r_pad = jnp.zeros((D_MODEL, 128), F32).at[:, :N_EXPERTS].set(router)
    row = pl.BlockSpec((tm, D_MODEL), lambda i, e, f: (i, 0))
    return pl.pallas_call(
        _moe_kernel,
        grid=(m // tm, N_EXPERTS, dff // tf),
        in_specs=[row, _resident((D_MODEL, 128)),
                  pl.BlockSpec((1, D_MODEL, tf), lambda i, e, f: (e, 0, f)),
                  pl.BlockSpec((1, D_MODEL, tf), lambda i, e, f: (e, 0, f)),
                  pl.BlockSpec((1, tf, D_MODEL), lambda i, e, f: (e, f, 0)),
                  _resident((1, D_MODEL)), _resident((1, D_MODEL))],
        out_specs=row,
        out_shape=jax.ShapeDtypeStruct((m, D_MODEL), F32),
        scratch_shapes=[pltpu.VMEM((tm, D_MODEL), BF16), pltpu.VMEM((tm, 128), F32), pltpu.VMEM((tm, D_MODEL), F32)],
        compiler_params=pltpu.CompilerParams(dimension_semantics=("parallel", "arbitrary", "arbitrary"),
                                             vmem_limit_bytes=VMEM_LIMIT),
        name="moe_ln",
    )(x, r_pad, wg, wu, wd, ln_g.reshape(1, -1), ln_b.reshape(1, -1))


def _token_mix(h, pos0, rw_prev, wkv0, pool_buf, ret0, kv_bufs, lw, w_mix, w_gate, w_branch, w_out, ln_g, ln_b):
    b, t, _ = h.shape
    m = b * t
    tm = 512 if m % 512 == 0 else m
    hf = h.reshape(m, D_MODEL)
    p_rw, p_pool, p_ret, p_dil = _project(hf, w_mix, tm)
    p_rw, p_pool = p_rw.reshape(b, t, -1), p_pool.reshape(b, t, -1)
    p_ret, p_dil = p_ret.reshape(b, t, -1), p_dil.reshape(b, t, -1)
    o_a, wkv_new = _rwkv_branch(p_rw, rw_prev, wkv0, lw)
    o_b = _pool_branch(p_pool, pool_buf, pos0, lw)
    pool_new = jnp.concatenate([pool_buf, p_pool], axis=1)[:, -POOL_BUF:]
    o_c, ret_new = _ret_branch(p_ret, ret0, pos0, lw)
    def kv_rows(g, keep):
        lo = (3 * g + 1) * WIDTH
        return p_dil[:, t - keep:, lo:lo + 2 * WIDTH].reshape(b, keep, 2, N_HEADS, HEAD_DIM)

    if kv_bufs is None:
        o_d = _dil_prompt(p_dil)
        kv_new = [kv_rows(g, min(win, t)) for g, (win, _) in enumerate(DIL_PATTERNS)]
    else:
        o_d = _dil_step(p_dil, kv_bufs)
        kv_new = [kv_rows(g, t) for g in range(DIL_GROUPS)]
    branches = [x.reshape(m, WIDTH) for x in (o_a, o_b, o_c, o_d)]
    x1 = _merge(hf, branches, w_gate, w_branch, w_out, ln_g, ln_b, tm)
    return x1, (wkv_new, p_rw[:, -1], pool_new, ret_new, kv_new[0], kv_new[1], kv_new[2])


def kernel(x_prompt, x_sample, state_wkv, state_shift, state_pool, state_ret, cache_kv_w128, cache_kv_w512, cache_kv_w2048, w_in, rw_mu, rw_w0, rw_w2, rw_a0, rw_a2, rw_g2, rw_kk, rw_ka, rw_rk, rw_gn_g, rw_gn_b, pool_w, pool_scale, ret_gn_g, ret_gn_b, w_branch, w_out, ln_g, ln_b, ffn_w_gate, ffn_w_up, ffn_w_down, moe_router, moe_w_gate, moe_w_up, moe_w_down):
    hp, hs = x_prompt, x_sample
    bp, tp, _ = hp.shape
    bs, ts, _ = hs.shape
    names = ('rw_mu', 'rw_w0', 'rw_w2', 'rw_a0', 'rw_a2', 'rw_g2', 'rw_kk', 'rw_ka', 'rw_rk', 'rw_gn_g',
             'rw_gn_b', 'pool_w', 'pool_scale', 'ret_gn_g', 'ret_gn_b')
    stacked = (rw_mu, rw_w0, rw_w2, rw_a0, rw_a2, rw_g2, rw_kk, rw_ka, rw_rk, rw_gn_g, rw_gn_b, pool_w,
               pool_scale, ret_gn_g, ret_gn_b)
    new_p = [[] for _ in range(7)]
    new_s = [[] for _ in range(7)]
    zeros = lambda *shape: jnp.zeros(shape, F32)
    for l in range(DEPTH):
        lw = {k: v[l] for k, v in zip(names, stacked)}
        w_mix = w_in[l, :, :COL_GATE].astype(BF16)
        w_gate = w_in[l, :, COL_GATE:].astype(BF16)
        wb, wo = w_branch[l].astype(BF16), w_out[l].astype(BF16)
        xp, st_p = _token_mix(hp, 0, zeros(bp, RW_COLS), zeros(bp, N_HEADS, HEAD_DIM, HEAD_DIM),
                              zeros(bp, POOL_BUF, WIDTH), zeros(bp, N_HEADS, HEAD_DIM, HEAD_DIM), None,
                              lw, w_mix, w_gate, wb, wo, ln_g[l, 0], ln_b[l, 0])
        xs, st_s = _token_mix(hs, 8192, state_shift[l], state_wkv[l], state_pool[l], state_ret[l],
                              (cache_kv_w128[l], cache_kv_w512[l], cache_kv_w2048[l]),
                              lw, w_mix, w_gate, wb, wo, ln_g[l, 0], ln_b[l, 0])
        j = l // 2
        if l % 2 == 0:
            ws = [w[j].astype(BF16) for w in (ffn_w_gate, ffn_w_up, ffn_w_down)]
            xp = _ffn(xp, *ws, ln_g[l, 1], ln_b[l, 1], 256)
            xs = _ffn(xs, *ws, ln_g[l, 1], ln_b[l, 1], xs.shape[0])
        else:
            ws = [w[j].astype(BF16) for w in (moe_w_gate, moe_w_up, moe_w_down)]
            xp = _moe(xp, moe_router[j], *ws, ln_g[l, 1], ln_b[l, 1], 512)
            xs = _moe(xs, moe_router[j], *ws, ln_g[l, 1], ln_b[l, 1], xs.shape[0])
        hp, hs = xp.reshape(bp, tp, D_MODEL), xs.reshape(bs, ts, D_MODEL)
        for i in range(7):
            new_p[i].append(st_p[i])
            new_s[i].append(st_s[i])
    outs_p = [jnp.stack(x) for x in new_p]
    outs_s = [jnp.stack(x) for x in new_s]
    return (hp, hs, *outs_p, *outs_s)
```

```python
import functools
import math

import jax
import jax.numpy as jnp
from jax import lax
from jax.experimental import pallas as pl
from jax.experimental.pallas import tpu as pltpu

F32 = jnp.float32
BF16 = jnp.bfloat16

D_MODEL = 1024
DEPTH = 2
HEAD_DIM = 64
N_HEADS = 4
WIDTH = N_HEADS * HEAD_DIM
RW_COLS = 1024
RW_GN_EPS = 64e-5
POOL_WINDOWS = (2, 4, 8, 16)
POOL_BUF = 15
RET_CHUNK = 128
ROPE_BASE = 10000.0
DIL_PATTERNS = ((128, 1), (512, 4), (2048, 16))
DIL_GROUPS = 3
DIL_BLOCK = 128
DIL_SPAN = 2048
DIL_COLS = 3 * DIL_GROUPS * WIDTH
N_BRANCH = 4
COL_POOL = RW_COLS
COL_RET = COL_POOL + WIDTH
COL_DIL = COL_RET + 4 * WIDTH
COL_GATE = COL_DIL + DIL_COLS
N_EXPERTS = 8
DN_ALPHA = (2 * DEPTH) ** 0.25
LN_EPS = 1e-5
RW_CHUNK = 64
MOE_TILE = 288
RW_P = 1
RW_BATCH = 4
NEG_BIG = -1e30

NN = (((1,), (0,)), ((), ()))
NT = (((1,), (1,)), ((), ()))
TN = (((0,), (0,)), ((), ()))

VMEM_LIMIT = 56 * 1024 * 1024


def _split(x, n):
    if x.dtype == BF16:
        return [x]
    parts, rem = [], x
    for i in range(n):
        p = rem.astype(BF16)
        parts.append(p)
        if i + 1 < n:
            rem = rem - p.astype(F32)
    return parts


def _mm(a, b, dims=NN, pa=1, pb=1):
    a_parts, b_parts = _split(a, pa), _split(b, pb)
    depth = max(len(a_parts), len(b_parts))
    acc = None
    for i, ai in enumerate(a_parts):
        for j, bj in enumerate(b_parts):
            if i + j < depth:
                t = lax.dot_general(ai, bj, dims, preferred_element_type=F32)
                acc = t if acc is None else acc + t
    return acc


def _iota(shape, axis):
    return lax.broadcasted_iota(jnp.int32, shape, axis)


def _div(x, d):
    assert d & (d - 1) == 0
    return x >> (d.bit_length() - 1)


def _mod(x, d):
    assert d & (d - 1) == 0
    return x & (d - 1)


def _per_head(head, values):
    out = jnp.full(head.shape, values[-1], F32)
    for h in range(len(values) - 2, -1, -1):
        out = jnp.where(head == h, values[h], out)
    return out


def _head_mask(rows_per_head, n_rows):
    return _div(_iota((n_rows, WIDTH), 0), rows_per_head) == _div(_iota((n_rows, WIDTH), 1), HEAD_DIM)


def _stack_heads(x, mask):
    return jnp.where(mask, jnp.concatenate([x] * N_HEADS, axis=0), 0.0)


def _unstack_heads(x_st, mask, c):
    x_st = jnp.where(mask, x_st, 0.0)
    out = x_st[0:c]
    for h in range(1, N_HEADS):
        out = out + x_st[h * c:(h + 1) * c]
    return out


def _ones_bd():
    return _same_head().astype(BF16)


def _same_head():
    return _div(_iota((WIDTH, WIDTH), 0), HEAD_DIM) == _div(_iota((WIDTH, WIDTH), 1), HEAD_DIM)


def _head_norm(x, ones_bd, g, b, eps):
    mu = _mm(x, ones_bd, pa=2) * (1.0 / HEAD_DIM)
    d = x - mu
    var = _mm(d * d, ones_bd, pa=2) * (1.0 / HEAD_DIM)
    return d * lax.rsqrt(var + eps) * g + b


def _layer_norm(x, g, b):
    mu = jnp.mean(x, axis=-1, keepdims=True)
    d = x - mu
    var = jnp.mean(d * d, axis=-1, keepdims=True)
    return d * lax.rsqrt(var + LN_EPS) * g + b


def _load_rows(p_ref, buf_ref, t_in, c):
    if t_in == c:
        return p_ref[0]
    buf_ref[...] = jnp.zeros_like(buf_ref)
    buf_ref[0:t_in, :] = p_ref[0]
    return buf_ref[...]


def _resident(shape):
    nd = len(shape)
    return pl.BlockSpec(shape, lambda *_: (0,) * nd, pipeline_mode=pl.Buffered(1))


def _proj_kernel(x_ref, w_ref, rw_ref, pool_ref, ret_ref, dil_ref):
    xb = x_ref[...].astype(BF16)
    for ref, lo, hi in ((rw_ref, 0, COL_POOL), (pool_ref, COL_POOL, COL_RET),
                        (ret_ref, COL_RET, COL_DIL), (dil_ref, COL_DIL, COL_GATE)):
        for s in range(lo, hi, 512):
            e = min(s + 512, hi)
            ref[:, s - lo:e - lo] = _mm(xb, w_ref[:, s:e])


def _project(x, w_mix, tm):
    m = x.shape[0]
    widths = (COL_POOL, WIDTH, 4 * WIDTH, DIL_COLS)
    return pl.pallas_call(
        _proj_kernel,
        grid=(m // tm,),
        in_specs=[pl.BlockSpec((tm, D_MODEL), lambda i: (i, 0)), _resident((D_MODEL, COL_GATE))],
        out_specs=[pl.BlockSpec((tm, w), lambda i: (i, 0)) for w in widths],
        out_shape=[jax.ShapeDtypeStruct((m, w), F32) for w in widths],
        compiler_params=pltpu.CompilerParams(dimension_semantics=("parallel",), vmem_limit_bytes=VMEM_LIMIT),
        name="proj",
    )(x, w_mix)


BNT = (((2,), (2,)), ((0,), (0,)))
BNN = (((2,), (1,)), ((0,), (0,)))
BTN = (((1,), (1,)), ((0,), (0,)))


def _rwkv_kernel(p_ref, prev_ref, s0_ref, mu_ref, w0_ref, w2_ref, a0_ref, a2_ref, g2_ref, kk_ref, ka_ref,
                 rk_ref, gng_ref, gnb_ref, o_ref, st_ref, s_scr, prev_scr, buf_scr, *, c, t_in, nb):
    ci = pl.program_id(1)

    @pl.when(ci == 0)
    def _():
        s_scr[...] = s0_ref[...]
        for b in range(nb):
            prev_scr[b, 0:1, :] = prev_ref[b]

    first = _iota((c, RW_COLS), 0) == 0
    ps, shs = [], []
    for b in range(nb):
        if t_in == c:
            pb_ = p_ref[b]
        else:
            buf_scr[b] = jnp.zeros((c, RW_COLS), F32)
            buf_scr[b, 0:t_in, :] = p_ref[b]
            pb_ = buf_scr[b]
        shs.append(jnp.where(first, prev_scr[b, 0:1, :], pltpu.roll(pb_, 1, 0)))
        prev_scr[b, 0:1, :] = pb_[c - 1:c, :]
        ps.append(pb_)
    p, shifted = jnp.concatenate(ps, axis=0), jnp.concatenate(shs, axis=0)
    m = nb * c
    u = p + (shifted - p) * mu_ref[...]
    r, k, v, ul = u[:, 0:WIDTH], u[:, WIDTH:2 * WIDTH], u[:, 2 * WIDTH:3 * WIDTH], u[:, 3 * WIDTH:]

    w_pre = w0_ref[...] + _mm(jnp.tanh(ul), w2_ref[...])
    logw = -math.exp(-0.5) * jax.nn.sigmoid(w_pre)
    a = jax.nn.sigmoid(a0_ref[...] + _mm(ul, a2_ref[...]))
    g = _mm(jax.nn.sigmoid(ul), g2_ref[...])

    ones_bd = _ones_bd()
    kk = k * kk_ref[...]
    kk = kk * lax.rsqrt(jnp.maximum(_mm(kk * kk, ones_bd, pa=2), 1e-24))
    k_mod = k * (1.0 + (a - 1.0) * ka_ref[...])
    a_vec, b_vec = -kk, kk * a
    if t_in < c:
        live = _mod(_iota((m, WIDTH), 0), c) < t_in
        logw = jnp.where(live, logw, 0.0)
        a_vec, b_vec = jnp.where(live, a_vec, 0.0), jnp.where(live, b_vec, 0.0)
        k_mod, v = jnp.where(live, k_mod, 0.0), jnp.where(live, v, 0.0)

    qi, qj = _iota((m, m), 0), _iota((m, m), 1)
    tri = ((_div(qi, c) == _div(qj, c)) & (qi >= qj)).astype(BF16)
    cum2 = _mm(tri, logw, pb=3)
    seq = lambda x: x.reshape(nb, c, WIDTH)
    cum, lw3 = seq(cum2), seq(logw)
    cum_end = cum[:, c - 1:c, :]
    e_neg, e_end = jnp.exp(-cum), jnp.exp(cum_end - cum)
    a_t = seq(a_vec) * jnp.exp(cum - lw3)
    r_t = seq(r) * jnp.exp(cum)
    b_t, k_t = seq(b_vec) * e_neg, seq(k_mod) * e_neg
    b_e, k_e = seq(b_vec) * e_end, seq(k_mod) * e_end
    v3 = seq(v)

    n = N_HEADS * c
    hm = _head_mask(c, n)
    stack = lambda x: jnp.where(hm, jnp.concatenate([x] * N_HEADS, axis=1), 0.0)
    a_st, r_st, b_st, v_st = stack(a_t), stack(r_t), stack(b_t), stack(v3)
    ri, rj = _iota((n, n), 0), _iota((n, n), 1)
    same = _div(ri, c) == _div(rj, c)
    strict_bd = same & (_mod(ri, c) > _mod(rj, c))
    incl_bd = same & (_mod(ri, c) >= _mod(rj, c))
    ti, tj = _mod(_iota((n, c), 0), c), _iota((n, c), 1)
    strict_st, incl_st = ti > tj, ti >= tj

    mm = functools.partial(_mm, pa=RW_P, pb=RW_P)
    a_ab = jnp.where(strict_bd, mm(a_st, b_st, BNT), 0.0)
    inv = jnp.where(ri == rj, 1.0, 0.0) + a_ab
    pw = a_ab
    for _ in range(int(math.log2(c)) - 1):
        pw = mm(pw, pw, BNN)
        inv = inv + mm(inv, pw, BNN)
    a_ak = jnp.where(strict_st, mm(a_st, k_t, BNT), 0.0)
    z_st = jnp.where(hm, mm(a_ak, v3, BNN), 0.0)
    wu = mm(inv, jnp.concatenate([a_st, z_st], axis=2), BNN)
    w_st, u0_st = wu[:, :, 0:WIDTH], wu[:, :, WIDTH:]

    s0 = s_scr[...]
    u_st = mm(w_st, s0, BNT) + u0_st
    lhs = jnp.concatenate([u_st, v_st], axis=1)
    rhs = jnp.concatenate([stack(b_e), stack(k_e)], axis=1)
    s_scr[...] = s0 * jnp.exp(cum_end) + mm(lhs, rhs, BTN)

    a_rb = jnp.where(incl_bd, mm(r_st, b_st, BNT), 0.0)
    a_rk = jnp.where(incl_st, mm(r_st, k_t, BNT), 0.0)
    o_st = jnp.where(hm, mm(r_st, s0, BNT) + mm(a_rb, u_st, BNN) + mm(a_rk, v3, BNN), 0.0)
    o3 = o_st[:, 0:c]
    for h in range(1, N_HEADS):
        o3 = o3 + o_st[:, h * c:(h + 1) * c]
    o = o3.reshape(m, WIDTH)

    o = _head_norm(o, ones_bd, gng_ref[...], gnb_ref[...], RW_GN_EPS)
    o = o + _mm(r * k_mod * rk_ref[...], ones_bd, pa=2) * v
    o_ref[...] = (o * g).reshape(nb, c, WIDTH)[:, 0:t_in]

    @pl.when(ci == pl.num_programs(1) - 1)
    def _():
        st_ref[...] = s_scr[...]


def _block_diag_heads(s):
    b = s.shape[0]
    eye = jnp.eye(N_HEADS, dtype=s.dtype)
    return jnp.einsum('bhij,hg->bhigj', s, eye).reshape(b, WIDTH, WIDTH)


def _diag_heads(s_bd):
    b = s_bd.shape[0]
    s = s_bd.reshape(b, N_HEADS, HEAD_DIM, N_HEADS, HEAD_DIM)
    return jnp.stack([s[:, h, :, h, :] for h in range(N_HEADS)], axis=1)


def _rwkv_branch(p, p_prev, wkv0, lw):
    b, t, _ = p.shape
    c = RW_CHUNK
    t_in = c if t % c == 0 else t
    nc = t // c if t % c == 0 else 1
    vec = lambda x: x.reshape(1, -1)
    pad_rows = lambda w, lo: jnp.zeros((WIDTH, WIDTH), F32).at[lo:lo + w.shape[0]].set(w).astype(BF16)
    params = [vec(lw['rw_mu']), vec(lw['rw_w0']), pad_rows(lw['rw_w2'], 0), vec(lw['rw_a0']),
              pad_rows(lw['rw_a2'], 64), pad_rows(lw['rw_g2'], 128), vec(lw['rw_kk']), vec(lw['rw_ka']),
              vec(lw['rw_rk']), vec(lw['rw_gn_g']), vec(lw['rw_gn_b'])]
    nb = RW_BATCH
    assert b % nb == 0
    o, st = pl.pallas_call(
        functools.partial(_rwkv_kernel, c=c, t_in=t_in, nb=nb),
        grid=(b // nb, nc),
        in_specs=[pl.BlockSpec((nb, t_in, RW_COLS), lambda i, j: (i, j, 0)),
                  pl.BlockSpec((nb, 1, RW_COLS), lambda i, j: (i, 0, 0)),
                  pl.BlockSpec((nb, WIDTH, WIDTH), lambda i, j: (i, 0, 0))]
                 + [_resident(x.shape) for x in params],
        out_specs=[pl.BlockSpec((nb, t_in, WIDTH), lambda i, j: (i, j, 0)),
                   pl.BlockSpec((nb, WIDTH, WIDTH), lambda i, j: (i, 0, 0))],
        out_shape=[jax.ShapeDtypeStruct((b, t, WIDTH), F32), jax.ShapeDtypeStruct((b, WIDTH, WIDTH), F32)],
        scratch_shapes=[pltpu.VMEM((nb, WIDTH, WIDTH), F32), pltpu.VMEM((nb, 8, RW_COLS), F32),
                        pltpu.VMEM((nb, c, RW_COLS), F32)],
        compiler_params=pltpu.CompilerParams(dimension_semantics=("parallel", "arbitrary"),
                                             vmem_limit_bytes=VMEM_LIMIT),
        name="rwkv7",
    )(p, p_prev.reshape(b, 1, RW_COLS), _block_diag_heads(wkv0), *params)
    return o, _diag_heads(st)


def _pool_kernel(u_ref, buf_ref, w_ref, scale_ref, o_ref, ext_scr, *, c, t_in, pos0):
    ci = pl.program_id(1)

    @pl.when(ci == 0)
    def _():
        ext_scr[0:16, :] = buf_ref[0]

    if t_in < c:
        ext_scr[16:16 + c, :] = jnp.zeros((c, WIDTH), F32)
    ext_scr[16:16 + t_in, :] = u_ref[0]
    x = ext_scr[16:16 + c, :]
    sums, acc, off = [], x, 1
    for win in POOL_WINDOWS:
        while off < win:
            acc = acc + ext_scr[16 - off:16 - off + c, :]
            off += 1
        sums.append(acc)
    pos = pos0 + ci * c + _iota((c, WIDTH), 0)
    grp = _div(_iota((c, WIDTH), 1), HEAD_DIM)
    mean = jnp.zeros((c, WIDTH), F32)
    for gi, win in enumerate(POOL_WINDOWS):
        cnt = jnp.minimum(win, pos + 1).astype(F32)
        mean = jnp.where(grp == gi, sums[gi] / cnt, mean)
    mixed = _mm(mean - x, w_ref[...]) * scale_ref[...]
    o_ref[0] = mixed[0:t_in]
    ext_scr[0:16, :] = ext_scr[c:c + 16, :]


def _pool_branch(u, buf, pos0, lw):
    b, t, _ = u.shape
    c = 512 if t % 512 == 0 else 8
    t_in = c if t % c == 0 else t
    nc = t // c if t % c == 0 else 1
    w_bd = jnp.einsum('gcd,gh->gchd', lw['pool_w'], jnp.eye(N_HEADS, dtype=F32)).reshape(WIDTH, WIDTH).astype(BF16)
    buf16 = jnp.pad(buf, ((0, 0), (1, 0), (0, 0)))
    return pl.pallas_call(
        functools.partial(_pool_kernel, c=c, t_in=t_in, pos0=pos0),
        grid=(b, nc),
        in_specs=[pl.BlockSpec((1, t_in, WIDTH), lambda i, j: (i, j, 0)),
                  pl.BlockSpec((1, 16, WIDTH), lambda i, j: (i, 0, 0)),
                  _resident((WIDTH, WIDTH)), _resident((1, WIDTH))],
        out_specs=pl.BlockSpec((1, t_in, WIDTH), lambda i, j: (i, j, 0)),
        out_shape=jax.ShapeDtypeStruct((b, t, WIDTH), F32),
        scratch_shapes=[pltpu.VMEM((16 + c, WIDTH), F32)],
        compiler_params=pltpu.CompilerParams(dimension_semantics=("parallel", "arbitrary")),
        name="pool",
    )(u, buf16, w_bd, lw['pool_scale'].reshape(1, WIDTH))


def _rot_half(x):
    first = _mod(_iota(x.shape, 1), HEAD_DIM) < (HEAD_DIM // 2)
    return jnp.where(first, pltpu.roll(x, WIDTH - HEAD_DIM // 2, 1), pltpu.roll(x, HEAD_DIM // 2, 1))


RET_LOG_DECAY = tuple(math.log(1.0 - 2.0 ** (-5.0 - h)) for h in range(N_HEADS))
ALIBI_SLOPES = tuple(2.0 ** (-8.0 * (i + 1) / (DIL_GROUPS * N_HEADS)) for i in range(DIL_GROUPS * N_HEADS))


def _ret_kernel(p_ref, cos_ref, sin_ref, s0_ref, gng_ref, gnb_ref, o_ref, st_ref, s_scr, buf_scr,
                *, c, t_in):
    ci = pl.program_id(1)

    @pl.when(ci == 0)
    def _():
        s_scr[...] = s0_ref[0]

    p = _load_rows(p_ref, buf_scr, t_in, c)
    q, k, v, g = (p[:, i * WIDTH:(i + 1) * WIDTH] for i in range(4))
    cos, sin = cos_ref[...], sin_ref[...]
    q = q * cos + _rot_half(q) * sin
    k = (k * cos + _rot_half(k) * sin) * HEAD_DIM ** -0.5
    if t_in < c:
        live = _iota((c, WIDTH), 0) < t_in
        k, v = jnp.where(live, k, 0.0), jnp.where(live, v, 0.0)

    lg = _per_head(_div(_iota((1, WIDTH), 1), HEAD_DIM), RET_LOG_DECAY)
    idx = _iota((c, WIDTH), 0).astype(F32)
    q_dec = q * jnp.exp(lg * (idx + 1.0))
    k_dec = k * jnp.exp(lg * (t_in - 1.0 - idx))
    n = N_HEADS * c
    hm = _head_mask(c, n)
    q_st = _stack_heads(q, hm)
    rel = _mod(_iota((n, c), 0), c) - _iota((n, c), 1)
    lg_rows = _per_head(_div(_iota((n, c), 0), c), RET_LOG_DECAY)
    dmask = jnp.where(rel >= 0, jnp.exp(lg_rows * jnp.maximum(rel, 0).astype(F32)), 0.0)
    inner = _mm(q_st, k, NT) * dmask
    s0 = s_scr[...]
    o = _unstack_heads(_mm(inner, v), hm, c) + _mm(q_dec, s0)
    s1 = s0 * jnp.exp(lg * float(t_in)) + jnp.where(_same_head(), _mm(k_dec, v, TN), 0.0)
    s_scr[...] = s1

    o = jax.nn.silu(g) * _head_norm(o, _ones_bd(), gng_ref[...], gnb_ref[...], LN_EPS)
    o_ref[0] = o[0:t_in]

    @pl.when(ci == pl.num_programs(1) - 1)
    def _():
        st_ref[0] = s1


def _rope_tables(pos0, t, rows):
    half = HEAD_DIM // 2
    inv = ROPE_BASE ** (-jnp.arange(half, dtype=F32) / half)
    ang = (pos0 + jnp.arange(t, dtype=jnp.int32)).astype(F32)[:, None] * inv[None, :]
    cos = jnp.tile(jnp.cos(ang), (1, 2 * N_HEADS))
    sin = jnp.tile(jnp.concatenate([-jnp.sin(ang), jnp.sin(ang)], axis=1), (1, N_HEADS))
    pad = ((0, rows - t), (0, 0))
    return jnp.pad(cos, pad), jnp.pad(sin, pad)


def _ret_branch(p, s0, pos0, lw):
    b, t, _ = p.shape
    c = RET_CHUNK
    t_in = c if t % c == 0 else t
    nc = t // c if t % c == 0 else 1
    cos, sin = _rope_tables(pos0, t, nc * c)
    o, st = pl.pallas_call(
        functools.partial(_ret_kernel, c=c, t_in=t_in),
        grid=(b, nc),
        in_specs=[pl.BlockSpec((1, t_in, 4 * WIDTH), lambda i, j: (i, j, 0)),
                  pl.BlockSpec((c, WIDTH), lambda i, j: (j, 0)),
                  pl.BlockSpec((c, WIDTH), lambda i, j: (j, 0)),
                  pl.BlockSpec((1, WIDTH, WIDTH), lambda i, j: (i, 0, 0)),
                  _resident((1, WIDTH)), _resident((1, WIDTH))],
        out_specs=[pl.BlockSpec((1, t_in, WIDTH), lambda i, j: (i, j, 0)),
                   pl.BlockSpec((1, WIDTH, WIDTH), lambda i, j: (i, 0, 0))],
        out_shape=[jax.ShapeDtypeStruct((b, t, WIDTH), F32), jax.ShapeDtypeStruct((b, WIDTH, WIDTH), F32)],
        scratch_shapes=[pltpu.VMEM((WIDTH, WIDTH), F32), pltpu.VMEM((c, 4 * WIDTH), F32)],
        compiler_params=pltpu.CompilerParams(dimension_semantics=("parallel", "arbitrary")),
        name="retention",
    )(p, cos, sin, _block_diag_heads(s0), lw['ret_gn_g'].reshape(1, WIDTH), lw['ret_gn_b'].reshape(1, WIDTH))
    return o, _diag_heads(st)


def _alibi_slope_rows(group, rows_per_head):
    head = _div(_iota((N_HEADS * rows_per_head, 1), 0), rows_per_head)
    return _per_head(head, ALIBI_SLOPES[group * N_HEADS:(group + 1) * N_HEADS])


def _dil_prompt_kernel(*refs, group, dil, span):
    q_refs, kc_refs, kp_refs, vc_refs, vp_refs = (refs[2 * i:2 * i + 2] for i in range(5))
    o_ref, lse_ref, k_scr, v_scr, o_scr, lse_scr = refs[10:]
    blk = DIL_BLOCK
    tail = blk * dil
    si = pl.program_id(1)
    for half in range(2):
        k_scr[half, 0:tail, :] = kp_refs[half][0]
        k_scr[half, tail:tail + span, :] = kc_refs[half][0]
        v_scr[half, 0:tail, :] = vp_refs[half][0]
        v_scr[half, tail:tail + span, :] = vc_refs[half][0]
    both = lambda ref3, rows: jnp.concatenate([ref3[0, rows, :], ref3[1, rows, :]], axis=1)
    n = N_HEADS * blk
    hm = _head_mask(blk, n)
    ki = _iota((n, 2 * blk), 1)
    steps = blk + _mod(_iota((n, 2 * blk), 0), blk) - ki
    band = (steps >= 0) & (steps <= blk)
    bias = _alibi_slope_rows(group, blk) * (dil * steps).astype(F32)
    for cc in range(span // tail):
        valid = band & ((si > 0) | (ki >= blk)) if cc == 0 else band
        for r in range(dil):
            rows_q = pl.ds(cc * tail + r, blk, stride=dil) if dil > 1 else pl.ds(cc * tail, blk)
            rows_kv = pl.ds(cc * tail + r, 2 * blk, stride=dil) if dil > 1 else pl.ds(cc * tail, 2 * blk)
            q = jnp.concatenate([q_refs[0][0, rows_q, :], q_refs[1][0, rows_q, :]], axis=1)
            q_st = _stack_heads(q, hm)
            s = _mm(q_st, both(k_scr, rows_kv), NT) * HEAD_DIM ** -0.5 - bias
            s = jnp.where(valid, s, NEG_BIG)
            m = jnp.max(s, axis=1, keepdims=True)
            e = jnp.exp(s - m)
            l = jnp.sum(e, axis=1, keepdims=True)
            o_st = _mm(e, both(v_scr, rows_kv)) / l
            lse = m + jnp.log(l)
            o = _unstack_heads(o_st, hm, blk)
            lse = _unstack_heads(jnp.broadcast_to(lse, (n, WIDTH)), hm, blk)
            for half in range(2):
                o_scr[half, rows_q, :] = o[:, half * 128:(half + 1) * 128]
                lse_scr[half, rows_q, :] = lse[:, half * 128:(half + 1) * 128]
    o_ref[0] = jnp.concatenate([o_scr[0], o_scr[1]], axis=1)
    lse_ref[0] = jnp.concatenate([lse_scr[0], lse_scr[1]], axis=1)


def _dil_prompt_group(pd, group):
    b, s, _ = pd.shape
    win, dil = DIL_PATTERNS[group]
    span = DIL_SPAN
    tail = DIL_BLOCK * dil
    assert win // dil == DIL_BLOCK and span % tail == 0 and s % span == 0
    base = group * 3

    def cur(col):
        return [pl.BlockSpec((1, span, 128), lambda i, j, h=h: (i, j, 2 * (base + col) + h)) for h in range(2)]

    def prev(col):
        return [pl.BlockSpec((1, tail, 128),
                             lambda i, j, h=h: (i, jnp.maximum(j * (span // tail) - 1, 0), 2 * (base + col) + h))
                for h in range(2)]

    out_spec = pl.BlockSpec((1, span, WIDTH), lambda i, j: (i, j, 0))
    return pl.pallas_call(
        functools.partial(_dil_prompt_kernel, group=group, dil=dil, span=span),
        grid=(b, s // span),
        in_specs=cur(0) + cur(1) + prev(1) + cur(2) + prev(2),
        out_specs=[out_spec, out_spec],
        out_shape=[jax.ShapeDtypeStruct((b, s, WIDTH), F32)] * 2,
        scratch_shapes=[pltpu.VMEM((2, tail + span, 128), F32)] * 2 + [pltpu.VMEM((2, span, 128), F32)] * 2,
        compiler_params=pltpu.CompilerParams(dimension_semantics=("parallel", "arbitrary"),
                                             vmem_limit_bytes=VMEM_LIMIT),
        name=f"dil_prompt_g{group}",
    )(*([pd] * 10))


def _dil_combine_kernel(o0, o1, o2, l0, l1, l2, out_ref):
    ls = [l0[...], l1[...], l2[...]]
    m = jnp.maximum(jnp.maximum(ls[0], ls[1]), ls[2])
    es = [jnp.exp(x - m) for x in ls]
    out_ref[...] = (es[0] * o0[...] + es[1] * o1[...] + es[2] * o2[...]) / (es[0] + es[1] + es[2])


def _dil_prompt(pd):
    b, s, _ = pd.shape
    outs = [_dil_prompt_group(pd, g) for g in range(DIL_GROUPS)]
    flat = [x[0].reshape(b * s, WIDTH) for x in outs] + [x[1].reshape(b * s, WIDTH) for x in outs]
    tm = 1024
    spec = pl.BlockSpec((tm, WIDTH), lambda i: (i, 0))
    o = pl.pallas_call(
        _dil_combine_kernel, grid=(b * s // tm,), in_specs=[spec] * 6, out_specs=spec,
        out_shape=jax.ShapeDtypeStruct((b * s, WIDTH), F32),
        compiler_params=pltpu.CompilerParams(dimension_semantics=("parallel",)),
        name="dil_combine",
    )(*flat)
    return o.reshape(b, s, WIDTH)


def _dil_step_kernel(pd_ref, c0_ref, c1_ref, c2_ref, o_ref, buf_scr, *, t, tp):
    buf_scr[...] = jnp.zeros_like(buf_scr)
    buf_scr[0:t, :] = pd_ref[0]
    pd = buf_scr[...]
    n = N_HEADS * tp
    hm = _head_mask(tp, n)
    qt = _mod(_iota((n, 1), 0), tp)
    outs, lses = [], []
    for g, cache_ref in enumerate((c0_ref, c1_ref, c2_ref)):
        win, dil = DIL_PATTERNS[g]
        length = cache_ref.shape[1]
        q = pd[:, g * 3 * WIDTH:g * 3 * WIDTH + WIDTH]
        k_new = pd[:, g * 3 * WIDTH + WIDTH:g * 3 * WIDTH + 2 * WIDTH]
        v_new = pd[:, g * 3 * WIDTH + 2 * WIDTH:g * 3 * WIDTH + 3 * WIDTH]
        k_old, v_old = cache_ref[0, :, 0:WIDTH], cache_ref[0, :, WIDTH:2 * WIDTH]
        q_st = _stack_heads(q, hm)
        slope = _alibi_slope_rows(g, tp)
        d_old = length + qt - _iota((n, length), 1)
        ok_old = (_mod(d_old, dil) == 0) & (d_old <= win)
        s_old = _mm(q_st, k_old, NT) * HEAD_DIM ** -0.5 - slope * d_old.astype(F32)
        s_old = jnp.where(ok_old, s_old, NEG_BIG)
        d_new = qt - _iota((n, tp), 1)
        ok_new = (d_new >= 0) & (_mod(d_new, dil) == 0)
        s_new = _mm(q_st, k_new, NT) * HEAD_DIM ** -0.5 - slope * d_new.astype(F32)
        s_new = jnp.where(ok_new, s_new, NEG_BIG)
        m = jnp.maximum(jnp.max(s_old, axis=1, keepdims=True), jnp.max(s_new, axis=1, keepdims=True))
        e_old, e_new = jnp.exp(s_old - m), jnp.exp(s_new - m)
        l = jnp.sum(e_old, axis=1, keepdims=True) + jnp.sum(e_new, axis=1, keepdims=True)
        outs.append((_mm(e_old, v_old) + _mm(e_new, v_new)) / l)
        lses.append(m + jnp.log(l))
    m = jnp.maximum(jnp.maximum(lses[0], lses[1]), lses[2])
    es = [jnp.exp(x - m) for x in lses]
    o_st = (es[0] * outs[0] + es[1] * outs[1] + es[2] * outs[2]) / (es[0] + es[1] + es[2])
    o_ref[0] = _unstack_heads(o_st, hm, tp)[0:t]


def _dil_step(pd, caches):
    b, t, _ = pd.shape
    tp = 8
    flat = [c.reshape(b, c.shape[1], 2 * WIDTH) for c in caches]
    return pl.pallas_call(
        functools.partial(_dil_step_kernel, t=t, tp=tp),
        grid=(b,),
        in_specs=[pl.BlockSpec((1, t, DIL_COLS), lambda i: (i, 0, 0))]
                 + [pl.BlockSpec((1, c.shape[1], 2 * WIDTH), lambda i: (i, 0, 0)) for c in flat],
        out_specs=pl.BlockSpec((1, t, WIDTH), lambda i: (i, 0, 0)),
        out_shape=jax.ShapeDtypeStruct((b, t, WIDTH), F32),
        scratch_shapes=[pltpu.VMEM((tp, DIL_COLS), F32)],
        compiler_params=pltpu.CompilerParams(dimension_semantics=("parallel",), vmem_limit_bytes=VMEM_LIMIT),
        name="dil_step",
    )(pd, *flat)


def _merge_kernel(h_ref, oa_ref, ob_ref, oc_ref, od_ref, wg_ref, wb_ref, wo_ref, g_ref, b_ref, out_ref):
    h = h_ref[...]
    hb = h.astype(BF16)
    z = None
    for n, o_ref in enumerate((oa_ref, ob_ref, oc_ref, od_ref)):
        gate = jax.nn.sigmoid(_mm(hb, wg_ref[:, n * D_MODEL:(n + 1) * D_MODEL]))
        term = gate * _mm(o_ref[...], wb_ref[n])
        z = term if z is None else z + term
    y = _mm(z, wo_ref[...])
    out_ref[...] = _layer_norm(DN_ALPHA * h + y, g_ref[...], b_ref[...])


def _merge(h, branches, w_gate, w_branch, w_out, ln_g, ln_b, tm):
    m = h.shape[0]
    row = lambda w: pl.BlockSpec((tm, w), lambda i: (i, 0))
    return pl.pallas_call(
        _merge_kernel,
        grid=(m // tm,),
        in_specs=[row(D_MODEL)] + [row(WIDTH)] * 4
                 + [_resident(w_gate.shape), _resident(w_branch.shape), _resident(w_out.shape),
                    _resident((1, D_MODEL)), _resident((1, D_MODEL))],
        out_specs=row(D_MODEL),
        out_shape=jax.ShapeDtypeStruct((m, D_MODEL), F32),
        compiler_params=pltpu.CompilerParams(dimension_semantics=("parallel",), vmem_limit_bytes=VMEM_LIMIT),
        name="merge_ln",
    )(h, *branches, w_gate, w_branch, w_out, ln_g.reshape(1, -1), ln_b.reshape(1, -1))


def _ffn_kernel(x_ref, wg_ref, wu_ref, wd_ref, g_ref, b_ref, out_ref):
    x = x_ref[...]
    xb = x.astype(BF16)
    act = jax.nn.silu(_mm(xb, wg_ref[...])) * _mm(xb, wu_ref[...])
    out_ref[...] = _layer_norm(DN_ALPHA * x + _mm(act, wd_ref[...]), g_ref[...], b_ref[...])


def _ffn(x, wg, wu, wd, ln_g, ln_b, tm):
    m = x.shape[0]
    row = pl.BlockSpec((tm, D_MODEL), lambda i: (i, 0))
    return pl.pallas_call(
        _ffn_kernel,
        grid=(m // tm,),
        in_specs=[row, _resident(wg.shape), _resident(wu.shape), _resident(wd.shape),
                  _resident((1, D_MODEL)), _resident((1, D_MODEL))],
        out_specs=row,
        out_shape=jax.ShapeDtypeStruct((m, D_MODEL), F32),
        compiler_params=pltpu.CompilerParams(dimension_semantics=("parallel",), vmem_limit_bytes=VMEM_LIMIT),
        name="ffn_ln",
    )(x, wg, wu, wd, ln_g.reshape(1, -1), ln_b.reshape(1, -1))


def _route_kernel(x_ref, rt_ref, tri_ref, gate_ref, rank_ref, cnt_ref):
    logits = _mm(rt_ref[...], x_ref[...], NT, 2, 2)
    sub = _iota(logits.shape, 0).astype(F32)
    m1 = jnp.max(logits, axis=0, keepdims=True)
    i1 = jnp.min(jnp.where(logits == m1, sub, float(N_EXPERTS)), axis=0, keepdims=True)
    rest = jnp.where(sub == i1, NEG_BIG, logits)
    m2 = jnp.max(rest, axis=0, keepdims=True)
    i2 = jnp.min(jnp.where(rest == m2, sub, float(N_EXPERTS)), axis=0, keepdims=True)
    e2 = jnp.exp(m2 - m1)
    gate_ref[0] = jnp.where(sub == i1, 1.0 / (1.0 + e2), 0.0) + jnp.where(sub == i2, e2 / (1.0 + e2), 0.0)
    chosen = (sub == i1) | (sub == i2)
    sel = jnp.where(chosen, 1.0, 0.0)
    rank_ref[0] = jnp.where(chosen, _mm(sel, tri_ref[...]), -1.0)
    cnt_ref[0] = jnp.broadcast_to(jnp.sum(sel, axis=1, keepdims=True), cnt_ref.shape[1:])


def _moe_kernel(cnt_ref, x_ref, gate_ref, rank_ref, wg_ref, wu_ref, wd_ref, g_ref, b_ref, out_ref,
                xb_scr, acc_scr, xg_scr, yg_scr, *, ts):
    i, e, f = pl.program_id(0), pl.program_id(1), pl.program_id(2)
    last_f = f == pl.num_programs(2) - 1
    n_tiles = (cnt_ref[i * N_EXPERTS + e] + (ts - 1)) // ts

    @pl.when((e == 0) & (f == 0))
    def _():
        xb_scr[...] = x_ref[...].astype(BF16)
        acc_scr[...] = jnp.zeros_like(acc_scr)

    rank_row = rank_ref[0, pl.ds(e, 1), :]
    gate_row = gate_ref[0, pl.ds(e, 1), :]
    tb = rank_row.shape[1]
    slot = _iota((ts, tb), 0)

    def tile(j, carry):
        pick = jnp.where(rank_row == (slot + j * ts).astype(F32), 1.0, 0.0)

        @pl.when(f == 0)
        def _():
            xg_scr[j] = _mm(pick, xb_scr[...]).astype(BF16)

        xg = xg_scr[j]
        y = _mm(jax.nn.silu(_mm(xg, wg_ref[0])) * _mm(xg, wu_ref[0]), wd_ref[0])

        @pl.when(f == 0)
        def _():
            yg_scr[j] = y

        @pl.when(f > 0)
        def _():
            yg_scr[j] += y

        @pl.when(last_f)
        def _():
            w_slot = jnp.sum(pick * gate_row, axis=1, keepdims=True)
            acc_scr[...] += _mm(pick, yg_scr[j] * w_slot, TN, 1, 2)
        return carry

    lax.fori_loop(0, n_tiles, tile, 0)

    @pl.when((e == pl.num_programs(1) - 1) & last_f)
    def _():
        out_ref[...] = _layer_norm(DN_ALPHA * x_ref[...] + acc_scr[...], g_ref[...], b_ref[...])


def _moe(x, router, wg, wu, wd, ln_g, ln_b, tm):
    m = x.shape[0]
    nblk = m // tm
    dff = wg.shape[2]
    nf = 4
    tf = dff // nf
    ts = min(MOE_TILE, tm)
    n_tiles = -(-tm // ts)
    assert m % tm == 0 and tf % 128 == 0
    tri = (jnp.arange(tm)[:, None] < jnp.arange(tm)[None, :]).astype(BF16)
    gate, rank, cnt = pl.pallas_call(
        _route_kernel,
        grid=(nblk,),
        in_specs=[pl.BlockSpec((tm, D_MODEL), lambda i: (i, 0)), _resident((N_EXPERTS, D_MODEL)),
                  _resident((tm, tm))],
        out_specs=[pl.BlockSpec((1, N_EXPERTS, tm), lambda i: (i, 0, 0)),
                   pl.BlockSpec((1, N_EXPERTS, tm), lambda i: (i, 0, 0)),
                   pl.BlockSpec((1, N_EXPERTS, 128), lambda i: (i, 0, 0))],
        out_shape=[jax.ShapeDtypeStruct((nblk, N_EXPERTS, tm), F32), jax.ShapeDtypeStruct((nblk, N_EXPERTS, tm), F32),
                   jax.ShapeDtypeStruct((nblk, N_EXPERTS, 128), F32)],
        compiler_params=pltpu.CompilerParams(dimension_semantics=("parallel",), vmem_limit_bytes=VMEM_LIMIT),
        name="moe_route",
    )(x, router.T, tri)
    counts = cnt[:, :, 0].astype(jnp.int32).reshape(-1)
    row = pl.BlockSpec((tm, D_MODEL), lambda i, e, f, c: (i, 0))
    meta = pl.BlockSpec((1, N_EXPERTS, tm), lambda i, e, f, c: (i, 0, 0))
    vec = pl.BlockSpec((1, D_MODEL), lambda i, e, f, c: (0, 0), pipeline_mode=pl.Buffered(1))
    return pl.pallas_call(
        functools.partial(_moe_kernel, ts=ts),
        grid_spec=pltpu.PrefetchScalarGridSpec(
            num_scalar_prefetch=1,
            grid=(nblk, N_EXPERTS, nf),
            in_specs=[row, meta, meta,
                      pl.BlockSpec((1, D_MODEL, tf), lambda i, e, f, c: (e, 0, f)),
                      pl.BlockSpec((1, D_MODEL, tf), lambda i, e, f, c: (e, 0, f)),
                      pl.BlockSpec((1, tf, D_MODEL), lambda i, e, f, c: (e, f, 0)),
                      vec, vec],
            out_specs=row,
            scratch_shapes=[pltpu.VMEM((tm, D_MODEL), BF16), pltpu.VMEM((tm, D_MODEL), F32),
                            pltpu.VMEM((n_tiles, ts, D_MODEL), BF16), pltpu.VMEM((n_tiles, ts, D_MODEL), F32)]),
        out_shape=jax.ShapeDtypeStruct((m, D_MODEL), F32),
        compiler_params=pltpu.CompilerParams(dimension_semantics=("parallel", "arbitrary", "arbitrary"),
                                             vmem_limit_bytes=VMEM_LIMIT),
        name="moe_ln",
    )(counts, x, gate, rank, wg, wu, wd, ln_g.reshape(1, -1), ln_b.reshape(1, -1))


def _token_mix(h, pos0, rw_prev, wkv0, pool_buf, ret0, kv_bufs, lw, w_mix, w_gate, w_branch, w_out, ln_g, ln_b):
    b, t, _ = h.shape
    m = b * t
    tm = 512 if m % 512 == 0 else m
    hf = h.reshape(m, D_MODEL)
    p_rw, p_pool, p_ret, p_dil = _project(hf, w_mix, tm)
    p_rw, p_pool = p_rw.reshape(b, t, -1), p_pool.reshape(b, t, -1)
    p_ret, p_dil = p_ret.reshape(b, t, -1), p_dil.reshape(b, t, -1)
    o_a, wkv_new = _rwkv_branch(p_rw, rw_prev, wkv0, lw)
    o_b = _pool_branch(p_pool, pool_buf, pos0, lw)
    pool_new = jnp.concatenate([pool_buf, p_pool], axis=1)[:, -POOL_BUF:]
    o_c, ret_new = _ret_branch(p_ret, ret0, pos0, lw)
    def kv_rows(g, keep):
        lo = (3 * g + 1) * WIDTH
        return p_dil[:, t - keep:, lo:lo + 2 * WIDTH].reshape(b, keep, 2, N_HEADS, HEAD_DIM)

    if kv_bufs is None:
        o_d = _dil_prompt(p_dil)
        kv_new = [kv_rows(g, min(win, t)) for g, (win, _) in enumerate(DIL_PATTERNS)]
    else:
        o_d = _dil_step(p_dil, kv_bufs)
        kv_new = [kv_rows(g, t) for g in range(DIL_GROUPS)]
    branches = [x.reshape(m, WIDTH) for x in (o_a, o_b, o_c, o_d)]
    x1 = _merge(hf, branches, w_gate, w_branch, w_out, ln_g, ln_b, tm)
    return x1, (wkv_new, p_rw[:, -1], pool_new, ret_new, kv_new[0], kv_new[1], kv_new[2])


def kernel(x_prompt, x_sample, state_wkv, state_shift, state_pool, state_ret, cache_kv_w128, cache_kv_w512, cache_kv_w2048, w_in, rw_mu, rw_w0, rw_w2, rw_a0, rw_a2, rw_g2, rw_kk, rw_ka, rw_rk, rw_gn_g, rw_gn_b, pool_w, pool_scale, ret_gn_g, ret_gn_b, w_branch, w_out, ln_g, ln_b, ffn_w_gate, ffn_w_up, ffn_w_down, moe_router, moe_w_gate, moe_w_up, moe_w_down):
    hp, hs = x_prompt, x_sample
    bp, tp, _ = hp.shape
    bs, ts, _ = hs.shape
    names = ('rw_mu', 'rw_w0', 'rw_w2', 'rw_a0', 'rw_a2', 'rw_g2', 'rw_kk', 'rw_ka', 'rw_rk', 'rw_gn_g',
             'rw_gn_b', 'pool_w', 'pool_scale', 'ret_gn_g', 'ret_gn_b')
    stacked = (rw_mu, rw_w0, rw_w2, rw_a0, rw_a2, rw_g2, rw_kk, rw_ka, rw_rk, rw_gn_g, rw_gn_b, pool_w,
               pool_scale, ret_gn_g, ret_gn_b)
    new_p = [[] for _ in range(7)]
    new_s = [[] for _ in range(7)]
    zeros = lambda *shape: jnp.zeros(shape, F32)
    for l in range(DEPTH):
        lw = {k: v[l] for k, v in zip(names, stacked)}
        w_mix = w_in[l, :, :COL_GATE].astype(BF16)
        w_gate = w_in[l, :, COL_GATE:].astype(BF16)
        wb, wo = w_branch[l].astype(BF16), w_out[l].astype(BF16)
        xp, st_p = _token_mix(hp, 0, zeros(bp, RW_COLS), zeros(bp, N_HEADS, HEAD_DIM, HEAD_DIM),
                              zeros(bp, POOL_BUF, WIDTH), zeros(bp, N_HEADS, HEAD_DIM, HEAD_DIM), None,
                              lw, w_mix, w_gate, wb, wo, ln_g[l, 0], ln_b[l, 0])
        xs, st_s = _token_mix(hs, 8192, state_shift[l], state_wkv[l], state_pool[l], state_ret[l],
                              (cache_kv_w128[l], cache_kv_w512[l], cache_kv_w2048[l]),
                              lw, w_mix, w_gate, wb, wo, ln_g[l, 0], ln_b[l, 0])
        j = l // 2
        if l % 2 == 0:
            ws = [w[j].astype(BF16) for w in (ffn_w_gate, ffn_w_up, ffn_w_down)]
            xp = _ffn(xp, *ws, ln_g[l, 1], ln_b[l, 1], 256)
            xs = _ffn(xs, *ws, ln_g[l, 1], ln_b[l, 1], xs.shape[0])
        else:
            ws = [w[j].astype(BF16) for w in (moe_w_gate, moe_w_up, moe_w_down)]
            xp = _moe(xp, moe_router[j], *ws, ln_g[l, 1], ln_b[l, 1], 1024)
            xs = _moe(xs, moe_router[j], *ws, ln_g[l, 1], ln_b[l, 1], xs.shape[0])
        hp, hs = xp.reshape(bp, tp, D_MODEL), xs.reshape(bs, ts, D_MODEL)
        for i in range(7):
            new_p[i].append(st_p[i])
            new_s[i].append(st_s[i])
    outs_p = [jnp.stack(x) for x in new_p]
    outs_s = [jnp.stack(x) for x in new_s]
    return (hp, hs, *outs_p, *outs_s)
```

```python
import functools
import math

import jax
import jax.numpy as jnp
from jax import lax
from jax.experimental import pallas as pl
from jax.experimental.pallas import tpu as pltpu

F32 = jnp.float32
BF16 = jnp.bfloat16

D_MODEL = 1024
DEPTH = 2
HEAD_DIM = 64
N_HEADS = 4
WIDTH = N_HEADS * HEAD_DIM
RW_COLS = 1024
RW_GN_EPS = 64e-5
POOL_WINDOWS = (2, 4, 8, 16)
POOL_BUF = 15
RET_CHUNK = 128
ROPE_BASE = 10000.0
DIL_PATTERNS = ((128, 1), (512, 4), (2048, 16))
DIL_GROUPS = 3
DIL_BLOCK = 128
DIL_SPAN = 2048
DIL_COLS = 3 * DIL_GROUPS * WIDTH
N_BRANCH = 4
COL_POOL = RW_COLS
COL_RET = COL_POOL + WIDTH
COL_DIL = COL_RET + 4 * WIDTH
COL_GATE = COL_DIL + DIL_COLS
N_EXPERTS = 8
DN_ALPHA = (2 * DEPTH) ** 0.25
LN_EPS = 1e-5
RW_CHUNK = 64
MOE_TILE = 288
RW_P = 1
RW_BATCH = 4
NEG_BIG = -1e30

NN = (((1,), (0,)), ((), ()))
NT = (((1,), (1,)), ((), ()))
TN = (((0,), (0,)), ((), ()))

VMEM_LIMIT = 56 * 1024 * 1024


def _split(x, n):
    if x.dtype == BF16:
        return [x]
    parts, rem = [], x
    for i in range(n):
        p = rem.astype(BF16)
        parts.append(p)
        if i + 1 < n:
            rem = rem - p.astype(F32)
    return parts


def _mm(a, b, dims=NN, pa=1, pb=1):
    a_parts, b_parts = _split(a, pa), _split(b, pb)
    depth = max(len(a_parts), len(b_parts))
    acc = None
    for i, ai in enumerate(a_parts):
        for j, bj in enumerate(b_parts):
            if i + j < depth:
                t = lax.dot_general(ai, bj, dims, preferred_element_type=F32)
                acc = t if acc is None else acc + t
    return acc


def _iota(shape, axis):
    return lax.broadcasted_iota(jnp.int32, shape, axis)


def _div(x, d):
    assert d & (d - 1) == 0
    return x >> (d.bit_length() - 1)


def _mod(x, d):
    assert d & (d - 1) == 0
    return x & (d - 1)


def _per_head(head, values):
    out = jnp.full(head.shape, values[-1], F32)
    for h in range(len(values) - 2, -1, -1):
        out = jnp.where(head == h, values[h], out)
    return out


def _head_mask(rows_per_head, n_rows):
    return _div(_iota((n_rows, WIDTH), 0), rows_per_head) == _div(_iota((n_rows, WIDTH), 1), HEAD_DIM)


def _stack_heads(x, mask):
    return jnp.where(mask, jnp.concatenate([x] * N_HEADS, axis=0), 0.0)


def _unstack_heads(x_st, mask, c):
    x_st = jnp.where(mask, x_st, 0.0)
    out = x_st[0:c]
    for h in range(1, N_HEADS):
        out = out + x_st[h * c:(h + 1) * c]
    return out


def _ones_bd():
    return _same_head().astype(BF16)


def _same_head():
    return _div(_iota((WIDTH, WIDTH), 0), HEAD_DIM) == _div(_iota((WIDTH, WIDTH), 1), HEAD_DIM)


def _head_norm(x, ones_bd, g, b, eps):
    mu = _mm(x, ones_bd, pa=2) * (1.0 / HEAD_DIM)
    d = x - mu
    var = _mm(d * d, ones_bd, pa=2) * (1.0 / HEAD_DIM)
    return d * lax.rsqrt(var + eps) * g + b


def _layer_norm(x, g, b):
    mu = jnp.mean(x, axis=-1, keepdims=True)
    d = x - mu
    var = jnp.mean(d * d, axis=-1, keepdims=True)
    return d * lax.rsqrt(var + LN_EPS) * g + b


def _resident(shape):
    nd = len(shape)
    return pl.BlockSpec(shape, lambda *_: (0,) * nd, pipeline_mode=pl.Buffered(1))


def _proj_kernel(x_ref, w_ref, rw_ref, pool_ref, ret_ref, dil_ref):
    xb = x_ref[...].astype(BF16)
    for ref, lo, hi in ((rw_ref, 0, COL_POOL), (pool_ref, COL_POOL, COL_RET),
                        (ret_ref, COL_RET, COL_DIL), (dil_ref, COL_DIL, COL_GATE)):
        for s in range(lo, hi, 512):
            e = min(s + 512, hi)
            ref[:, s - lo:e - lo] = _mm(xb, w_ref[:, s:e])


def _project(x, w_mix, tm):
    m = x.shape[0]
    widths = (COL_POOL, WIDTH, 4 * WIDTH, DIL_COLS)
    return pl.pallas_call(
        _proj_kernel,
        grid=(m // tm,),
        in_specs=[pl.BlockSpec((tm, D_MODEL), lambda i: (i, 0)), _resident((D_MODEL, COL_GATE))],
        out_specs=[pl.BlockSpec((tm, w), lambda i: (i, 0)) for w in widths],
        out_shape=[jax.ShapeDtypeStruct((m, w), F32) for w in widths],
        compiler_params=pltpu.CompilerParams(dimension_semantics=("parallel",), vmem_limit_bytes=VMEM_LIMIT),
        name="proj",
    )(x, w_mix)


BNT = (((2,), (2,)), ((0,), (0,)))
BNN = (((2,), (1,)), ((0,), (0,)))
BTN = (((1,), (1,)), ((0,), (0,)))


def _rwkv_kernel(p_ref, prev_ref, s0_ref, mu_ref, w0_ref, w2_ref, a0_ref, a2_ref, g2_ref, kk_ref, ka_ref,
                 rk_ref, gng_ref, gnb_ref, o_ref, st_ref, s_scr, prev_scr, buf_scr, *, c, t_in, nb):
    ci = pl.program_id(1)

    @pl.when(ci == 0)
    def _():
        s_scr[...] = s0_ref[...]
        for b in range(nb):
            prev_scr[b, 0:1, :] = prev_ref[b]

    first = _iota((c, RW_COLS), 0) == 0
    ps, shs = [], []
    for b in range(nb):
        if t_in == c:
            pb_ = p_ref[b]
        else:
            buf_scr[b] = jnp.zeros((c, RW_COLS), F32)
            buf_scr[b, 0:t_in, :] = p_ref[b]
            pb_ = buf_scr[b]
        shs.append(jnp.where(first, prev_scr[b, 0:1, :], pltpu.roll(pb_, 1, 0)))
        prev_scr[b, 0:1, :] = pb_[c - 1:c, :]
        ps.append(pb_)
    p, shifted = jnp.concatenate(ps, axis=0), jnp.concatenate(shs, axis=0)
    m = nb * c
    u = p + (shifted - p) * mu_ref[...]
    r, k, v, ul = u[:, 0:WIDTH], u[:, WIDTH:2 * WIDTH], u[:, 2 * WIDTH:3 * WIDTH], u[:, 3 * WIDTH:]

    w_pre = w0_ref[...] + _mm(jnp.tanh(ul), w2_ref[...])
    logw = -math.exp(-0.5) * jax.nn.sigmoid(w_pre)
    a = jax.nn.sigmoid(a0_ref[...] + _mm(ul, a2_ref[...]))
    g = _mm(jax.nn.sigmoid(ul), g2_ref[...])

    ones_bd = _ones_bd()
    kk = k * kk_ref[...]
    kk = kk * lax.rsqrt(jnp.maximum(_mm(kk * kk, ones_bd, pa=2), 1e-24))
    k_mod = k * (1.0 + (a - 1.0) * ka_ref[...])
    a_vec, b_vec = -kk, kk * a
    if t_in < c:
        live = _mod(_iota((m, WIDTH), 0), c) < t_in
        logw = jnp.where(live, logw, 0.0)
        a_vec, b_vec = jnp.where(live, a_vec, 0.0), jnp.where(live, b_vec, 0.0)
        k_mod, v = jnp.where(live, k_mod, 0.0), jnp.where(live, v, 0.0)

    qi, qj = _iota((m, m), 0), _iota((m, m), 1)
    tri = ((_div(qi, c) == _div(qj, c)) & (qi >= qj)).astype(BF16)
    cum2 = _mm(tri, logw, pb=3)
    seq = lambda x: x.reshape(nb, c, WIDTH)
    cum, lw3 = seq(cum2), seq(logw)
    cum_end = cum[:, c - 1:c, :]
    e_neg, e_end = jnp.exp(-cum), jnp.exp(cum_end - cum)
    a_t = seq(a_vec) * jnp.exp(cum - lw3)
    r_t = seq(r) * jnp.exp(cum)
    b_t, k_t = seq(b_vec) * e_neg, seq(k_mod) * e_neg
    b_e, k_e = seq(b_vec) * e_end, seq(k_mod) * e_end
    v3 = seq(v)

    n = N_HEADS * c
    hm = _head_mask(c, n)
    stack = lambda x: jnp.where(hm, jnp.concatenate([x] * N_HEADS, axis=1), 0.0)
    a_st, r_st, b_st, v_st = stack(a_t), stack(r_t), stack(b_t), stack(v3)
    ri, rj = _iota((n, n), 0), _iota((n, n), 1)
    same = _div(ri, c) == _div(rj, c)
    strict_bd = same & (_mod(ri, c) > _mod(rj, c))
    incl_bd = same & (_mod(ri, c) >= _mod(rj, c))
    ti, tj = _mod(_iota((n, c), 0), c), _iota((n, c), 1)
    strict_st, incl_st = ti > tj, ti >= tj

    mm = functools.partial(_mm, pa=RW_P, pb=RW_P)
    a_ab = jnp.where(strict_bd, mm(a_st, b_st, BNT), 0.0)
    inv = jnp.where(ri == rj, 1.0, 0.0) + a_ab
    pw = a_ab
    for _ in range(int(math.log2(c)) - 1):
        pw = mm(pw, pw, BNN)
        inv = inv + mm(inv, pw, BNN)
    a_ak = jnp.where(strict_st, mm(a_st, k_t, BNT), 0.0)
    z_st = jnp.where(hm, mm(a_ak, v3, BNN), 0.0)
    wu = mm(inv, jnp.concatenate([a_st, z_st], axis=2), BNN)
    w_st, u0_st = wu[:, :, 0:WIDTH], wu[:, :, WIDTH:]

    s0 = s_scr[...]
    u_st = mm(w_st, s0, BNT) + u0_st
    lhs = jnp.concatenate([u_st, v_st], axis=1)
    rhs = jnp.concatenate([stack(b_e), stack(k_e)], axis=1)
    s_scr[...] = s0 * jnp.exp(cum_end) + mm(lhs, rhs, BTN)

    a_rb = jnp.where(incl_bd, mm(r_st, b_st, BNT), 0.0)
    a_rk = jnp.where(incl_st, mm(r_st, k_t, BNT), 0.0)
    o_st = jnp.where(hm, mm(r_st, s0, BNT) + mm(a_rb, u_st, BNN) + mm(a_rk, v3, BNN), 0.0)
    o3 = o_st[:, 0:c]
    for h in range(1, N_HEADS):
        o3 = o3 + o_st[:, h * c:(h + 1) * c]
    o = o3.reshape(m, WIDTH)

    o = _head_norm(o, ones_bd, gng_ref[...], gnb_ref[...], RW_GN_EPS)
    o = o + _mm(r * k_mod * rk_ref[...], ones_bd, pa=2) * v
    o_ref[...] = (o * g).reshape(nb, c, WIDTH)[:, 0:t_in]

    @pl.when(ci == pl.num_programs(1) - 1)
    def _():
        st_ref[...] = s_scr[...]


def _block_diag_heads(s):
    b = s.shape[0]
    eye = jnp.eye(N_HEADS, dtype=s.dtype)
    return jnp.einsum('bhij,hg->bhigj', s, eye).reshape(b, WIDTH, WIDTH)


def _diag_heads(s_bd):
    b = s_bd.shape[0]
    s = s_bd.reshape(b, N_HEADS, HEAD_DIM, N_HEADS, HEAD_DIM)
    return jnp.stack([s[:, h, :, h, :] for h in range(N_HEADS)], axis=1)


def _rwkv_branch(p, p_prev, wkv0, lw):
    b, t, _ = p.shape
    c = RW_CHUNK
    t_in = c if t % c == 0 else t
    nc = t // c if t % c == 0 else 1
    vec = lambda x: x.reshape(1, -1)
    pad_rows = lambda w, lo: jnp.zeros((WIDTH, WIDTH), F32).at[lo:lo + w.shape[0]].set(w).astype(BF16)
    params = [vec(lw['rw_mu']), vec(lw['rw_w0']), pad_rows(lw['rw_w2'], 0), vec(lw['rw_a0']),
              pad_rows(lw['rw_a2'], 64), pad_rows(lw['rw_g2'], 128), vec(lw['rw_kk']), vec(lw['rw_ka']),
              vec(lw['rw_rk']), vec(lw['rw_gn_g']), vec(lw['rw_gn_b'])]
    nb = RW_BATCH
    assert b % nb == 0
    o, st = pl.pallas_call(
        functools.partial(_rwkv_kernel, c=c, t_in=t_in, nb=nb),
        grid=(b // nb, nc),
        in_specs=[pl.BlockSpec((nb, t_in, RW_COLS), lambda i, j: (i, j, 0)),
                  pl.BlockSpec((nb, 1, RW_COLS), lambda i, j: (i, 0, 0)),
                  pl.BlockSpec((nb, WIDTH, WIDTH), lambda i, j: (i, 0, 0))]
                 + [_resident(x.shape) for x in params],
        out_specs=[pl.BlockSpec((nb, t_in, WIDTH), lambda i, j: (i, j, 0)),
                   pl.BlockSpec((nb, WIDTH, WIDTH), lambda i, j: (i, 0, 0))],
        out_shape=[jax.ShapeDtypeStruct((b, t, WIDTH), F32), jax.ShapeDtypeStruct((b, WIDTH, WIDTH), F32)],
        scratch_shapes=[pltpu.VMEM((nb, WIDTH, WIDTH), F32), pltpu.VMEM((nb, 8, RW_COLS), F32),
                        pltpu.VMEM((nb, c, RW_COLS), F32)],
        compiler_params=pltpu.CompilerParams(dimension_semantics=("parallel", "arbitrary"),
                                             vmem_limit_bytes=VMEM_LIMIT),
        name="rwkv7",
    )(p, p_prev.reshape(b, 1, RW_COLS), _block_diag_heads(wkv0), *params)
    return o, _diag_heads(st)


def _pool_kernel(u_ref, buf_ref, w_ref, scale_ref, o_ref, ext_scr, *, c, t_in, pos0):
    ci = pl.program_id(1)

    @pl.when(ci == 0)
    def _():
        ext_scr[0:16, :] = buf_ref[0]

    if t_in < c:
        ext_scr[16:16 + c, :] = jnp.zeros((c, WIDTH), F32)
    ext_scr[16:16 + t_in, :] = u_ref[0]
    x = ext_scr[16:16 + c, :]
    sums, acc, off = [], x, 1
    for win in POOL_WINDOWS:
        while off < win:
            acc = acc + ext_scr[16 - off:16 - off + c, :]
            off += 1
        sums.append(acc)
    pos = pos0 + ci * c + _iota((c, WIDTH), 0)
    grp = _div(_iota((c, WIDTH), 1), HEAD_DIM)
    mean = jnp.zeros((c, WIDTH), F32)
    for gi, win in enumerate(POOL_WINDOWS):
        cnt = jnp.minimum(win, pos + 1).astype(F32)
        mean = jnp.where(grp == gi, sums[gi] / cnt, mean)
    mixed = _mm(mean - x, w_ref[...]) * scale_ref[...]
    o_ref[0] = mixed[0:t_in]
    ext_scr[0:16, :] = ext_scr[c:c + 16, :]


def _pool_branch(u, buf, pos0, lw):
    b, t, _ = u.shape
    c = 512 if t % 512 == 0 else 8
    t_in = c if t % c == 0 else t
    nc = t // c if t % c == 0 else 1
    w_bd = jnp.einsum('gcd,gh->gchd', lw['pool_w'], jnp.eye(N_HEADS, dtype=F32)).reshape(WIDTH, WIDTH).astype(BF16)
    buf16 = jnp.pad(buf, ((0, 0), (1, 0), (0, 0)))
    return pl.pallas_call(
        functools.partial(_pool_kernel, c=c, t_in=t_in, pos0=pos0),
        grid=(b, nc),
        in_specs=[pl.BlockSpec((1, t_in, WIDTH), lambda i, j: (i, j, 0)),
                  pl.BlockSpec((1, 16, WIDTH), lambda i, j: (i, 0, 0)),
                  _resident((WIDTH, WIDTH)), _resident((1, WIDTH))],
        out_specs=pl.BlockSpec((1, t_in, WIDTH), lambda i, j: (i, j, 0)),
        out_shape=jax.ShapeDtypeStruct((b, t, WIDTH), F32),
        scratch_shapes=[pltpu.VMEM((16 + c, WIDTH), F32)],
        compiler_params=pltpu.CompilerParams(dimension_semantics=("parallel", "arbitrary")),
        name="pool",
    )(u, buf16, w_bd, lw['pool_scale'].reshape(1, WIDTH))


def _rot_half(x):
    first = _mod(_iota(x.shape, 1), HEAD_DIM) < (HEAD_DIM // 2)
    return jnp.where(first, pltpu.roll(x, WIDTH - HEAD_DIM // 2, 1), pltpu.roll(x, HEAD_DIM // 2, 1))


RET_LOG_DECAY = tuple(math.log(1.0 - 2.0 ** (-5.0 - h)) for h in range(N_HEADS))
ALIBI_SLOPES = tuple(2.0 ** (-8.0 * (i + 1) / (DIL_GROUPS * N_HEADS)) for i in range(DIL_GROUPS * N_HEADS))


def _ret_kernel(p_ref, cos_ref, sin_ref, s0_ref, gng_ref, gnb_ref, o_ref, st_ref, s_scr, buf_scr,
                dmask_scr, qd_scr, kd_scr, *, c, t_in, nb):
    ci = pl.program_id(1)
    n = N_HEADS * c
    lg = _per_head(_div(_iota((1, WIDTH), 1), HEAD_DIM), RET_LOG_DECAY)

    @pl.when(ci == 0)
    def _():
        s_scr[...] = s0_ref[...]
        idx = _iota((c, WIDTH), 0).astype(F32)
        qd_scr[...] = jnp.exp(lg * (idx + 1.0))
        kd_scr[...] = jnp.exp(lg * (t_in - 1.0 - idx))
        rel = _mod(_iota((n, c), 0), c) - _iota((n, c), 1)
        lg_rows = _per_head(_div(_iota((n, c), 0), c), RET_LOG_DECAY)
        dmask_scr[...] = jnp.where(rel >= 0, jnp.exp(lg_rows * jnp.maximum(rel, 0).astype(F32)), 0.0)

    hm = _head_mask(c, n)
    cos, sin = cos_ref[...], sin_ref[...]
    chunk_decay = jnp.exp(lg * float(t_in))
    same_head, ones_bd = _same_head(), _ones_bd()
    for b in range(nb):
        if t_in == c:
            p = p_ref[b]
        else:
            buf_scr[b] = jnp.zeros((c, 4 * WIDTH), F32)
            buf_scr[b, 0:t_in, :] = p_ref[b]
            p = buf_scr[b]
        q, k, v, g = (p[:, i * WIDTH:(i + 1) * WIDTH] for i in range(4))
        q = q * cos + _rot_half(q) * sin
        k = (k * cos + _rot_half(k) * sin) * HEAD_DIM ** -0.5
        if t_in < c:
            live = _iota((c, WIDTH), 0) < t_in
            k, v = jnp.where(live, k, 0.0), jnp.where(live, v, 0.0)
        inner = _mm(_stack_heads(q, hm), k, NT) * dmask_scr[...]
        s0 = s_scr[b]
        o = _unstack_heads(_mm(inner, v), hm, c) + _mm(q * qd_scr[...], s0)
        s_scr[b] = s0 * chunk_decay + jnp.where(same_head, _mm(k * kd_scr[...], v, TN), 0.0)
        o = jax.nn.silu(g) * _head_norm(o, ones_bd, gng_ref[...], gnb_ref[...], LN_EPS)
        o_ref[b] = o[0:t_in]

    @pl.when(ci == pl.num_programs(1) - 1)
    def _():
        st_ref[...] = s_scr[...]


def _rope_tables(pos0, t, rows):
    half = HEAD_DIM // 2
    inv = ROPE_BASE ** (-jnp.arange(half, dtype=F32) / half)
    ang = (pos0 + jnp.arange(t, dtype=jnp.int32)).astype(F32)[:, None] * inv[None, :]
    cos = jnp.tile(jnp.cos(ang), (1, 2 * N_HEADS))
    sin = jnp.tile(jnp.concatenate([-jnp.sin(ang), jnp.sin(ang)], axis=1), (1, N_HEADS))
    pad = ((0, rows - t), (0, 0))
    return jnp.pad(cos, pad), jnp.pad(sin, pad)


def _ret_branch(p, s0, pos0, lw):
    b, t, _ = p.shape
    c = RET_CHUNK
    t_in = c if t % c == 0 else t
    nc = t // c if t % c == 0 else 1
    cos, sin = _rope_tables(pos0, t, nc * c)
    nb = RW_BATCH
    assert b % nb == 0
    o, st = pl.pallas_call(
        functools.partial(_ret_kernel, c=c, t_in=t_in, nb=nb),
        grid=(b // nb, nc),
        in_specs=[pl.BlockSpec((nb, t_in, 4 * WIDTH), lambda i, j: (i, j, 0)),
                  pl.BlockSpec((c, WIDTH), lambda i, j: (j, 0)),
                  pl.BlockSpec((c, WIDTH), lambda i, j: (j, 0)),
                  pl.BlockSpec((nb, WIDTH, WIDTH), lambda i, j: (i, 0, 0)),
                  _resident((1, WIDTH)), _resident((1, WIDTH))],
        out_specs=[pl.BlockSpec((nb, t_in, WIDTH), lambda i, j: (i, j, 0)),
                   pl.BlockSpec((nb, WIDTH, WIDTH), lambda i, j: (i, 0, 0))],
        out_shape=[jax.ShapeDtypeStruct((b, t, WIDTH), F32), jax.ShapeDtypeStruct((b, WIDTH, WIDTH), F32)],
        scratch_shapes=[pltpu.VMEM((nb, WIDTH, WIDTH), F32), pltpu.VMEM((nb, c, 4 * WIDTH), F32),
                        pltpu.VMEM((N_HEADS * c, c), F32), pltpu.VMEM((c, WIDTH), F32), pltpu.VMEM((c, WIDTH), F32)],
        compiler_params=pltpu.CompilerParams(dimension_semantics=("parallel", "arbitrary")),
        name="retention",
    )(p, cos, sin, _block_diag_heads(s0), lw['ret_gn_g'].reshape(1, WIDTH), lw['ret_gn_b'].reshape(1, WIDTH))
    return o, _diag_heads(st)


def _alibi_slope_rows(group, rows_per_head):
    head = _div(_iota((N_HEADS * rows_per_head, 1), 0), rows_per_head)
    return _per_head(head, ALIBI_SLOPES[group * N_HEADS:(group + 1) * N_HEADS])


def _dil_prompt_kernel(*refs, group, dil, span):
    q_refs, kc_refs, kp_refs, vc_refs, vp_refs = (refs[2 * i:2 * i + 2] for i in range(5))
    o_ref, lse_ref, k_scr, v_scr, o_scr, lse_scr = refs[10:]
    blk = DIL_BLOCK
    tail = blk * dil
    si = pl.program_id(1)
    for half in range(2):
        k_scr[half, 0:tail, :] = kp_refs[half][0]
        k_scr[half, tail:tail + span, :] = kc_refs[half][0]
        v_scr[half, 0:tail, :] = vp_refs[half][0]
        v_scr[half, tail:tail + span, :] = vc_refs[half][0]
    both = lambda ref3, rows: jnp.concatenate([ref3[0, rows, :], ref3[1, rows, :]], axis=1)
    n = N_HEADS * blk
    hm = _head_mask(blk, n)
    ki = _iota((n, 2 * blk), 1)
    steps = blk + _mod(_iota((n, 2 * blk), 0), blk) - ki
    band = (steps >= 0) & (steps <= blk)
    bias = _alibi_slope_rows(group, blk) * (dil * steps).astype(F32)
    for cc in range(span // tail):
        valid = band & ((si > 0) | (ki >= blk)) if cc == 0 else band
        for r in range(dil):
            rows_q = pl.ds(cc * tail + r, blk, stride=dil) if dil > 1 else pl.ds(cc * tail, blk)
            rows_kv = pl.ds(cc * tail + r, 2 * blk, stride=dil) if dil > 1 else pl.ds(cc * tail, 2 * blk)
            q = jnp.concatenate([q_refs[0][0, rows_q, :], q_refs[1][0, rows_q, :]], axis=1)
            q_st = _stack_heads(q, hm)
            s = _mm(q_st, both(k_scr, rows_kv), NT) * HEAD_DIM ** -0.5 - bias
            s = jnp.where(valid, s, NEG_BIG)
            m = jnp.max(s, axis=1, keepdims=True)
            e = jnp.exp(s - m)
            l = jnp.sum(e, axis=1, keepdims=True)
            o_st = _mm(e, both(v_scr, rows_kv)) / l
            lse = m + jnp.log(l)
            o = _unstack_heads(o_st, hm, blk)
            lse = _unstack_heads(jnp.broadcast_to(lse, (n, WIDTH)), hm, blk)
            for half in range(2):
                o_scr[half, rows_q, :] = o[:, half * 128:(half + 1) * 128]
                lse_scr[half, rows_q, :] = lse[:, half * 128:(half + 1) * 128]
    o_ref[0] = jnp.concatenate([o_scr[0], o_scr[1]], axis=1)
    lse_ref[0] = jnp.concatenate([lse_scr[0], lse_scr[1]], axis=1)


def _dil_prompt_group(pd, group):
    b, s, _ = pd.shape
    win, dil = DIL_PATTERNS[group]
    span = DIL_SPAN
    tail = DIL_BLOCK * dil
    assert win // dil == DIL_BLOCK and span % tail == 0 and s % span == 0
    base = group * 3

    def cur(col):
        return [pl.BlockSpec((1, span, 128), lambda i, j, h=h: (i, j, 2 * (base + col) + h)) for h in range(2)]

    def prev(col):
        return [pl.BlockSpec((1, tail, 128),
                             lambda i, j, h=h: (i, jnp.maximum(j * (span // tail) - 1, 0), 2 * (base + col) + h))
                for h in range(2)]

    out_spec = pl.BlockSpec((1, span, WIDTH), lambda i, j: (i, j, 0))
    return pl.pallas_call(
        functools.partial(_dil_prompt_kernel, group=group, dil=dil, span=span),
        grid=(b, s // span),
        in_specs=cur(0) + cur(1) + prev(1) + cur(2) + prev(2),
        out_specs=[out_spec, out_spec],
        out_shape=[jax.ShapeDtypeStruct((b, s, WIDTH), F32)] * 2,
        scratch_shapes=[pltpu.VMEM((2, tail + span, 128), F32)] * 2 + [pltpu.VMEM((2, span, 128), F32)] * 2,
        compiler_params=pltpu.CompilerParams(dimension_semantics=("parallel", "arbitrary"),
                                             vmem_limit_bytes=VMEM_LIMIT),
        name=f"dil_prompt_g{group}",
    )(*([pd] * 10))


def _dil_combine(os_, ls):
    m = jnp.maximum(jnp.maximum(ls[0], ls[1]), ls[2])
    es = [jnp.exp(x - m) for x in ls]
    return (es[0] * os_[0] + es[1] * os_[1] + es[2] * os_[2]) / (es[0] + es[1] + es[2])


def _dil_prompt(pd):
    b, s, _ = pd.shape
    outs = [_dil_prompt_group(pd, g) for g in range(DIL_GROUPS)]
    return [x[0].reshape(b * s, WIDTH) for x in outs] + [x[1].reshape(b * s, WIDTH) for x in outs]


def _dil_step_kernel(pd_ref, c0_ref, c1_ref, c2_ref, o_ref, buf_scr, *, t, tp):
    buf_scr[...] = jnp.zeros_like(buf_scr)
    buf_scr[0:t, :] = pd_ref[0]
    pd = buf_scr[...]
    n = N_HEADS * tp
    hm = _head_mask(tp, n)
    qt = _mod(_iota((n, 1), 0), tp)
    outs, lses = [], []
    for g, cache_ref in enumerate((c0_ref, c1_ref, c2_ref)):
        win, dil = DIL_PATTERNS[g]
        length = cache_ref.shape[1]
        q = pd[:, g * 3 * WIDTH:g * 3 * WIDTH + WIDTH]
        k_new = pd[:, g * 3 * WIDTH + WIDTH:g * 3 * WIDTH + 2 * WIDTH]
        v_new = pd[:, g * 3 * WIDTH + 2 * WIDTH:g * 3 * WIDTH + 3 * WIDTH]
        k_old, v_old = cache_ref[0, :, 0:WIDTH], cache_ref[0, :, WIDTH:2 * WIDTH]
        q_st = _stack_heads(q, hm)
        slope = _alibi_slope_rows(g, tp)
        d_old = length + qt - _iota((n, length), 1)
        ok_old = (_mod(d_old, dil) == 0) & (d_old <= win)
        s_old = _mm(q_st, k_old, NT) * HEAD_DIM ** -0.5 - slope * d_old.astype(F32)
        s_old = jnp.where(ok_old, s_old, NEG_BIG)
        d_new = qt - _iota((n, tp), 1)
        ok_new = (d_new >= 0) & (_mod(d_new, dil) == 0)
        s_new = _mm(q_st, k_new, NT) * HEAD_DIM ** -0.5 - slope * d_new.astype(F32)
        s_new = jnp.where(ok_new, s_new, NEG_BIG)
        m = jnp.maximum(jnp.max(s_old, axis=1, keepdims=True), jnp.max(s_new, axis=1, keepdims=True))
        e_old, e_new = jnp.exp(s_old - m), jnp.exp(s_new - m)
        l = jnp.sum(e_old, axis=1, keepdims=True) + jnp.sum(e_new, axis=1, keepdims=True)
        outs.append((_mm(e_old, v_old) + _mm(e_new, v_new)) / l)
        lses.append(m + jnp.log(l))
    o_ref[0] = _unstack_heads(_dil_combine(outs, lses), hm, tp)[0:t]


def _dil_step(pd, caches):
    b, t, _ = pd.shape
    tp = 8
    flat = [c.reshape(b, c.shape[1], 2 * WIDTH) for c in caches]
    return pl.pallas_call(
        functools.partial(_dil_step_kernel, t=t, tp=tp),
        grid=(b,),
        in_specs=[pl.BlockSpec((1, t, DIL_COLS), lambda i: (i, 0, 0))]
                 + [pl.BlockSpec((1, c.shape[1], 2 * WIDTH), lambda i: (i, 0, 0)) for c in flat],
        out_specs=pl.BlockSpec((1, t, WIDTH), lambda i: (i, 0, 0)),
        out_shape=jax.ShapeDtypeStruct((b, t, WIDTH), F32),
        scratch_shapes=[pltpu.VMEM((tp, DIL_COLS), F32)],
        compiler_params=pltpu.CompilerParams(dimension_semantics=("parallel",), vmem_limit_bytes=VMEM_LIMIT),
        name="dil_step",
    )(pd, *flat)


def _merge_kernel(h_ref, *refs):
    wg_ref, wb_ref, wo_ref, g_ref, b_ref, out_ref = refs[-6:]
    branch_refs = refs[:-6]
    branches = [r[...] for r in branch_refs[:3]]
    if len(branch_refs) == 4:
        branches.append(branch_refs[3][...])
    else:
        branches.append(_dil_combine([r[...] for r in branch_refs[3:6]], [r[...] for r in branch_refs[6:9]]))
    h = h_ref[...]
    hb = h.astype(BF16)
    z = None
    for n, o in enumerate(branches):
        gate = jax.nn.sigmoid(_mm(hb, wg_ref[:, n * D_MODEL:(n + 1) * D_MODEL]))
        term = gate * _mm(o, wb_ref[n])
        z = term if z is None else z + term
    y = _mm(z, wo_ref[...])
    out_ref[...] = _layer_norm(DN_ALPHA * h + y, g_ref[...], b_ref[...])


def _merge(h, branches, w_gate, w_branch, w_out, ln_g, ln_b, tm):
    m = h.shape[0]
    row = lambda w: pl.BlockSpec((tm, w), lambda i: (i, 0))
    return pl.pallas_call(
        _merge_kernel,
        grid=(m // tm,),
        in_specs=[row(D_MODEL)] + [row(WIDTH)] * len(branches)
                 + [_resident(w_gate.shape), _resident(w_branch.shape), _resident(w_out.shape),
                    _resident((1, D_MODEL)), _resident((1, D_MODEL))],
        out_specs=row(D_MODEL),
        out_shape=jax.ShapeDtypeStruct((m, D_MODEL), F32),
        compiler_params=pltpu.CompilerParams(dimension_semantics=("parallel",), vmem_limit_bytes=VMEM_LIMIT),
        name="merge_ln",
    )(h, *branches, w_gate, w_branch, w_out, ln_g.reshape(1, -1), ln_b.reshape(1, -1))


def _ffn_kernel(x_ref, wg_ref, wu_ref, wd_ref, g_ref, b_ref, out_ref):
    x = x_ref[...]
    xb = x.astype(BF16)
    act = jax.nn.silu(_mm(xb, wg_ref[...])) * _mm(xb, wu_ref[...])
    out_ref[...] = _layer_norm(DN_ALPHA * x + _mm(act, wd_ref[...]), g_ref[...], b_ref[...])


def _ffn(x, wg, wu, wd, ln_g, ln_b, tm):
    m = x.shape[0]
    row = pl.BlockSpec((tm, D_MODEL), lambda i: (i, 0))
    return pl.pallas_call(
        _ffn_kernel,
        grid=(m // tm,),
        in_specs=[row, _resident(wg.shape), _resident(wu.shape), _resident(wd.shape),
                  _resident((1, D_MODEL)), _resident((1, D_MODEL))],
        out_specs=row,
        out_shape=jax.ShapeDtypeStruct((m, D_MODEL), F32),
        compiler_params=pltpu.CompilerParams(dimension_semantics=("parallel",), vmem_limit_bytes=VMEM_LIMIT),
        name="ffn_ln",
    )(x, wg, wu, wd, ln_g.reshape(1, -1), ln_b.reshape(1, -1))


def _route_kernel(x_ref, rt_ref, tri_ref, gate_ref, rank_ref, cnt_ref):
    logits = _mm(rt_ref[...], x_ref[...], NT, 2, 2)
    sub = _iota(logits.shape, 0).astype(F32)
    m1 = jnp.max(logits, axis=0, keepdims=True)
    i1 = jnp.min(jnp.where(logits == m1, sub, float(N_EXPERTS)), axis=0, keepdims=True)
    rest = jnp.where(sub == i1, NEG_BIG, logits)
    m2 = jnp.max(rest, axis=0, keepdims=True)
    i2 = jnp.min(jnp.where(rest == m2, sub, float(N_EXPERTS)), axis=0, keepdims=True)
    e2 = jnp.exp(m2 - m1)
    gate_ref[0] = jnp.where(sub == i1, 1.0 / (1.0 + e2), 0.0) + jnp.where(sub == i2, e2 / (1.0 + e2), 0.0)
    chosen = (sub == i1) | (sub == i2)
    sel = jnp.where(chosen, 1.0, 0.0)
    rank_ref[0] = jnp.where(chosen, _mm(sel, tri_ref[...]), -1.0)
    cnt_ref[0] = jnp.broadcast_to(jnp.sum(sel, axis=1, keepdims=True), cnt_ref.shape[1:])


def _moe_kernel(cnt_ref, x_ref, gate_ref, rank_ref, wg_ref, wu_ref, wd_ref, g_ref, b_ref, out_ref,
                xb_scr, xg_scr, yg_scr, *, ts, tb, nsb):
    i, e, f = pl.program_id(0), pl.program_id(1), pl.program_id(2)
    last_f = f == pl.num_programs(2) - 1

    @pl.when((e == 0) & (f == 0))
    def _():
        xb_scr[...] = x_ref[...].astype(BF16)
        out_ref[...] = jnp.zeros_like(out_ref)

    slot = _iota((ts, tb), 0)

    def expert(xg):
        return _mm(jax.nn.silu(_mm(xg, wg_ref[0])) * _mm(xg, wu_ref[0]), wd_ref[0])

    for sb in range(nsb):
        rows = slice(sb * tb, (sb + 1) * tb)
        n_tiles = (cnt_ref[(i * nsb + sb) * N_EXPERTS + e] + (ts - 1)) // ts
        rank_row = rank_ref[sb, pl.ds(e, 1), :]
        gate_row = gate_ref[sb, pl.ds(e, 1), :]

        def pick_of(j, rank_row=rank_row):
            return jnp.where(rank_row == (slot + j * ts).astype(F32), 1.0, 0.0)

        def scatter(pick, y, rows=rows, gate_row=gate_row):
            w_slot = jnp.sum(pick * gate_row, axis=1, keepdims=True)
            out_ref[rows, :] += _mm(pick, y * w_slot, TN)

        @pl.when(n_tiles > 0)
        def _(sb=sb, rows=rows, pick_of=pick_of, scatter=scatter):
            @pl.when(f == 0)
            def _():
                xg_scr[sb] = _mm(pick_of(0), xb_scr[rows, :]).astype(BF16)

            y = expert(xg_scr[sb])

            @pl.when(f == 0)
            def _():
                yg_scr[sb] = y

            @pl.when(f > 0)
            def _():
                yg_scr[sb] += y

            @pl.when(last_f)
            def _():
                scatter(pick_of(0), yg_scr[sb])

        def overflow(j, carry, rows=rows, pick_of=pick_of, scatter=scatter):
            pick = pick_of(j)
            scatter(pick, expert(_mm(pick, xb_scr[rows, :]).astype(BF16)))
            return carry

        lax.fori_loop(1, n_tiles, overflow, 0)

    @pl.when((e == pl.num_programs(1) - 1) & last_f)
    def _():
        out_ref[...] = _layer_norm(DN_ALPHA * x_ref[...] + out_ref[...], g_ref[...], b_ref[...])


def _moe(x, router, wg, wu, wd, ln_g, ln_b, tm):
    m = x.shape[0]
    nblk = m // tm
    dff = wg.shape[2]
    nf = 4
    tf = dff // nf
    ts = min(MOE_TILE, tm)
    nsb = 2 if nblk % 2 == 0 else 1
    assert m % tm == 0 and tf % 128 == 0
    tri = (jnp.arange(tm)[:, None] < jnp.arange(tm)[None, :]).astype(BF16)
    gate, rank, cnt = pl.pallas_call(
        _route_kernel,
        grid=(nblk,),
        in_specs=[pl.BlockSpec((tm, D_MODEL), lambda i: (i, 0)), _resident((N_EXPERTS, D_MODEL)),
                  _resident((tm, tm))],
        out_specs=[pl.BlockSpec((1, N_EXPERTS, tm), lambda i: (i, 0, 0)),
                   pl.BlockSpec((1, N_EXPERTS, tm), lambda i: (i, 0, 0)),
                   pl.BlockSpec((1, N_EXPERTS, 128), lambda i: (i, 0, 0))],
        out_shape=[jax.ShapeDtypeStruct((nblk, N_EXPERTS, tm), F32), jax.ShapeDtypeStruct((nblk, N_EXPERTS, tm), F32),
                   jax.ShapeDtypeStruct((nblk, N_EXPERTS, 128), F32)],
        compiler_params=pltpu.CompilerParams(dimension_semantics=("parallel",), vmem_limit_bytes=VMEM_LIMIT),
        name="moe_route",
    )(x, router.T, tri)
    counts = cnt[:, :, 0].astype(jnp.int32).reshape(-1)
    rows = nsb * tm
    x_in = pl.BlockSpec((rows, D_MODEL), lambda i, e, f, c: (i, 0), pipeline_mode=pl.Buffered(1))
    meta = pl.BlockSpec((nsb, N_EXPERTS, tm), lambda i, e, f, c: (i, 0, 0))
    vec = pl.BlockSpec((1, D_MODEL), lambda i, e, f, c: (0, 0), pipeline_mode=pl.Buffered(1))
    return pl.pallas_call(
        functools.partial(_moe_kernel, ts=ts, tb=tm, nsb=nsb),
        grid_spec=pltpu.PrefetchScalarGridSpec(
            num_scalar_prefetch=1,
            grid=(nblk // nsb, N_EXPERTS, nf),
            in_specs=[x_in, meta, meta,
                      pl.BlockSpec((1, D_MODEL, tf), lambda i, e, f, c: (e, 0, f)),
                      pl.BlockSpec((1, D_MODEL, tf), lambda i, e, f, c: (e, 0, f)),
                      pl.BlockSpec((1, tf, D_MODEL), lambda i, e, f, c: (e, f, 0)),
                      vec, vec],
            out_specs=pl.BlockSpec((rows, D_MODEL), lambda i, e, f, c: (i, 0)),
            scratch_shapes=[pltpu.VMEM((rows, D_MODEL), BF16),
                            pltpu.VMEM((nsb, ts, D_MODEL), BF16), pltpu.VMEM((nsb, ts, D_MODEL), F32)]),
        out_shape=jax.ShapeDtypeStruct((m, D_MODEL), F32),
        compiler_params=pltpu.CompilerParams(dimension_semantics=("parallel", "arbitrary", "arbitrary"),
                                             vmem_limit_bytes=VMEM_LIMIT),
        name="moe_ln",
    )(counts, x, gate, rank, wg, wu, wd, ln_g.reshape(1, -1), ln_b.reshape(1, -1))


def _token_mix(h, pos0, rw_prev, wkv0, pool_buf, ret0, kv_bufs, lw, w_mix, w_gate, w_branch, w_out, ln_g, ln_b):
    b, t, _ = h.shape
    m = b * t
    tm = 512 if m % 512 == 0 else m
    hf = h.reshape(m, D_MODEL)
    p_rw, p_pool, p_ret, p_dil = _project(hf, w_mix, tm)
    p_rw, p_pool = p_rw.reshape(b, t, -1), p_pool.reshape(b, t, -1)
    p_ret, p_dil = p_ret.reshape(b, t, -1), p_dil.reshape(b, t, -1)
    o_a, wkv_new = _rwkv_branch(p_rw, rw_prev, wkv0, lw)
    o_b = _pool_branch(p_pool, pool_buf, pos0, lw)
    pool_new = jnp.concatenate([pool_buf, p_pool], axis=1)[:, -POOL_BUF:]
    o_c, ret_new = _ret_branch(p_ret, ret0, pos0, lw)
    def kv_rows(g, keep):
        lo = (3 * g + 1) * WIDTH
        return p_dil[:, t - keep:, lo:lo + 2 * WIDTH].reshape(b, keep, 2, N_HEADS, HEAD_DIM)

    if kv_bufs is None:
        dil_parts = _dil_prompt(p_dil)
        kv_new = [kv_rows(g, min(win, t)) for g, (win, _) in enumerate(DIL_PATTERNS)]
    else:
        dil_parts = [_dil_step(p_dil, kv_bufs).reshape(m, WIDTH)]
        kv_new = [kv_rows(g, t) for g in range(DIL_GROUPS)]
    branches = [x.reshape(m, WIDTH) for x in (o_a, o_b, o_c)] + dil_parts
    x1 = _merge(hf, branches, w_gate, w_branch, w_out, ln_g, ln_b, tm)
    return x1, (wkv_new, p_rw[:, -1], pool_new, ret_new, kv_new[0], kv_new[1], kv_new[2])


def kernel(x_prompt, x_sample, state_wkv, state_shift, state_pool, state_ret, cache_kv_w128, cache_kv_w512, cache_kv_w2048, w_in, rw_mu, rw_w0, rw_w2, rw_a0, rw_a2, rw_g2, rw_kk, rw_ka, rw_rk, rw_gn_g, rw_gn_b, pool_w, pool_scale, ret_gn_g, ret_gn_b, w_branch, w_out, ln_g, ln_b, ffn_w_gate, ffn_w_up, ffn_w_down, moe_router, moe_w_gate, moe_w_up, moe_w_down):
    hp, hs = x_prompt, x_sample
    bp, tp, _ = hp.shape
    bs, ts, _ = hs.shape
    names = ('rw_mu', 'rw_w0', 'rw_w2', 'rw_a0', 'rw_a2', 'rw_g2', 'rw_kk', 'rw_ka', 'rw_rk', 'rw_gn_g',
             'rw_gn_b', 'pool_w', 'pool_scale', 'ret_gn_g', 'ret_gn_b')
    stacked = (rw_mu, rw_w0, rw_w2, rw_a0, rw_a2, rw_g2, rw_kk, rw_ka, rw_rk, rw_gn_g, rw_gn_b, pool_w,
               pool_scale, ret_gn_g, ret_gn_b)
    new_p = [[] for _ in range(7)]
    new_s = [[] for _ in range(7)]
    zeros = lambda *shape: jnp.zeros(shape, F32)
    for l in range(DEPTH):
        lw = {k: v[l] for k, v in zip(names, stacked)}
        w_mix = w_in[l, :, :COL_GATE].astype(BF16)
        w_gate = w_in[l, :, COL_GATE:].astype(BF16)
        wb, wo = w_branch[l].astype(BF16), w_out[l].astype(BF16)
        xp, st_p = _token_mix(hp, 0, zeros(bp, RW_COLS), zeros(bp, N_HEADS, HEAD_DIM, HEAD_DIM),
                              zeros(bp, POOL_BUF, WIDTH), zeros(bp, N_HEADS, HEAD_DIM, HEAD_DIM), None,
                              lw, w_mix, w_gate, wb, wo, ln_g[l, 0], ln_b[l, 0])
        xs, st_s = _token_mix(hs, 8192, state_shift[l], state_wkv[l], state_pool[l], state_ret[l],
                              (cache_kv_w128[l], cache_kv_w512[l], cache_kv_w2048[l]),
                              lw, w_mix, w_gate, wb, wo, ln_g[l, 0], ln_b[l, 0])
        j = l // 2
        if l % 2 == 0:
            ws = [w[j].astype(BF16) for w in (ffn_w_gate, ffn_w_up, ffn_w_down)]
            xp = _ffn(xp, *ws, ln_g[l, 1], ln_b[l, 1], 256)
            xs = _ffn(xs, *ws, ln_g[l, 1], ln_b[l, 1], xs.shape[0])
        else:
            ws = [w[j].astype(BF16) for w in (moe_w_gate, moe_w_up, moe_w_down)]
            xp = _moe(xp, moe_router[j], *ws, ln_g[l, 1], ln_b[l, 1], 1024)
            xs = _moe(xs, moe_router[j], *ws, ln_g[l, 1], ln_b[l, 1], xs.shape[0])
        hp, hs = xp.reshape(bp, tp, D_MODEL), xs.reshape(bs, ts, D_MODEL)
        for i in range(7):
            new_p[i].append(st_p[i])
            new_s[i].append(st_s[i])
    outs_p = [jnp.stack(x) for x in new_p]
    outs_s = [jnp.stack(x) for x in new_s]
    return (hp, hs, *outs_p, *outs_s)
```

```python
import functools
import math

import jax
import jax.numpy as jnp
from jax import lax
from jax.experimental import pallas as pl
from jax.experimental.pallas import tpu as pltpu

F32 = jnp.float32
BF16 = jnp.bfloat16

D_MODEL = 1024
DEPTH = 2
HEAD_DIM = 64
N_HEADS = 4
WIDTH = N_HEADS * HEAD_DIM
RW_COLS = 1024
RW_GN_EPS = 64e-5
POOL_WINDOWS = (2, 4, 8, 16)
POOL_BUF = 15
RET_CHUNK = 128
ROPE_BASE = 10000.0
DIL_PATTERNS = ((128, 1), (512, 4), (2048, 16))
DIL_GROUPS = 3
DIL_BLOCK = 128
DIL_SPAN = 2048
DIL_COLS = 3 * DIL_GROUPS * WIDTH
N_BRANCH = 4
COL_POOL = RW_COLS
COL_RET = COL_POOL + WIDTH
COL_DIL = COL_RET + 4 * WIDTH
COL_GATE = COL_DIL + DIL_COLS
N_EXPERTS = 8
DN_ALPHA = (2 * DEPTH) ** 0.25
LN_EPS = 1e-5
RW_CHUNK = 64
MOE_TILE = 288
RW_P = 1
RW_BATCH = 4
NEG_BIG = -1e30

NN = (((1,), (0,)), ((), ()))
NT = (((1,), (1,)), ((), ()))
TN = (((0,), (0,)), ((), ()))

VMEM_LIMIT = 56 * 1024 * 1024


def _split(x, n):
    if x.dtype == BF16:
        return [x]
    parts, rem = [], x
    for i in range(n):
        p = rem.astype(BF16)
        parts.append(p)
        if i + 1 < n:
            rem = rem - p.astype(F32)
    return parts


def _mm(a, b, dims=NN, pa=1, pb=1):
    a_parts, b_parts = _split(a, pa), _split(b, pb)
    depth = max(len(a_parts), len(b_parts))
    acc = None
    for i, ai in enumerate(a_parts):
        for j, bj in enumerate(b_parts):
            if i + j < depth:
                t = lax.dot_general(ai, bj, dims, preferred_element_type=F32)
                acc = t if acc is None else acc + t
    return acc


def _iota(shape, axis):
    return lax.broadcasted_iota(jnp.int32, shape, axis)


def _div(x, d):
    assert d & (d - 1) == 0
    return x >> (d.bit_length() - 1)


def _mod(x, d):
    assert d & (d - 1) == 0
    return x & (d - 1)


def _per_head(head, values):
    out = jnp.full(head.shape, values[-1], F32)
    for h in range(len(values) - 2, -1, -1):
        out = jnp.where(head == h, values[h], out)
    return out


def _head_mask(rows_per_head, n_rows):
    return _div(_iota((n_rows, WIDTH), 0), rows_per_head) == _div(_iota((n_rows, WIDTH), 1), HEAD_DIM)


def _stack_heads(x, mask):
    return jnp.where(mask, jnp.concatenate([x] * N_HEADS, axis=0), 0.0)


def _unstack_heads(x_st, mask, c):
    x_st = jnp.where(mask, x_st, 0.0)
    out = x_st[0:c]
    for h in range(1, N_HEADS):
        out = out + x_st[h * c:(h + 1) * c]
    return out


def _ones_bd():
    return _same_head().astype(BF16)


def _same_head():
    return _div(_iota((WIDTH, WIDTH), 0), HEAD_DIM) == _div(_iota((WIDTH, WIDTH), 1), HEAD_DIM)


def _head_norm(x, ones_bd, g, b, eps):
    mu = _mm(x, ones_bd, pa=2) * (1.0 / HEAD_DIM)
    d = x - mu
    var = _mm(d * d, ones_bd, pa=2) * (1.0 / HEAD_DIM)
    return d * lax.rsqrt(var + eps) * g + b


def _layer_norm(x, g, b):
    mu = jnp.mean(x, axis=-1, keepdims=True)
    d = x - mu
    var = jnp.mean(d * d, axis=-1, keepdims=True)
    return d * lax.rsqrt(var + LN_EPS) * g + b


def _resident(shape):
    nd = len(shape)
    return pl.BlockSpec(shape, lambda *_: (0,) * nd, pipeline_mode=pl.Buffered(1))


def _proj_kernel(x_ref, w_ref, rw_ref, pool_ref, ret_ref, dil_ref):
    xb = x_ref[...].astype(BF16)
    for ref, lo, hi in ((rw_ref, 0, COL_POOL), (pool_ref, COL_POOL, COL_RET),
                        (ret_ref, COL_RET, COL_DIL), (dil_ref, COL_DIL, COL_GATE)):
        for s in range(lo, hi, 512):
            e = min(s + 512, hi)
            ref[:, s - lo:e - lo] = _mm(xb, w_ref[:, s:e])


def _project(x, w_mix, tm):
    m = x.shape[0]
    widths = (COL_POOL, WIDTH, 4 * WIDTH, DIL_COLS)
    return pl.pallas_call(
        _proj_kernel,
        grid=(m // tm,),
        in_specs=[pl.BlockSpec((tm, D_MODEL), lambda i: (i, 0)), _resident((D_MODEL, COL_GATE))],
        out_specs=[pl.BlockSpec((tm, w), lambda i: (i, 0)) for w in widths],
        out_shape=[jax.ShapeDtypeStruct((m, w), F32) for w in widths],
        compiler_params=pltpu.CompilerParams(dimension_semantics=("parallel",), vmem_limit_bytes=VMEM_LIMIT),
        name="proj",
    )(x, w_mix)


BNT = (((2,), (2,)), ((0,), (0,)))
BNN = (((2,), (1,)), ((0,), (0,)))
BTN = (((1,), (1,)), ((0,), (0,)))


def _rwkv_kernel(p_ref, prev_ref, s0_ref, mu_ref, w0_ref, w2_ref, a0_ref, a2_ref, g2_ref, kk_ref, ka_ref,
                 rk_ref, gng_ref, gnb_ref, o_ref, st_ref, s_scr, prev_scr, buf_scr, *, c, t_in, nb):
    ci = pl.program_id(1)

    @pl.when(ci == 0)
    def _():
        s_scr[...] = s0_ref[...]
        for b in range(nb):
            prev_scr[b, 0:1, :] = prev_ref[b]

    first = _iota((c, RW_COLS), 0) == 0
    ps, shs = [], []
    for b in range(nb):
        if t_in == c:
            pb_ = p_ref[b]
        else:
            buf_scr[b] = jnp.zeros((c, RW_COLS), F32)
            buf_scr[b, 0:t_in, :] = p_ref[b]
            pb_ = buf_scr[b]
        shs.append(jnp.where(first, prev_scr[b, 0:1, :], pltpu.roll(pb_, 1, 0)))
        prev_scr[b, 0:1, :] = pb_[c - 1:c, :]
        ps.append(pb_)
    p, shifted = jnp.concatenate(ps, axis=0), jnp.concatenate(shs, axis=0)
    m = nb * c
    u = p + (shifted - p) * mu_ref[...]
    r, k, v, ul = u[:, 0:WIDTH], u[:, WIDTH:2 * WIDTH], u[:, 2 * WIDTH:3 * WIDTH], u[:, 3 * WIDTH:]

    w_pre = w0_ref[...] + _mm(jnp.tanh(ul), w2_ref[...])
    logw = -math.exp(-0.5) * jax.nn.sigmoid(w_pre)
    a = jax.nn.sigmoid(a0_ref[...] + _mm(ul, a2_ref[...]))
    g = _mm(jax.nn.sigmoid(ul), g2_ref[...])

    ones_bd = _ones_bd()
    kk = k * kk_ref[...]
    kk = kk * lax.rsqrt(jnp.maximum(_mm(kk * kk, ones_bd, pa=2), 1e-24))
    k_mod = k * (1.0 + (a - 1.0) * ka_ref[...])
    a_vec, b_vec = -kk, kk * a
    if t_in < c:
        live = _mod(_iota((m, WIDTH), 0), c) < t_in
        logw = jnp.where(live, logw, 0.0)
        a_vec, b_vec = jnp.where(live, a_vec, 0.0), jnp.where(live, b_vec, 0.0)
        k_mod, v = jnp.where(live, k_mod, 0.0), jnp.where(live, v, 0.0)

    qi, qj = _iota((m, m), 0), _iota((m, m), 1)
    tri = ((_div(qi, c) == _div(qj, c)) & (qi >= qj)).astype(BF16)
    cum2 = _mm(tri, logw, pb=3)
    seq = lambda x: x.reshape(nb, c, WIDTH)
    cum, lw3 = seq(cum2), seq(logw)
    cum_end = cum[:, c - 1:c, :]
    e_neg, e_end = jnp.exp(-cum), jnp.exp(cum_end - cum)
    a_t = seq(a_vec) * jnp.exp(cum - lw3)
    r_t = seq(r) * jnp.exp(cum)
    b_t, k_t = seq(b_vec) * e_neg, seq(k_mod) * e_neg
    b_e, k_e = seq(b_vec) * e_end, seq(k_mod) * e_end
    v3 = seq(v)

    n = N_HEADS * c
    hm = _head_mask(c, n)
    stack = lambda x: jnp.where(hm, jnp.concatenate([x] * N_HEADS, axis=1), 0.0)
    a_st, r_st, b_st, v_st = stack(a_t), stack(r_t), stack(b_t), stack(v3)
    ri, rj = _iota((n, n), 0), _iota((n, n), 1)
    same = _div(ri, c) == _div(rj, c)
    strict_bd = same & (_mod(ri, c) > _mod(rj, c))
    incl_bd = same & (_mod(ri, c) >= _mod(rj, c))
    ti, tj = _mod(_iota((n, c), 0), c), _iota((n, c), 1)
    strict_st, incl_st = ti > tj, ti >= tj

    mm = functools.partial(_mm, pa=RW_P, pb=RW_P)
    a_ab = jnp.where(strict_bd, mm(a_st, b_st, BNT), 0.0)
    inv = jnp.where(ri == rj, 1.0, 0.0) + a_ab
    pw = a_ab
    for _ in range(int(math.log2(c)) - 1):
        pw = mm(pw, pw, BNN)
        inv = inv + mm(inv, pw, BNN)
    a_ak = jnp.where(strict_st, mm(a_st, k_t, BNT), 0.0)
    z_st = jnp.where(hm, mm(a_ak, v3, BNN), 0.0)
    wu = mm(inv, jnp.concatenate([a_st, z_st], axis=2), BNN)
    w_st, u0_st = wu[:, :, 0:WIDTH], wu[:, :, WIDTH:]

    s0 = s_scr[...]
    u_st = mm(w_st, s0, BNT) + u0_st
    lhs = jnp.concatenate([u_st, v_st], axis=1)
    rhs = jnp.concatenate([stack(b_e), stack(k_e)], axis=1)
    s_scr[...] = s0 * jnp.exp(cum_end) + mm(lhs, rhs, BTN)

    a_rb = jnp.where(incl_bd, mm(r_st, b_st, BNT), 0.0)
    a_rk = jnp.where(incl_st, mm(r_st, k_t, BNT), 0.0)
    o_st = jnp.where(hm, mm(r_st, s0, BNT) + mm(a_rb, u_st, BNN) + mm(a_rk, v3, BNN), 0.0)
    o3 = o_st[:, 0:c]
    for h in range(1, N_HEADS):
        o3 = o3 + o_st[:, h * c:(h + 1) * c]
    o = o3.reshape(m, WIDTH)

    o = _head_norm(o, ones_bd, gng_ref[...], gnb_ref[...], RW_GN_EPS)
    o = o + _mm(r * k_mod * rk_ref[...], ones_bd, pa=2) * v
    o_ref[...] = (o * g).reshape(nb, c, WIDTH)[:, 0:t_in]

    @pl.when(ci == pl.num_programs(1) - 1)
    def _():
        st_ref[...] = s_scr[...]


def _block_diag_heads(s):
    b = s.shape[0]
    eye = jnp.eye(N_HEADS, dtype=s.dtype)
    return jnp.einsum('bhij,hg->bhigj', s, eye).reshape(b, WIDTH, WIDTH)


def _diag_heads(s_bd):
    b = s_bd.shape[0]
    s = s_bd.reshape(b, N_HEADS, HEAD_DIM, N_HEADS, HEAD_DIM)
    return jnp.stack([s[:, h, :, h, :] for h in range(N_HEADS)], axis=1)


def _rwkv_branch(p, p_prev, wkv0, lw):
    b, t, _ = p.shape
    c = RW_CHUNK
    t_in = c if t % c == 0 else t
    nc = t // c if t % c == 0 else 1
    vec = lambda x: x.reshape(1, -1)
    pad_rows = lambda w, lo: jnp.zeros((WIDTH, WIDTH), F32).at[lo:lo + w.shape[0]].set(w).astype(BF16)
    params = [vec(lw['rw_mu']), vec(lw['rw_w0']), pad_rows(lw['rw_w2'], 0), vec(lw['rw_a0']),
              pad_rows(lw['rw_a2'], 64), pad_rows(lw['rw_g2'], 128), vec(lw['rw_kk']), vec(lw['rw_ka']),
              vec(lw['rw_rk']), vec(lw['rw_gn_g']), vec(lw['rw_gn_b'])]
    nb = RW_BATCH
    assert b % nb == 0
    o, st = pl.pallas_call(
        functools.partial(_rwkv_kernel, c=c, t_in=t_in, nb=nb),
        grid=(b // nb, nc),
        in_specs=[pl.BlockSpec((nb, t_in, RW_COLS), lambda i, j: (i, j, 0)),
                  pl.BlockSpec((nb, 1, RW_COLS), lambda i, j: (i, 0, 0)),
                  pl.BlockSpec((nb, WIDTH, WIDTH), lambda i, j: (i, 0, 0))]
                 + [_resident(x.shape) for x in params],
        out_specs=[pl.BlockSpec((nb, t_in, WIDTH), lambda i, j: (i, j, 0)),
                   pl.BlockSpec((nb, WIDTH, WIDTH), lambda i, j: (i, 0, 0))],
        out_shape=[jax.ShapeDtypeStruct((b, t, WIDTH), F32), jax.ShapeDtypeStruct((b, WIDTH, WIDTH), F32)],
        scratch_shapes=[pltpu.VMEM((nb, WIDTH, WIDTH), F32), pltpu.VMEM((nb, 8, RW_COLS), F32),
                        pltpu.VMEM((nb, c, RW_COLS), F32)],
        compiler_params=pltpu.CompilerParams(dimension_semantics=("parallel", "arbitrary"),
                                             vmem_limit_bytes=VMEM_LIMIT),
        name="rwkv7",
    )(p, p_prev.reshape(b, 1, RW_COLS), _block_diag_heads(wkv0), *params)
    return o, _diag_heads(st)


def _pool_kernel(u_ref, buf_ref, w_ref, scale_ref, o_ref, ext_scr, *, c, t_in, pos0):
    ci = pl.program_id(1)

    @pl.when(ci == 0)
    def _():
        ext_scr[0:16, :] = buf_ref[0]

    if t_in < c:
        ext_scr[16:16 + c, :] = jnp.zeros((c, WIDTH), F32)
    ext_scr[16:16 + t_in, :] = u_ref[0]
    x = ext_scr[16:16 + c, :]
    sums, acc, off = [], x, 1
    for win in POOL_WINDOWS:
        while off < win:
            acc = acc + ext_scr[16 - off:16 - off + c, :]
            off += 1
        sums.append(acc)
    pos = pos0 + ci * c + _iota((c, WIDTH), 0)
    grp = _div(_iota((c, WIDTH), 1), HEAD_DIM)
    mean = jnp.zeros((c, WIDTH), F32)
    for gi, win in enumerate(POOL_WINDOWS):
        cnt = jnp.minimum(win, pos + 1).astype(F32)
        mean = jnp.where(grp == gi, sums[gi] / cnt, mean)
    mixed = _mm(mean - x, w_ref[...]) * scale_ref[...]
    o_ref[0] = mixed[0:t_in]
    ext_scr[0:16, :] = ext_scr[c:c + 16, :]


def _pool_branch(u, buf, pos0, lw):
    b, t, _ = u.shape
    c = 512 if t % 512 == 0 else 8
    t_in = c if t % c == 0 else t
    nc = t // c if t % c == 0 else 1
    w_bd = jnp.einsum('gcd,gh->gchd', lw['pool_w'], jnp.eye(N_HEADS, dtype=F32)).reshape(WIDTH, WIDTH).astype(BF16)
    buf16 = jnp.pad(buf, ((0, 0), (1, 0), (0, 0)))
    return pl.pallas_call(
        functools.partial(_pool_kernel, c=c, t_in=t_in, pos0=pos0),
        grid=(b, nc),
        in_specs=[pl.BlockSpec((1, t_in, WIDTH), lambda i, j: (i, j, 0)),
                  pl.BlockSpec((1, 16, WIDTH), lambda i, j: (i, 0, 0)),
                  _resident((WIDTH, WIDTH)), _resident((1, WIDTH))],
        out_specs=pl.BlockSpec((1, t_in, WIDTH), lambda i, j: (i, j, 0)),
        out_shape=jax.ShapeDtypeStruct((b, t, WIDTH), F32),
        scratch_shapes=[pltpu.VMEM((16 + c, WIDTH), F32)],
        compiler_params=pltpu.CompilerParams(dimension_semantics=("parallel", "arbitrary")),
        name="pool",
    )(u, buf16, w_bd, lw['pool_scale'].reshape(1, WIDTH))


def _rot_half(x):
    first = _mod(_iota(x.shape, 1), HEAD_DIM) < (HEAD_DIM // 2)
    return jnp.where(first, pltpu.roll(x, WIDTH - HEAD_DIM // 2, 1), pltpu.roll(x, HEAD_DIM // 2, 1))


RET_LOG_DECAY = tuple(math.log(1.0 - 2.0 ** (-5.0 - h)) for h in range(N_HEADS))
ALIBI_SLOPES = tuple(2.0 ** (-8.0 * (i + 1) / (DIL_GROUPS * N_HEADS)) for i in range(DIL_GROUPS * N_HEADS))


def _ret_kernel(p_ref, cos_ref, sin_ref, s0_ref, gng_ref, gnb_ref, o_ref, st_ref, s_scr, buf_scr,
                dmask_scr, qd_scr, kd_scr, *, c, t_in, nb):
    ci = pl.program_id(1)
    n = N_HEADS * c
    lg = _per_head(_div(_iota((1, WIDTH), 1), HEAD_DIM), RET_LOG_DECAY)

    @pl.when(ci == 0)
    def _():
        s_scr[...] = s0_ref[...]
        idx = _iota((c, WIDTH), 0).astype(F32)
        qd_scr[...] = jnp.exp(lg * (idx + 1.0))
        kd_scr[...] = jnp.exp(lg * (t_in - 1.0 - idx))
        rel = _mod(_iota((n, c), 0), c) - _iota((n, c), 1)
        lg_rows = _per_head(_div(_iota((n, c), 0), c), RET_LOG_DECAY)
        dmask_scr[...] = jnp.where(rel >= 0, jnp.exp(lg_rows * jnp.maximum(rel, 0).astype(F32)), 0.0)

    hm = _head_mask(c, n)
    cos, sin = cos_ref[...], sin_ref[...]
    chunk_decay = jnp.exp(lg * float(t_in))
    same_head, ones_bd = _same_head(), _ones_bd()
    for b in range(nb):
        if t_in == c:
            p = p_ref[b]
        else:
            buf_scr[b] = jnp.zeros((c, 4 * WIDTH), F32)
            buf_scr[b, 0:t_in, :] = p_ref[b]
            p = buf_scr[b]
        q, k, v, g = (p[:, i * WIDTH:(i + 1) * WIDTH] for i in range(4))
        q = q * cos + _rot_half(q) * sin
        k = (k * cos + _rot_half(k) * sin) * HEAD_DIM ** -0.5
        if t_in < c:
            live = _iota((c, WIDTH), 0) < t_in
            k, v = jnp.where(live, k, 0.0), jnp.where(live, v, 0.0)
        inner = _mm(_stack_heads(q, hm), k, NT) * dmask_scr[...]
        s0 = s_scr[b]
        o = _unstack_heads(_mm(inner, v), hm, c) + _mm(q * qd_scr[...], s0)
        s_scr[b] = s0 * chunk_decay + jnp.where(same_head, _mm(k * kd_scr[...], v, TN), 0.0)
        o = jax.nn.silu(g) * _head_norm(o, ones_bd, gng_ref[...], gnb_ref[...], LN_EPS)
        o_ref[b] = o[0:t_in]

    @pl.when(ci == pl.num_programs(1) - 1)
    def _():
        st_ref[...] = s_scr[...]


def _rope_tables(pos0, t, rows):
    half = HEAD_DIM // 2
    inv = ROPE_BASE ** (-jnp.arange(half, dtype=F32) / half)
    ang = (pos0 + jnp.arange(t, dtype=jnp.int32)).astype(F32)[:, None] * inv[None, :]
    cos = jnp.tile(jnp.cos(ang), (1, 2 * N_HEADS))
    sin = jnp.tile(jnp.concatenate([-jnp.sin(ang), jnp.sin(ang)], axis=1), (1, N_HEADS))
    pad = ((0, rows - t), (0, 0))
    return jnp.pad(cos, pad), jnp.pad(sin, pad)


def _ret_branch(p, s0, pos0, lw):
    b, t, _ = p.shape
    c = RET_CHUNK
    t_in = c if t % c == 0 else t
    nc = t // c if t % c == 0 else 1
    cos, sin = _rope_tables(pos0, t, nc * c)
    nb = RW_BATCH
    assert b % nb == 0
    o, st = pl.pallas_call(
        functools.partial(_ret_kernel, c=c, t_in=t_in, nb=nb),
        grid=(b // nb, nc),
        in_specs=[pl.BlockSpec((nb, t_in, 4 * WIDTH), lambda i, j: (i, j, 0)),
                  pl.BlockSpec((c, WIDTH), lambda i, j: (j, 0)),
                  pl.BlockSpec((c, WIDTH), lambda i, j: (j, 0)),
                  pl.BlockSpec((nb, WIDTH, WIDTH), lambda i, j: (i, 0, 0)),
                  _resident((1, WIDTH)), _resident((1, WIDTH))],
        out_specs=[pl.BlockSpec((nb, t_in, WIDTH), lambda i, j: (i, j, 0)),
                   pl.BlockSpec((nb, WIDTH, WIDTH), lambda i, j: (i, 0, 0))],
        out_shape=[jax.ShapeDtypeStruct((b, t, WIDTH), F32), jax.ShapeDtypeStruct((b, WIDTH, WIDTH), F32)],
        scratch_shapes=[pltpu.VMEM((nb, WIDTH, WIDTH), F32), pltpu.VMEM((nb, c, 4 * WIDTH), F32),
                        pltpu.VMEM((N_HEADS * c, c), F32), pltpu.VMEM((c, WIDTH), F32), pltpu.VMEM((c, WIDTH), F32)],
        compiler_params=pltpu.CompilerParams(dimension_semantics=("parallel", "arbitrary")),
        name="retention",
    )(p, cos, sin, _block_diag_heads(s0), lw['ret_gn_g'].reshape(1, WIDTH), lw['ret_gn_b'].reshape(1, WIDTH))
    return o, _diag_heads(st)


def _alibi_slope_rows(group, rows_per_head):
    head = _div(_iota((N_HEADS * rows_per_head, 1), 0), rows_per_head)
    return _per_head(head, ALIBI_SLOPES[group * N_HEADS:(group + 1) * N_HEADS])


def _dil_prompt_kernel(*refs, group, dil, span):
    q_refs, kc_refs, kp_refs, vc_refs, vp_refs = (refs[2 * i:2 * i + 2] for i in range(5))
    o_ref, lse_ref, k_scr, v_scr, o_scr, lse_scr = refs[10:]
    blk = DIL_BLOCK
    tail = blk * dil
    si = pl.program_id(1)
    for half in range(2):
        k_scr[half, 0:tail, :] = kp_refs[half][0]
        k_scr[half, tail:tail + span, :] = kc_refs[half][0]
        v_scr[half, 0:tail, :] = vp_refs[half][0]
        v_scr[half, tail:tail + span, :] = vc_refs[half][0]
    both = lambda ref3, rows: jnp.concatenate([ref3[0, rows, :], ref3[1, rows, :]], axis=1)
    n = N_HEADS * blk
    hm = _head_mask(blk, n)
    ki = _iota((n, 2 * blk), 1)
    steps = blk + _mod(_iota((n, 2 * blk), 0), blk) - ki
    band = (steps >= 0) & (steps <= blk)
    bias = _alibi_slope_rows(group, blk) * (dil * steps).astype(F32)
    for cc in range(span // tail):
        valid = band & ((si > 0) | (ki >= blk)) if cc == 0 else band
        for r in range(dil):
            rows_q = pl.ds(cc * tail + r, blk, stride=dil) if dil > 1 else pl.ds(cc * tail, blk)
            rows_kv = pl.ds(cc * tail + r, 2 * blk, stride=dil) if dil > 1 else pl.ds(cc * tail, 2 * blk)
            q = jnp.concatenate([q_refs[0][0, rows_q, :], q_refs[1][0, rows_q, :]], axis=1)
            q_st = _stack_heads(q, hm)
            s = _mm(q_st, both(k_scr, rows_kv), NT) * HEAD_DIM ** -0.5 - bias
            s = jnp.where(valid, s, NEG_BIG)
            m = jnp.max(s, axis=1, keepdims=True)
            e = jnp.exp(s - m)
            l = jnp.sum(e, axis=1, keepdims=True)
            o_st = _mm(e, both(v_scr, rows_kv)) / l
            lse = m + jnp.log(l)
            o = _unstack_heads(o_st, hm, blk)
            lse = _unstack_heads(jnp.broadcast_to(lse, (n, WIDTH)), hm, blk)
            for half in range(2):
                o_scr[half, rows_q, :] = o[:, half * 128:(half + 1) * 128]
                lse_scr[half, rows_q, :] = lse[:, half * 128:(half + 1) * 128]
    o_ref[0] = jnp.concatenate([o_scr[0], o_scr[1]], axis=1)
    lse_ref[0] = jnp.concatenate([lse_scr[0], lse_scr[1]], axis=1)


def _dil_prompt_group(pd, group):
    b, s, _ = pd.shape
    win, dil = DIL_PATTERNS[group]
    span = DIL_SPAN
    tail = DIL_BLOCK * dil
    assert win // dil == DIL_BLOCK and span % tail == 0 and s % span == 0
    base = group * 3

    def cur(col):
        return [pl.BlockSpec((1, span, 128), lambda i, j, h=h: (i, j, 2 * (base + col) + h)) for h in range(2)]

    def prev(col):
        return [pl.BlockSpec((1, tail, 128),
                             lambda i, j, h=h: (i, jnp.maximum(j * (span // tail) - 1, 0), 2 * (base + col) + h))
                for h in range(2)]

    out_spec = pl.BlockSpec((1, span, WIDTH), lambda i, j: (i, j, 0))
    return pl.pallas_call(
        functools.partial(_dil_prompt_kernel, group=group, dil=dil, span=span),
        grid=(b, s // span),
        in_specs=cur(0) + cur(1) + prev(1) + cur(2) + prev(2),
        out_specs=[out_spec, out_spec],
        out_shape=[jax.ShapeDtypeStruct((b, s, WIDTH), F32)] * 2,
        scratch_shapes=[pltpu.VMEM((2, tail + span, 128), F32)] * 2 + [pltpu.VMEM((2, span, 128), F32)] * 2,
        compiler_params=pltpu.CompilerParams(dimension_semantics=("parallel", "arbitrary"),
                                             vmem_limit_bytes=VMEM_LIMIT),
        name=f"dil_prompt_g{group}",
    )(*([pd] * 10))


def _dil_combine(os_, ls):
    m = jnp.maximum(jnp.maximum(ls[0], ls[1]), ls[2])
    es = [jnp.exp(x - m) for x in ls]
    return (es[0] * os_[0] + es[1] * os_[1] + es[2] * os_[2]) / (es[0] + es[1] + es[2])


def _dil_prompt(pd):
    b, s, _ = pd.shape
    outs = [_dil_prompt_group(pd, g) for g in range(DIL_GROUPS)]
    return [x[0].reshape(b * s, WIDTH) for x in outs] + [x[1].reshape(b * s, WIDTH) for x in outs]


def _dil_step_kernel(pd_ref, c0_ref, c1_ref, c2_ref, o_ref, buf_scr, *, t, tp):
    buf_scr[...] = jnp.zeros_like(buf_scr)
    buf_scr[0:t, :] = pd_ref[0]
    pd = buf_scr[...]
    n = N_HEADS * tp
    hm = _head_mask(tp, n)
    qt = _mod(_iota((n, 1), 0), tp)
    outs, lses = [], []
    for g, cache_ref in enumerate((c0_ref, c1_ref, c2_ref)):
        win, dil = DIL_PATTERNS[g]
        length = cache_ref.shape[1]
        q = pd[:, g * 3 * WIDTH:g * 3 * WIDTH + WIDTH]
        k_new = pd[:, g * 3 * WIDTH + WIDTH:g * 3 * WIDTH + 2 * WIDTH]
        v_new = pd[:, g * 3 * WIDTH + 2 * WIDTH:g * 3 * WIDTH + 3 * WIDTH]
        k_old, v_old = cache_ref[0, :, 0:WIDTH], cache_ref[0, :, WIDTH:2 * WIDTH]
        q_st = _stack_heads(q, hm)
        slope = _alibi_slope_rows(g, tp)
        d_old = length + qt - _iota((n, length), 1)
        ok_old = (_mod(d_old, dil) == 0) & (d_old <= win)
        s_old = _mm(q_st, k_old, NT) * HEAD_DIM ** -0.5 - slope * d_old.astype(F32)
        s_old = jnp.where(ok_old, s_old, NEG_BIG)
        d_new = qt - _iota((n, tp), 1)
        ok_new = (d_new >= 0) & (_mod(d_new, dil) == 0)
        s_new = _mm(q_st, k_new, NT) * HEAD_DIM ** -0.5 - slope * d_new.astype(F32)
        s_new = jnp.where(ok_new, s_new, NEG_BIG)
        m = jnp.maximum(jnp.max(s_old, axis=1, keepdims=True), jnp.max(s_new, axis=1, keepdims=True))
        e_old, e_new = jnp.exp(s_old - m), jnp.exp(s_new - m)
        l = jnp.sum(e_old, axis=1, keepdims=True) + jnp.sum(e_new, axis=1, keepdims=True)
        outs.append((_mm(e_old, v_old) + _mm(e_new, v_new)) / l)
        lses.append(m + jnp.log(l))
    o_ref[0] = _unstack_heads(_dil_combine(outs, lses), hm, tp)[0:t]


def _dil_step(pd, caches):
    b, t, _ = pd.shape
    tp = 8
    flat = [c.reshape(b, c.shape[1], 2 * WIDTH) for c in caches]
    return pl.pallas_call(
        functools.partial(_dil_step_kernel, t=t, tp=tp),
        grid=(b,),
        in_specs=[pl.BlockSpec((1, t, DIL_COLS), lambda i: (i, 0, 0))]
                 + [pl.BlockSpec((1, c.shape[1], 2 * WIDTH), lambda i: (i, 0, 0)) for c in flat],
        out_specs=pl.BlockSpec((1, t, WIDTH), lambda i: (i, 0, 0)),
        out_shape=jax.ShapeDtypeStruct((b, t, WIDTH), F32),
        scratch_shapes=[pltpu.VMEM((tp, DIL_COLS), F32)],
        compiler_params=pltpu.CompilerParams(dimension_semantics=("parallel",), vmem_limit_bytes=VMEM_LIMIT),
        name="dil_step",
    )(pd, *flat)


def _merge_kernel(h_ref, *refs):
    wg_ref, wb_ref, wo_ref, g_ref, b_ref, out_ref = refs[-6:]
    branch_refs = refs[:-6]
    branches = [r[...] for r in branch_refs[:3]]
    if len(branch_refs) == 4:
        branches.append(branch_refs[3][...])
    else:
        branches.append(_dil_combine([r[...] for r in branch_refs[3:6]], [r[...] for r in branch_refs[6:9]]))
    h = h_ref[...]
    hb = h.astype(BF16)
    z = None
    for n, o in enumerate(branches):
        gate = jax.nn.sigmoid(_mm(hb, wg_ref[:, n * D_MODEL:(n + 1) * D_MODEL]))
        term = gate * _mm(o, wb_ref[n])
        z = term if z is None else z + term
    y = _mm(z, wo_ref[...])
    out_ref[...] = _layer_norm(DN_ALPHA * h + y, g_ref[...], b_ref[...])


def _merge(h, branches, w_gate, w_branch, w_out, ln_g, ln_b, tm):
    m = h.shape[0]
    row = lambda w: pl.BlockSpec((tm, w), lambda i: (i, 0))
    return pl.pallas_call(
        _merge_kernel,
        grid=(m // tm,),
        in_specs=[row(D_MODEL)] + [row(WIDTH)] * len(branches)
                 + [_resident(w_gate.shape), _resident(w_branch.shape), _resident(w_out.shape),
                    _resident((1, D_MODEL)), _resident((1, D_MODEL))],
        out_specs=row(D_MODEL),
        out_shape=jax.ShapeDtypeStruct((m, D_MODEL), F32),
        compiler_params=pltpu.CompilerParams(dimension_semantics=("parallel",), vmem_limit_bytes=VMEM_LIMIT),
        name="merge_ln",
    )(h, *branches, w_gate, w_branch, w_out, ln_g.reshape(1, -1), ln_b.reshape(1, -1))


def _ffn_kernel(x_ref, wg_ref, wu_ref, wd_ref, g_ref, b_ref, out_ref):
    x = x_ref[...]
    xb = x.astype(BF16)
    act = jax.nn.silu(_mm(xb, wg_ref[...])) * _mm(xb, wu_ref[...])
    out_ref[...] = _layer_norm(DN_ALPHA * x + _mm(act, wd_ref[...]), g_ref[...], b_ref[...])


def _ffn(x, wg, wu, wd, ln_g, ln_b, tm):
    m = x.shape[0]
    row = pl.BlockSpec((tm, D_MODEL), lambda i: (i, 0))
    return pl.pallas_call(
        _ffn_kernel,
        grid=(m // tm,),
        in_specs=[row, _resident(wg.shape), _resident(wu.shape), _resident(wd.shape),
                  _resident((1, D_MODEL)), _resident((1, D_MODEL))],
        out_specs=row,
        out_shape=jax.ShapeDtypeStruct((m, D_MODEL), F32),
        compiler_params=pltpu.CompilerParams(dimension_semantics=("parallel",), vmem_limit_bytes=VMEM_LIMIT),
        name="ffn_ln",
    )(x, wg, wu, wd, ln_g.reshape(1, -1), ln_b.reshape(1, -1))


def _route_kernel(x_ref, rt_ref, tri_ref, gate_ref, rank_ref, cnt_ref):
    logits = _mm(rt_ref[...], x_ref[...], NT, 2, 2)
    sub = _iota(logits.shape, 0).astype(F32)
    m1 = jnp.max(logits, axis=0, keepdims=True)
    i1 = jnp.min(jnp.where(logits == m1, sub, float(N_EXPERTS)), axis=0, keepdims=True)
    rest = jnp.where(sub == i1, NEG_BIG, logits)
    m2 = jnp.max(rest, axis=0, keepdims=True)
    i2 = jnp.min(jnp.where(rest == m2, sub, float(N_EXPERTS)), axis=0, keepdims=True)
    e2 = jnp.exp(m2 - m1)
    gate_ref[0] = jnp.where(sub == i1, 1.0 / (1.0 + e2), 0.0) + jnp.where(sub == i2, e2 / (1.0 + e2), 0.0)
    chosen = (sub == i1) | (sub == i2)
    sel = jnp.where(chosen, 1.0, 0.0)
    rank_ref[0] = jnp.where(chosen, _mm(sel, tri_ref[...]), -1.0)
    cnt_ref[0] = jnp.broadcast_to(jnp.sum(sel, axis=1, keepdims=True), cnt_ref.shape[1:])


def _moe_kernel(cnt_ref, x_ref, gate_ref, rank_ref, wg_ref, wu_ref, wd_ref, g_ref, b_ref, out_ref,
                xb_scr, xg_scr, yg_scr, *, ts, tb, nsb):
    i, e, f = pl.program_id(0), pl.program_id(1), pl.program_id(2)
    last_f = f == pl.num_programs(2) - 1

    @pl.when((e == 0) & (f == 0))
    def _():
        xb_scr[...] = x_ref[...].astype(BF16)
        out_ref[...] = jnp.zeros_like(out_ref)

    slot = _iota((ts, tb), 0)

    def expert(xg):
        return _mm(jax.nn.silu(_mm(xg, wg_ref[0])) * _mm(xg, wu_ref[0]), wd_ref[0])

    def pick_of(sb, j):
        rank_row = rank_ref[sb, pl.ds(e, 1), :]
        return jnp.where(rank_row == (slot + j * ts).astype(F32), 1.0, 0.0)

    def scatter(sb, pick, y):
        w_slot = jnp.sum(pick * gate_ref[sb, pl.ds(e, 1), :], axis=1, keepdims=True)
        out_ref[sb * tb:(sb + 1) * tb, :] += _mm(pick, y * w_slot, TN)

    @pl.when(f == 0)
    def _():
        for sb in range(nsb):
            xg_scr[sb] = _mm(pick_of(sb, 0), xb_scr[sb * tb:(sb + 1) * tb, :]).astype(BF16)
        yg_scr[...] = jnp.zeros_like(yg_scr)

    for sb in range(nsb):
        yg_scr[sb] += expert(xg_scr[sb])

    @pl.when(last_f)
    def _():
        for sb in range(nsb):
            scatter(sb, pick_of(sb, 0), yg_scr[sb])

    for sb in range(nsb):
        n_tiles = (cnt_ref[(i * nsb + sb) * N_EXPERTS + e] + (ts - 1)) // ts

        def overflow(j, carry, sb=sb):
            pick = pick_of(sb, j)
            scatter(sb, pick, expert(_mm(pick, xb_scr[sb * tb:(sb + 1) * tb, :]).astype(BF16)))
            return carry

        lax.fori_loop(1, n_tiles, overflow, 0)

    @pl.when((e == pl.num_programs(1) - 1) & last_f)
    def _():
        out_ref[...] = _layer_norm(DN_ALPHA * x_ref[...] + out_ref[...], g_ref[...], b_ref[...])


def _moe(x, router, wg, wu, wd, ln_g, ln_b, tm):
    m = x.shape[0]
    nblk = m // tm
    dff = wg.shape[2]
    nf = 4
    tf = dff // nf
    ts = min(MOE_TILE, tm)
    nsb = 2 if nblk % 2 == 0 else 1
    assert m % tm == 0 and tf % 128 == 0
    tri = (jnp.arange(tm)[:, None] < jnp.arange(tm)[None, :]).astype(BF16)
    gate, rank, cnt = pl.pallas_call(
        _route_kernel,
        grid=(nblk,),
        in_specs=[pl.BlockSpec((tm, D_MODEL), lambda i: (i, 0)), _resident((N_EXPERTS, D_MODEL)),
                  _resident((tm, tm))],
        out_specs=[pl.BlockSpec((1, N_EXPERTS, tm), lambda i: (i, 0, 0)),
                   pl.BlockSpec((1, N_EXPERTS, tm), lambda i: (i, 0, 0)),
                   pl.BlockSpec((1, N_EXPERTS, 128), lambda i: (i, 0, 0))],
        out_shape=[jax.ShapeDtypeStruct((nblk, N_EXPERTS, tm), F32), jax.ShapeDtypeStruct((nblk, N_EXPERTS, tm), F32),
                   jax.ShapeDtypeStruct((nblk, N_EXPERTS, 128), F32)],
        compiler_params=pltpu.CompilerParams(dimension_semantics=("parallel",), vmem_limit_bytes=VMEM_LIMIT),
        name="moe_route",
    )(x, router.T, tri)
    counts = cnt[:, :, 0].astype(jnp.int32).reshape(-1)
    rows = nsb * tm
    x_in = pl.BlockSpec((rows, D_MODEL), lambda i, e, f, c: (i, 0), pipeline_mode=pl.Buffered(1))
    meta = pl.BlockSpec((nsb, N_EXPERTS, tm), lambda i, e, f, c: (i, 0, 0))
    vec = pl.BlockSpec((1, D_MODEL), lambda i, e, f, c: (0, 0), pipeline_mode=pl.Buffered(1))
    return pl.pallas_call(
        functools.partial(_moe_kernel, ts=ts, tb=tm, nsb=nsb),
        grid_spec=pltpu.PrefetchScalarGridSpec(
            num_scalar_prefetch=1,
            grid=(nblk // nsb, N_EXPERTS, nf),
            in_specs=[x_in, meta, meta,
                      pl.BlockSpec((1, D_MODEL, tf), lambda i, e, f, c: (e, 0, f)),
                      pl.BlockSpec((1, D_MODEL, tf), lambda i, e, f, c: (e, 0, f)),
                      pl.BlockSpec((1, tf, D_MODEL), lambda i, e, f, c: (e, f, 0)),
                      vec, vec],
            out_specs=pl.BlockSpec((rows, D_MODEL), lambda i, e, f, c: (i, 0)),
            scratch_shapes=[pltpu.VMEM((rows, D_MODEL), BF16),
                            pltpu.VMEM((nsb, ts, D_MODEL), BF16), pltpu.VMEM((nsb, ts, D_MODEL), F32)]),
        out_shape=jax.ShapeDtypeStruct((m, D_MODEL), F32),
        compiler_params=pltpu.CompilerParams(dimension_semantics=("parallel", "arbitrary", "arbitrary"),
                                             vmem_limit_bytes=VMEM_LIMIT),
        name="moe_ln",
    )(counts, x, gate, rank, wg, wu, wd, ln_g.reshape(1, -1), ln_b.reshape(1, -1))


def _token_mix(h, pos0, rw_prev, wkv0, pool_buf, ret0, kv_bufs, lw, w_mix, w_gate, w_branch, w_out, ln_g, ln_b):
    b, t, _ = h.shape
    m = b * t
    tm = 512 if m % 512 == 0 else m
    hf = h.reshape(m, D_MODEL)
    p_rw, p_pool, p_ret, p_dil = _project(hf, w_mix, tm)
    p_rw, p_pool = p_rw.reshape(b, t, -1), p_pool.reshape(b, t, -1)
    p_ret, p_dil = p_ret.reshape(b, t, -1), p_dil.reshape(b, t, -1)
    o_a, wkv_new = _rwkv_branch(p_rw, rw_prev, wkv0, lw)
    o_b = _pool_branch(p_pool, pool_buf, pos0, lw)
    pool_new = jnp.concatenate([pool_buf, p_pool], axis=1)[:, -POOL_BUF:]
    o_c, ret_new = _ret_branch(p_ret, ret0, pos0, lw)
    def kv_rows(g, keep):
        lo = (3 * g + 1) * WIDTH
        return p_dil[:, t - keep:, lo:lo + 2 * WIDTH].reshape(b, keep, 2, N_HEADS, HEAD_DIM)

    if kv_bufs is None:
        dil_parts = _dil_prompt(p_dil)
        kv_new = [kv_rows(g, min(win, t)) for g, (win, _) in enumerate(DIL_PATTERNS)]
    else:
        dil_parts = [_dil_step(p_dil, kv_bufs).reshape(m, WIDTH)]
        kv_new = [kv_rows(g, t) for g in range(DIL_GROUPS)]
    branches = [x.reshape(m, WIDTH) for x in (o_a, o_b, o_c)] + dil_parts
    x1 = _merge(hf, branches, w_gate, w_branch, w_out, ln_g, ln_b, tm)
    return x1, (wkv_new, p_rw[:, -1], pool_new, ret_new, kv_new[0], kv_new[1], kv_new[2])


def kernel(x_prompt, x_sample, state_wkv, state_shift, state_pool, state_ret, cache_kv_w128, cache_kv_w512, cache_kv_w2048, w_in, rw_mu, rw_w0, rw_w2, rw_a0, rw_a2, rw_g2, rw_kk, rw_ka, rw_rk, rw_gn_g, rw_gn_b, pool_w, pool_scale, ret_gn_g, ret_gn_b, w_branch, w_out, ln_g, ln_b, ffn_w_gate, ffn_w_up, ffn_w_down, moe_router, moe_w_gate, moe_w_up, moe_w_down):
    hp, hs = x_prompt, x_sample
    bp, tp, _ = hp.shape
    bs, ts, _ = hs.shape
    names = ('rw_mu', 'rw_w0', 'rw_w2', 'rw_a0', 'rw_a2', 'rw_g2', 'rw_kk', 'rw_ka', 'rw_rk', 'rw_gn_g',
             'rw_gn_b', 'pool_w', 'pool_scale', 'ret_gn_g', 'ret_gn_b')
    stacked = (rw_mu, rw_w0, rw_w2, rw_a0, rw_a2, rw_g2, rw_kk, rw_ka, rw_rk, rw_gn_g, rw_gn_b, pool_w,
               pool_scale, ret_gn_g, ret_gn_b)
    new_p = [[] for _ in range(7)]
    new_s = [[] for _ in range(7)]
    zeros = lambda *shape: jnp.zeros(shape, F32)
    for l in range(DEPTH):
        lw = {k: v[l] for k, v in zip(names, stacked)}
        w_mix = w_in[l, :, :COL_GATE].astype(BF16)
        w_gate = w_in[l, :, COL_GATE:].astype(BF16)
        wb, wo = w_branch[l].astype(BF16), w_out[l].astype(BF16)
        xp, st_p = _token_mix(hp, 0, zeros(bp, RW_COLS), zeros(bp, N_HEADS, HEAD_DIM, HEAD_DIM),
                              zeros(bp, POOL_BUF, WIDTH), zeros(bp, N_HEADS, HEAD_DIM, HEAD_DIM), None,
                              lw, w_mix, w_gate, wb, wo, ln_g[l, 0], ln_b[l, 0])
        xs, st_s = _token_mix(hs, 8192, state_shift[l], state_wkv[l], state_pool[l], state_ret[l],
                              (cache_kv_w128[l], cache_kv_w512[l], cache_kv_w2048[l]),
                              lw, w_mix, w_gate, wb, wo, ln_g[l, 0], ln_b[l, 0])
        j = l // 2
        if l % 2 == 0:
            ws = [w[j].astype(BF16) for w in (ffn_w_gate, ffn_w_up, ffn_w_down)]
            xp = _ffn(xp, *ws, ln_g[l, 1], ln_b[l, 1], 256)
            xs = _ffn(xs, *ws, ln_g[l, 1], ln_b[l, 1], xs.shape[0])
        else:
            ws = [w[j].astype(BF16) for w in (moe_w_gate, moe_w_up, moe_w_down)]
            xp = _moe(xp, moe_router[j], *ws, ln_g[l, 1], ln_b[l, 1], 1024)
            xs = _moe(xs, moe_router[j], *ws, ln_g[l, 1], ln_b[l, 1], xs.shape[0])
        hp, hs = xp.reshape(bp, tp, D_MODEL), xs.reshape(bs, ts, D_MODEL)
        for i in range(7):
            new_p[i].append(st_p[i])
            new_s[i].append(st_s[i])
    outs_p = [jnp.stack(x) for x in new_p]
    outs_s = [jnp.stack(x) for x in new_s]
    return (hp, hs, *outs_p, *outs_s)
```

```python
import functools
import math

import jax
import jax.numpy as jnp
from jax import lax
from jax.experimental import pallas as pl
from jax.experimental.pallas import tpu as pltpu

F32 = jnp.float32
BF16 = jnp.bfloat16

D_MODEL = 1024
DEPTH = 2
HEAD_DIM = 64
N_HEADS = 4
WIDTH = N_HEADS * HEAD_DIM
RW_COLS = 1024
RW_GN_EPS = 64e-5
POOL_WINDOWS = (2, 4, 8, 16)
POOL_BUF = 15
RET_CHUNK = 128
ROPE_BASE = 10000.0
DIL_PATTERNS = ((128, 1), (512, 4), (2048, 16))
DIL_GROUPS = 3
DIL_BLOCK = 128
DIL_SPAN = 2048
DIL_COLS = 3 * DIL_GROUPS * WIDTH
N_BRANCH = 4
COL_POOL = RW_COLS
COL_RET = COL_POOL + WIDTH
COL_DIL = COL_RET + 4 * WIDTH
COL_GATE = COL_DIL + DIL_COLS
N_EXPERTS = 8
DN_ALPHA = (2 * DEPTH) ** 0.25
LN_EPS = 1e-5
RW_CHUNK = 64
MOE_TILE = 288
RW_P = 1
RW_BATCH = 4
NEG_BIG = -1e30

NN = (((1,), (0,)), ((), ()))
NT = (((1,), (1,)), ((), ()))
TN = (((0,), (0,)), ((), ()))

VMEM_LIMIT = 56 * 1024 * 1024


def _split(x, n):
    if x.dtype == BF16:
        return [x]
    parts, rem = [], x
    for i in range(n):
        p = rem.astype(BF16)
        parts.append(p)
        if i + 1 < n:
            rem = rem - p.astype(F32)
    return parts


def _mm(a, b, dims=NN, pa=1, pb=1):
    a_parts, b_parts = _split(a, pa), _split(b, pb)
    depth = max(len(a_parts), len(b_parts))
    acc = None
    for i, ai in enumerate(a_parts):
        for j, bj in enumerate(b_parts):
            if i + j < depth:
                t = lax.dot_general(ai, bj, dims, preferred_element_type=F32)
                acc = t if acc is None else acc + t
    return acc


def _iota(shape, axis):
    return lax.broadcasted_iota(jnp.int32, shape, axis)


def _div(x, d):
    assert d & (d - 1) == 0
    return x >> (d.bit_length() - 1)


def _mod(x, d):
    assert d & (d - 1) == 0
    return x & (d - 1)


def _per_head(head, values):
    out = jnp.full(head.shape, values[-1], F32)
    for h in range(len(values) - 2, -1, -1):
        out = jnp.where(head == h, values[h], out)
    return out


def _head_mask(rows_per_head, n_rows):
    return _div(_iota((n_rows, WIDTH), 0), rows_per_head) == _div(_iota((n_rows, WIDTH), 1), HEAD_DIM)


def _stack_heads(x, mask):
    return jnp.where(mask, jnp.concatenate([x] * N_HEADS, axis=0), 0.0)


def _unstack_heads(x_st, mask, c):
    x_st = jnp.where(mask, x_st, 0.0)
    out = x_st[0:c]
    for h in range(1, N_HEADS):
        out = out + x_st[h * c:(h + 1) * c]
    return out


def _ones_bd():
    return _same_head().astype(BF16)


def _same_head():
    return _div(_iota((WIDTH, WIDTH), 0), HEAD_DIM) == _div(_iota((WIDTH, WIDTH), 1), HEAD_DIM)


def _head_norm(x, ones_bd, g, b, eps):
    mu = _mm(x, ones_bd, pa=2) * (1.0 / HEAD_DIM)
    d = x - mu
    var = _mm(d * d, ones_bd, pa=2) * (1.0 / HEAD_DIM)
    return d * lax.rsqrt(var + eps) * g + b


def _layer_norm(x, g, b):
    mu = jnp.mean(x, axis=-1, keepdims=True)
    d = x - mu
    var = jnp.mean(d * d, axis=-1, keepdims=True)
    return d * lax.rsqrt(var + LN_EPS) * g + b


def _resident(shape):
    nd = len(shape)
    return pl.BlockSpec(shape, lambda *_: (0,) * nd, pipeline_mode=pl.Buffered(1))


def _proj_kernel(x_ref, w_ref, rw_ref, pool_ref, ret_ref, dil_ref):
    xb = x_ref[...].astype(BF16)
    for ref, lo, hi in ((rw_ref, 0, COL_POOL), (pool_ref, COL_POOL, COL_RET),
                        (ret_ref, COL_RET, COL_DIL), (dil_ref, COL_DIL, COL_GATE)):
        for s in range(lo, hi, 512):
            e = min(s + 512, hi)
            ref[:, s - lo:e - lo] = _mm(xb, w_ref[:, s:e])


def _layer_resident(w, l):
    nd = w.ndim - 1
    return pl.BlockSpec((None,) + w.shape[1:], lambda *_: (l,) + (0,) * nd, pipeline_mode=pl.Buffered(1))


def _project(x, w_in_b, l, tm):
    m = x.shape[0]
    widths = (COL_POOL, WIDTH, 4 * WIDTH, DIL_COLS)
    return pl.pallas_call(
        _proj_kernel,
        grid=(m // tm,),
        in_specs=[pl.BlockSpec((tm, D_MODEL), lambda i: (i, 0)), _layer_resident(w_in_b, l)],
        out_specs=[pl.BlockSpec((tm, w), lambda i: (i, 0)) for w in widths],
        out_shape=[jax.ShapeDtypeStruct((m, w), F32) for w in widths],
        compiler_params=pltpu.CompilerParams(dimension_semantics=("parallel",), vmem_limit_bytes=VMEM_LIMIT),
        name="proj",
    )(x, w_in_b)


BNT = (((2,), (2,)), ((0,), (0,)))
BNN = (((2,), (1,)), ((0,), (0,)))
BTN = (((1,), (1,)), ((0,), (0,)))


def _rwkv_kernel(p_ref, prev_ref, s0_ref, mu_ref, w0_ref, w2_ref, a0_ref, a2_ref, g2_ref, kk_ref, ka_ref,
                 rk_ref, gng_ref, gnb_ref, o_ref, st_ref, s_scr, prev_scr, buf_scr, *, c, t_in, nb):
    ci = pl.program_id(1)

    @pl.when(ci == 0)
    def _():
        s_scr[...] = s0_ref[...]
        for b in range(nb):
            prev_scr[b, 0:1, :] = prev_ref[b]

    first = _iota((c, RW_COLS), 0) == 0
    ps, shs = [], []
    for b in range(nb):
        if t_in == c:
            pb_ = p_ref[b]
        else:
            buf_scr[b] = jnp.zeros((c, RW_COLS), F32)
            buf_scr[b, 0:t_in, :] = p_ref[b]
            pb_ = buf_scr[b]
        shs.append(jnp.where(first, prev_scr[b, 0:1, :], pltpu.roll(pb_, 1, 0)))
        prev_scr[b, 0:1, :] = pb_[c - 1:c, :]
        ps.append(pb_)
    p, shifted = jnp.concatenate(ps, axis=0), jnp.concatenate(shs, axis=0)
    m = nb * c
    u = p + (shifted - p) * mu_ref[...]
    r, k, v, ul = u[:, 0:WIDTH], u[:, WIDTH:2 * WIDTH], u[:, 2 * WIDTH:3 * WIDTH], u[:, 3 * WIDTH:]

    w_pre = w0_ref[...] + _mm(jnp.tanh(ul), w2_ref[...])
    logw = -math.exp(-0.5) * jax.nn.sigmoid(w_pre)
    a = jax.nn.sigmoid(a0_ref[...] + _mm(ul, a2_ref[...]))
    g = _mm(jax.nn.sigmoid(ul), g2_ref[...])

    ones_bd = _ones_bd()
    kk = k * kk_ref[...]
    kk = kk * lax.rsqrt(jnp.maximum(_mm(kk * kk, ones_bd, pa=2), 1e-24))
    k_mod = k * (1.0 + (a - 1.0) * ka_ref[...])
    a_vec, b_vec = -kk, kk * a
    if t_in < c:
        live = _mod(_iota((m, WIDTH), 0), c) < t_in
        logw = jnp.where(live, logw, 0.0)
        a_vec, b_vec = jnp.where(live, a_vec, 0.0), jnp.where(live, b_vec, 0.0)
        k_mod, v = jnp.where(live, k_mod, 0.0), jnp.where(live, v, 0.0)

    qi, qj = _iota((m, m), 0), _iota((m, m), 1)
    tri = ((_div(qi, c) == _div(qj, c)) & (qi >= qj)).astype(BF16)
    cum2 = _mm(tri, logw, pb=3)
    seq = lambda x: x.reshape(nb, c, WIDTH)
    cum, lw3 = seq(cum2), seq(logw)
    cum_end = cum[:, c - 1:c, :]
    e_neg, e_end = jnp.exp(-cum), jnp.exp(cum_end - cum)
    a_t = seq(a_vec) * jnp.exp(cum - lw3)
    r_t = seq(r) * jnp.exp(cum)
    b_t, k_t = seq(b_vec) * e_neg, seq(k_mod) * e_neg
    b_e, k_e = seq(b_vec) * e_end, seq(k_mod) * e_end
    v3 = seq(v)

    n = N_HEADS * c
    hm = _head_mask(c, n)
    stack = lambda x: jnp.where(hm, jnp.concatenate([x] * N_HEADS, axis=1), 0.0)
    a_st, r_st, b_st, v_st = stack(a_t), stack(r_t), stack(b_t), stack(v3)
    ri, rj = _iota((n, n), 0), _iota((n, n), 1)
    same = _div(ri, c) == _div(rj, c)
    strict_bd = same & (_mod(ri, c) > _mod(rj, c))
    incl_bd = same & (_mod(ri, c) >= _mod(rj, c))
    ti, tj = _mod(_iota((n, c), 0), c), _iota((n, c), 1)
    strict_st, incl_st = ti > tj, ti >= tj

    mm = functools.partial(_mm, pa=RW_P, pb=RW_P)
    a_ab = jnp.where(strict_bd, mm(a_st, b_st, BNT), 0.0)
    inv = jnp.where(ri == rj, 1.0, 0.0) + a_ab
    pw = a_ab
    for _ in range(int(math.log2(c)) - 1):
        pw = mm(pw, pw, BNN)
        inv = inv + mm(inv, pw, BNN)
    a_ak = jnp.where(strict_st, mm(a_st, k_t, BNT), 0.0)
    z_st = jnp.where(hm, mm(a_ak, v3, BNN), 0.0)
    wu = mm(inv, jnp.concatenate([a_st, z_st], axis=2), BNN)
    w_st, u0_st = wu[:, :, 0:WIDTH], wu[:, :, WIDTH:]

    s0 = s_scr[...]
    u_st = mm(w_st, s0, BNT) + u0_st
    lhs = jnp.concatenate([u_st, v_st], axis=1)
    rhs = jnp.concatenate([stack(b_e), stack(k_e)], axis=1)
    s_scr[...] = s0 * jnp.exp(cum_end) + mm(lhs, rhs, BTN)

    a_rb = jnp.where(incl_bd, mm(r_st, b_st, BNT), 0.0)
    a_rk = jnp.where(incl_st, mm(r_st, k_t, BNT), 0.0)
    o_st = jnp.where(hm, mm(r_st, s0, BNT) + mm(a_rb, u_st, BNN) + mm(a_rk, v3, BNN), 0.0)
    o3 = o_st[:, 0:c]
    for h in range(1, N_HEADS):
        o3 = o3 + o_st[:, h * c:(h + 1) * c]
    o = o3.reshape(m, WIDTH)

    o = _head_norm(o, ones_bd, gng_ref[...], gnb_ref[...], RW_GN_EPS)
    o = o + _mm(r * k_mod * rk_ref[...], ones_bd, pa=2) * v
    o_ref[...] = (o * g).reshape(nb, c, WIDTH)[:, 0:t_in]

    @pl.when(ci == pl.num_programs(1) - 1)
    def _():
        st_ref[...] = s_scr[...]


def _block_diag_heads(s):
    b = s.shape[0]
    eye = jnp.eye(N_HEADS, dtype=s.dtype)
    return jnp.einsum('bhij,hg->bhigj', s, eye).reshape(b, WIDTH, WIDTH)


def _diag_heads(s_bd):
    b = s_bd.shape[0]
    s = s_bd.reshape(b, N_HEADS, HEAD_DIM, N_HEADS, HEAD_DIM)
    return jnp.stack([s[:, h, :, h, :] for h in range(N_HEADS)], axis=1)


def _rwkv_branch(p, p_prev, wkv0, lw):
    b, t, _ = p.shape
    c = RW_CHUNK
    t_in = c if t % c == 0 else t
    nc = t // c if t % c == 0 else 1
    vec = lambda x: x.reshape(1, -1)
    pad_rows = lambda w, lo: jnp.zeros((WIDTH, WIDTH), F32).at[lo:lo + w.shape[0]].set(w).astype(BF16)
    params = [vec(lw['rw_mu']), vec(lw['rw_w0']), pad_rows(lw['rw_w2'], 0), vec(lw['rw_a0']),
              pad_rows(lw['rw_a2'], 64), pad_rows(lw['rw_g2'], 128), vec(lw['rw_kk']), vec(lw['rw_ka']),
              vec(lw['rw_rk']), vec(lw['rw_gn_g']), vec(lw['rw_gn_b'])]
    nb = RW_BATCH
    assert b % nb == 0
    o, st = pl.pallas_call(
        functools.partial(_rwkv_kernel, c=c, t_in=t_in, nb=nb),
        grid=(b // nb, nc),
        in_specs=[pl.BlockSpec((nb, t_in, RW_COLS), lambda i, j: (i, j, 0)),
                  pl.BlockSpec((nb, 1, RW_COLS), lambda i, j: (i, 0, 0)),
                  pl.BlockSpec((nb, WIDTH, WIDTH), lambda i, j: (i, 0, 0))]
                 + [_resident(x.shape) for x in params],
        out_specs=[pl.BlockSpec((nb, t_in, WIDTH), lambda i, j: (i, j, 0)),
                   pl.BlockSpec((nb, WIDTH, WIDTH), lambda i, j: (i, 0, 0))],
        out_shape=[jax.ShapeDtypeStruct((b, t, WIDTH), F32), jax.ShapeDtypeStruct((b, WIDTH, WIDTH), F32)],
        scratch_shapes=[pltpu.VMEM((nb, WIDTH, WIDTH), F32), pltpu.VMEM((nb, 8, RW_COLS), F32),
                        pltpu.VMEM((nb, c, RW_COLS), F32)],
        compiler_params=pltpu.CompilerParams(dimension_semantics=("parallel", "arbitrary"),
                                             vmem_limit_bytes=VMEM_LIMIT),
        name="rwkv7",
    )(p, p_prev.reshape(b, 1, RW_COLS), _block_diag_heads(wkv0), *params)
    return o, _diag_heads(st)


def _pool_kernel(u_ref, buf_ref, w_ref, scale_ref, o_ref, ext_scr, *, c, t_in, pos0):
    ci = pl.program_id(1)

    @pl.when(ci == 0)
    def _():
        ext_scr[0:16, :] = buf_ref[0]

    if t_in < c:
        ext_scr[16:16 + c, :] = jnp.zeros((c, WIDTH), F32)
    ext_scr[16:16 + t_in, :] = u_ref[0]
    x = ext_scr[16:16 + c, :]
    sums, acc, off = [], x, 1
    for win in POOL_WINDOWS:
        while off < win:
            acc = acc + ext_scr[16 - off:16 - off + c, :]
            off += 1
        sums.append(acc)
    pos = pos0 + ci * c + _iota((c, WIDTH), 0)
    grp = _div(_iota((c, WIDTH), 1), HEAD_DIM)
    mean = jnp.zeros((c, WIDTH), F32)
    for gi, win in enumerate(POOL_WINDOWS):
        cnt = jnp.minimum(win, pos + 1).astype(F32)
        mean = jnp.where(grp == gi, sums[gi] / cnt, mean)
    mixed = _mm(mean - x, w_ref[...]) * scale_ref[...]
    o_ref[0] = mixed[0:t_in]
    ext_scr[0:16, :] = ext_scr[c:c + 16, :]


def _pool_branch(u, buf, pos0, lw):
    b, t, _ = u.shape
    c = 512 if t % 512 == 0 else 8
    t_in = c if t % c == 0 else t
    nc = t // c if t % c == 0 else 1
    w_bd = jnp.einsum('gcd,gh->gchd', lw['pool_w'], jnp.eye(N_HEADS, dtype=F32)).reshape(WIDTH, WIDTH).astype(BF16)
    buf16 = jnp.pad(buf, ((0, 0), (1, 0), (0, 0)))
    return pl.pallas_call(
        functools.partial(_pool_kernel, c=c, t_in=t_in, pos0=pos0),
        grid=(b, nc),
        in_specs=[pl.BlockSpec((1, t_in, WIDTH), lambda i, j: (i, j, 0)),
                  pl.BlockSpec((1, 16, WIDTH), lambda i, j: (i, 0, 0)),
                  _resident((WIDTH, WIDTH)), _resident((1, WIDTH))],
        out_specs=pl.BlockSpec((1, t_in, WIDTH), lambda i, j: (i, j, 0)),
        out_shape=jax.ShapeDtypeStruct((b, t, WIDTH), F32),
        scratch_shapes=[pltpu.VMEM((16 + c, WIDTH), F32)],
        compiler_params=pltpu.CompilerParams(dimension_semantics=("parallel", "arbitrary")),
        name="pool",
    )(u, buf16, w_bd, lw['pool_scale'].reshape(1, WIDTH))


def _rot_half(x):
    first = _mod(_iota(x.shape, 1), HEAD_DIM) < (HEAD_DIM // 2)
    return jnp.where(first, pltpu.roll(x, WIDTH - HEAD_DIM // 2, 1), pltpu.roll(x, HEAD_DIM // 2, 1))


RET_LOG_DECAY = tuple(math.log(1.0 - 2.0 ** (-5.0 - h)) for h in range(N_HEADS))
ALIBI_SLOPES = tuple(2.0 ** (-8.0 * (i + 1) / (DIL_GROUPS * N_HEADS)) for i in range(DIL_GROUPS * N_HEADS))


def _ret_kernel(p_ref, cos_ref, sin_ref, s0_ref, gng_ref, gnb_ref, o_ref, st_ref, s_scr, buf_scr,
                dmask_scr, qd_scr, kd_scr, *, c, t_in, nb):
    ci = pl.program_id(1)
    n = N_HEADS * c
    lg = _per_head(_div(_iota((1, WIDTH), 1), HEAD_DIM), RET_LOG_DECAY)

    @pl.when(ci == 0)
    def _():
        s_scr[...] = s0_ref[...]
        idx = _iota((c, WIDTH), 0).astype(F32)
        qd_scr[...] = jnp.exp(lg * (idx + 1.0))
        kd_scr[...] = jnp.exp(lg * (t_in - 1.0 - idx))
        rel = _mod(_iota((n, c), 0), c) - _iota((n, c), 1)
        lg_rows = _per_head(_div(_iota((n, c), 0), c), RET_LOG_DECAY)
        dmask_scr[...] = jnp.where(rel >= 0, jnp.exp(lg_rows * jnp.maximum(rel, 0).astype(F32)), 0.0)

    hm = _head_mask(c, n)
    cos, sin = cos_ref[...], sin_ref[...]
    chunk_decay = jnp.exp(lg * float(t_in))
    same_head, ones_bd = _same_head(), _ones_bd()
    for b in range(nb):
        if t_in == c:
            p = p_ref[b]
        else:
            buf_scr[b] = jnp.zeros((c, 4 * WIDTH), F32)
            buf_scr[b, 0:t_in, :] = p_ref[b]
            p = buf_scr[b]
        q, k, v, g = (p[:, i * WIDTH:(i + 1) * WIDTH] for i in range(4))
        q = q * cos + _rot_half(q) * sin
        k = (k * cos + _rot_half(k) * sin) * HEAD_DIM ** -0.5
        if t_in < c:
            live = _iota((c, WIDTH), 0) < t_in
            k, v = jnp.where(live, k, 0.0), jnp.where(live, v, 0.0)
        inner = _mm(_stack_heads(q, hm), k, NT) * dmask_scr[...]
        s0 = s_scr[b]
        o = _unstack_heads(_mm(inner, v), hm, c) + _mm(q * qd_scr[...], s0)
        s_scr[b] = s0 * chunk_decay + jnp.where(same_head, _mm(k * kd_scr[...], v, TN), 0.0)
        o = jax.nn.silu(g) * _head_norm(o, ones_bd, gng_ref[...], gnb_ref[...], LN_EPS)
        o_ref[b] = o[0:t_in]

    @pl.when(ci == pl.num_programs(1) - 1)
    def _():
        st_ref[...] = s_scr[...]


def _rope_tables(pos0, t, rows):
    half = HEAD_DIM // 2
    inv = ROPE_BASE ** (-jnp.arange(half, dtype=F32) / half)
    ang = (pos0 + jnp.arange(t, dtype=jnp.int32)).astype(F32)[:, None] * inv[None, :]
    cos = jnp.tile(jnp.cos(ang), (1, 2 * N_HEADS))
    sin = jnp.tile(jnp.concatenate([-jnp.sin(ang), jnp.sin(ang)], axis=1), (1, N_HEADS))
    pad = ((0, rows - t), (0, 0))
    return jnp.pad(cos, pad), jnp.pad(sin, pad)


def _ret_branch(p, s0, pos0, lw):
    b, t, _ = p.shape
    c = RET_CHUNK
    t_in = c if t % c == 0 else t
    nc = t // c if t % c == 0 else 1
    cos, sin = _rope_tables(pos0, t, nc * c)
    nb = RW_BATCH
    assert b % nb == 0
    o, st = pl.pallas_call(
        functools.partial(_ret_kernel, c=c, t_in=t_in, nb=nb),
        grid=(b // nb, nc),
        in_specs=[pl.BlockSpec((nb, t_in, 4 * WIDTH), lambda i, j: (i, j, 0)),
                  pl.BlockSpec((c, WIDTH), lambda i, j: (j, 0)),
                  pl.BlockSpec((c, WIDTH), lambda i, j: (j, 0)),
                  pl.BlockSpec((nb, WIDTH, WIDTH), lambda i, j: (i, 0, 0)),
                  _resident((1, WIDTH)), _resident((1, WIDTH))],
        out_specs=[pl.BlockSpec((nb, t_in, WIDTH), lambda i, j: (i, j, 0)),
                   pl.BlockSpec((nb, WIDTH, WIDTH), lambda i, j: (i, 0, 0))],
        out_shape=[jax.ShapeDtypeStruct((b, t, WIDTH), F32), jax.ShapeDtypeStruct((b, WIDTH, WIDTH), F32)],
        scratch_shapes=[pltpu.VMEM((nb, WIDTH, WIDTH), F32), pltpu.VMEM((nb, c, 4 * WIDTH), F32),
                        pltpu.VMEM((N_HEADS * c, c), F32), pltpu.VMEM((c, WIDTH), F32), pltpu.VMEM((c, WIDTH), F32)],
        compiler_params=pltpu.CompilerParams(dimension_semantics=("parallel", "arbitrary")),
        name="retention",
    )(p, cos, sin, _block_diag_heads(s0), lw['ret_gn_g'].reshape(1, WIDTH), lw['ret_gn_b'].reshape(1, WIDTH))
    return o, _diag_heads(st)


def _alibi_slope_rows(group, rows_per_head):
    head = _div(_iota((N_HEADS * rows_per_head, 1), 0), rows_per_head)
    return _per_head(head, ALIBI_SLOPES[group * N_HEADS:(group + 1) * N_HEADS])


def _dil_prompt_kernel(*refs, group, dil, span):
    q_refs, kc_refs, kp_refs, vc_refs, vp_refs = (refs[2 * i:2 * i + 2] for i in range(5))
    o_ref, lse_ref, k_scr, v_scr, o_scr, lse_scr = refs[10:]
    blk = DIL_BLOCK
    tail = blk * dil
    si = pl.program_id(1)
    for half in range(2):
        k_scr[half, 0:tail, :] = kp_refs[half][0]
        k_scr[half, tail:tail + span, :] = kc_refs[half][0]
        v_scr[half, 0:tail, :] = vp_refs[half][0]
        v_scr[half, tail:tail + span, :] = vc_refs[half][0]
    ki = _iota((blk, 2 * blk), 1)
    steps = blk + _iota((blk, 2 * blk), 0) - ki
    band = (steps >= 0) & (steps <= blk)
    masks = [jnp.where(band, (-ALIBI_SLOPES[group * N_HEADS + h] * dil) * steps.astype(F32), NEG_BIG)
             for h in range(N_HEADS)]
    sub_head = _div(_iota((blk, 128), 1), HEAD_DIM)
    for cc in range(span // tail):
        has_prev = (si > 0) | (ki >= blk)
        cc_masks = [jnp.where(has_prev, x, NEG_BIG) for x in masks] if cc == 0 else masks
        for r in range(dil):
            rows_q = pl.ds(cc * tail + r, blk, stride=dil) if dil > 1 else pl.ds(cc * tail, blk)
            rows_kv = pl.ds(cc * tail + r, 2 * blk, stride=dil) if dil > 1 else pl.ds(cc * tail, 2 * blk)
            for half in range(2):
                q2 = q_refs[half][0, rows_q, :] * HEAD_DIM ** -0.5
                k2 = k_scr[half, rows_kv, :].astype(BF16)
                v2 = v_scr[half, rows_kv, :].astype(BF16)
                o2 = lse2 = None
                for sub in range(2):
                    mine = sub_head == sub
                    s = _mm(jnp.where(mine, q2, 0.0), k2, NT) + cc_masks[2 * half + sub]
                    m = jnp.max(s, axis=1, keepdims=True)
                    e = jnp.exp(s - m)
                    l = jnp.sum(e, axis=1, keepdims=True)
                    o_h = _mm(e, v2) / l
                    lse_h = jnp.broadcast_to(m + jnp.log(l), (blk, 128))
                    o2 = o_h if o2 is None else jnp.where(mine, o_h, o2)
                    lse2 = lse_h if lse2 is None else jnp.where(mine, lse_h, lse2)
                o_scr[half, rows_q, :] = o2
                lse_scr[half, rows_q, :] = lse2
    o_ref[0] = jnp.concatenate([o_scr[0], o_scr[1]], axis=1)
    lse_ref[0] = jnp.concatenate([lse_scr[0], lse_scr[1]], axis=1)


def _dil_prompt_group(pd, group):
    b, s, _ = pd.shape
    win, dil = DIL_PATTERNS[group]
    span = DIL_SPAN
    tail = DIL_BLOCK * dil
    assert win // dil == DIL_BLOCK and span % tail == 0 and s % span == 0
    base = group * 3

    def cur(col):
        return [pl.BlockSpec((1, span, 128), lambda i, j, h=h: (i, j, 2 * (base + col) + h)) for h in range(2)]

    def prev(col):
        return [pl.BlockSpec((1, tail, 128),
                             lambda i, j, h=h: (i, jnp.maximum(j * (span // tail) - 1, 0), 2 * (base + col) + h))
                for h in range(2)]

    out_spec = pl.BlockSpec((1, span, WIDTH), lambda i, j: (i, j, 0))
    return pl.pallas_call(
        functools.partial(_dil_prompt_kernel, group=group, dil=dil, span=span),
        grid=(b, s // span),
        in_specs=cur(0) + cur(1) + prev(1) + cur(2) + prev(2),
        out_specs=[out_spec, out_spec],
        out_shape=[jax.ShapeDtypeStruct((b, s, WIDTH), F32)] * 2,
        scratch_shapes=[pltpu.VMEM((2, tail + span, 128), F32)] * 2 + [pltpu.VMEM((2, span, 128), F32)] * 2,
        compiler_params=pltpu.CompilerParams(dimension_semantics=("parallel", "arbitrary"),
                                             vmem_limit_bytes=VMEM_LIMIT),
        name=f"dil_prompt_g{group}",
    )(*([pd] * 10))


def _dil_combine(os_, ls):
    m = jnp.maximum(jnp.maximum(ls[0], ls[1]), ls[2])
    es = [jnp.exp(x - m) for x in ls]
    return (es[0] * os_[0] + es[1] * os_[1] + es[2] * os_[2]) / (es[0] + es[1] + es[2])


def _dil_prompt(pd):
    b, s, _ = pd.shape
    outs = [_dil_prompt_group(pd, g) for g in range(DIL_GROUPS)]
    return [x[0].reshape(b * s, WIDTH) for x in outs] + [x[1].reshape(b * s, WIDTH) for x in outs]


def _dil_step_kernel(pd_ref, c0_ref, c1_ref, c2_ref, o_ref, buf_scr, *, t, tp):
    buf_scr[...] = jnp.zeros_like(buf_scr)
    buf_scr[0:t, :] = pd_ref[0]
    pd = buf_scr[...]
    n = N_HEADS * tp
    hm = _head_mask(tp, n)
    qt = _mod(_iota((n, 1), 0), tp)
    outs, lses = [], []
    for g, cache_ref in enumerate((c0_ref, c1_ref, c2_ref)):
        win, dil = DIL_PATTERNS[g]
        length = cache_ref.shape[1]
        q = pd[:, g * 3 * WIDTH:g * 3 * WIDTH + WIDTH]
        k_new = pd[:, g * 3 * WIDTH + WIDTH:g * 3 * WIDTH + 2 * WIDTH]
        v_new = pd[:, g * 3 * WIDTH + 2 * WIDTH:g * 3 * WIDTH + 3 * WIDTH]
        k_old, v_old = cache_ref[0, :, 0:WIDTH], cache_ref[0, :, WIDTH:2 * WIDTH]
        q_st = _stack_heads(q, hm)
        slope = _alibi_slope_rows(g, tp)
        d_old = length + qt - _iota((n, length), 1)
        ok_old = (_mod(d_old, dil) == 0) & (d_old <= win)
        s_old = _mm(q_st, k_old, NT) * HEAD_DIM ** -0.5 - slope * d_old.astype(F32)
        s_old = jnp.where(ok_old, s_old, NEG_BIG)
        d_new = qt - _iota((n, tp), 1)
        ok_new = (d_new >= 0) & (_mod(d_new, dil) == 0)
        s_new = _mm(q_st, k_new, NT) * HEAD_DIM ** -0.5 - slope * d_new.astype(F32)
        s_new = jnp.where(ok_new, s_new, NEG_BIG)
        m = jnp.maximum(jnp.max(s_old, axis=1, keepdims=True), jnp.max(s_new, axis=1, keepdims=True))
        e_old, e_new = jnp.exp(s_old - m), jnp.exp(s_new - m)
        l = jnp.sum(e_old, axis=1, keepdims=True) + jnp.sum(e_new, axis=1, keepdims=True)
        outs.append((_mm(e_old, v_old) + _mm(e_new, v_new)) / l)
        lses.append(m + jnp.log(l))
    o_ref[0] = _unstack_heads(_dil_combine(outs, lses), hm, tp)[0:t]


def _dil_step(pd, caches):
    b, t, _ = pd.shape
    tp = 8
    flat = [c.reshape(b, c.shape[1], 2 * WIDTH) for c in caches]
    return pl.pallas_call(
        functools.partial(_dil_step_kernel, t=t, tp=tp),
        grid=(b,),
        in_specs=[pl.BlockSpec((1, t, DIL_COLS), lambda i: (i, 0, 0))]
                 + [pl.BlockSpec((1, c.shape[1], 2 * WIDTH), lambda i: (i, 0, 0)) for c in flat],
        out_specs=pl.BlockSpec((1, t, WIDTH), lambda i: (i, 0, 0)),
        out_shape=jax.ShapeDtypeStruct((b, t, WIDTH), F32),
        scratch_shapes=[pltpu.VMEM((tp, DIL_COLS), F32)],
        compiler_params=pltpu.CompilerParams(dimension_semantics=("parallel",), vmem_limit_bytes=VMEM_LIMIT),
        name="dil_step",
    )(pd, *flat)


def _merge_kernel(h_ref, *refs):
    wg_ref, wb_ref, wo_ref, g_ref, b_ref, out_ref = refs[-6:]
    branch_refs = refs[:-6]
    branches = [r[...] for r in branch_refs[:3]]
    if len(branch_refs) == 4:
        branches.append(branch_refs[3][...])
    else:
        branches.append(_dil_combine([r[...] for r in branch_refs[3:6]], [r[...] for r in branch_refs[6:9]]))
    h = h_ref[...]
    hb = h.astype(BF16)
    z = None
    for n, o in enumerate(branches):
        gate = jax.nn.sigmoid(_mm(hb, wg_ref[:, COL_GATE + n * D_MODEL:COL_GATE + (n + 1) * D_MODEL]))
        term = gate * _mm(o, wb_ref[n])
        z = term if z is None else z + term
    y = _mm(z, wo_ref[...])
    out_ref[...] = _layer_norm(DN_ALPHA * h + y, g_ref[...], b_ref[...])


def _merge(h, branches, w_in_b, w_branch_b, w_out_b, l, ln_g, ln_b, tm):
    m = h.shape[0]
    row = lambda w: pl.BlockSpec((tm, w), lambda i: (i, 0))
    return pl.pallas_call(
        _merge_kernel,
        grid=(m // tm,),
        in_specs=[row(D_MODEL)] + [row(WIDTH)] * len(branches)
                 + [_layer_resident(w_in_b, l), _layer_resident(w_branch_b, l), _layer_resident(w_out_b, l),
                    _resident((1, D_MODEL)), _resident((1, D_MODEL))],
        out_specs=row(D_MODEL),
        out_shape=jax.ShapeDtypeStruct((m, D_MODEL), F32),
        compiler_params=pltpu.CompilerParams(dimension_semantics=("parallel",), vmem_limit_bytes=VMEM_LIMIT),
        name="merge_ln",
    )(h, *branches, w_in_b, w_branch_b, w_out_b, ln_g.reshape(1, -1), ln_b.reshape(1, -1))


def _ffn_kernel(x_ref, wg_ref, wu_ref, wd_ref, g_ref, b_ref, out_ref):
    x = x_ref[...]
    xb = x.astype(BF16)
    act = jax.nn.silu(_mm(xb, wg_ref[...])) * _mm(xb, wu_ref[...])
    out_ref[...] = _layer_norm(DN_ALPHA * x + _mm(act, wd_ref[...]), g_ref[...], b_ref[...])


def _ffn(x, wg, wu, wd, ln_g, ln_b, tm):
    m = x.shape[0]
    row = pl.BlockSpec((tm, D_MODEL), lambda i: (i, 0))
    return pl.pallas_call(
        _ffn_kernel,
        grid=(m // tm,),
        in_specs=[row, _resident(wg.shape), _resident(wu.shape), _resident(wd.shape),
                  _resident((1, D_MODEL)), _resident((1, D_MODEL))],
        out_specs=row,
        out_shape=jax.ShapeDtypeStruct((m, D_MODEL), F32),
        compiler_params=pltpu.CompilerParams(dimension_semantics=("parallel",), vmem_limit_bytes=VMEM_LIMIT),
        name="ffn_ln",
    )(x, wg, wu, wd, ln_g.reshape(1, -1), ln_b.reshape(1, -1))


def _route_kernel(x_ref, rt_ref, tri_ref, gate_ref, rank_ref, cnt_ref):
    logits = _mm(rt_ref[...], x_ref[...], NT, 2, 2)
    sub = _iota(logits.shape, 0).astype(F32)
    m1 = jnp.max(logits, axis=0, keepdims=True)
    i1 = jnp.min(jnp.where(logits == m1, sub, float(N_EXPERTS)), axis=0, keepdims=True)
    rest = jnp.where(sub == i1, NEG_BIG, logits)
    m2 = jnp.max(rest, axis=0, keepdims=True)
    i2 = jnp.min(jnp.where(rest == m2, sub, float(N_EXPERTS)), axis=0, keepdims=True)
    e2 = jnp.exp(m2 - m1)
    gate_ref[0] = jnp.where(sub == i1, 1.0 / (1.0 + e2), 0.0) + jnp.where(sub == i2, e2 / (1.0 + e2), 0.0)
    chosen = (sub == i1) | (sub == i2)
    sel = jnp.where(chosen, 1.0, 0.0)
    rank_ref[0] = jnp.where(chosen, _mm(sel, tri_ref[...]), -1.0)
    cnt_ref[0] = jnp.broadcast_to(jnp.sum(sel, axis=1, keepdims=True), cnt_ref.shape[1:])


def _moe_kernel(cnt_ref, x_ref, gate_ref, rank_ref, wg_ref, wu_ref, wd_ref, g_ref, b_ref, out_ref,
                xb_scr, xg_scr, yg_scr, *, ts, tb, nsb):
    i, e, f = pl.program_id(0), pl.program_id(1), pl.program_id(2)
    last_f = f == pl.num_programs(2) - 1

    @pl.when((e == 0) & (f == 0))
    def _():
        xb_scr[...] = x_ref[...].astype(BF16)
        out_ref[...] = jnp.zeros_like(out_ref)

    slot = _iota((ts, tb), 0)

    def expert(xg):
        return _mm(jax.nn.silu(_mm(xg, wg_ref[0])) * _mm(xg, wu_ref[0]), wd_ref[0])

    def pick_of(sb, j):
        rank_row = rank_ref[sb, pl.ds(e, 1), :]
        return jnp.where(rank_row == (slot + j * ts).astype(F32), 1.0, 0.0)

    def scatter(sb, pick, y):
        w_slot = jnp.sum(pick * gate_ref[sb, pl.ds(e, 1), :], axis=1, keepdims=True)
        out_ref[sb * tb:(sb + 1) * tb, :] += _mm(pick, y * w_slot, TN)

    @pl.when(f == 0)
    def _():
        for sb in range(nsb):
            xg_scr[sb] = _mm(pick_of(sb, 0), xb_scr[sb * tb:(sb + 1) * tb, :]).astype(BF16)
        yg_scr[...] = jnp.zeros_like(yg_scr)

    for sb in range(nsb):
        yg_scr[sb] += expert(xg_scr[sb])

    @pl.when(last_f)
    def _():
        for sb in range(nsb):
            scatter(sb, pick_of(sb, 0), yg_scr[sb])

    for sb in range(nsb):
        n_tiles = (cnt_ref[(i * nsb + sb) * N_EXPERTS + e] + (ts - 1)) // ts

        def overflow(j, carry, sb=sb):
            pick = pick_of(sb, j)
            scatter(sb, pick, expert(_mm(pick, xb_scr[sb * tb:(sb + 1) * tb, :]).astype(BF16)))
            return carry

        lax.fori_loop(1, n_tiles, overflow, 0)

    @pl.when((e == pl.num_programs(1) - 1) & last_f)
    def _():
        out_ref[...] = _layer_norm(DN_ALPHA * x_ref[...] + out_ref[...], g_ref[...], b_ref[...])


def _moe(x, router, wg, wu, wd, ln_g, ln_b, tm):
    m = x.shape[0]
    nblk = m // tm
    dff = wg.shape[2]
    nf = 4
    tf = dff // nf
    ts = min(MOE_TILE, tm)
    nsb = 2 if nblk % 2 == 0 else 1
    assert m % tm == 0 and tf % 128 == 0
    tri = (jnp.arange(tm)[:, None] < jnp.arange(tm)[None, :]).astype(BF16)
    gate, rank, cnt = pl.pallas_call(
        _route_kernel,
        grid=(nblk,),
        in_specs=[pl.BlockSpec((tm, D_MODEL), lambda i: (i, 0)), _resident((N_EXPERTS, D_MODEL)),
                  _resident((tm, tm))],
        out_specs=[pl.BlockSpec((1, N_EXPERTS, tm), lambda i: (i, 0, 0)),
                   pl.BlockSpec((1, N_EXPERTS, tm), lambda i: (i, 0, 0)),
                   pl.BlockSpec((1, N_EXPERTS, 128), lambda i: (i, 0, 0))],
        out_shape=[jax.ShapeDtypeStruct((nblk, N_EXPERTS, tm), F32), jax.ShapeDtypeStruct((nblk, N_EXPERTS, tm), F32),
                   jax.ShapeDtypeStruct((nblk, N_EXPERTS, 128), F32)],
        compiler_params=pltpu.CompilerParams(dimension_semantics=("parallel",), vmem_limit_bytes=VMEM_LIMIT),
        name="moe_route",
    )(x, router.T, tri)
    counts = cnt[:, :, 0].astype(jnp.int32).reshape(-1)
    rows = nsb * tm
    x_in = pl.BlockSpec((rows, D_MODEL), lambda i, e, f, c: (i, 0), pipeline_mode=pl.Buffered(1))
    meta = pl.BlockSpec((nsb, N_EXPERTS, tm), lambda i, e, f, c: (i, 0, 0))
    vec = pl.BlockSpec((1, D_MODEL), lambda i, e, f, c: (0, 0), pipeline_mode=pl.Buffered(1))
    return pl.pallas_call(
        functools.partial(_moe_kernel, ts=ts, tb=tm, nsb=nsb),
        grid_spec=pltpu.PrefetchScalarGridSpec(
            num_scalar_prefetch=1,
            grid=(nblk // nsb, N_EXPERTS, nf),
            in_specs=[x_in, meta, meta,
                      pl.BlockSpec((1, D_MODEL, tf), lambda i, e, f, c: (e, 0, f)),
                      pl.BlockSpec((1, D_MODEL, tf), lambda i, e, f, c: (e, 0, f)),
                      pl.BlockSpec((1, tf, D_MODEL), lambda i, e, f, c: (e, f, 0)),
                      vec, vec],
            out_specs=pl.BlockSpec((rows, D_MODEL), lambda i, e, f, c: (i, 0)),
            scratch_shapes=[pltpu.VMEM((rows, D_MODEL), BF16),
                            pltpu.VMEM((nsb, ts, D_MODEL), BF16), pltpu.VMEM((nsb, ts, D_MODEL), F32)]),
        out_shape=jax.ShapeDtypeStruct((m, D_MODEL), F32),
        compiler_params=pltpu.CompilerParams(dimension_semantics=("parallel", "arbitrary", "arbitrary"),
                                             vmem_limit_bytes=VMEM_LIMIT),
        name="moe_ln",
    )(counts, x, gate, rank, wg, wu, wd, ln_g.reshape(1, -1), ln_b.reshape(1, -1))


def _token_mix(h, pos0, rw_prev, wkv0, pool_buf, ret0, kv_bufs, lw, l, w_in_b, w_branch_b, w_out_b, ln_g, ln_b):
    b, t, _ = h.shape
    m = b * t
    tm = 512 if m % 512 == 0 else m
    hf = h.reshape(m, D_MODEL)
    p_rw, p_pool, p_ret, p_dil = _project(hf, w_in_b, l, tm)
    p_rw, p_pool = p_rw.reshape(b, t, -1), p_pool.reshape(b, t, -1)
    p_ret, p_dil = p_ret.reshape(b, t, -1), p_dil.reshape(b, t, -1)
    o_a, wkv_new = _rwkv_branch(p_rw, rw_prev, wkv0, lw)
    o_b = _pool_branch(p_pool, pool_buf, pos0, lw)
    pool_new = jnp.concatenate([pool_buf, p_pool], axis=1)[:, -POOL_BUF:]
    o_c, ret_new = _ret_branch(p_ret, ret0, pos0, lw)
    def kv_rows(g, keep):
        lo = (3 * g + 1) * WIDTH
        return p_dil[:, t - keep:, lo:lo + 2 * WIDTH].reshape(b, keep, 2, N_HEADS, HEAD_DIM)

    if kv_bufs is None:
        dil_parts = _dil_prompt(p_dil)
        kv_new = [kv_rows(g, min(win, t)) for g, (win, _) in enumerate(DIL_PATTERNS)]
    else:
        dil_parts = [_dil_step(p_dil, kv_bufs).reshape(m, WIDTH)]
        kv_new = [kv_rows(g, t) for g in range(DIL_GROUPS)]
    branches = [x.reshape(m, WIDTH) for x in (o_a, o_b, o_c)] + dil_parts
    x1 = _merge(hf, branches, w_in_b, w_branch_b, w_out_b, l, ln_g, ln_b, tm)
    return x1, (wkv_new, p_rw[:, -1], pool_new, ret_new, kv_new[0], kv_new[1], kv_new[2])


def kernel(x_prompt, x_sample, state_wkv, state_shift, state_pool, state_ret, cache_kv_w128, cache_kv_w512, cache_kv_w2048, w_in, rw_mu, rw_w0, rw_w2, rw_a0, rw_a2, rw_g2, rw_kk, rw_ka, rw_rk, rw_gn_g, rw_gn_b, pool_w, pool_scale, ret_gn_g, ret_gn_b, w_branch, w_out, ln_g, ln_b, ffn_w_gate, ffn_w_up, ffn_w_down, moe_router, moe_w_gate, moe_w_up, moe_w_down):
    hp, hs = x_prompt, x_sample
    bp, tp, _ = hp.shape
    bs, ts, _ = hs.shape
    names = ('rw_mu', 'rw_w0', 'rw_w2', 'rw_a0', 'rw_a2', 'rw_g2', 'rw_kk', 'rw_ka', 'rw_rk', 'rw_gn_g',
             'rw_gn_b', 'pool_w', 'pool_scale', 'ret_gn_g', 'ret_gn_b')
    stacked = (rw_mu, rw_w0, rw_w2, rw_a0, rw_a2, rw_g2, rw_kk, rw_ka, rw_rk, rw_gn_g, rw_gn_b, pool_w,
               pool_scale, ret_gn_g, ret_gn_b)
    new_p = [[] for _ in range(7)]
    new_s = [[] for _ in range(7)]
    zeros = lambda *shape: jnp.zeros(shape, F32)
    w_in_b, wb, wo = w_in.astype(BF16), w_branch.astype(BF16), w_out.astype(BF16)
    for l in range(DEPTH):
        lw = {k: v[l] for k, v in zip(names, stacked)}
        xp, st_p = _token_mix(hp, 0, zeros(bp, RW_COLS), zeros(bp, N_HEADS, HEAD_DIM, HEAD_DIM),
                              zeros(bp, POOL_BUF, WIDTH), zeros(bp, N_HEADS, HEAD_DIM, HEAD_DIM), None,
                              lw, l, w_in_b, wb, wo, ln_g[l, 0], ln_b[l, 0])
        xs, st_s = _token_mix(hs, 8192, state_shift[l], state_wkv[l], state_pool[l], state_ret[l],
                              (cache_kv_w128[l], cache_kv_w512[l], cache_kv_w2048[l]),
                              lw, l, w_in_b, wb, wo, ln_g[l, 0], ln_b[l, 0])
        j = l // 2
        if l % 2 == 0:
            ws = [w[j].astype(BF16) for w in (ffn_w_gate, ffn_w_up, ffn_w_down)]
            xp = _ffn(xp, *ws, ln_g[l, 1], ln_b[l, 1], 256)
            xs = _ffn(xs, *ws, ln_g[l, 1], ln_b[l, 1], xs.shape[0])
        else:
            ws = [w[j].astype(BF16) for w in (moe_w_gate, moe_w_up, moe_w_down)]
            xp = _moe(xp, moe_router[j], *ws, ln_g[l, 1], ln_b[l, 1], 1024)
            xs = _moe(xs, moe_router[j], *ws, ln_g[l, 1], ln_b[l, 1], xs.shape[0])
        hp, hs = xp.reshape(bp, tp, D_MODEL), xs.reshape(bs, ts, D_MODEL)
        for i in range(7):
            new_p[i].append(st_p[i])
            new_s[i].append(st_s[i])
    outs_p = [jnp.stack(x) for x in new_p]
    outs_s = [jnp.stack(x) for x in new_s]
    return (hp, hs, *outs_p, *outs_s)
```

```python
import functools
import math

import jax
import jax.numpy as jnp
from jax import lax
from jax.experimental import pallas as pl
from jax.experimental.pallas import tpu as pltpu

F32 = jnp.float32
BF16 = jnp.bfloat16

D_MODEL = 1024
DEPTH = 2
HEAD_DIM = 64
N_HEADS = 4
WIDTH = N_HEADS * HEAD_DIM
RW_COLS = 1024
RW_GN_EPS = 64e-5
POOL_WINDOWS = (2, 4, 8, 16)
POOL_BUF = 15
RET_CHUNK = 128
ROPE_BASE = 10000.0
DIL_PATTERNS = ((128, 1), (512, 4), (2048, 16))
DIL_GROUPS = 3
DIL_BLOCK = 128
DIL_SPAN = 2048
DIL_COLS = 3 * DIL_GROUPS * WIDTH
N_BRANCH = 4
COL_POOL = RW_COLS
COL_RET = COL_POOL + WIDTH
COL_DIL = COL_RET + 4 * WIDTH
COL_GATE = COL_DIL + DIL_COLS
N_EXPERTS = 8
DN_ALPHA = (2 * DEPTH) ** 0.25
LN_EPS = 1e-5
RW_CHUNK = 64
MOE_TILE = 288
RW_P = 1
RW_BATCH = 4
NEG_BIG = -1e30

NN = (((1,), (0,)), ((), ()))
NT = (((1,), (1,)), ((), ()))
TN = (((0,), (0,)), ((), ()))

VMEM_LIMIT = 56 * 1024 * 1024


def _split(x, n):
    if x.dtype == BF16:
        return [x]
    parts, rem = [], x
    for i in range(n):
        p = rem.astype(BF16)
        parts.append(p)
        if i + 1 < n:
            rem = rem - p.astype(F32)
    return parts


def _mm(a, b, dims=NN, pa=1, pb=1):
    a_parts, b_parts = _split(a, pa), _split(b, pb)
    depth = max(len(a_parts), len(b_parts))
    acc = None
    for i, ai in enumerate(a_parts):
        for j, bj in enumerate(b_parts):
            if i + j < depth:
                t = lax.dot_general(ai, bj, dims, preferred_element_type=F32)
                acc = t if acc is None else acc + t
    return acc


def _sigmoid(x):
    return 0.5 * jnp.tanh(0.5 * x) + 0.5


def _silu(x):
    return x * _sigmoid(x)


def _iota(shape, axis):
    return lax.broadcasted_iota(jnp.int32, shape, axis)


def _div(x, d):
    assert d & (d - 1) == 0
    return x >> (d.bit_length() - 1)


def _mod(x, d):
    assert d & (d - 1) == 0
    return x & (d - 1)


def _per_head(head, values):
    out = jnp.full(head.shape, values[-1], F32)
    for h in range(len(values) - 2, -1, -1):
        out = jnp.where(head == h, values[h], out)
    return out


def _head_mask(rows_per_head, n_rows):
    return _div(_iota((n_rows, WIDTH), 0), rows_per_head) == _div(_iota((n_rows, WIDTH), 1), HEAD_DIM)


def _stack_heads(x, mask):
    return jnp.where(mask, jnp.concatenate([x] * N_HEADS, axis=0), 0.0)


def _unstack_heads(x_st, mask, c):
    x_st = jnp.where(mask, x_st, 0.0)
    out = x_st[0:c]
    for h in range(1, N_HEADS):
        out = out + x_st[h * c:(h + 1) * c]
    return out


def _ones_bd():
    return _same_head().astype(BF16)


def _same_head():
    return _div(_iota((WIDTH, WIDTH), 0), HEAD_DIM) == _div(_iota((WIDTH, WIDTH), 1), HEAD_DIM)


def _head_norm(x, ones_bd, g, b, eps):
    mu = _mm(x, ones_bd, pa=2) * (1.0 / HEAD_DIM)
    d = x - mu
    var = _mm(d * d, ones_bd, pa=2) * (1.0 / HEAD_DIM)
    return d * lax.rsqrt(var + eps) * g + b


def _layer_norm(x, g, b):
    mu = jnp.mean(x, axis=-1, keepdims=True)
    d = x - mu
    var = jnp.mean(d * d, axis=-1, keepdims=True)
    return d * lax.rsqrt(var + LN_EPS) * g + b


def _resident(shape):
    nd = len(shape)
    return pl.BlockSpec(shape, lambda *_: (0,) * nd, pipeline_mode=pl.Buffered(1))


def _proj_kernel(x_ref, w_ref, rw_ref, pool_ref, ret_ref, dil_ref):
    xb = x_ref[...].astype(BF16)
    for ref, lo, hi in ((rw_ref, 0, COL_POOL), (pool_ref, COL_POOL, COL_RET),
                        (ret_ref, COL_RET, COL_DIL), (dil_ref, COL_DIL, COL_GATE)):
        for s in range(lo, hi, 512):
            e = min(s + 512, hi)
            ref[:, s - lo:e - lo] = _mm(xb, w_ref[:, s:e])


def _layer_resident(w, l):
    nd = w.ndim - 1
    return pl.BlockSpec((None,) + w.shape[1:], lambda *_: (l,) + (0,) * nd, pipeline_mode=pl.Buffered(1))


def _project(x, w_in_b, l, tm):
    m = x.shape[0]
    widths = (COL_POOL, WIDTH, 4 * WIDTH, DIL_COLS)
    return pl.pallas_call(
        _proj_kernel,
        grid=(m // tm,),
        in_specs=[pl.BlockSpec((tm, D_MODEL), lambda i: (i, 0)), _layer_resident(w_in_b, l)],
        out_specs=[pl.BlockSpec((tm, w), lambda i: (i, 0)) for w in widths],
        out_shape=[jax.ShapeDtypeStruct((m, w), F32) for w in widths],
        compiler_params=pltpu.CompilerParams(dimension_semantics=("parallel",), vmem_limit_bytes=VMEM_LIMIT),
        name="proj",
    )(x, w_in_b)


BNT = (((2,), (2,)), ((0,), (0,)))
BNN = (((2,), (1,)), ((0,), (0,)))
BTN = (((1,), (1,)), ((0,), (0,)))


def _rwkv_kernel(p_ref, prev_ref, s0_ref, mu_ref, w0_ref, w2_ref, a0_ref, a2_ref, g2_ref, kk_ref, ka_ref,
                 rk_ref, gng_ref, gnb_ref, o_ref, st_ref, s_scr, prev_scr, buf_scr, *, c, t_in, nb):
    ci = pl.program_id(1)

    @pl.when(ci == 0)
    def _():
        s_scr[...] = s0_ref[...]
        for b in range(nb):
            prev_scr[b, 0:1, :] = prev_ref[b]

    first = _iota((c, RW_COLS), 0) == 0
    ps, shs = [], []
    for b in range(nb):
        if t_in == c:
            pb_ = p_ref[b]
        else:
            buf_scr[b] = jnp.zeros((c, RW_COLS), F32)
            buf_scr[b, 0:t_in, :] = p_ref[b]
            pb_ = buf_scr[b]
        shs.append(jnp.where(first, prev_scr[b, 0:1, :], pltpu.roll(pb_, 1, 0)))
        prev_scr[b, 0:1, :] = pb_[c - 1:c, :]
        ps.append(pb_)
    p, shifted = jnp.concatenate(ps, axis=0), jnp.concatenate(shs, axis=0)
    m = nb * c
    u = p + (shifted - p) * mu_ref[...]
    r, k, v, ul = u[:, 0:WIDTH], u[:, WIDTH:2 * WIDTH], u[:, 2 * WIDTH:3 * WIDTH], u[:, 3 * WIDTH:]

    w_pre = w0_ref[...] + _mm(jnp.tanh(ul), w2_ref[...])
    logw = -math.exp(-0.5) * _sigmoid(w_pre)
    a = _sigmoid(a0_ref[...] + _mm(ul, a2_ref[...]))
    g = _mm(_sigmoid(ul), g2_ref[...])

    ones_bd = _ones_bd()
    kk = k * kk_ref[...]
    kk = kk * lax.rsqrt(jnp.maximum(_mm(kk * kk, ones_bd, pa=2), 1e-24))
    k_mod = k * (1.0 + (a - 1.0) * ka_ref[...])
    a_vec, b_vec = -kk, kk * a
    if t_in < c:
        live = _mod(_iota((m, WIDTH), 0), c) < t_in
        logw = jnp.where(live, logw, 0.0)
        a_vec, b_vec = jnp.where(live, a_vec, 0.0), jnp.where(live, b_vec, 0.0)
        k_mod, v = jnp.where(live, k_mod, 0.0), jnp.where(live, v, 0.0)

    qi, qj = _iota((m, m), 0), _iota((m, m), 1)
    tri = ((_div(qi, c) == _div(qj, c)) & (qi >= qj)).astype(BF16)
    cum2 = _mm(tri, logw, pb=3)
    seq = lambda x: x.reshape(nb, c, WIDTH)
    cum, lw3 = seq(cum2), seq(logw)
    cum_end = cum[:, c - 1:c, :]
    e_neg, e_end = jnp.exp(-cum), jnp.exp(cum_end - cum)
    a_t = seq(a_vec) * jnp.exp(cum - lw3)
    r_t = seq(r) * jnp.exp(cum)
    b_t, k_t = seq(b_vec) * e_neg, seq(k_mod) * e_neg
    b_e, k_e = seq(b_vec) * e_end, seq(k_mod) * e_end
    v3 = seq(v)

    n = N_HEADS * c
    hm = _head_mask(c, n)
    stack = lambda x: jnp.where(hm, jnp.concatenate([x] * N_HEADS, axis=1), 0.0)
    a_st, r_st, b_st, v_st = stack(a_t), stack(r_t), stack(b_t), stack(v3)
    ri, rj = _iota((n, n), 0), _iota((n, n), 1)
    same = _div(ri, c) == _div(rj, c)
    strict_bd = same & (_mod(ri, c) > _mod(rj, c))
    incl_bd = same & (_mod(ri, c) >= _mod(rj, c))
    ti, tj = _mod(_iota((n, c), 0), c), _iota((n, c), 1)
    strict_st, incl_st = ti > tj, ti >= tj

    mm = functools.partial(_mm, pa=RW_P, pb=RW_P)
    a_ab = jnp.where(strict_bd, mm(a_st, b_st, BNT), 0.0)
    inv = jnp.where(ri == rj, 1.0, 0.0) + a_ab
    pw = a_ab
    for _ in range(int(math.log2(c)) - 1):
        pw = mm(pw, pw, BNN)
        inv = inv + mm(inv, pw, BNN)
    a_ak = jnp.where(strict_st, mm(a_st, k_t, BNT), 0.0)
    z_st = jnp.where(hm, mm(a_ak, v3, BNN), 0.0)
    wu = mm(inv, jnp.concatenate([a_st, z_st], axis=2), BNN)
    w_st, u0_st = wu[:, :, 0:WIDTH], wu[:, :, WIDTH:]

    s0 = s_scr[...]
    u_st = mm(w_st, s0, BNT) + u0_st
    lhs = jnp.concatenate([u_st, v_st], axis=1)
    rhs = jnp.concatenate([stack(b_e), stack(k_e)], axis=1)
    s_scr[...] = s0 * jnp.exp(cum_end) + mm(lhs, rhs, BTN)

    a_rb = jnp.where(incl_bd, mm(r_st, b_st, BNT), 0.0)
    a_rk = jnp.where(incl_st, mm(r_st, k_t, BNT), 0.0)
    o_st = jnp.where(hm, mm(r_st, s0, BNT) + mm(a_rb, u_st, BNN) + mm(a_rk, v3, BNN), 0.0)
    o3 = o_st[:, 0:c]
    for h in range(1, N_HEADS):
        o3 = o3 + o_st[:, h * c:(h + 1) * c]
    o = o3.reshape(m, WIDTH)

    o = _head_norm(o, ones_bd, gng_ref[...], gnb_ref[...], RW_GN_EPS)
    o = o + _mm(r * k_mod * rk_ref[...], ones_bd, pa=2) * v
    o_ref[...] = (o * g).reshape(nb, c, WIDTH)[:, 0:t_in]

    @pl.when(ci == pl.num_programs(1) - 1)
    def _():
        st_ref[...] = s_scr[...]


def _block_diag_heads(s):
    b = s.shape[0]
    eye = jnp.eye(N_HEADS, dtype=s.dtype)
    return jnp.einsum('bhij,hg->bhigj', s, eye).reshape(b, WIDTH, WIDTH)


def _diag_heads(s_bd):
    b = s_bd.shape[0]
    s = s_bd.reshape(b, N_HEADS, HEAD_DIM, N_HEADS, HEAD_DIM)
    return jnp.stack([s[:, h, :, h, :] for h in range(N_HEADS)], axis=1)


def _rwkv_branch(p, p_prev, wkv0, lw):
    b, t, _ = p.shape
    c = RW_CHUNK
    t_in = c if t % c == 0 else t
    nc = t // c if t % c == 0 else 1
    vec = lambda x: x.reshape(1, -1)
    pad_rows = lambda w, lo: jnp.zeros((WIDTH, WIDTH), F32).at[lo:lo + w.shape[0]].set(w).astype(BF16)
    params = [vec(lw['rw_mu']), vec(lw['rw_w0']), pad_rows(lw['rw_w2'], 0), vec(lw['rw_a0']),
              pad_rows(lw['rw_a2'], 64), pad_rows(lw['rw_g2'], 128), vec(lw['rw_kk']), vec(lw['rw_ka']),
              vec(lw['rw_rk']), vec(lw['rw_gn_g']), vec(lw['rw_gn_b'])]
    nb = RW_BATCH
    assert b % nb == 0
    o, st = pl.pallas_call(
        functools.partial(_rwkv_kernel, c=c, t_in=t_in, nb=nb),
        grid=(b // nb, nc),
        in_specs=[pl.BlockSpec((nb, t_in, RW_COLS), lambda i, j: (i, j, 0)),
                  pl.BlockSpec((nb, 1, RW_COLS), lambda i, j: (i, 0, 0)),
                  pl.BlockSpec((nb, WIDTH, WIDTH), lambda i, j: (i, 0, 0))]
                 + [_resident(x.shape) for x in params],
        out_specs=[pl.BlockSpec((nb, t_in, WIDTH), lambda i, j: (i, j, 0)),
                   pl.BlockSpec((nb, WIDTH, WIDTH), lambda i, j: (i, 0, 0))],
        out_shape=[jax.ShapeDtypeStruct((b, t, WIDTH), F32), jax.ShapeDtypeStruct((b, WIDTH, WIDTH), F32)],
        scratch_shapes=[pltpu.VMEM((nb, WIDTH, WIDTH), F32), pltpu.VMEM((nb, 8, RW_COLS), F32),
                        pltpu.VMEM((nb, c, RW_COLS), F32)],
        compiler_params=pltpu.CompilerParams(dimension_semantics=("parallel", "arbitrary"),
                                             vmem_limit_bytes=VMEM_LIMIT),
        name="rwkv7",
    )(p, p_prev.reshape(b, 1, RW_COLS), _block_diag_heads(wkv0), *params)
    return o, _diag_heads(st)


def _pool_kernel(u_ref, buf_ref, w_ref, scale_ref, o_ref, ext_scr, *, c, t_in, pos0):
    ci = pl.program_id(1)

    @pl.when(ci == 0)
    def _():
        ext_scr[0:16, :] = buf_ref[0]

    if t_in < c:
        ext_scr[16:16 + c, :] = jnp.zeros((c, WIDTH), F32)
    ext_scr[16:16 + t_in, :] = u_ref[0]
    x = ext_scr[16:16 + c, :]
    sums, acc, off = [], x, 1
    for win in POOL_WINDOWS:
        while off < win:
            acc = acc + ext_scr[16 - off:16 - off + c, :]
            off += 1
        sums.append(acc)
    pos = pos0 + ci * c + _iota((c, WIDTH), 0)
    grp = _div(_iota((c, WIDTH), 1), HEAD_DIM)
    mean = jnp.zeros((c, WIDTH), F32)
    for gi, win in enumerate(POOL_WINDOWS):
        cnt = jnp.minimum(win, pos + 1).astype(F32)
        mean = jnp.where(grp == gi, sums[gi] / cnt, mean)
    mixed = _mm(mean - x, w_ref[...]) * scale_ref[...]
    o_ref[0] = mixed[0:t_in]
    ext_scr[0:16, :] = ext_scr[c:c + 16, :]


def _pool_branch(u, buf, pos0, lw):
    b, t, _ = u.shape
    c = 512 if t % 512 == 0 else 8
    t_in = c if t % c == 0 else t
    nc = t // c if t % c == 0 else 1
    w_bd = jnp.einsum('gcd,gh->gchd', lw['pool_w'], jnp.eye(N_HEADS, dtype=F32)).reshape(WIDTH, WIDTH).astype(BF16)
    buf16 = jnp.pad(buf, ((0, 0), (1, 0), (0, 0)))
    return pl.pallas_call(
        functools.partial(_pool_kernel, c=c, t_in=t_in, pos0=pos0),
        grid=(b, nc),
        in_specs=[pl.BlockSpec((1, t_in, WIDTH), lambda i, j: (i, j, 0)),
                  pl.BlockSpec((1, 16, WIDTH), lambda i, j: (i, 0, 0)),
                  _resident((WIDTH, WIDTH)), _resident((1, WIDTH))],
        out_specs=pl.BlockSpec((1, t_in, WIDTH), lambda i, j: (i, j, 0)),
        out_shape=jax.ShapeDtypeStruct((b, t, WIDTH), F32),
        scratch_shapes=[pltpu.VMEM((16 + c, WIDTH), F32)],
        compiler_params=pltpu.CompilerParams(dimension_semantics=("parallel", "arbitrary")),
        name="pool",
    )(u, buf16, w_bd, lw['pool_scale'].reshape(1, WIDTH))


def _rot_half(x):
    first = _mod(_iota(x.shape, 1), HEAD_DIM) < (HEAD_DIM // 2)
    return jnp.where(first, pltpu.roll(x, WIDTH - HEAD_DIM // 2, 1), pltpu.roll(x, HEAD_DIM // 2, 1))


RET_LOG_DECAY = tuple(math.log(1.0 - 2.0 ** (-5.0 - h)) for h in range(N_HEADS))
ALIBI_SLOPES = tuple(2.0 ** (-8.0 * (i + 1) / (DIL_GROUPS * N_HEADS)) for i in range(DIL_GROUPS * N_HEADS))


def _ret_kernel(p_ref, cos_ref, sin_ref, s0_ref, gng_ref, gnb_ref, o_ref, st_ref, s_scr, buf_scr,
                dmask_scr, qd_scr, kd_scr, *, c, t_in, nb):
    ci = pl.program_id(1)
    n = N_HEADS * c
    lg = _per_head(_div(_iota((1, WIDTH), 1), HEAD_DIM), RET_LOG_DECAY)

    @pl.when(ci == 0)
    def _():
        s_scr[...] = s0_ref[...]
        idx = _iota((c, WIDTH), 0).astype(F32)
        qd_scr[...] = jnp.exp(lg * (idx + 1.0))
        kd_scr[...] = jnp.exp(lg * (t_in - 1.0 - idx))
        rel = _mod(_iota((n, c), 0), c) - _iota((n, c), 1)
        lg_rows = _per_head(_div(_iota((n, c), 0), c), RET_LOG_DECAY)
        dmask_scr[...] = jnp.where(rel >= 0, jnp.exp(lg_rows * jnp.maximum(rel, 0).astype(F32)), 0.0)

    hm = _head_mask(c, n)
    cos, sin = cos_ref[...], sin_ref[...]
    chunk_decay = jnp.exp(lg * float(t_in))
    same_head, ones_bd = _same_head(), _ones_bd()
    for b in range(nb):
        if t_in == c:
            p = p_ref[b]
        else:
            buf_scr[b] = jnp.zeros((c, 4 * WIDTH), F32)
            buf_scr[b, 0:t_in, :] = p_ref[b]
            p = buf_scr[b]
        q, k, v, g = (p[:, i * WIDTH:(i + 1) * WIDTH] for i in range(4))
        q = q * cos + _rot_half(q) * sin
        k = (k * cos + _rot_half(k) * sin) * HEAD_DIM ** -0.5
        if t_in < c:
            live = _iota((c, WIDTH), 0) < t_in
            k, v = jnp.where(live, k, 0.0), jnp.where(live, v, 0.0)
        inner = _mm(_stack_heads(q, hm), k, NT) * dmask_scr[...]
        s0 = s_scr[b]
        o = _unstack_heads(_mm(inner, v), hm, c) + _mm(q * qd_scr[...], s0)
        s_scr[b] = s0 * chunk_decay + jnp.where(same_head, _mm(k * kd_scr[...], v, TN), 0.0)
        o = _silu(g) * _head_norm(o, ones_bd, gng_ref[...], gnb_ref[...], LN_EPS)
        o_ref[b] = o[0:t_in]

    @pl.when(ci == pl.num_programs(1) - 1)
    def _():
        st_ref[...] = s_scr[...]


def _rope_tables(pos0, t, rows):
    half = HEAD_DIM // 2
    inv = ROPE_BASE ** (-jnp.arange(half, dtype=F32) / half)
    ang = (pos0 + jnp.arange(t, dtype=jnp.int32)).astype(F32)[:, None] * inv[None, :]
    cos = jnp.tile(jnp.cos(ang), (1, 2 * N_HEADS))
    sin = jnp.tile(jnp.concatenate([-jnp.sin(ang), jnp.sin(ang)], axis=1), (1, N_HEADS))
    pad = ((0, rows - t), (0, 0))
    return jnp.pad(cos, pad), jnp.pad(sin, pad)


def _ret_branch(p, s0, pos0, lw):
    b, t, _ = p.shape
    c = RET_CHUNK
    t_in = c if t % c == 0 else t
    nc = t // c if t % c == 0 else 1
    cos, sin = _rope_tables(pos0, t, nc * c)
    nb = RW_BATCH
    assert b % nb == 0
    o, st = pl.pallas_call(
        functools.partial(_ret_kernel, c=c, t_in=t_in, nb=nb),
        grid=(b // nb, nc),
        in_specs=[pl.BlockSpec((nb, t_in, 4 * WIDTH), lambda i, j: (i, j, 0)),
                  pl.BlockSpec((c, WIDTH), lambda i, j: (j, 0)),
                  pl.BlockSpec((c, WIDTH), lambda i, j: (j, 0)),
                  pl.BlockSpec((nb, WIDTH, WIDTH), lambda i, j: (i, 0, 0)),
                  _resident((1, WIDTH)), _resident((1, WIDTH))],
        out_specs=[pl.BlockSpec((nb, t_in, WIDTH), lambda i, j: (i, j, 0)),
                   pl.BlockSpec((nb, WIDTH, WIDTH), lambda i, j: (i, 0, 0))],
        out_shape=[jax.ShapeDtypeStruct((b, t, WIDTH), F32), jax.ShapeDtypeStruct((b, WIDTH, WIDTH), F32)],
        scratch_shapes=[pltpu.VMEM((nb, WIDTH, WIDTH), F32), pltpu.VMEM((nb, c, 4 * WIDTH), F32),
                        pltpu.VMEM((N_HEADS * c, c), F32), pltpu.VMEM((c, WIDTH), F32), pltpu.VMEM((c, WIDTH), F32)],
        compiler_params=pltpu.CompilerParams(dimension_semantics=("parallel", "arbitrary")),
        name="retention",
    )(p, cos, sin, _block_diag_heads(s0), lw['ret_gn_g'].reshape(1, WIDTH), lw['ret_gn_b'].reshape(1, WIDTH))
    return o, _diag_heads(st)


def _alibi_slope_rows(group, rows_per_head):
    head = _div(_iota((N_HEADS * rows_per_head, 1), 0), rows_per_head)
    return _per_head(head, ALIBI_SLOPES[group * N_HEADS:(group + 1) * N_HEADS])


def _dil_prompt_kernel(*refs, group, dil, span):
    q_refs, kc_refs, kp_refs, vc_refs, vp_refs = (refs[2 * i:2 * i + 2] for i in range(5))
    o_ref, lse_ref, k_scr, v_scr, o_scr, lse_scr = refs[10:]
    blk = DIL_BLOCK
    tail = blk * dil
    si = pl.program_id(1)
    for half in range(2):
        k_scr[half, 0:tail, :] = kp_refs[half][0]
        k_scr[half, tail:tail + span, :] = kc_refs[half][0]
        v_scr[half, 0:tail, :] = vp_refs[half][0]
        v_scr[half, tail:tail + span, :] = vc_refs[half][0]
    ki = _iota((blk, 2 * blk), 1)
    steps = blk + _iota((blk, 2 * blk), 0) - ki
    band = (steps >= 0) & (steps <= blk)
    masks = [jnp.where(band, (-ALIBI_SLOPES[group * N_HEADS + h] * dil) * steps.astype(F32), NEG_BIG)
             for h in range(N_HEADS)]
    sub_head = _div(_iota((blk, 128), 1), HEAD_DIM)
    for cc in range(span // tail):
        has_prev = (si > 0) | (ki >= blk)
        cc_masks = [jnp.where(has_prev, x, NEG_BIG) for x in masks] if cc == 0 else masks
        for r in range(dil):
            rows_q = pl.ds(cc * tail + r, blk, stride=dil) if dil > 1 else pl.ds(cc * tail, blk)
            rows_kv = pl.ds(cc * tail + r, 2 * blk, stride=dil) if dil > 1 else pl.ds(cc * tail, 2 * blk)
            for half in range(2):
                q2 = q_refs[half][0, rows_q, :] * HEAD_DIM ** -0.5
                k2 = k_scr[half, rows_kv, :].astype(BF16)
                v2 = v_scr[half, rows_kv, :].astype(BF16)
                o2 = lse2 = None
                for sub in range(2):
                    mine = sub_head == sub
                    s = _mm(jnp.where(mine, q2, 0.0), k2, NT) + cc_masks[2 * half + sub]
                    m = jnp.max(s, axis=1, keepdims=True)
                    e = jnp.exp(s - m)
                    l = jnp.sum(e, axis=1, keepdims=True)
                    o_h = _mm(e, v2) / l
                    lse_h = jnp.broadcast_to(m + jnp.log(l), (blk, 128))
                    o2 = o_h if o2 is None else jnp.where(mine, o_h, o2)
                    lse2 = lse_h if lse2 is None else jnp.where(mine, lse_h, lse2)
                o_scr[half, rows_q, :] = o2
                lse_scr[half, rows_q, :] = lse2
    o_ref[0] = jnp.concatenate([o_scr[0], o_scr[1]], axis=1)
    lse_ref[0] = jnp.concatenate([lse_scr[0], lse_scr[1]], axis=1)


def _dil_prompt_group(pd, group):
    b, s, _ = pd.shape
    win, dil = DIL_PATTERNS[group]
    span = DIL_SPAN
    tail = DIL_BLOCK * dil
    assert win // dil == DIL_BLOCK and span % tail == 0 and s % span == 0
    base = group * 3

    def cur(col):
        return [pl.BlockSpec((1, span, 128), lambda i, j, h=h: (i, j, 2 * (base + col) + h)) for h in range(2)]

    def prev(col):
        return [pl.BlockSpec((1, tail, 128),
                             lambda i, j, h=h: (i, jnp.maximum(j * (span // tail) - 1, 0), 2 * (base + col) + h))
                for h in range(2)]

    out_spec = pl.BlockSpec((1, span, WIDTH), lambda i, j: (i, j, 0))
    return pl.pallas_call(
        functools.partial(_dil_prompt_kernel, group=group, dil=dil, span=span),
        grid=(b, s // span),
        in_specs=cur(0) + cur(1) + prev(1) + cur(2) + prev(2),
        out_specs=[out_spec, out_spec],
        out_shape=[jax.ShapeDtypeStruct((b, s, WIDTH), F32)] * 2,
        scratch_shapes=[pltpu.VMEM((2, tail + span, 128), F32)] * 2 + [pltpu.VMEM((2, span, 128), F32)] * 2,
        compiler_params=pltpu.CompilerParams(dimension_semantics=("parallel", "arbitrary"),
                                             vmem_limit_bytes=VMEM_LIMIT),
        name=f"dil_prompt_g{group}",
    )(*([pd] * 10))


def _dil_combine(os_, ls):
    m = jnp.maximum(jnp.maximum(ls[0], ls[1]), ls[2])
    es = [jnp.exp(x - m) for x in ls]
    return (es[0] * os_[0] + es[1] * os_[1] + es[2] * os_[2]) / (es[0] + es[1] + es[2])


def _dil_prompt(pd):
    b, s, _ = pd.shape
    outs = [_dil_prompt_group(pd, g) for g in range(DIL_GROUPS)]
    return [x[0].reshape(b * s, WIDTH) for x in outs] + [x[1].reshape(b * s, WIDTH) for x in outs]


def _dil_step_kernel(pd_ref, c0_ref, c1_ref, c2_ref, o_ref, buf_scr, *, t, tp):
    buf_scr[...] = jnp.zeros_like(buf_scr)
    buf_scr[0:t, :] = pd_ref[0]
    pd = buf_scr[...]
    n = N_HEADS * tp
    hm = _head_mask(tp, n)
    qt = _mod(_iota((n, 1), 0), tp)
    outs, lses = [], []
    for g, cache_ref in enumerate((c0_ref, c1_ref, c2_ref)):
        win, dil = DIL_PATTERNS[g]
        length = cache_ref.shape[1]
        q = pd[:, g * 3 * WIDTH:g * 3 * WIDTH + WIDTH]
        k_new = pd[:, g * 3 * WIDTH + WIDTH:g * 3 * WIDTH + 2 * WIDTH]
        v_new = pd[:, g * 3 * WIDTH + 2 * WIDTH:g * 3 * WIDTH + 3 * WIDTH]
        k_old, v_old = cache_ref[0, :, 0:WIDTH], cache_ref[0, :, WIDTH:2 * WIDTH]
        q_st = _stack_heads(q, hm)
        slope = _alibi_slope_rows(g, tp)
        d_old = length + qt - _iota((n, length), 1)
        ok_old = (_mod(d_old, dil) == 0) & (d_old <= win)
        s_old = _mm(q_st, k_old, NT) * HEAD_DIM ** -0.5 - slope * d_old.astype(F32)
        s_old = jnp.where(ok_old, s_old, NEG_BIG)
        d_new = qt - _iota((n, tp), 1)
        ok_new = (d_new >= 0) & (_mod(d_new, dil) == 0)
        s_new = _mm(q_st, k_new, NT) * HEAD_DIM ** -0.5 - slope * d_new.astype(F32)
        s_new = jnp.where(ok_new, s_new, NEG_BIG)
        m = jnp.maximum(jnp.max(s_old, axis=1, keepdims=True), jnp.max(s_new, axis=1, keepdims=True))
        e_old, e_new = jnp.exp(s_old - m), jnp.exp(s_new - m)
        l = jnp.sum(e_old, axis=1, keepdims=True) + jnp.sum(e_new, axis=1, keepdims=True)
        outs.append((_mm(e_old, v_old) + _mm(e_new, v_new)) / l)
        lses.append(m + jnp.log(l))
    o_ref[0] = _unstack_heads(_dil_combine(outs, lses), hm, tp)[0:t]


def _dil_step(pd, caches, l):
    b, t, _ = pd.shape
    tp = 8
    flat = caches
    return pl.pallas_call(
        functools.partial(_dil_step_kernel, t=t, tp=tp),
        grid=(b,),
        in_specs=[pl.BlockSpec((1, t, DIL_COLS), lambda i: (i, 0, 0))]
                 + [pl.BlockSpec((None, 1, c.shape[2], 2 * WIDTH), lambda i: (l, i, 0, 0)) for c in flat],
        out_specs=pl.BlockSpec((1, t, WIDTH), lambda i: (i, 0, 0)),
        out_shape=jax.ShapeDtypeStruct((b, t, WIDTH), F32),
        scratch_shapes=[pltpu.VMEM((tp, DIL_COLS), F32)],
        compiler_params=pltpu.CompilerParams(dimension_semantics=("parallel",), vmem_limit_bytes=VMEM_LIMIT),
        name="dil_step",
    )(pd, *flat)


def _merge_kernel(h_ref, *refs):
    wg_ref, wb_ref, wo_ref, g_ref, b_ref, out_ref = refs[-6:]
    branch_refs = refs[:-6]
    branches = [r[...] for r in branch_refs[:3]]
    if len(branch_refs) == 4:
        branches.append(branch_refs[3][...])
    else:
        branches.append(_dil_combine([r[...] for r in branch_refs[3:6]], [r[...] for r in branch_refs[6:9]]))
    h = h_ref[...]
    hb = h.astype(BF16)
    z = None
    for n, o in enumerate(branches):
        gate = _sigmoid(_mm(hb, wg_ref[:, COL_GATE + n * D_MODEL:COL_GATE + (n + 1) * D_MODEL]))
        term = gate * _mm(o, wb_ref[n])
        z = term if z is None else z + term
    y = _mm(z, wo_ref[...])
    out_ref[...] = _layer_norm(DN_ALPHA * h + y, g_ref[...], b_ref[...])


def _merge(h, branches, w_in_b, w_branch_b, w_out_b, l, ln_g, ln_b, tm):
    m = h.shape[0]
    row = lambda w: pl.BlockSpec((tm, w), lambda i: (i, 0))
    return pl.pallas_call(
        _merge_kernel,
        grid=(m // tm,),
        in_specs=[row(D_MODEL)] + [row(WIDTH)] * len(branches)
                 + [_layer_resident(w_in_b, l), _layer_resident(w_branch_b, l), _layer_resident(w_out_b, l),
                    _resident((1, D_MODEL)), _resident((1, D_MODEL))],
        out_specs=row(D_MODEL),
        out_shape=jax.ShapeDtypeStruct((m, D_MODEL), F32),
        compiler_params=pltpu.CompilerParams(dimension_semantics=("parallel",), vmem_limit_bytes=VMEM_LIMIT),
        name="merge_ln",
    )(h, *branches, w_in_b, w_branch_b, w_out_b, ln_g.reshape(1, -1), ln_b.reshape(1, -1))


def _ffn_kernel(x_ref, wg_ref, wu_ref, wd_ref, g_ref, b_ref, out_ref):
    x = x_ref[...]
    xb = x.astype(BF16)
    act = _silu(_mm(xb, wg_ref[...])) * _mm(xb, wu_ref[...])
    out_ref[...] = _layer_norm(DN_ALPHA * x + _mm(act, wd_ref[...]), g_ref[...], b_ref[...])


def _ffn(x, wg, wu, wd, ln_g, ln_b, tm):
    m = x.shape[0]
    row = pl.BlockSpec((tm, D_MODEL), lambda i: (i, 0))
    return pl.pallas_call(
        _ffn_kernel,
        grid=(m // tm,),
        in_specs=[row, _resident(wg.shape), _resident(wu.shape), _resident(wd.shape),
                  _resident((1, D_MODEL)), _resident((1, D_MODEL))],
        out_specs=row,
        out_shape=jax.ShapeDtypeStruct((m, D_MODEL), F32),
        compiler_params=pltpu.CompilerParams(dimension_semantics=("parallel",), vmem_limit_bytes=VMEM_LIMIT),
        name="ffn_ln",
    )(x, wg, wu, wd, ln_g.reshape(1, -1), ln_b.reshape(1, -1))


def _route_kernel(x_ref, rt_ref, tri_ref, gate_ref, rank_ref, cnt_ref):
    logits = _mm(rt_ref[...], x_ref[...], NT, 2, 2)
    sub = _iota(logits.shape, 0).astype(F32)
    m1 = jnp.max(logits, axis=0, keepdims=True)
    i1 = jnp.min(jnp.where(logits == m1, sub, float(N_EXPERTS)), axis=0, keepdims=True)
    rest = jnp.where(sub == i1, NEG_BIG, logits)
    m2 = jnp.max(rest, axis=0, keepdims=True)
    i2 = jnp.min(jnp.where(rest == m2, sub, float(N_EXPERTS)), axis=0, keepdims=True)
    e2 = jnp.exp(m2 - m1)
    gate_ref[0] = jnp.where(sub == i1, 1.0 / (1.0 + e2), 0.0) + jnp.where(sub == i2, e2 / (1.0 + e2), 0.0)
    chosen = (sub == i1) | (sub == i2)
    sel = jnp.where(chosen, 1.0, 0.0)
    rank_ref[0] = jnp.where(chosen, _mm(sel, tri_ref[...]), -1.0)
    cnt_ref[0] = jnp.broadcast_to(jnp.sum(sel, axis=1, keepdims=True), cnt_ref.shape[1:])


def _moe_kernel(cnt_ref, x_ref, gate_ref, rank_ref, wg_ref, wu_ref, wd_ref, g_ref, b_ref, out_ref,
                xb_scr, xg_scr, yg_scr, *, ts, tb, nsb):
    i, e, f = pl.program_id(0), pl.program_id(1), pl.program_id(2)
    last_f = f == pl.num_programs(2) - 1

    @pl.when((e == 0) & (f == 0))
    def _():
        xb_scr[...] = x_ref[...].astype(BF16)
        out_ref[...] = jnp.zeros_like(out_ref)

    slot = _iota((ts, tb), 0)

    def expert(xg):
        return _mm(_silu(_mm(xg, wg_ref[0])) * _mm(xg, wu_ref[0]), wd_ref[0])

    def pick_of(sb, j):
        rank_row = rank_ref[sb, pl.ds(e, 1), :]
        return jnp.where(rank_row == (slot + j * ts).astype(F32), 1.0, 0.0)

    def scatter(sb, pick, y):
        w_slot = jnp.sum(pick * gate_ref[sb, pl.ds(e, 1), :], axis=1, keepdims=True)
        out_ref[sb * tb:(sb + 1) * tb, :] += _mm(pick, y * w_slot, TN)

    @pl.when(f == 0)
    def _():
        for sb in range(nsb):
            xg_scr[sb] = _mm(pick_of(sb, 0), xb_scr[sb * tb:(sb + 1) * tb, :]).astype(BF16)
        yg_scr[...] = jnp.zeros_like(yg_scr)

    yg_scr[...] += expert(xg_scr[...].reshape(nsb * ts, D_MODEL)).reshape(nsb, ts, D_MODEL)

    @pl.when(last_f)
    def _():
        for sb in range(nsb):
            scatter(sb, pick_of(sb, 0), yg_scr[sb])

    for sb in range(nsb):
        n_tiles = (cnt_ref[(i * nsb + sb) * N_EXPERTS + e] + (ts - 1)) // ts

        def overflow(j, carry, sb=sb):
            pick = pick_of(sb, j)
            scatter(sb, pick, expert(_mm(pick, xb_scr[sb * tb:(sb + 1) * tb, :]).astype(BF16)))
            return carry

        lax.fori_loop(1, n_tiles, overflow, 0)

    @pl.when((e == pl.num_programs(1) - 1) & last_f)
    def _():
        out_ref[...] = _layer_norm(DN_ALPHA * x_ref[...] + out_ref[...], g_ref[...], b_ref[...])


def _moe(x, router, wg, wu, wd, ln_g, ln_b, tm):
    m = x.shape[0]
    nblk = m // tm
    dff = wg.shape[2]
    nf = 4
    tf = dff // nf
    ts = min(MOE_TILE, tm)
    nsb = 2 if nblk % 2 == 0 else 1
    assert m % tm == 0 and tf % 128 == 0
    tri = (jnp.arange(tm)[:, None] < jnp.arange(tm)[None, :]).astype(BF16)
    gate, rank, cnt = pl.pallas_call(
        _route_kernel,
        grid=(nblk,),
        in_specs=[pl.BlockSpec((tm, D_MODEL), lambda i: (i, 0)), _resident((N_EXPERTS, D_MODEL)),
                  _resident((tm, tm))],
        out_specs=[pl.BlockSpec((1, N_EXPERTS, tm), lambda i: (i, 0, 0)),
                   pl.BlockSpec((1, N_EXPERTS, tm), lambda i: (i, 0, 0)),
                   pl.BlockSpec((1, N_EXPERTS, 128), lambda i: (i, 0, 0))],
        out_shape=[jax.ShapeDtypeStruct((nblk, N_EXPERTS, tm), F32), jax.ShapeDtypeStruct((nblk, N_EXPERTS, tm), F32),
                   jax.ShapeDtypeStruct((nblk, N_EXPERTS, 128), F32)],
        compiler_params=pltpu.CompilerParams(dimension_semantics=("parallel",), vmem_limit_bytes=VMEM_LIMIT),
        name="moe_route",
    )(x, router.T, tri)
    counts = cnt[:, :, 0].astype(jnp.int32).reshape(-1)
    rows = nsb * tm
    x_in = pl.BlockSpec((rows, D_MODEL), lambda i, e, f, c: (i, 0), pipeline_mode=pl.Buffered(1))
    meta = pl.BlockSpec((nsb, N_EXPERTS, tm), lambda i, e, f, c: (i, 0, 0))
    vec = pl.BlockSpec((1, D_MODEL), lambda i, e, f, c: (0, 0), pipeline_mode=pl.Buffered(1))
    return pl.pallas_call(
        functools.partial(_moe_kernel, ts=ts, tb=tm, nsb=nsb),
        grid_spec=pltpu.PrefetchScalarGridSpec(
            num_scalar_prefetch=1,
            grid=(nblk // nsb, N_EXPERTS, nf),
            in_specs=[x_in, meta, meta,
                      pl.BlockSpec((1, D_MODEL, tf), lambda i, e, f, c: (e, 0, f)),
                      pl.BlockSpec((1, D_MODEL, tf), lambda i, e, f, c: (e, 0, f)),
                      pl.BlockSpec((1, tf, D_MODEL), lambda i, e, f, c: (e, f, 0)),
                      vec, vec],
            out_specs=pl.BlockSpec((rows, D_MODEL), lambda i, e, f, c: (i, 0)),
            scratch_shapes=[pltpu.VMEM((rows, D_MODEL), BF16),
                            pltpu.VMEM((nsb, ts, D_MODEL), BF16), pltpu.VMEM((nsb, ts, D_MODEL), F32)]),
        out_shape=jax.ShapeDtypeStruct((m, D_MODEL), F32),
        compiler_params=pltpu.CompilerParams(dimension_semantics=("parallel", "arbitrary", "arbitrary"),
                                             vmem_limit_bytes=VMEM_LIMIT),
        name="moe_ln",
    )(counts, x, gate, rank, wg, wu, wd, ln_g.reshape(1, -1), ln_b.reshape(1, -1))


def _token_mix(h, pos0, rw_prev, wkv0, pool_buf, ret0, kv_bufs, lw, l, w_in_b, w_branch_b, w_out_b, ln_g, ln_b):
    b, t, _ = h.shape
    m = b * t
    tm = 512 if m % 512 == 0 else m
    hf = h.reshape(m, D_MODEL)
    p_rw, p_pool, p_ret, p_dil = _project(hf, w_in_b, l, tm)
    p_rw, p_pool = p_rw.reshape(b, t, -1), p_pool.reshape(b, t, -1)
    p_ret, p_dil = p_ret.reshape(b, t, -1), p_dil.reshape(b, t, -1)
    o_a, wkv_new = _rwkv_branch(p_rw, rw_prev, wkv0, lw)
    o_b = _pool_branch(p_pool, pool_buf, pos0, lw)
    pool_new = jnp.concatenate([pool_buf, p_pool], axis=1)[:, -POOL_BUF:]
    o_c, ret_new = _ret_branch(p_ret, ret0, pos0, lw)
    def kv_rows(g, keep):
        lo = (3 * g + 1) * WIDTH
        return p_dil[:, t - keep:, lo:lo + 2 * WIDTH].reshape(b, keep, 2, N_HEADS, HEAD_DIM)

    if kv_bufs is None:
        dil_parts = _dil_prompt(p_dil)
        kv_new = [kv_rows(g, min(win, t)) for g, (win, _) in enumerate(DIL_PATTERNS)]
    else:
        dil_parts = [_dil_step(p_dil, kv_bufs, l).reshape(m, WIDTH)]
        kv_new = [kv_rows(g, t) for g in range(DIL_GROUPS)]
    branches = [x.reshape(m, WIDTH) for x in (o_a, o_b, o_c)] + dil_parts
    x1 = _merge(hf, branches, w_in_b, w_branch_b, w_out_b, l, ln_g, ln_b, tm)
    return x1, (wkv_new, p_rw[:, -1], pool_new, ret_new, kv_new[0], kv_new[1], kv_new[2])


def kernel(x_prompt, x_sample, state_wkv, state_shift, state_pool, state_ret, cache_kv_w128, cache_kv_w512, cache_kv_w2048, w_in, rw_mu, rw_w0, rw_w2, rw_a0, rw_a2, rw_g2, rw_kk, rw_ka, rw_rk, rw_gn_g, rw_gn_b, pool_w, pool_scale, ret_gn_g, ret_gn_b, w_branch, w_out, ln_g, ln_b, ffn_w_gate, ffn_w_up, ffn_w_down, moe_router, moe_w_gate, moe_w_up, moe_w_down):
    hp, hs = x_prompt, x_sample
    bp, tp, _ = hp.shape
    bs, ts, _ = hs.shape
    names = ('rw_mu', 'rw_w0', 'rw_w2', 'rw_a0', 'rw_a2', 'rw_g2', 'rw_kk', 'rw_ka', 'rw_rk', 'rw_gn_g',
             'rw_gn_b', 'pool_w', 'pool_scale', 'ret_gn_g', 'ret_gn_b')
    stacked = (rw_mu, rw_w0, rw_w2, rw_a0, rw_a2, rw_g2, rw_kk, rw_ka, rw_rk, rw_gn_g, rw_gn_b, pool_w,
               pool_scale, ret_gn_g, ret_gn_b)
    new_p = [[] for _ in range(7)]
    new_s = [[] for _ in range(7)]
    zeros = lambda *shape: jnp.zeros(shape, F32)
    w_in_b, wb, wo = w_in.astype(BF16), w_branch.astype(BF16), w_out.astype(BF16)
    caches = [c.reshape(DEPTH, bs, c.shape[2], 2 * WIDTH) for c in (cache_kv_w128, cache_kv_w512, cache_kv_w2048)]
    for l in range(DEPTH):
        lw = {k: v[l] for k, v in zip(names, stacked)}
        xp, st_p = _token_mix(hp, 0, zeros(bp, RW_COLS), zeros(bp, N_HEADS, HEAD_DIM, HEAD_DIM),
                              zeros(bp, POOL_BUF, WIDTH), zeros(bp, N_HEADS, HEAD_DIM, HEAD_DIM), None,
                              lw, l, w_in_b, wb, wo, ln_g[l, 0], ln_b[l, 0])
        xs, st_s = _token_mix(hs, 8192, state_shift[l], state_wkv[l], state_pool[l], state_ret[l], caches,
                              lw, l, w_in_b, wb, wo, ln_g[l, 0], ln_b[l, 0])
        j = l // 2
        if l % 2 == 0:
            ws = [w[j].astype(BF16) for w in (ffn_w_gate, ffn_w_up, ffn_w_down)]
            xp = _ffn(xp, *ws, ln_g[l, 1], ln_b[l, 1], 256)
            xs = _ffn(xs, *ws, ln_g[l, 1], ln_b[l, 1], xs.shape[0])
        else:
            ws = [w[j].astype(BF16) for w in (moe_w_gate, moe_w_up, moe_w_down)]
            xp = _moe(xp, moe_router[j], *ws, ln_g[l, 1], ln_b[l, 1], 1024)
            xs = _moe(xs, moe_router[j], *ws, ln_g[l, 1], ln_b[l, 1], xs.shape[0])
        hp, hs = xp.reshape(bp, tp, D_MODEL), xs.reshape(bs, ts, D_MODEL)
        for i in range(7):
            new_p[i].append(st_p[i])
            new_s[i].append(st_s[i])
    outs_p = [jnp.stack(x) for x in new_p]
    outs_s = [jnp.stack(x) for x in new_s]
    return (hp, hs, *outs_p, *outs_s)
```

```python
import functools
import math

import jax
import jax.numpy as jnp
from jax import lax
from jax.experimental import pallas as pl
from jax.experimental.pallas import tpu as pltpu

F32 = jnp.float32
BF16 = jnp.bfloat16

D_MODEL = 1024
DEPTH = 2
HEAD_DIM = 64
N_HEADS = 4
WIDTH = N_HEADS * HEAD_DIM
RW_COLS = 1024
RW_GN_EPS = 64e-5
POOL_WINDOWS = (2, 4, 8, 16)
POOL_BUF = 15
RET_CHUNK = 128
ROPE_BASE = 10000.0
DIL_PATTERNS = ((128, 1), (512, 4), (2048, 16))
DIL_GROUPS = 3
DIL_BLOCK = 128
DIL_SPAN = 2048
DIL_COLS = 3 * DIL_GROUPS * WIDTH
N_BRANCH = 4
COL_POOL = RW_COLS
COL_RET = COL_POOL + WIDTH
COL_DIL = COL_RET + 4 * WIDTH
COL_GATE = COL_DIL + DIL_COLS
N_EXPERTS = 8
DN_ALPHA = (2 * DEPTH) ** 0.25
LN_EPS = 1e-5
RW_CHUNK = 64
MOE_TILE = 288
RW_P = 1
RW_BATCH = 4
NEG_BIG = -1e30

NN = (((1,), (0,)), ((), ()))
NT = (((1,), (1,)), ((), ()))
TN = (((0,), (0,)), ((), ()))

VMEM_LIMIT = 56 * 1024 * 1024


def _split(x, n):
    if x.dtype == BF16:
        return [x]
    parts, rem = [], x
    for i in range(n):
        p = rem.astype(BF16)
        parts.append(p)
        if i + 1 < n:
            rem = rem - p.astype(F32)
    return parts


def _mm(a, b, dims=NN, pa=1, pb=1):
    a_parts, b_parts = _split(a, pa), _split(b, pb)
    depth = max(len(a_parts), len(b_parts))
    acc = None
    for i, ai in enumerate(a_parts):
        for j, bj in enumerate(b_parts):
            if i + j < depth:
                t = lax.dot_general(ai, bj, dims, preferred_element_type=F32)
                acc = t if acc is None else acc + t
    return acc


def _sigmoid(x):
    return 0.5 * jnp.tanh(0.5 * x) + 0.5


def _silu(x):
    return x * _sigmoid(x)


def _iota(shape, axis):
    return lax.broadcasted_iota(jnp.int32, shape, axis)


def _div(x, d):
    assert d & (d - 1) == 0
    return x >> (d.bit_length() - 1)


def _mod(x, d):
    assert d & (d - 1) == 0
    return x & (d - 1)


def _per_head(head, values):
    out = jnp.full(head.shape, values[-1], F32)
    for h in range(len(values) - 2, -1, -1):
        out = jnp.where(head == h, values[h], out)
    return out


def _head_mask(rows_per_head, n_rows):
    return _div(_iota((n_rows, WIDTH), 0), rows_per_head) == _div(_iota((n_rows, WIDTH), 1), HEAD_DIM)


def _stack_heads(x, mask):
    return jnp.where(mask, jnp.concatenate([x] * N_HEADS, axis=0), 0.0)


def _unstack_heads(x_st, mask, c):
    x_st = jnp.where(mask, x_st, 0.0)
    out = x_st[0:c]
    for h in range(1, N_HEADS):
        out = out + x_st[h * c:(h + 1) * c]
    return out


def _ones_bd():
    return _same_head().astype(BF16)


def _same_head():
    return _div(_iota((WIDTH, WIDTH), 0), HEAD_DIM) == _div(_iota((WIDTH, WIDTH), 1), HEAD_DIM)


def _head_norm(x, ones_bd, g, b, eps):
    mu = _mm(x, ones_bd, pa=2) * (1.0 / HEAD_DIM)
    d = x - mu
    var = _mm(d * d, ones_bd, pa=2) * (1.0 / HEAD_DIM)
    return d * lax.rsqrt(var + eps) * g + b


def _layer_norm(x, g, b):
    mu = jnp.mean(x, axis=-1, keepdims=True)
    d = x - mu
    var = jnp.mean(d * d, axis=-1, keepdims=True)
    return d * lax.rsqrt(var + LN_EPS) * g + b


def _resident(shape):
    nd = len(shape)
    return pl.BlockSpec(shape, lambda *_: (0,) * nd, pipeline_mode=pl.Buffered(1))


def _proj_kernel(x_ref, w_ref, *refs, seq_steps):
    if seq_steps is None:
        rw_ref, pool_ref, ret_ref, dil_ref = refs
    else:
        pw_ref, ps_ref, rw_ref, pool_ref, ret_ref, dil_ref, ob_ref, ext_scr = refs
    xb = x_ref[...].astype(BF16)
    for ref, lo, hi in ((rw_ref, 0, COL_POOL), (pool_ref, COL_POOL, COL_RET),
                        (ret_ref, COL_RET, COL_DIL), (dil_ref, COL_DIL, COL_GATE)):
        for s in range(lo, hi, 512):
            e = min(s + 512, hi)
            ref[:, s - lo:e - lo] = _mm(xb, w_ref[:, s:e])
    if seq_steps is not None:
        tm = x_ref.shape[0]
        step = lax.rem(pl.program_id(0), seq_steps)

        @pl.when(step == 0)
        def _():
            ext_scr[0:16, :] = jnp.zeros((16, WIDTH), F32)

        ext_scr[16:16 + tm, :] = pool_ref[...]
        ob_ref[...] = _pool_rows(ext_scr, tm, step * tm, pw_ref[...], ps_ref[...])


def _layer_resident(w, l):
    nd = w.ndim - 1
    return pl.BlockSpec((None,) + w.shape[1:], lambda *_: (l,) + (0,) * nd, pipeline_mode=pl.Buffered(1))


def _project(x, w_in_b, l, tm, pool=None):
    m = x.shape[0]
    widths = (COL_POOL, WIDTH, 4 * WIDTH, DIL_COLS)
    row = lambda w: pl.BlockSpec((tm, w), lambda i: (i, 0))
    if pool is None:
        extra_in, extra_ops, extra_w, scratch, seq_steps = [], [], (), [], None
    else:
        w_bd, scale, seq_len = pool
        assert seq_len % tm == 0
        extra_in, extra_ops, extra_w = [_resident(w_bd.shape), _resident(scale.shape)], [w_bd, scale], (WIDTH,)
        scratch, seq_steps = [pltpu.VMEM((16 + tm, WIDTH), F32)], seq_len // tm
    return pl.pallas_call(
        functools.partial(_proj_kernel, seq_steps=seq_steps),
        grid=(m // tm,),
        in_specs=[row(D_MODEL), _layer_resident(w_in_b, l)] + extra_in,
        out_specs=[row(w) for w in widths + extra_w],
        out_shape=[jax.ShapeDtypeStruct((m, w), F32) for w in widths + extra_w],
        scratch_shapes=scratch,
        compiler_params=pltpu.CompilerParams(dimension_semantics=("arbitrary",), vmem_limit_bytes=VMEM_LIMIT),
        name="proj",
    )(x, w_in_b, *extra_ops)


BNT = (((2,), (2,)), ((0,), (0,)))
BNN = (((2,), (1,)), ((0,), (0,)))
BTN = (((1,), (1,)), ((0,), (0,)))


def _rwkv_kernel(p_ref, prev_ref, s0_ref, mu_ref, w0_ref, w2_ref, a0_ref, a2_ref, g2_ref, kk_ref, ka_ref,
                 rk_ref, gng_ref, gnb_ref, o_ref, st_ref, s_scr, prev_scr, buf_scr, *, c, t_in, nb):
    ci = pl.program_id(1)

    @pl.when(ci == 0)
    def _():
        s_scr[...] = s0_ref[...]
        for b in range(nb):
            prev_scr[b, 0:1, :] = prev_ref[b]

    first = _iota((c, RW_COLS), 0) == 0
    ps, shs = [], []
    for b in range(nb):
        if t_in == c:
            pb_ = p_ref[b]
        else:
            buf_scr[b] = jnp.zeros((c, RW_COLS), F32)
            buf_scr[b, 0:t_in, :] = p_ref[b]
            pb_ = buf_scr[b]
        shs.append(jnp.where(first, prev_scr[b, 0:1, :], pltpu.roll(pb_, 1, 0)))
        prev_scr[b, 0:1, :] = pb_[c - 1:c, :]
        ps.append(pb_)
    p, shifted = jnp.concatenate(ps, axis=0), jnp.concatenate(shs, axis=0)
    m = nb * c
    u = p + (shifted - p) * mu_ref[...]
    r, k, v, ul = u[:, 0:WIDTH], u[:, WIDTH:2 * WIDTH], u[:, 2 * WIDTH:3 * WIDTH], u[:, 3 * WIDTH:]

    w_pre = w0_ref[...] + _mm(jnp.tanh(ul), w2_ref[...])
    logw = -math.exp(-0.5) * _sigmoid(w_pre)
    a = _sigmoid(a0_ref[...] + _mm(ul, a2_ref[...]))
    g = _mm(_sigmoid(ul), g2_ref[...])

    ones_bd = _ones_bd()
    kk = k * kk_ref[...]
    kk = kk * lax.rsqrt(jnp.maximum(_mm(kk * kk, ones_bd, pa=2), 1e-24))
    k_mod = k * (1.0 + (a - 1.0) * ka_ref[...])
    a_vec, b_vec = -kk, kk * a
    if t_in < c:
        live = _mod(_iota((m, WIDTH), 0), c) < t_in
        logw = jnp.where(live, logw, 0.0)
        a_vec, b_vec = jnp.where(live, a_vec, 0.0), jnp.where(live, b_vec, 0.0)
        k_mod, v = jnp.where(live, k_mod, 0.0), jnp.where(live, v, 0.0)

    qi, qj = _iota((m, m), 0), _iota((m, m), 1)
    tri = ((_div(qi, c) == _div(qj, c)) & (qi >= qj)).astype(BF16)
    cum2 = _mm(tri, logw, pb=3)
    seq = lambda x: x.reshape(nb, c, WIDTH)
    cum, lw3 = seq(cum2), seq(logw)
    cum_end = cum[:, c - 1:c, :]
    e_neg, e_end = jnp.exp(-cum), jnp.exp(cum_end - cum)
    a_t = seq(a_vec) * jnp.exp(cum - lw3)
    r_t = seq(r) * jnp.exp(cum)
    b_t, k_t = seq(b_vec) * e_neg, seq(k_mod) * e_neg
    b_e, k_e = seq(b_vec) * e_end, seq(k_mod) * e_end
    v3 = seq(v)

    n = N_HEADS * c
    hm = _head_mask(c, n)
    stack = lambda x: jnp.where(hm, jnp.concatenate([x] * N_HEADS, axis=1), 0.0)
    a_st, r_st, b_st, v_st = stack(a_t), stack(r_t), stack(b_t), stack(v3)
    ri, rj = _iota((n, n), 0), _iota((n, n), 1)
    same = _div(ri, c) == _div(rj, c)
    strict_bd = same & (_mod(ri, c) > _mod(rj, c))
    incl_bd = same & (_mod(ri, c) >= _mod(rj, c))
    ti, tj = _mod(_iota((n, c), 0), c), _iota((n, c), 1)
    strict_st, incl_st = ti > tj, ti >= tj

    mm = functools.partial(_mm, pa=RW_P, pb=RW_P)
    a_ab = jnp.where(strict_bd, mm(a_st, b_st, BNT), 0.0)
    inv = jnp.where(ri == rj, 1.0, 0.0) + a_ab
    pw = a_ab
    for _ in range(int(math.log2(c)) - 1):
        pw = mm(pw, pw, BNN)
        inv = inv + mm(inv, pw, BNN)
    a_ak = jnp.where(strict_st, mm(a_st, k_t, BNT), 0.0)
    z_st = jnp.where(hm, mm(a_ak, v3, BNN), 0.0)
    wu = mm(inv, jnp.concatenate([a_st, z_st], axis=2), BNN)
    w_st, u0_st = wu[:, :, 0:WIDTH], wu[:, :, WIDTH:]

    s0 = s_scr[...]
    u_st = mm(w_st, s0, BNT) + u0_st
    lhs = jnp.concatenate([u_st, v_st], axis=1)
    rhs = jnp.concatenate([stack(b_e), stack(k_e)], axis=1)
    s_scr[...] = s0 * jnp.exp(cum_end) + mm(lhs, rhs, BTN)

    a_rb = jnp.where(incl_bd, mm(r_st, b_st, BNT), 0.0)
    a_rk = jnp.where(incl_st, mm(r_st, k_t, BNT), 0.0)
    o_st = jnp.where(hm, mm(r_st, s0, BNT) + mm(a_rb, u_st, BNN) + mm(a_rk, v3, BNN), 0.0)
    o3 = o_st[:, 0:c]
    for h in range(1, N_HEADS):
        o3 = o3 + o_st[:, h * c:(h + 1) * c]
    o = o3.reshape(m, WIDTH)

    o = _head_norm(o, ones_bd, gng_ref[...], gnb_ref[...], RW_GN_EPS)
    o = o + _mm(r * k_mod * rk_ref[...], ones_bd, pa=2) * v
    o_ref[...] = (o * g).reshape(nb, c, WIDTH)[:, 0:t_in]

    @pl.when(ci == pl.num_programs(1) - 1)
    def _():
        st_ref[...] = s_scr[...]


def _block_diag_heads(s):
    b = s.shape[0]
    eye = jnp.eye(N_HEADS, dtype=s.dtype)
    return jnp.einsum('bhij,hg->bhigj', s, eye).reshape(b, WIDTH, WIDTH)


def _diag_heads(s_bd):
    b = s_bd.shape[0]
    s = s_bd.reshape(b, N_HEADS, HEAD_DIM, N_HEADS, HEAD_DIM)
    return jnp.stack([s[:, h, :, h, :] for h in range(N_HEADS)], axis=1)


def _rwkv_branch(p, p_prev, wkv0, lw):
    b, t, _ = p.shape
    c = RW_CHUNK
    t_in = c if t % c == 0 else t
    nc = t // c if t % c == 0 else 1
    vec = lambda x: x.reshape(1, -1)
    pad_rows = lambda w, lo: jnp.zeros((WIDTH, WIDTH), F32).at[lo:lo + w.shape[0]].set(w).astype(BF16)
    params = [vec(lw['rw_mu']), vec(lw['rw_w0']), pad_rows(lw['rw_w2'], 0), vec(lw['rw_a0']),
              pad_rows(lw['rw_a2'], 64), pad_rows(lw['rw_g2'], 128), vec(lw['rw_kk']), vec(lw['rw_ka']),
              vec(lw['rw_rk']), vec(lw['rw_gn_g']), vec(lw['rw_gn_b'])]
    nb = RW_BATCH
    assert b % nb == 0
    o, st = pl.pallas_call(
        functools.partial(_rwkv_kernel, c=c, t_in=t_in, nb=nb),
        grid=(b // nb, nc),
        in_specs=[pl.BlockSpec((nb, t_in, RW_COLS), lambda i, j: (i, j, 0)),
                  pl.BlockSpec((nb, 1, RW_COLS), lambda i, j: (i, 0, 0)),
                  pl.BlockSpec((nb, WIDTH, WIDTH), lambda i, j: (i, 0, 0))]
                 + [_resident(x.shape) for x in params],
        out_specs=[pl.BlockSpec((nb, t_in, WIDTH), lambda i, j: (i, j, 0)),
                   pl.BlockSpec((nb, WIDTH, WIDTH), lambda i, j: (i, 0, 0))],
        out_shape=[jax.ShapeDtypeStruct((b, t, WIDTH), F32), jax.ShapeDtypeStruct((b, WIDTH, WIDTH), F32)],
        scratch_shapes=[pltpu.VMEM((nb, WIDTH, WIDTH), F32), pltpu.VMEM((nb, 8, RW_COLS), F32),
                        pltpu.VMEM((nb, c, RW_COLS), F32)],
        compiler_params=pltpu.CompilerParams(dimension_semantics=("parallel", "arbitrary"),
                                             vmem_limit_bytes=VMEM_LIMIT),
        name="rwkv7",
    )(p, p_prev.reshape(b, 1, RW_COLS), _block_diag_heads(wkv0), *params)
    return o, _diag_heads(st)


def _pool_rows(ext_scr, c, pos_first, w, scale):
    x = ext_scr[16:16 + c, :]
    sums, acc, off = [], x, 1
    for win in POOL_WINDOWS:
        while off < win:
            acc = acc + ext_scr[16 - off:16 - off + c, :]
            off += 1
        sums.append(acc)
    pos = pos_first + _iota((c, WIDTH), 0)
    grp = _div(_iota((c, WIDTH), 1), HEAD_DIM)
    mean = jnp.zeros((c, WIDTH), F32)
    for gi, win in enumerate(POOL_WINDOWS):
        cnt = jnp.minimum(win, pos + 1).astype(F32)
        mean = jnp.where(grp == gi, sums[gi] / cnt, mean)
    ext_scr[0:16, :] = ext_scr[c:c + 16, :]
    return _mm(mean - x, w) * scale


def _pool_kernel(u_ref, buf_ref, w_ref, scale_ref, o_ref, ext_scr, *, c, t_in, pos0):
    ci = pl.program_id(1)

    @pl.when(ci == 0)
    def _():
        ext_scr[0:16, :] = buf_ref[0]

    if t_in < c:
        ext_scr[16:16 + c, :] = jnp.zeros((c, WIDTH), F32)
    ext_scr[16:16 + t_in, :] = u_ref[0]
    o_ref[0] = _pool_rows(ext_scr, c, pos0 + ci * c, w_ref[...], scale_ref[...])[0:t_in]


def _pool_weights(lw):
    w_bd = jnp.einsum('gcd,gh->gchd', lw['pool_w'], jnp.eye(N_HEADS, dtype=F32)).reshape(WIDTH, WIDTH)
    return w_bd.astype(BF16), lw['pool_scale'].reshape(1, WIDTH)


def _pool_branch(u, buf, pos0, lw):
    b, t, _ = u.shape
    c = 512 if t % 512 == 0 else 8
    t_in = c if t % c == 0 else t
    nc = t // c if t % c == 0 else 1
    w_bd, scale = _pool_weights(lw)
    buf16 = jnp.pad(buf, ((0, 0), (1, 0), (0, 0)))
    return pl.pallas_call(
        functools.partial(_pool_kernel, c=c, t_in=t_in, pos0=pos0),
        grid=(b, nc),
        in_specs=[pl.BlockSpec((1, t_in, WIDTH), lambda i, j: (i, j, 0)),
                  pl.BlockSpec((1, 16, WIDTH), lambda i, j: (i, 0, 0)),
                  _resident((WIDTH, WIDTH)), _resident((1, WIDTH))],
        out_specs=pl.BlockSpec((1, t_in, WIDTH), lambda i, j: (i, j, 0)),
        out_shape=jax.ShapeDtypeStruct((b, t, WIDTH), F32),
        scratch_shapes=[pltpu.VMEM((16 + c, WIDTH), F32)],
        compiler_params=pltpu.CompilerParams(dimension_semantics=("parallel", "arbitrary")),
        name="pool",
    )(u, buf16, w_bd, scale)


def _rot_half(x):
    first = _mod(_iota(x.shape, 1), HEAD_DIM) < (HEAD_DIM // 2)
    return jnp.where(first, pltpu.roll(x, WIDTH - HEAD_DIM // 2, 1), pltpu.roll(x, HEAD_DIM // 2, 1))


RET_LOG_DECAY = tuple(math.log(1.0 - 2.0 ** (-5.0 - h)) for h in range(N_HEADS))
ALIBI_SLOPES = tuple(2.0 ** (-8.0 * (i + 1) / (DIL_GROUPS * N_HEADS)) for i in range(DIL_GROUPS * N_HEADS))


def _ret_kernel(p_ref, cos_ref, sin_ref, s0_ref, gng_ref, gnb_ref, o_ref, st_ref, s_scr, buf_scr,
                dmask_scr, qd_scr, kd_scr, *, c, t_in, nb):
    ci = pl.program_id(1)
    n = N_HEADS * c
    lg = _per_head(_div(_iota((1, WIDTH), 1), HEAD_DIM), RET_LOG_DECAY)

    @pl.when(ci == 0)
    def _():
        s_scr[...] = s0_ref[...]
        idx = _iota((c, WIDTH), 0).astype(F32)
        qd_scr[...] = jnp.exp(lg * (idx + 1.0))
        kd_scr[...] = jnp.exp(lg * (t_in - 1.0 - idx))
        rel = _mod(_iota((n, c), 0), c) - _iota((n, c), 1)
        lg_rows = _per_head(_div(_iota((n, c), 0), c), RET_LOG_DECAY)
        dmask_scr[...] = jnp.where(rel >= 0, jnp.exp(lg_rows * jnp.maximum(rel, 0).astype(F32)), 0.0)

    hm = _head_mask(c, n)
    cos, sin = cos_ref[...], sin_ref[...]
    chunk_decay = jnp.exp(lg * float(t_in))
    same_head, ones_bd = _same_head(), _ones_bd()
    for b in range(nb):
        if t_in == c:
            p = p_ref[b]
        else:
            buf_scr[b] = jnp.zeros((c, 4 * WIDTH), F32)
            buf_scr[b, 0:t_in, :] = p_ref[b]
            p = buf_scr[b]
        q, k, v, g = (p[:, i * WIDTH:(i + 1) * WIDTH] for i in range(4))
        q = q * cos + _rot_half(q) * sin
        k = (k * cos + _rot_half(k) * sin) * HEAD_DIM ** -0.5
        if t_in < c:
            live = _iota((c, WIDTH), 0) < t_in
            k, v = jnp.where(live, k, 0.0), jnp.where(live, v, 0.0)
        inner = _mm(_stack_heads(q, hm), k, NT) * dmask_scr[...]
        s0 = s_scr[b]
        o = _unstack_heads(_mm(inner, v), hm, c) + _mm(q * qd_scr[...], s0)
        s_scr[b] = s0 * chunk_decay + jnp.where(same_head, _mm(k * kd_scr[...], v, TN), 0.0)
        o = _silu(g) * _head_norm(o, ones_bd, gng_ref[...], gnb_ref[...], LN_EPS)
        o_ref[b] = o[0:t_in]

    @pl.when(ci == pl.num_programs(1) - 1)
    def _():
        st_ref[...] = s_scr[...]


def _rope_tables(pos0, t, rows):
    half = HEAD_DIM // 2
    inv = ROPE_BASE ** (-jnp.arange(half, dtype=F32) / half)
    ang = (pos0 + jnp.arange(t, dtype=jnp.int32)).astype(F32)[:, None] * inv[None, :]
    cos = jnp.tile(jnp.cos(ang), (1, 2 * N_HEADS))
    sin = jnp.tile(jnp.concatenate([-jnp.sin(ang), jnp.sin(ang)], axis=1), (1, N_HEADS))
    pad = ((0, rows - t), (0, 0))
    return jnp.pad(cos, pad), jnp.pad(sin, pad)


def _ret_branch(p, s0, pos0, lw):
    b, t, _ = p.shape
    c = RET_CHUNK
    t_in = c if t % c == 0 else t
    nc = t // c if t % c == 0 else 1
    cos, sin = _rope_tables(pos0, t, nc * c)
    nb = RW_BATCH
    assert b % nb == 0
    o, st = pl.pallas_call(
        functools.partial(_ret_kernel, c=c, t_in=t_in, nb=nb),
        grid=(b // nb, nc),
        in_specs=[pl.BlockSpec((nb, t_in, 4 * WIDTH), lambda i, j: (i, j, 0)),
                  pl.BlockSpec((c, WIDTH), lambda i, j: (j, 0)),
                  pl.BlockSpec((c, WIDTH), lambda i, j: (j, 0)),
                  pl.BlockSpec((nb, WIDTH, WIDTH), lambda i, j: (i, 0, 0)),
                  _resident((1, WIDTH)), _resident((1, WIDTH))],
        out_specs=[pl.BlockSpec((nb, t_in, WIDTH), lambda i, j: (i, j, 0)),
                   pl.BlockSpec((nb, WIDTH, WIDTH), lambda i, j: (i, 0, 0))],
        out_shape=[jax.ShapeDtypeStruct((b, t, WIDTH), F32), jax.ShapeDtypeStruct((b, WIDTH, WIDTH), F32)],
        scratch_shapes=[pltpu.VMEM((nb, WIDTH, WIDTH), F32), pltpu.VMEM((nb, c, 4 * WIDTH), F32),
                        pltpu.VMEM((N_HEADS * c, c), F32), pltpu.VMEM((c, WIDTH), F32), pltpu.VMEM((c, WIDTH), F32)],
        compiler_params=pltpu.CompilerParams(dimension_semantics=("parallel", "arbitrary")),
        name="retention",
    )(p, cos, sin, _block_diag_heads(s0), lw['ret_gn_g'].reshape(1, WIDTH), lw['ret_gn_b'].reshape(1, WIDTH))
    return o, _diag_heads(st)


def _alibi_slope_rows(group, rows_per_head):
    head = _div(_iota((N_HEADS * rows_per_head, 1), 0), rows_per_head)
    return _per_head(head, ALIBI_SLOPES[group * N_HEADS:(group + 1) * N_HEADS])


def _dil_prompt_kernel(*refs, group, dil, span):
    q_refs, kc_refs, kp_refs, vc_refs, vp_refs = (refs[2 * i:2 * i + 2] for i in range(5))
    o_ref, lse_ref, k_scr, v_scr, o_scr, lse_scr = refs[10:]
    blk = DIL_BLOCK
    tail = blk * dil
    si = pl.program_id(1)
    for half in range(2):
        k_scr[half, 0:tail, :] = kp_refs[half][0]
        k_scr[half, tail:tail + span, :] = kc_refs[half][0]
        v_scr[half, 0:tail, :] = vp_refs[half][0]
        v_scr[half, tail:tail + span, :] = vc_refs[half][0]
    ki = _iota((blk, 2 * blk), 1)
    steps = blk + _iota((blk, 2 * blk), 0) - ki
    band = (steps >= 0) & (steps <= blk)
    masks = [jnp.where(band, (-ALIBI_SLOPES[group * N_HEADS + h] * dil) * steps.astype(F32), NEG_BIG)
             for h in range(N_HEADS)]
    sub_head = _div(_iota((blk, 128), 1), HEAD_DIM)
    for cc in range(span // tail):
        has_prev = (si > 0) | (ki >= blk)
        cc_masks = [jnp.where(has_prev, x, NEG_BIG) for x in masks] if cc == 0 else masks
        for r in range(dil):
            rows_q = pl.ds(cc * tail + r, blk, stride=dil) if dil > 1 else pl.ds(cc * tail, blk)
            rows_kv = pl.ds(cc * tail + r, 2 * blk, stride=dil) if dil > 1 else pl.ds(cc * tail, 2 * blk)
            for half in range(2):
                q2 = q_refs[half][0, rows_q, :] * HEAD_DIM ** -0.5
                k2 = k_scr[half, rows_kv, :].astype(BF16)
                v2 = v_scr[half, rows_kv, :].astype(BF16)
                o2 = lse2 = None
                for sub in range(2):
                    mine = sub_head == sub
                    s = _mm(jnp.where(mine, q2, 0.0), k2, NT) + cc_masks[2 * half + sub]
                    m = jnp.max(s, axis=1, keepdims=True)
                    e = jnp.exp(s - m)
                    l = jnp.sum(e, axis=1, keepdims=True)
                    o_h = _mm(e, v2) / l
                    lse_h = jnp.broadcast_to(m + jnp.log(l), (blk, 128))
                    o2 = o_h if o2 is None else jnp.where(mine, o_h, o2)
                    lse2 = lse_h if lse2 is None else jnp.where(mine, lse_h, lse2)
                o_scr[half, rows_q, :] = o2
                lse_scr[half, rows_q, :] = lse2
    o_ref[0] = jnp.concatenate([o_scr[0], o_scr[1]], axis=1)
    lse_ref[0] = jnp.concatenate([lse_scr[0], lse_scr[1]], axis=1)


def _dil_prompt_group(pd, group):
    b, s, _ = pd.shape
    win, dil = DIL_PATTERNS[group]
    span = DIL_SPAN
    tail = DIL_BLOCK * dil
    assert win // dil == DIL_BLOCK and span % tail == 0 and s % span == 0
    base = group * 3

    def cur(col):
        return [pl.BlockSpec((1, span, 128), lambda i, j, h=h: (i, j, 2 * (base + col) + h)) for h in range(2)]

    def prev(col):
        return [pl.BlockSpec((1, tail, 128),
                             lambda i, j, h=h: (i, jnp.maximum(j * (span // tail) - 1, 0), 2 * (base + col) + h))
                for h in range(2)]

    out_spec = pl.BlockSpec((1, span, WIDTH), lambda i, j: (i, j, 0))
    return pl.pallas_call(
        functools.partial(_dil_prompt_kernel, group=group, dil=dil, span=span),
        grid=(b, s // span),
        in_specs=cur(0) + cur(1) + prev(1) + cur(2) + prev(2),
        out_specs=[out_spec, out_spec],
        out_shape=[jax.ShapeDtypeStruct((b, s, WIDTH), F32)] * 2,
        scratch_shapes=[pltpu.VMEM((2, tail + span, 128), F32)] * 2 + [pltpu.VMEM((2, span, 128), F32)] * 2,
        compiler_params=pltpu.CompilerParams(dimension_semantics=("parallel", "arbitrary"),
                                             vmem_limit_bytes=VMEM_LIMIT),
        name=f"dil_prompt_g{group}",
    )(*([pd] * 10))


def _dil_combine(os_, ls):
    m = jnp.maximum(jnp.maximum(ls[0], ls[1]), ls[2])
    es = [jnp.exp(x - m) for x in ls]
    return (es[0] * os_[0] + es[1] * os_[1] + es[2] * os_[2]) / (es[0] + es[1] + es[2])


def _dil_prompt(pd):
    b, s, _ = pd.shape
    outs = [_dil_prompt_group(pd, g) for g in range(DIL_GROUPS)]
    return [x[0].reshape(b * s, WIDTH) for x in outs] + [x[1].reshape(b * s, WIDTH) for x in outs]


def _window_rows(cache, dil, t):
    depth, b, length = cache.shape[:3]
    first = length % dil
    if length % dil == 0 and first + t <= dil and t & (t - 1) == 0 and t < dil:
        kept = cache.reshape(depth, b, length // dil, dil, 2 * WIDTH)[:, :, :, first:first + t]
        return kept.reshape(depth, b, (length // dil) * t, 2 * WIDTH), (t, first)
    return cache.reshape(depth, b, length, 2 * WIDTH), (dil, 0)


def _dil_step_kernel(pd_ref, c0_ref, c1_ref, c2_ref, o_ref, buf_scr, *, t, tp, row_maps, lengths):
    buf_scr[...] = jnp.zeros_like(buf_scr)
    buf_scr[0:t, :] = pd_ref[0]
    pd = buf_scr[...]
    n = N_HEADS * tp
    hm = _head_mask(tp, n)
    qt = _mod(_iota((n, 1), 0), tp)
    outs, lses = [], []
    for g, cache_ref in enumerate((c0_ref, c1_ref, c2_ref)):
        win, dil = DIL_PATTERNS[g]
        length, rows = lengths[g], cache_ref.shape[1]
        keep, first = row_maps[g]
        jj = _iota((n, rows), 1)
        j_buf = jj if keep == dil else _div(jj, keep) * dil + first + _mod(jj, keep)
        q = pd[:, g * 3 * WIDTH:g * 3 * WIDTH + WIDTH]
        k_new = pd[:, g * 3 * WIDTH + WIDTH:g * 3 * WIDTH + 2 * WIDTH]
        v_new = pd[:, g * 3 * WIDTH + 2 * WIDTH:g * 3 * WIDTH + 3 * WIDTH]
        k_old, v_old = cache_ref[0, :, 0:WIDTH], cache_ref[0, :, WIDTH:2 * WIDTH]
        q_st = _stack_heads(q, hm)
        slope = _alibi_slope_rows(g, tp)
        d_old = length + qt - j_buf
        ok_old = (_mod(d_old, dil) == 0) & (d_old <= win)
        s_old = _mm(q_st, k_old, NT) * HEAD_DIM ** -0.5 - slope * d_old.astype(F32)
        s_old = jnp.where(ok_old, s_old, NEG_BIG)
        d_new = qt - _iota((n, tp), 1)
        ok_new = (d_new >= 0) & (_mod(d_new, dil) == 0)
        s_new = _mm(q_st, k_new, NT) * HEAD_DIM ** -0.5 - slope * d_new.astype(F32)
        s_new = jnp.where(ok_new, s_new, NEG_BIG)
        m = jnp.maximum(jnp.max(s_old, axis=1, keepdims=True), jnp.max(s_new, axis=1, keepdims=True))
        e_old, e_new = jnp.exp(s_old - m), jnp.exp(s_new - m)
        l = jnp.sum(e_old, axis=1, keepdims=True) + jnp.sum(e_new, axis=1, keepdims=True)
        outs.append((_mm(e_old, v_old) + _mm(e_new, v_new)) / l)
        lses.append(m + jnp.log(l))
    o_ref[0] = _unstack_heads(_dil_combine(outs, lses), hm, tp)[0:t]


def _dil_step(pd, caches, l):
    b, t, _ = pd.shape
    tp = 8
    flat = [c[0][0] for c in caches]
    row_maps = tuple(c[0][1] for c in caches)
    lengths = tuple(c[1] for c in caches)
    return pl.pallas_call(
        functools.partial(_dil_step_kernel, t=t, tp=tp, row_maps=row_maps, lengths=lengths),
        grid=(b,),
        in_specs=[pl.BlockSpec((1, t, DIL_COLS), lambda i: (i, 0, 0))]
                 + [pl.BlockSpec((None, 1, c.shape[2], 2 * WIDTH), lambda i: (l, i, 0, 0)) for c in flat],
        out_specs=pl.BlockSpec((1, t, WIDTH), lambda i: (i, 0, 0)),
        out_shape=jax.ShapeDtypeStruct((b, t, WIDTH), F32),
        scratch_shapes=[pltpu.VMEM((tp, DIL_COLS), F32)],
        compiler_params=pltpu.CompilerParams(dimension_semantics=("parallel",), vmem_limit_bytes=VMEM_LIMIT),
        name="dil_step",
    )(pd, *flat)


def _merge_kernel(h_ref, *refs):
    wg_ref, wb_ref, wo_ref, g_ref, b_ref, out_ref = refs[-6:]
    branch_refs = refs[:-6]
    branches = [r[...] for r in branch_refs[:3]]
    if len(branch_refs) == 4:
        branches.append(branch_refs[3][...])
    else:
        branches.append(_dil_combine([r[...] for r in branch_refs[3:6]], [r[...] for r in branch_refs[6:9]]))
    h = h_ref[...]
    hb = h.astype(BF16)
    z = None
    for n, o in enumerate(branches):
        gate = _sigmoid(_mm(hb, wg_ref[:, COL_GATE + n * D_MODEL:COL_GATE + (n + 1) * D_MODEL]))
        term = gate * _mm(o, wb_ref[n])
        z = term if z is None else z + term
    y = _mm(z, wo_ref[...])
    out_ref[...] = _layer_norm(DN_ALPHA * h + y, g_ref[...], b_ref[...])


def _merge(h, branches, w_in_b, w_branch_b, w_out_b, l, ln_g, ln_b, tm):
    m = h.shape[0]
    row = lambda w: pl.BlockSpec((tm, w), lambda i: (i, 0))
    return pl.pallas_call(
        _merge_kernel,
        grid=(m // tm,),
        in_specs=[row(D_MODEL)] + [row(WIDTH)] * len(branches)
                 + [_layer_resident(w_in_b, l), _layer_resident(w_branch_b, l), _layer_resident(w_out_b, l),
                    _resident((1, D_MODEL)), _resident((1, D_MODEL))],
        out_specs=row(D_MODEL),
        out_shape=jax.ShapeDtypeStruct((m, D_MODEL), F32),
        compiler_params=pltpu.CompilerParams(dimension_semantics=("parallel",), vmem_limit_bytes=VMEM_LIMIT),
        name="merge_ln",
    )(h, *branches, w_in_b, w_branch_b, w_out_b, ln_g.reshape(1, -1), ln_b.reshape(1, -1))


def _ffn_kernel(x_ref, wg_ref, wu_ref, wd_ref, g_ref, b_ref, out_ref):
    x = x_ref[...]
    xb = x.astype(BF16)
    act = _silu(_mm(xb, wg_ref[...])) * _mm(xb, wu_ref[...])
    out_ref[...] = _layer_norm(DN_ALPHA * x + _mm(act, wd_ref[...]), g_ref[...], b_ref[...])


def _ffn(x, wg, wu, wd, ln_g, ln_b, tm):
    m = x.shape[0]
    row = pl.BlockSpec((tm, D_MODEL), lambda i: (i, 0))
    return pl.pallas_call(
        _ffn_kernel,
        grid=(m // tm,),
        in_specs=[row, _resident(wg.shape), _resident(wu.shape), _resident(wd.shape),
                  _resident((1, D_MODEL)), _resident((1, D_MODEL))],
        out_specs=row,
        out_shape=jax.ShapeDtypeStruct((m, D_MODEL), F32),
        compiler_params=pltpu.CompilerParams(dimension_semantics=("parallel",), vmem_limit_bytes=VMEM_LIMIT),
        name="ffn_ln",
    )(x, wg, wu, wd, ln_g.reshape(1, -1), ln_b.reshape(1, -1))


def _route_kernel(x_ref, rt_ref, tri_ref, gate_ref, rank_ref, cnt_ref):
    logits = _mm(rt_ref[...], x_ref[...], NT, 2, 2)
    sub = _iota(logits.shape, 0).astype(F32)
    m1 = jnp.max(logits, axis=0, keepdims=True)
    i1 = jnp.min(jnp.where(logits == m1, sub, float(N_EXPERTS)), axis=0, keepdims=True)
    rest = jnp.where(sub == i1, NEG_BIG, logits)
    m2 = jnp.max(rest, axis=0, keepdims=True)
    i2 = jnp.min(jnp.where(rest == m2, sub, float(N_EXPERTS)), axis=0, keepdims=True)
    e2 = jnp.exp(m2 - m1)
    gate_ref[0] = jnp.where(sub == i1, 1.0 / (1.0 + e2), 0.0) + jnp.where(sub == i2, e2 / (1.0 + e2), 0.0)
    chosen = (sub == i1) | (sub == i2)
    sel = jnp.where(chosen, 1.0, 0.0)
    rank_ref[0] = jnp.where(chosen, _mm(sel, tri_ref[...]), -1.0)
    cnt_ref[0] = jnp.broadcast_to(jnp.sum(sel, axis=1, keepdims=True), cnt_ref.shape[1:])


def _moe_kernel(cnt_ref, x_ref, gate_ref, rank_ref, wg_ref, wu_ref, wd_ref, g_ref, b_ref, out_ref,
                xb_scr, xg_scr, yg_scr, *, ts, tb, nsb):
    i, e, f = pl.program_id(0), pl.program_id(1), pl.program_id(2)
    last_f = f == pl.num_programs(2) - 1

    @pl.when((e == 0) & (f == 0))
    def _():
        xb_scr[...] = x_ref[...].astype(BF16)
        out_ref[...] = jnp.zeros_like(out_ref)

    slot = _iota((ts, tb), 0)

    def expert(xg):
        return _mm(_silu(_mm(xg, wg_ref[0])) * _mm(xg, wu_ref[0]), wd_ref[0])

    def pick_of(sb, j):
        rank_row = rank_ref[sb, pl.ds(e, 1), :]
        return jnp.where(rank_row == (slot + j * ts).astype(F32), 1.0, 0.0)

    def scatter(sb, pick, y):
        w_slot = jnp.sum(pick * gate_ref[sb, pl.ds(e, 1), :], axis=1, keepdims=True)
        out_ref[sb * tb:(sb + 1) * tb, :] += _mm(pick, y * w_slot, TN)

    @pl.when(f == 0)
    def _():
        for sb in range(nsb):
            xg_scr[sb] = _mm(pick_of(sb, 0), xb_scr[sb * tb:(sb + 1) * tb, :]).astype(BF16)
        yg_scr[...] = jnp.zeros_like(yg_scr)

    yg_scr[...] += expert(xg_scr[...].reshape(nsb * ts, D_MODEL)).reshape(nsb, ts, D_MODEL)

    @pl.when(last_f)
    def _():
        for sb in range(nsb):
            scatter(sb, pick_of(sb, 0), yg_scr[sb])

    for sb in range(nsb):
        n_tiles = (cnt_ref[(i * nsb + sb) * N_EXPERTS + e] + (ts - 1)) // ts

        def overflow(j, carry, sb=sb):
            pick = pick_of(sb, j)
            scatter(sb, pick, expert(_mm(pick, xb_scr[sb * tb:(sb + 1) * tb, :]).astype(BF16)))
            return carry

        lax.fori_loop(1, n_tiles, overflow, 0)

    @pl.when((e == pl.num_programs(1) - 1) & last_f)
    def _():
        out_ref[...] = _layer_norm(DN_ALPHA * x_ref[...] + out_ref[...], g_ref[...], b_ref[...])


def _moe(x, router, wg, wu, wd, ln_g, ln_b, tm):
    m = x.shape[0]
    nblk = m // tm
    dff = wg.shape[2]
    nf = 4
    tf = dff // nf
    ts = min(MOE_TILE, tm)
    nsb = 2 if nblk % 2 == 0 else 1
    assert m % tm == 0 and tf % 128 == 0
    tri = (jnp.arange(tm)[:, None] < jnp.arange(tm)[None, :]).astype(BF16)
    gate, rank, cnt = pl.pallas_call(
        _route_kernel,
        grid=(nblk,),
        in_specs=[pl.BlockSpec((tm, D_MODEL), lambda i: (i, 0)), _resident((N_EXPERTS, D_MODEL)),
                  _resident((tm, tm))],
        out_specs=[pl.BlockSpec((1, N_EXPERTS, tm), lambda i: (i, 0, 0)),
                   pl.BlockSpec((1, N_EXPERTS, tm), lambda i: (i, 0, 0)),
                   pl.BlockSpec((1, N_EXPERTS, 128), lambda i: (i, 0, 0))],
        out_shape=[jax.ShapeDtypeStruct((nblk, N_EXPERTS, tm), F32), jax.ShapeDtypeStruct((nblk, N_EXPERTS, tm), F32),
                   jax.ShapeDtypeStruct((nblk, N_EXPERTS, 128), F32)],
        compiler_params=pltpu.CompilerParams(dimension_semantics=("parallel",), vmem_limit_bytes=VMEM_LIMIT),
        name="moe_route",
    )(x, router.T, tri)
    counts = cnt[:, :, 0].astype(jnp.int32).reshape(-1)
    rows = nsb * tm
    x_in = pl.BlockSpec((rows, D_MODEL), lambda i, e, f, c: (i, 0), pipeline_mode=pl.Buffered(1))
    meta = pl.BlockSpec((nsb, N_EXPERTS, tm), lambda i, e, f, c: (i, 0, 0))
    vec = pl.BlockSpec((1, D_MODEL), lambda i, e, f, c: (0, 0), pipeline_mode=pl.Buffered(1))
    return pl.pallas_call(
        functools.partial(_moe_kernel, ts=ts, tb=tm, nsb=nsb),
        grid_spec=pltpu.PrefetchScalarGridSpec(
            num_scalar_prefetch=1,
            grid=(nblk // nsb, N_EXPERTS, nf),
            in_specs=[x_in, meta, meta,
                      pl.BlockSpec((1, D_MODEL, tf), lambda i, e, f, c: (e, 0, f)),
                      pl.BlockSpec((1, D_MODEL, tf), lambda i, e, f, c: (e, 0, f)),
                      pl.BlockSpec((1, tf, D_MODEL), lambda i, e, f, c: (e, f, 0)),
                      vec, vec],
            out_specs=pl.BlockSpec((rows, D_MODEL), lambda i, e, f, c: (i, 0)),
            scratch_shapes=[pltpu.VMEM((rows, D_MODEL), BF16),
                            pltpu.VMEM((nsb, ts, D_MODEL), BF16), pltpu.VMEM((nsb, ts, D_MODEL), F32)]),
        out_shape=jax.ShapeDtypeStruct((m, D_MODEL), F32),
        compiler_params=pltpu.CompilerParams(dimension_semantics=("parallel", "arbitrary", "arbitrary"),
                                             vmem_limit_bytes=VMEM_LIMIT),
        name="moe_ln",
    )(counts, x, gate, rank, wg, wu, wd, ln_g.reshape(1, -1), ln_b.reshape(1, -1))


def _token_mix(h, pos0, rw_prev, wkv0, pool_buf, ret0, kv_bufs, lw, l, w_in_b, w_branch_b, w_out_b, ln_g, ln_b):
    b, t, _ = h.shape
    m = b * t
    tm = 512 if m % 512 == 0 else m
    hf = h.reshape(m, D_MODEL)
    if pool_buf is None:
        assert pos0 == 0 and t >= POOL_BUF
        p_rw, p_pool, p_ret, p_dil, o_b = _project(hf, w_in_b, l, tm, pool=(*_pool_weights(lw), t))
    else:
        p_rw, p_pool, p_ret, p_dil = _project(hf, w_in_b, l, tm)
    p_rw, p_pool = p_rw.reshape(b, t, -1), p_pool.reshape(b, t, -1)
    p_ret, p_dil = p_ret.reshape(b, t, -1), p_dil.reshape(b, t, -1)
    o_a, wkv_new = _rwkv_branch(p_rw, rw_prev, wkv0, lw)
    if pool_buf is None:
        pool_new = p_pool[:, -POOL_BUF:]
    else:
        o_b = _pool_branch(p_pool, pool_buf, pos0, lw)
        pool_new = jnp.concatenate([pool_buf, p_pool], axis=1)[:, -POOL_BUF:]
    o_c, ret_new = _ret_branch(p_ret, ret0, pos0, lw)
    def kv_rows(g, keep):
        lo = (3 * g + 1) * WIDTH
        return p_dil[:, t - keep:, lo:lo + 2 * WIDTH].reshape(b, keep, 2, N_HEADS, HEAD_DIM)

    if kv_bufs is None:
        dil_parts = _dil_prompt(p_dil)
        kv_new = [kv_rows(g, min(win, t)) for g, (win, _) in enumerate(DIL_PATTERNS)]
    else:
        dil_parts = [_dil_step(p_dil, kv_bufs, l).reshape(m, WIDTH)]
        kv_new = [kv_rows(g, t) for g in range(DIL_GROUPS)]
    branches = [x.reshape(m, WIDTH) for x in (o_a, o_b, o_c)] + dil_parts
    x1 = _merge(hf, branches, w_in_b, w_branch_b, w_out_b, l, ln_g, ln_b, tm)
    return x1, (wkv_new, p_rw[:, -1], pool_new, ret_new, kv_new[0], kv_new[1], kv_new[2])


def kernel(x_prompt, x_sample, state_wkv, state_shift, state_pool, state_ret, cache_kv_w128, cache_kv_w512, cache_kv_w2048, w_in, rw_mu, rw_w0, rw_w2, rw_a0, rw_a2, rw_g2, rw_kk, rw_ka, rw_rk, rw_gn_g, rw_gn_b, pool_w, pool_scale, ret_gn_g, ret_gn_b, w_branch, w_out, ln_g, ln_b, ffn_w_gate, ffn_w_up, ffn_w_down, moe_router, moe_w_gate, moe_w_up, moe_w_down):
    hp, hs = x_prompt, x_sample
    bp, tp, _ = hp.shape
    bs, ts, _ = hs.shape
    names = ('rw_mu', 'rw_w0', 'rw_w2', 'rw_a0', 'rw_a2', 'rw_g2', 'rw_kk', 'rw_ka', 'rw_rk', 'rw_gn_g',
             'rw_gn_b', 'pool_w', 'pool_scale', 'ret_gn_g', 'ret_gn_b')
    stacked = (rw_mu, rw_w0, rw_w2, rw_a0, rw_a2, rw_g2, rw_kk, rw_ka, rw_rk, rw_gn_g, rw_gn_b, pool_w,
               pool_scale, ret_gn_g, ret_gn_b)
    new_p = [[] for _ in range(7)]
    new_s = [[] for _ in range(7)]
    zeros = lambda *shape: jnp.zeros(shape, F32)
    w_in_b, wb, wo = w_in.astype(BF16), w_branch.astype(BF16), w_out.astype(BF16)
    caches = [(_window_rows(c, dil, ts), c.shape[2])
              for c, (_, dil) in zip((cache_kv_w128, cache_kv_w512, cache_kv_w2048), DIL_PATTERNS)]
    for l in range(DEPTH):
        lw = {k: v[l] for k, v in zip(names, stacked)}
        xp, st_p = _token_mix(hp, 0, zeros(bp, RW_COLS), zeros(bp, N_HEADS, HEAD_DIM, HEAD_DIM),
                              None, zeros(bp, N_HEADS, HEAD_DIM, HEAD_DIM), None,
                              lw, l, w_in_b, wb, wo, ln_g[l, 0], ln_b[l, 0])
        xs, st_s = _token_mix(hs, 8192, state_shift[l], state_wkv[l], state_pool[l], state_ret[l], caches,
                              lw, l, w_in_b, wb, wo, ln_g[l, 0], ln_b[l, 0])
        j = l // 2
        if l % 2 == 0:
            ws = [w[j].astype(BF16) for w in (ffn_w_gate, ffn_w_up, ffn_w_down)]
            xp = _ffn(xp, *ws, ln_g[l, 1], ln_b[l, 1], 256)
            xs = _ffn(xs, *ws, ln_g[l, 1], ln_b[l, 1], xs.shape[0])
        else:
            ws = [w[j].astype(BF16) for w in (moe_w_gate, moe_w_up, moe_w_down)]
            xp = _moe(xp, moe_router[j], *ws, ln_g[l, 1], ln_b[l, 1], 1024)
            xs = _moe(xs, moe_router[j], *ws, ln_g[l, 1], ln_b[l, 1], xs.shape[0])
        hp, hs = xp.reshape(bp, tp, D_MODEL), xs.reshape(bs, ts, D_MODEL)
        for i in range(7):
            new_p[i].append(st_p[i])
            new_s[i].append(st_s[i])
    outs_p = [jnp.stack(x) for x in new_p]
    outs_s = [jnp.stack(x) for x in new_s]
    return (hp, hs, *outs_p, *outs_s)
```

```python
import functools
import math

import jax
import jax.numpy as jnp
from jax import lax
from jax.experimental import pallas as pl
from jax.experimental.pallas import tpu as pltpu

F32 = jnp.float32
BF16 = jnp.bfloat16

D_MODEL = 1024
DEPTH = 2
HEAD_DIM = 64
N_HEADS = 4
WIDTH = N_HEADS * HEAD_DIM
RW_COLS = 1024
RW_GN_EPS = 64e-5
POOL_WINDOWS = (2, 4, 8, 16)
POOL_BUF = 15
RET_CHUNK = 128
ROPE_BASE = 10000.0
DIL_PATTERNS = ((128, 1), (512, 4), (2048, 16))
DIL_GROUPS = 3
DIL_BLOCK = 128
DIL_SPAN = 2048
DIL_COLS = 3 * DIL_GROUPS * WIDTH
N_BRANCH = 4
COL_POOL = RW_COLS
COL_RET = COL_POOL + WIDTH
COL_DIL = COL_RET + 4 * WIDTH
COL_GATE = COL_DIL + DIL_COLS
N_EXPERTS = 8
DN_ALPHA = (2 * DEPTH) ** 0.25
LN_EPS = 1e-5
RW_CHUNK = 64
MOE_TILE = 288
RW_P = 1
RW_BATCH = 4
NEG_BIG = -1e30

NN = (((1,), (0,)), ((), ()))
NT = (((1,), (1,)), ((), ()))
TN = (((0,), (0,)), ((), ()))

VMEM_LIMIT = 56 * 1024 * 1024


def _split(x, n):
    if x.dtype == BF16:
        return [x]
    parts, rem = [], x
    for i in range(n):
        p = rem.astype(BF16)
        parts.append(p)
        if i + 1 < n:
            rem = rem - p.astype(F32)
    return parts


def _mm(a, b, dims=NN, pa=1, pb=1):
    a_parts, b_parts = _split(a, pa), _split(b, pb)
    depth = max(len(a_parts), len(b_parts))
    acc = None
    for i, ai in enumerate(a_parts):
        for j, bj in enumerate(b_parts):
            if i + j < depth:
                t = lax.dot_general(ai, bj, dims, preferred_element_type=F32)
                acc = t if acc is None else acc + t
    return acc


def _sigmoid(x):
    return 0.5 * jnp.tanh(0.5 * x) + 0.5


def _silu(x):
    return x * _sigmoid(x)


def _iota(shape, axis):
    return lax.broadcasted_iota(jnp.int32, shape, axis)


def _div(x, d):
    assert d & (d - 1) == 0
    return x >> (d.bit_length() - 1)


def _mod(x, d):
    assert d & (d - 1) == 0
    return x & (d - 1)


def _per_head(head, values):
    out = jnp.full(head.shape, values[-1], F32)
    for h in range(len(values) - 2, -1, -1):
        out = jnp.where(head == h, values[h], out)
    return out


def _head_mask(rows_per_head, n_rows):
    return _div(_iota((n_rows, WIDTH), 0), rows_per_head) == _div(_iota((n_rows, WIDTH), 1), HEAD_DIM)


def _stack_heads(x, mask):
    return jnp.where(mask, jnp.concatenate([x] * N_HEADS, axis=0), 0.0)


def _unstack_heads(x_st, mask, c):
    x_st = jnp.where(mask, x_st, 0.0)
    out = x_st[0:c]
    for h in range(1, N_HEADS):
        out = out + x_st[h * c:(h + 1) * c]
    return out


def _ones_bd():
    return _same_head().astype(BF16)


def _same_head():
    return _div(_iota((WIDTH, WIDTH), 0), HEAD_DIM) == _div(_iota((WIDTH, WIDTH), 1), HEAD_DIM)


def _head_norm(x, ones_bd, g, b, eps):
    mu = _mm(x, ones_bd, pa=2) * (1.0 / HEAD_DIM)
    d = x - mu
    var = _mm(d * d, ones_bd, pa=2) * (1.0 / HEAD_DIM)
    return d * lax.rsqrt(var + eps) * g + b


def _layer_norm(x, g, b):
    mu = jnp.mean(x, axis=-1, keepdims=True)
    d = x - mu
    var = jnp.mean(d * d, axis=-1, keepdims=True)
    return d * lax.rsqrt(var + LN_EPS) * g + b


def _resident(shape):
    nd = len(shape)
    return pl.BlockSpec(shape, lambda *_: (0,) * nd, pipeline_mode=pl.Buffered(1))


def _proj_kernel(x_ref, w_ref, rw_ref, pool_ref, ret_ref, dil_ref):
    xb = x_ref[...].astype(BF16)
    for ref, lo, hi in ((rw_ref, 0, COL_POOL), (pool_ref, COL_POOL, COL_RET),
                        (ret_ref, COL_RET, COL_DIL), (dil_ref, COL_DIL, COL_GATE)):
        for s in range(lo, hi, 512):
            e = min(s + 512, hi)
            ref[:, s - lo:e - lo] = _mm(xb, w_ref[:, s:e])


def _layer_resident(w, l):
    nd = w.ndim - 1
    return pl.BlockSpec((None,) + w.shape[1:], lambda *_: (l,) + (0,) * nd, pipeline_mode=pl.Buffered(1))


def _project(x, w_in_b, l, tm):
    m = x.shape[0]
    widths = (COL_POOL, WIDTH, 4 * WIDTH, DIL_COLS)
    row = lambda w: pl.BlockSpec((tm, w), lambda i: (i, 0))
    return pl.pallas_call(
        _proj_kernel,
        grid=(m // tm,),
        in_specs=[row(D_MODEL), _layer_resident(w_in_b, l)],
        out_specs=[row(w) for w in widths],
        out_shape=[jax.ShapeDtypeStruct((m, w), F32) for w in widths],
        compiler_params=pltpu.CompilerParams(dimension_semantics=("parallel",), vmem_limit_bytes=VMEM_LIMIT),
        name="proj",
    )(x, w_in_b)


BNT = (((2,), (2,)), ((0,), (0,)))
BNN = (((2,), (1,)), ((0,), (0,)))
BTN = (((1,), (1,)), ((0,), (0,)))


def _rwkv_kernel(p_ref, prev_ref, s0_ref, mu_ref, w0_ref, w2_ref, a0_ref, a2_ref, g2_ref, kk_ref, ka_ref,
                 rk_ref, gng_ref, gnb_ref, o_ref, st_ref, s_scr, prev_scr, buf_scr, *, c, t_in, nb):
    ci = pl.program_id(1)

    @pl.when(ci == 0)
    def _():
        s_scr[...] = s0_ref[...]
        for b in range(nb):
            prev_scr[b, 0:1, :] = prev_ref[b]

    first = _iota((c, RW_COLS), 0) == 0
    ps, shs = [], []
    for b in range(nb):
        if t_in == c:
            pb_ = p_ref[b]
        else:
            buf_scr[b] = jnp.zeros((c, RW_COLS), F32)
            buf_scr[b, 0:t_in, :] = p_ref[b]
            pb_ = buf_scr[b]
        shs.append(jnp.where(first, prev_scr[b, 0:1, :], pltpu.roll(pb_, 1, 0)))
        prev_scr[b, 0:1, :] = pb_[c - 1:c, :]
        ps.append(pb_)
    p, shifted = jnp.concatenate(ps, axis=0), jnp.concatenate(shs, axis=0)
    m = nb * c
    u = p + (shifted - p) * mu_ref[...]
    r, k, v, ul = u[:, 0:WIDTH], u[:, WIDTH:2 * WIDTH], u[:, 2 * WIDTH:3 * WIDTH], u[:, 3 * WIDTH:]

    w_pre = w0_ref[...] + _mm(jnp.tanh(ul), w2_ref[...])
    logw = -math.exp(-0.5) * _sigmoid(w_pre)
    a = _sigmoid(a0_ref[...] + _mm(ul, a2_ref[...]))
    g = _mm(_sigmoid(ul), g2_ref[...])

    ones_bd = _ones_bd()
    kk = k * kk_ref[...]
    kk = kk * lax.rsqrt(jnp.maximum(_mm(kk * kk, ones_bd, pa=2), 1e-24))
    k_mod = k * (1.0 + (a - 1.0) * ka_ref[...])
    a_vec, b_vec = -kk, kk * a
    if t_in < c:
        live = _mod(_iota((m, WIDTH), 0), c) < t_in
        logw = jnp.where(live, logw, 0.0)
        a_vec, b_vec = jnp.where(live, a_vec, 0.0), jnp.where(live, b_vec, 0.0)
        k_mod, v = jnp.where(live, k_mod, 0.0), jnp.where(live, v, 0.0)

    qi, qj = _iota((m, m), 0), _iota((m, m), 1)
    tri = ((_div(qi, c) == _div(qj, c)) & (qi >= qj)).astype(BF16)
    cum2 = _mm(tri, logw, pb=3)
    seq = lambda x: x.reshape(nb, c, WIDTH)
    cum, lw3 = seq(cum2), seq(logw)
    cum_end = cum[:, c - 1:c, :]
    e_neg, e_end = jnp.exp(-cum), jnp.exp(cum_end - cum)
    a_t = seq(a_vec) * jnp.exp(cum - lw3)
    r_t = seq(r) * jnp.exp(cum)
    b_t, k_t = seq(b_vec) * e_neg, seq(k_mod) * e_neg
    b_e, k_e = seq(b_vec) * e_end, seq(k_mod) * e_end
    v3 = seq(v)

    n = N_HEADS * c
    hm = _head_mask(c, n)
    stack = lambda x: jnp.where(hm, jnp.concatenate([x] * N_HEADS, axis=1), 0.0)
    a_st, r_st, b_st, v_st = stack(a_t), stack(r_t), stack(b_t), stack(v3)
    ri, rj = _iota((n, n), 0), _iota((n, n), 1)
    same = _div(ri, c) == _div(rj, c)
    strict_bd = same & (_mod(ri, c) > _mod(rj, c))
    incl_bd = same & (_mod(ri, c) >= _mod(rj, c))
    ti, tj = _mod(_iota((n, c), 0), c), _iota((n, c), 1)
    strict_st, incl_st = ti > tj, ti >= tj

    mm = functools.partial(_mm, pa=RW_P, pb=RW_P)
    a_ab = jnp.where(strict_bd, mm(a_st, b_st, BNT), 0.0)
    inv = jnp.where(ri == rj, 1.0, 0.0) + a_ab
    pw = a_ab
    for _ in range(int(math.log2(c)) - 1):
        pw = mm(pw, pw, BNN)
        inv = inv + mm(inv, pw, BNN)
    a_ak = jnp.where(strict_st, mm(a_st, k_t, BNT), 0.0)
    z_st = jnp.where(hm, mm(a_ak, v3, BNN), 0.0)
    wu = mm(inv, jnp.concatenate([a_st, z_st], axis=2), BNN)
    w_st, u0_st = wu[:, :, 0:WIDTH], wu[:, :, WIDTH:]

    s0 = s_scr[...]
    u_st = mm(w_st, s0, BNT) + u0_st
    lhs = jnp.concatenate([u_st, v_st], axis=1)
    rhs = jnp.concatenate([stack(b_e), stack(k_e)], axis=1)
    s_scr[...] = s0 * jnp.exp(cum_end) + mm(lhs, rhs, BTN)

    a_rb = jnp.where(incl_bd, mm(r_st, b_st, BNT), 0.0)
    a_rk = jnp.where(incl_st, mm(r_st, k_t, BNT), 0.0)
    o_st = jnp.where(hm, mm(r_st, s0, BNT) + mm(a_rb, u_st, BNN) + mm(a_rk, v3, BNN), 0.0)
    o3 = o_st[:, 0:c]
    for h in range(1, N_HEADS):
        o3 = o3 + o_st[:, h * c:(h + 1) * c]
    o = o3.reshape(m, WIDTH)

    o = _head_norm(o, ones_bd, gng_ref[...], gnb_ref[...], RW_GN_EPS)
    o = o + _mm(r * k_mod * rk_ref[...], ones_bd, pa=2) * v
    o_ref[...] = (o * g).reshape(nb, c, WIDTH)[:, 0:t_in]

    @pl.when(ci == pl.num_programs(1) - 1)
    def _():
        st_ref[...] = s_scr[...]


def _block_diag_heads(s):
    b = s.shape[0]
    eye = jnp.eye(N_HEADS, dtype=s.dtype)
    return jnp.einsum('bhij,hg->bhigj', s, eye).reshape(b, WIDTH, WIDTH)


def _diag_heads(s_bd):
    b = s_bd.shape[0]
    s = s_bd.reshape(b, N_HEADS, HEAD_DIM, N_HEADS, HEAD_DIM)
    return jnp.stack([s[:, h, :, h, :] for h in range(N_HEADS)], axis=1)


def _rwkv_branch(p, p_prev, wkv0, lw):
    b, t, _ = p.shape
    c = RW_CHUNK
    t_in = c if t % c == 0 else t
    nc = t // c if t % c == 0 else 1
    vec = lambda x: x.reshape(1, -1)
    pad_rows = lambda w, lo: jnp.zeros((WIDTH, WIDTH), F32).at[lo:lo + w.shape[0]].set(w).astype(BF16)
    params = [vec(lw['rw_mu']), vec(lw['rw_w0']), pad_rows(lw['rw_w2'], 0), vec(lw['rw_a0']),
              pad_rows(lw['rw_a2'], 64), pad_rows(lw['rw_g2'], 128), vec(lw['rw_kk']), vec(lw['rw_ka']),
              vec(lw['rw_rk']), vec(lw['rw_gn_g']), vec(lw['rw_gn_b'])]
    nb = RW_BATCH
    assert b % nb == 0
    o, st = pl.pallas_call(
        functools.partial(_rwkv_kernel, c=c, t_in=t_in, nb=nb),
        grid=(b // nb, nc),
        in_specs=[pl.BlockSpec((nb, t_in, RW_COLS), lambda i, j: (i, j, 0)),
                  pl.BlockSpec((nb, 1, RW_COLS), lambda i, j: (i, 0, 0)),
                  pl.BlockSpec((nb, WIDTH, WIDTH), lambda i, j: (i, 0, 0))]
                 + [_resident(x.shape) for x in params],
        out_specs=[pl.BlockSpec((nb, t_in, WIDTH), lambda i, j: (i, j, 0)),
                   pl.BlockSpec((nb, WIDTH, WIDTH), lambda i, j: (i, 0, 0))],
        out_shape=[jax.ShapeDtypeStruct((b, t, WIDTH), F32), jax.ShapeDtypeStruct((b, WIDTH, WIDTH), F32)],
        scratch_shapes=[pltpu.VMEM((nb, WIDTH, WIDTH), F32), pltpu.VMEM((nb, 8, RW_COLS), F32),
                        pltpu.VMEM((nb, c, RW_COLS), F32)],
        compiler_params=pltpu.CompilerParams(dimension_semantics=("parallel", "arbitrary"),
                                             vmem_limit_bytes=VMEM_LIMIT),
        name="rwkv7",
    )(p, p_prev.reshape(b, 1, RW_COLS), _block_diag_heads(wkv0), *params)
    return o, _diag_heads(st)


def _pool_rows(ext_scr, c, pos_first, w, scale):
    x = ext_scr[16:16 + c, :]
    sums, acc, off = [], x, 1
    for win in POOL_WINDOWS:
        while off < win:
            acc = acc + ext_scr[16 - off:16 - off + c, :]
            off += 1
        sums.append(acc)
    pos = pos_first + _iota((c, WIDTH), 0)
    grp = _div(_iota((c, WIDTH), 1), HEAD_DIM)
    mean = jnp.zeros((c, WIDTH), F32)
    for gi, win in enumerate(POOL_WINDOWS):
        cnt = jnp.minimum(win, pos + 1).astype(F32)
        mean = jnp.where(grp == gi, sums[gi] / cnt, mean)
    ext_scr[0:16, :] = ext_scr[c:c + 16, :]
    return _mm(mean - x, w) * scale


def _pool_kernel(u_ref, buf_ref, w_ref, scale_ref, o_ref, ext_scr, *, c, t_in, pos0):
    ci = pl.program_id(1)

    @pl.when(ci == 0)
    def _():
        ext_scr[0:16, :] = buf_ref[0]

    if t_in < c:
        ext_scr[16:16 + c, :] = jnp.zeros((c, WIDTH), F32)
    ext_scr[16:16 + t_in, :] = u_ref[0]
    o_ref[0] = _pool_rows(ext_scr, c, pos0 + ci * c, w_ref[...], scale_ref[...])[0:t_in]


def _pool_weights(lw):
    w_bd = jnp.einsum('gcd,gh->gchd', lw['pool_w'], jnp.eye(N_HEADS, dtype=F32)).reshape(WIDTH, WIDTH)
    return w_bd.astype(BF16), lw['pool_scale'].reshape(1, WIDTH)


def _pool_branch(u, buf, pos0, lw):
    b, t, _ = u.shape
    c = 512 if t % 512 == 0 else 8
    t_in = c if t % c == 0 else t
    nc = t // c if t % c == 0 else 1
    w_bd, scale = _pool_weights(lw)
    buf16 = jnp.pad(buf, ((0, 0), (1, 0), (0, 0)))
    return pl.pallas_call(
        functools.partial(_pool_kernel, c=c, t_in=t_in, pos0=pos0),
        grid=(b, nc),
        in_specs=[pl.BlockSpec((1, t_in, WIDTH), lambda i, j: (i, j, 0)),
                  pl.BlockSpec((1, 16, WIDTH), lambda i, j: (i, 0, 0)),
                  _resident((WIDTH, WIDTH)), _resident((1, WIDTH))],
        out_specs=pl.BlockSpec((1, t_in, WIDTH), lambda i, j: (i, j, 0)),
        out_shape=jax.ShapeDtypeStruct((b, t, WIDTH), F32),
        scratch_shapes=[pltpu.VMEM((16 + c, WIDTH), F32)],
        compiler_params=pltpu.CompilerParams(dimension_semantics=("parallel", "arbitrary")),
        name="pool",
    )(u, buf16, w_bd, scale)


def _rot_half(x):
    first = _mod(_iota(x.shape, 1), HEAD_DIM) < (HEAD_DIM // 2)
    return jnp.where(first, pltpu.roll(x, WIDTH - HEAD_DIM // 2, 1), pltpu.roll(x, HEAD_DIM // 2, 1))


RET_LOG_DECAY = tuple(math.log(1.0 - 2.0 ** (-5.0 - h)) for h in range(N_HEADS))
ALIBI_SLOPES = tuple(2.0 ** (-8.0 * (i + 1) / (DIL_GROUPS * N_HEADS)) for i in range(DIL_GROUPS * N_HEADS))


def _ret_kernel(p_ref, cos_ref, sin_ref, s0_ref, gng_ref, gnb_ref, o_ref, st_ref, s_scr, buf_scr,
                dmask_scr, qd_scr, kd_scr, *, c, t_in, nb):
    ci = pl.program_id(1)
    n = N_HEADS * c
    lg = _per_head(_div(_iota((1, WIDTH), 1), HEAD_DIM), RET_LOG_DECAY)

    @pl.when(ci == 0)
    def _():
        s_scr[...] = s0_ref[...]
        idx = _iota((c, WIDTH), 0).astype(F32)
        qd_scr[...] = jnp.exp(lg * (idx + 1.0))
        kd_scr[...] = jnp.exp(lg * (t_in - 1.0 - idx))
        rel = _mod(_iota((n, c), 0), c) - _iota((n, c), 1)
        lg_rows = _per_head(_div(_iota((n, c), 0), c), RET_LOG_DECAY)
        dmask_scr[...] = jnp.where(rel >= 0, jnp.exp(lg_rows * jnp.maximum(rel, 0).astype(F32)), 0.0)

    hm = _head_mask(c, n)
    cos, sin = cos_ref[...], sin_ref[...]
    chunk_decay = jnp.exp(lg * float(t_in))
    same_head, ones_bd = _same_head(), _ones_bd()
    for b in range(nb):
        if t_in == c:
            p = p_ref[b]
        else:
            buf_scr[b] = jnp.zeros((c, 4 * WIDTH), F32)
            buf_scr[b, 0:t_in, :] = p_ref[b]
            p = buf_scr[b]
        q, k, v, g = (p[:, i * WIDTH:(i + 1) * WIDTH] for i in range(4))
        q = q * cos + _rot_half(q) * sin
        k = (k * cos + _rot_half(k) * sin) * HEAD_DIM ** -0.5
        if t_in < c:
            live = _iota((c, WIDTH), 0) < t_in
            k, v = jnp.where(live, k, 0.0), jnp.where(live, v, 0.0)
        inner = _mm(_stack_heads(q, hm), k, NT) * dmask_scr[...]
        s0 = s_scr[b]
        o = _unstack_heads(_mm(inner, v), hm, c) + _mm(q * qd_scr[...], s0)
        s_scr[b] = s0 * chunk_decay + jnp.where(same_head, _mm(k * kd_scr[...], v, TN), 0.0)
        o = _silu(g) * _head_norm(o, ones_bd, gng_ref[...], gnb_ref[...], LN_EPS)
        o_ref[b] = o[0:t_in]

    @pl.when(ci == pl.num_programs(1) - 1)
    def _():
        st_ref[...] = s_scr[...]


def _rope_tables(pos0, t, rows):
    half = HEAD_DIM // 2
    inv = ROPE_BASE ** (-jnp.arange(half, dtype=F32) / half)
    ang = (pos0 + jnp.arange(t, dtype=jnp.int32)).astype(F32)[:, None] * inv[None, :]
    cos = jnp.tile(jnp.cos(ang), (1, 2 * N_HEADS))
    sin = jnp.tile(jnp.concatenate([-jnp.sin(ang), jnp.sin(ang)], axis=1), (1, N_HEADS))
    pad = ((0, rows - t), (0, 0))
    return jnp.pad(cos, pad), jnp.pad(sin, pad)


def _ret_branch(p, s0, pos0, lw):
    b, t, _ = p.shape
    c = RET_CHUNK
    t_in = c if t % c == 0 else t
    nc = t // c if t % c == 0 else 1
    cos, sin = _rope_tables(pos0, t, nc * c)
    nb = RW_BATCH
    assert b % nb == 0
    o, st = pl.pallas_call(
        functools.partial(_ret_kernel, c=c, t_in=t_in, nb=nb),
        grid=(b // nb, nc),
        in_specs=[pl.BlockSpec((nb, t_in, 4 * WIDTH), lambda i, j: (i, j, 0)),
                  pl.BlockSpec((c, WIDTH), lambda i, j: (j, 0)),
                  pl.BlockSpec((c, WIDTH), lambda i, j: (j, 0)),
                  pl.BlockSpec((nb, WIDTH, WIDTH), lambda i, j: (i, 0, 0)),
                  _resident((1, WIDTH)), _resident((1, WIDTH))],
        out_specs=[pl.BlockSpec((nb, t_in, WIDTH), lambda i, j: (i, j, 0)),
                   pl.BlockSpec((nb, WIDTH, WIDTH), lambda i, j: (i, 0, 0))],
        out_shape=[jax.ShapeDtypeStruct((b, t, WIDTH), F32), jax.ShapeDtypeStruct((b, WIDTH, WIDTH), F32)],
        scratch_shapes=[pltpu.VMEM((nb, WIDTH, WIDTH), F32), pltpu.VMEM((nb, c, 4 * WIDTH), F32),
                        pltpu.VMEM((N_HEADS * c, c), F32), pltpu.VMEM((c, WIDTH), F32), pltpu.VMEM((c, WIDTH), F32)],
        compiler_params=pltpu.CompilerParams(dimension_semantics=("parallel", "arbitrary")),
        name="retention",
    )(p, cos, sin, _block_diag_heads(s0), lw['ret_gn_g'].reshape(1, WIDTH), lw['ret_gn_b'].reshape(1, WIDTH))
    return o, _diag_heads(st)


def _alibi_slope_rows(group, rows_per_head):
    head = _div(_iota((N_HEADS * rows_per_head, 1), 0), rows_per_head)
    return _per_head(head, ALIBI_SLOPES[group * N_HEADS:(group + 1) * N_HEADS])


def _dil_prompt_kernel(*refs, group, dil, span):
    q_refs, kc_refs, kp_refs, vc_refs, vp_refs = (refs[2 * i:2 * i + 2] for i in range(5))
    o_ref, lse_ref, k_scr, v_scr, o_scr, lse_scr = refs[10:]
    blk = DIL_BLOCK
    tail = blk * dil
    si = pl.program_id(1)
    for half in range(2):
        k_scr[half, 0:tail, :] = kp_refs[half][0]
        k_scr[half, tail:tail + span, :] = kc_refs[half][0]
        v_scr[half, 0:tail, :] = vp_refs[half][0]
        v_scr[half, tail:tail + span, :] = vc_refs[half][0]
    ki = _iota((blk, 2 * blk), 1)
    steps = blk + _iota((blk, 2 * blk), 0) - ki
    band = (steps >= 0) & (steps <= blk)
    masks = [jnp.where(band, (-ALIBI_SLOPES[group * N_HEADS + h] * dil) * steps.astype(F32), NEG_BIG)
             for h in range(N_HEADS)]
    sub_head = _div(_iota((blk, 128), 1), HEAD_DIM)
    for cc in range(span // tail):
        has_prev = (si > 0) | (ki >= blk)
        cc_masks = [jnp.where(has_prev, x, NEG_BIG) for x in masks] if cc == 0 else masks
        for r in range(dil):
            rows_q = pl.ds(cc * tail + r, blk, stride=dil) if dil > 1 else pl.ds(cc * tail, blk)
            rows_kv = pl.ds(cc * tail + r, 2 * blk, stride=dil) if dil > 1 else pl.ds(cc * tail, 2 * blk)
            for half in range(2):
                q2 = q_refs[half][0, rows_q, :] * HEAD_DIM ** -0.5
                k2 = k_scr[half, rows_kv, :].astype(BF16)
                v2 = v_scr[half, rows_kv, :].astype(BF16)
                o2 = lse2 = None
                for sub in range(2):
                    mine = sub_head == sub
                    s = _mm(jnp.where(mine, q2, 0.0), k2, NT) + cc_masks[2 * half + sub]
                    m = jnp.max(s, axis=1, keepdims=True)
                    e = jnp.exp(s - m)
                    l = jnp.sum(e, axis=1, keepdims=True)
                    o_h = _mm(e, v2) / l
                    lse_h = jnp.broadcast_to(m + jnp.log(l), (blk, 128))
                    o2 = o_h if o2 is None else jnp.where(mine, o_h, o2)
                    lse2 = lse_h if lse2 is None else jnp.where(mine, lse_h, lse2)
                o_scr[half, rows_q, :] = o2
                lse_scr[half, rows_q, :] = lse2
    o_ref[0] = jnp.concatenate([o_scr[0], o_scr[1]], axis=1)
    lse_ref[0] = jnp.concatenate([lse_scr[0], lse_scr[1]], axis=1)


def _dil_prompt_group(pd, group):
    b, s, _ = pd.shape
    win, dil = DIL_PATTERNS[group]
    span = DIL_SPAN
    tail = DIL_BLOCK * dil
    assert win // dil == DIL_BLOCK and span % tail == 0 and s % span == 0
    base = group * 3

    def cur(col):
        return [pl.BlockSpec((1, span, 128), lambda i, j, h=h: (i, j, 2 * (base + col) + h)) for h in range(2)]

    def prev(col):
        return [pl.BlockSpec((1, tail, 128),
                             lambda i, j, h=h: (i, jnp.maximum(j * (span // tail) - 1, 0), 2 * (base + col) + h))
                for h in range(2)]

    out_spec = pl.BlockSpec((1, span, WIDTH), lambda i, j: (i, j, 0))
    return pl.pallas_call(
        functools.partial(_dil_prompt_kernel, group=group, dil=dil, span=span),
        grid=(b, s // span),
        in_specs=cur(0) + cur(1) + prev(1) + cur(2) + prev(2),
        out_specs=[out_spec, out_spec],
        out_shape=[jax.ShapeDtypeStruct((b, s, WIDTH), F32)] * 2,
        scratch_shapes=[pltpu.VMEM((2, tail + span, 128), F32)] * 2 + [pltpu.VMEM((2, span, 128), F32)] * 2,
        compiler_params=pltpu.CompilerParams(dimension_semantics=("parallel", "arbitrary"),
                                             vmem_limit_bytes=VMEM_LIMIT),
        name=f"dil_prompt_g{group}",
    )(*([pd] * 10))


def _dil_combine(os_, ls):
    m = jnp.maximum(jnp.maximum(ls[0], ls[1]), ls[2])
    es = [jnp.exp(x - m) for x in ls]
    return (es[0] * os_[0] + es[1] * os_[1] + es[2] * os_[2]) / (es[0] + es[1] + es[2])


def _dil_prompt(pd):
    b, s, _ = pd.shape
    outs = [_dil_prompt_group(pd, g) for g in range(DIL_GROUPS)]
    return [x[0].reshape(b * s, WIDTH) for x in outs] + [x[1].reshape(b * s, WIDTH) for x in outs]


def _dil_step_kernel(pd_ref, c0_ref, c1_ref, c2_ref, o_ref, buf_scr, *, t, tp):
    buf_scr[...] = jnp.zeros_like(buf_scr)
    buf_scr[0:t, :] = pd_ref[0]
    pd = buf_scr[...]
    qt = _iota((tp, 1), 0)
    d_new = qt - _iota((tp, tp), 1)
    outs = [[None] * DIL_GROUPS for _ in range(N_HEADS)]
    lses = [[None] * DIL_GROUPS for _ in range(N_HEADS)]
    for g, cache_ref in enumerate((c0_ref, c1_ref, c2_ref)):
        win, dil = DIL_PATTERNS[g]
        length = cache_ref.shape[-1]
        d_old = length + qt - _iota((tp, length), 1)
        ok_old = (_mod(d_old, dil) == 0) & (d_old <= win)
        ok_new = (d_new >= 0) & (_mod(d_new, dil) == 0)
        for h in range(N_HEADS):
            slope = ALIBI_SLOPES[g * N_HEADS + h]
            lo = g * 3 * WIDTH + h * HEAD_DIM
            q = pd[:, lo:lo + HEAD_DIM] * HEAD_DIM ** -0.5
            k_new = pd[:, lo + WIDTH:lo + WIDTH + HEAD_DIM]
            v_new = pd[:, lo + 2 * WIDTH:lo + 2 * WIDTH + HEAD_DIM]
            k_t, v_t = cache_ref[0, 0, h], cache_ref[0, 1, h]
            s_old = jnp.where(ok_old, _mm(q, k_t) - slope * d_old.astype(F32), NEG_BIG)
            s_new = jnp.where(ok_new, _mm(q, k_new, NT) - slope * d_new.astype(F32), NEG_BIG)
            m = jnp.maximum(jnp.max(s_old, axis=1, keepdims=True), jnp.max(s_new, axis=1, keepdims=True))
            e_old, e_new = jnp.exp(s_old - m), jnp.exp(s_new - m)
            l = jnp.sum(e_old, axis=1, keepdims=True) + jnp.sum(e_new, axis=1, keepdims=True)
            outs[h][g] = (_mm(e_old, v_t, NT) + _mm(e_new, v_new)) / l
            lses[h][g] = m + jnp.log(l)
    o = jnp.concatenate([_dil_combine(outs[h], lses[h]) for h in range(N_HEADS)], axis=1)
    o_ref[0] = o[0:t]


def _dil_step(pd, caches, l):
    b, t, _ = pd.shape
    tp = 8
    flat = caches
    return pl.pallas_call(
        functools.partial(_dil_step_kernel, t=t, tp=tp),
        grid=(b,),
        in_specs=[pl.BlockSpec((1, t, DIL_COLS), lambda i: (i, 0, 0))]
                 + [pl.BlockSpec((None, 1) + c.shape[2:], lambda i: (l, i, 0, 0, 0, 0)) for c in flat],
        out_specs=pl.BlockSpec((1, t, WIDTH), lambda i: (i, 0, 0)),
        out_shape=jax.ShapeDtypeStruct((b, t, WIDTH), F32),
        scratch_shapes=[pltpu.VMEM((tp, DIL_COLS), F32)],
        compiler_params=pltpu.CompilerParams(dimension_semantics=("parallel",), vmem_limit_bytes=VMEM_LIMIT),
        name="dil_step",
    )(pd, *flat)


def _merge_kernel(h_ref, *refs):
    wg_ref, wb_ref, wo_ref, g_ref, b_ref, out_ref = refs[-6:]
    branch_refs = refs[:-6]
    branches = [r[...] for r in branch_refs[:3]]
    if len(branch_refs) == 4:
        branches.append(branch_refs[3][...])
    else:
        branches.append(_dil_combine([r[...] for r in branch_refs[3:6]], [r[...] for r in branch_refs[6:9]]))
    h = h_ref[...]
    hb = h.astype(BF16)
    z = None
    for n, o in enumerate(branches):
        gate = _sigmoid(_mm(hb, wg_ref[:, COL_GATE + n * D_MODEL:COL_GATE + (n + 1) * D_MODEL]))
        term = gate * _mm(o, wb_ref[n])
        z = term if z is None else z + term
    y = _mm(z, wo_ref[...])
    out_ref[...] = _layer_norm(DN_ALPHA * h + y, g_ref[...], b_ref[...])


def _merge(h, branches, w_in_b, w_branch_b, w_out_b, l, ln_g, ln_b, tm):
    m = h.shape[0]
    row = lambda w: pl.BlockSpec((tm, w), lambda i: (i, 0))
    return pl.pallas_call(
        _merge_kernel,
        grid=(m // tm,),
        in_specs=[row(D_MODEL)] + [row(WIDTH)] * len(branches)
                 + [_layer_resident(w_in_b, l), _layer_resident(w_branch_b, l), _layer_resident(w_out_b, l),
                    _resident((1, D_MODEL)), _resident((1, D_MODEL))],
        out_specs=row(D_MODEL),
        out_shape=jax.ShapeDtypeStruct((m, D_MODEL), F32),
        compiler_params=pltpu.CompilerParams(dimension_semantics=("parallel",), vmem_limit_bytes=VMEM_LIMIT),
        name="merge_ln",
    )(h, *branches, w_in_b, w_branch_b, w_out_b, ln_g.reshape(1, -1), ln_b.reshape(1, -1))


def _ffn_kernel(x_ref, wg_ref, wu_ref, wd_ref, g_ref, b_ref, out_ref):
    x = x_ref[...]
    xb = x.astype(BF16)
    act = _silu(_mm(xb, wg_ref[...])) * _mm(xb, wu_ref[...])
    out_ref[...] = _layer_norm(DN_ALPHA * x + _mm(act, wd_ref[...]), g_ref[...], b_ref[...])


def _ffn(x, wg, wu, wd, ln_g, ln_b, tm):
    m = x.shape[0]
    row = pl.BlockSpec((tm, D_MODEL), lambda i: (i, 0))
    return pl.pallas_call(
        _ffn_kernel,
        grid=(m // tm,),
        in_specs=[row, _resident(wg.shape), _resident(wu.shape), _resident(wd.shape),
                  _resident((1, D_MODEL)), _resident((1, D_MODEL))],
        out_specs=row,
        out_shape=jax.ShapeDtypeStruct((m, D_MODEL), F32),
        compiler_params=pltpu.CompilerParams(dimension_semantics=("parallel",), vmem_limit_bytes=VMEM_LIMIT),
        name="ffn_ln",
    )(x, wg, wu, wd, ln_g.reshape(1, -1), ln_b.reshape(1, -1))


def _route_kernel(x_ref, rt_ref, tri_ref, gate_ref, rank_ref, cnt_ref):
    logits = _mm(rt_ref[...], x_ref[...], NT, 2, 2)
    sub = _iota(logits.shape, 0).astype(F32)
    m1 = jnp.max(logits, axis=0, keepdims=True)
    i1 = jnp.min(jnp.where(logits == m1, sub, float(N_EXPERTS)), axis=0, keepdims=True)
    rest = jnp.where(sub == i1, NEG_BIG, logits)
    m2 = jnp.max(rest, axis=0, keepdims=True)
    i2 = jnp.min(jnp.where(rest == m2, sub, float(N_EXPERTS)), axis=0, keepdims=True)
    e2 = jnp.exp(m2 - m1)
    gate_ref[0] = jnp.where(sub == i1, 1.0 / (1.0 + e2), 0.0) + jnp.where(sub == i2, e2 / (1.0 + e2), 0.0)
    chosen = (sub == i1) | (sub == i2)
    sel = jnp.where(chosen, 1.0, 0.0)
    rank_ref[0] = jnp.where(chosen, _mm(sel, tri_ref[...]), -1.0)
    cnt_ref[0] = jnp.broadcast_to(jnp.sum(sel, axis=1, keepdims=True), cnt_ref.shape[1:])


def _moe_kernel(cnt_ref, x_ref, gate_ref, rank_ref, wg_ref, wu_ref, wd_ref, g_ref, b_ref, out_ref,
                xb_scr, xg_scr, yg_scr, *, ts, tb, nsb):
    i, e, f = pl.program_id(0), pl.program_id(1), pl.program_id(2)
    last_f = f == pl.num_programs(2) - 1

    @pl.when((e == 0) & (f == 0))
    def _():
        xb_scr[...] = x_ref[...].astype(BF16)
        out_ref[...] = jnp.zeros_like(out_ref)

    slot = _iota((ts, tb), 0)

    def expert(xg):
        return _mm(_silu(_mm(xg, wg_ref[0])) * _mm(xg, wu_ref[0]), wd_ref[0])

    def pick_of(sb, j):
        rank_row = rank_ref[sb, pl.ds(e, 1), :]
        return jnp.where(rank_row == (slot + j * ts).astype(F32), 1.0, 0.0)

    def scatter(sb, pick, y):
        w_slot = jnp.sum(pick * gate_ref[sb, pl.ds(e, 1), :], axis=1, keepdims=True)
        out_ref[sb * tb:(sb + 1) * tb, :] += _mm(pick, y * w_slot, TN)

    @pl.when(f == 0)
    def _():
        for sb in range(nsb):
            xg_scr[sb] = _mm(pick_of(sb, 0), xb_scr[sb * tb:(sb + 1) * tb, :]).astype(BF16)
        yg_scr[...] = jnp.zeros_like(yg_scr)

    yg_scr[...] += expert(xg_scr[...].reshape(nsb * ts, D_MODEL)).reshape(nsb, ts, D_MODEL)

    @pl.when(last_f)
    def _():
        for sb in range(nsb):
            scatter(sb, pick_of(sb, 0), yg_scr[sb])

    for sb in range(nsb):
        n_tiles = (cnt_ref[(i * nsb + sb) * N_EXPERTS + e] + (ts - 1)) // ts

        def overflow(j, carry, sb=sb):
            pick = pick_of(sb, j)
            scatter(sb, pick, expert(_mm(pick, xb_scr[sb * tb:(sb + 1) * tb, :]).astype(BF16)))
            return carry

        lax.fori_loop(1, n_tiles, overflow, 0)

    @pl.when((e == pl.num_programs(1) - 1) & last_f)
    def _():
        out_ref[...] = _layer_norm(DN_ALPHA * x_ref[...] + out_ref[...], g_ref[...], b_ref[...])


def _moe(x, router, wg, wu, wd, ln_g, ln_b, tm):
    m = x.shape[0]
    nblk = m // tm
    dff = wg.shape[2]
    nf = 4
    tf = dff // nf
    ts = min(MOE_TILE, tm)
    nsb = 2 if nblk % 2 == 0 else 1
    assert m % tm == 0 and tf % 128 == 0
    tri = (jnp.arange(tm)[:, None] < jnp.arange(tm)[None, :]).astype(BF16)
    gate, rank, cnt = pl.pallas_call(
        _route_kernel,
        grid=(nblk,),
        in_specs=[pl.BlockSpec((tm, D_MODEL), lambda i: (i, 0)), _resident((N_EXPERTS, D_MODEL)),
                  _resident((tm, tm))],
        out_specs=[pl.BlockSpec((1, N_EXPERTS, tm), lambda i: (i, 0, 0)),
                   pl.BlockSpec((1, N_EXPERTS, tm), lambda i: (i, 0, 0)),
                   pl.BlockSpec((1, N_EXPERTS, 128), lambda i: (i, 0, 0))],
        out_shape=[jax.ShapeDtypeStruct((nblk, N_EXPERTS, tm), F32), jax.ShapeDtypeStruct((nblk, N_EXPERTS, tm), F32),
                   jax.ShapeDtypeStruct((nblk, N_EXPERTS, 128), F32)],
        compiler_params=pltpu.CompilerParams(dimension_semantics=("parallel",), vmem_limit_bytes=VMEM_LIMIT),
        name="moe_route",
    )(x, router.T, tri)
    counts = cnt[:, :, 0].astype(jnp.int32).reshape(-1)
    rows = nsb * tm
    x_in = pl.BlockSpec((rows, D_MODEL), lambda i, e, f, c: (i, 0), pipeline_mode=pl.Buffered(1))
    meta = pl.BlockSpec((nsb, N_EXPERTS, tm), lambda i, e, f, c: (i, 0, 0))
    vec = pl.BlockSpec((1, D_MODEL), lambda i, e, f, c: (0, 0), pipeline_mode=pl.Buffered(1))
    return pl.pallas_call(
        functools.partial(_moe_kernel, ts=ts, tb=tm, nsb=nsb),
        grid_spec=pltpu.PrefetchScalarGridSpec(
            num_scalar_prefetch=1,
            grid=(nblk // nsb, N_EXPERTS, nf),
            in_specs=[x_in, meta, meta,
                      pl.BlockSpec((1, D_MODEL, tf), lambda i, e, f, c: (e, 0, f)),
                      pl.BlockSpec((1, D_MODEL, tf), lambda i, e, f, c: (e, 0, f)),
                      pl.BlockSpec((1, tf, D_MODEL), lambda i, e, f, c: (e, f, 0)),
                      vec, vec],
            out_specs=pl.BlockSpec((rows, D_MODEL), lambda i, e, f, c: (i, 0)),
            scratch_shapes=[pltpu.VMEM((rows, D_MODEL), BF16),
                            pltpu.VMEM((nsb, ts, D_MODEL), BF16), pltpu.VMEM((nsb, ts, D_MODEL), F32)]),
        out_shape=jax.ShapeDtypeStruct((m, D_MODEL), F32),
        compiler_params=pltpu.CompilerParams(dimension_semantics=("parallel", "arbitrary", "arbitrary"),
                                             vmem_limit_bytes=VMEM_LIMIT),
        name="moe_ln",
    )(counts, x, gate, rank, wg, wu, wd, ln_g.reshape(1, -1), ln_b.reshape(1, -1))


def _token_mix(h, pos0, rw_prev, wkv0, pool_buf, ret0, kv_bufs, lw, l, w_in_b, w_branch_b, w_out_b, ln_g, ln_b):
    b, t, _ = h.shape
    m = b * t
    tm = 512 if m % 512 == 0 else m
    hf = h.reshape(m, D_MODEL)
    p_rw, p_pool, p_ret, p_dil = _project(hf, w_in_b, l, tm)
    p_rw, p_pool = p_rw.reshape(b, t, -1), p_pool.reshape(b, t, -1)
    p_ret, p_dil = p_ret.reshape(b, t, -1), p_dil.reshape(b, t, -1)
    o_a, wkv_new = _rwkv_branch(p_rw, rw_prev, wkv0, lw)
    o_b = _pool_branch(p_pool, pool_buf, pos0, lw)
    pool_new = jnp.concatenate([pool_buf, p_pool], axis=1)[:, -POOL_BUF:]
    o_c, ret_new = _ret_branch(p_ret, ret0, pos0, lw)
    def kv_rows(g, keep):
        lo = (3 * g + 1) * WIDTH
        return p_dil[:, t - keep:, lo:lo + 2 * WIDTH].reshape(b, keep, 2, N_HEADS, HEAD_DIM)

    if kv_bufs is None:
        dil_parts = _dil_prompt(p_dil)
        kv_new = [kv_rows(g, min(win, t)) for g, (win, _) in enumerate(DIL_PATTERNS)]
    else:
        dil_parts = [_dil_step(p_dil, kv_bufs, l).reshape(m, WIDTH)]
        kv_new = [kv_rows(g, t) for g in range(DIL_GROUPS)]
    branches = [x.reshape(m, WIDTH) for x in (o_a, o_b, o_c)] + dil_parts
    x1 = _merge(hf, branches, w_in_b, w_branch_b, w_out_b, l, ln_g, ln_b, tm)
    return x1, (wkv_new, p_rw[:, -1], pool_new, ret_new, kv_new[0], kv_new[1], kv_new[2])


def kernel(x_prompt, x_sample, state_wkv, state_shift, state_pool, state_ret, cache_kv_w128, cache_kv_w512, cache_kv_w2048, w_in, rw_mu, rw_w0, rw_w2, rw_a0, rw_a2, rw_g2, rw_kk, rw_ka, rw_rk, rw_gn_g, rw_gn_b, pool_w, pool_scale, ret_gn_g, ret_gn_b, w_branch, w_out, ln_g, ln_b, ffn_w_gate, ffn_w_up, ffn_w_down, moe_router, moe_w_gate, moe_w_up, moe_w_down):
    hp, hs = x_prompt, x_sample
    bp, tp, _ = hp.shape
    bs, ts, _ = hs.shape
    names = ('rw_mu', 'rw_w0', 'rw_w2', 'rw_a0', 'rw_a2', 'rw_g2', 'rw_kk', 'rw_ka', 'rw_rk', 'rw_gn_g',
             'rw_gn_b', 'pool_w', 'pool_scale', 'ret_gn_g', 'ret_gn_b')
    stacked = (rw_mu, rw_w0, rw_w2, rw_a0, rw_a2, rw_g2, rw_kk, rw_ka, rw_rk, rw_gn_g, rw_gn_b, pool_w,
               pool_scale, ret_gn_g, ret_gn_b)
    new_p = [[] for _ in range(7)]
    new_s = [[] for _ in range(7)]
    zeros = lambda *shape: jnp.zeros(shape, F32)
    w_in_b, wb, wo = w_in.astype(BF16), w_branch.astype(BF16), w_out.astype(BF16)
    caches = [jnp.transpose(c, (0, 1, 3, 4, 5, 2)) for c in (cache_kv_w128, cache_kv_w512, cache_kv_w2048)]
    for l in range(DEPTH):
        lw = {k: v[l] for k, v in zip(names, stacked)}
        xp, st_p = _token_mix(hp, 0, zeros(bp, RW_COLS), zeros(bp, N_HEADS, HEAD_DIM, HEAD_DIM),
                              zeros(bp, POOL_BUF, WIDTH), zeros(bp, N_HEADS, HEAD_DIM, HEAD_DIM), None,
                              lw, l, w_in_b, wb, wo, ln_g[l, 0], ln_b[l, 0])
        xs, st_s = _token_mix(hs, 8192, state_shift[l], state_wkv[l], state_pool[l], state_ret[l], caches,
                              lw, l, w_in_b, wb, wo, ln_g[l, 0], ln_b[l, 0])
        j = l // 2
        if l % 2 == 0:
            ws = [w[j].astype(BF16) for w in (ffn_w_gate, ffn_w_up, ffn_w_down)]
            xp = _ffn(xp, *ws, ln_g[l, 1], ln_b[l, 1], 256)
            xs = _ffn(xs, *ws, ln_g[l, 1], ln_b[l, 1], xs.shape[0])
        else:
            ws = [w[j].astype(BF16) for w in (moe_w_gate, moe_w_up, moe_w_down)]
            xp = _moe(xp, moe_router[j], *ws, ln_g[l, 1], ln_b[l, 1], 1024)
            xs = _moe(xs, moe_router[j], *ws, ln_g[l, 1], ln_b[l, 1], xs.shape[0])
        hp, hs = xp.reshape(bp, tp, D_MODEL), xs.reshape(bs, ts, D_MODEL)
        for i in range(7):
            new_p[i].append(st_p[i])
            new_s[i].append(st_s[i])
    outs_p = [jnp.stack(x) for x in new_p]
    outs_s = [jnp.stack(x) for x in new_s]
    return (hp, hs, *outs_p, *outs_s)
```

```python
import functools
import math

import jax
import jax.numpy as jnp
from jax import lax
from jax.experimental import pallas as pl
from jax.experimental.pallas import tpu as pltpu

F32 = jnp.float32
BF16 = jnp.bfloat16

D_MODEL = 1024
DEPTH = 2
HEAD_DIM = 64
N_HEADS = 4
WIDTH = N_HEADS * HEAD_DIM
RW_COLS = 1024
RW_GN_EPS = 64e-5
POOL_WINDOWS = (2, 4, 8, 16)
POOL_BUF = 15
RET_CHUNK = 128
ROPE_BASE = 10000.0
DIL_PATTERNS = ((128, 1), (512, 4), (2048, 16))
DIL_GROUPS = 3
DIL_BLOCK = 128
DIL_SPAN = 2048
DIL_COLS = 3 * DIL_GROUPS * WIDTH
N_BRANCH = 4
COL_POOL = RW_COLS
COL_RET = COL_POOL + WIDTH
COL_DIL = COL_RET + 4 * WIDTH
COL_GATE = COL_DIL + DIL_COLS
N_EXPERTS = 8
DN_ALPHA = (2 * DEPTH) ** 0.25
LN_EPS = 1e-5
RW_CHUNK = 64
MOE_TILE = 288
RW_P = 1
RW_BATCH = 4
NEG_BIG = -1e30

NN = (((1,), (0,)), ((), ()))
NT = (((1,), (1,)), ((), ()))
TN = (((0,), (0,)), ((), ()))

VMEM_LIMIT = 56 * 1024 * 1024


def _split(x, n):
    if x.dtype == BF16:
        return [x]
    parts, rem = [], x
    for i in range(n):
        p = rem.astype(BF16)
        parts.append(p)
        if i + 1 < n:
            rem = rem - p.astype(F32)
    return parts


def _mm(a, b, dims=NN, pa=1, pb=1):
    a_parts, b_parts = _split(a, pa), _split(b, pb)
    depth = max(len(a_parts), len(b_parts))
    acc = None
    for i, ai in enumerate(a_parts):
        for j, bj in enumerate(b_parts):
            if i + j < depth:
                t = lax.dot_general(ai, bj, dims, preferred_element_type=F32)
                acc = t if acc is None else acc + t
    return acc


def _sigmoid(x):
    return 0.5 * jnp.tanh(0.5 * x) + 0.5


def _silu(x):
    return x * _sigmoid(x)


def _iota(shape, axis):
    return lax.broadcasted_iota(jnp.int32, shape, axis)


def _div(x, d):
    assert d & (d - 1) == 0
    return x >> (d.bit_length() - 1)


def _mod(x, d):
    assert d & (d - 1) == 0
    return x & (d - 1)


def _per_head(head, values):
    out = jnp.full(head.shape, values[-1], F32)
    for h in range(len(values) - 2, -1, -1):
        out = jnp.where(head == h, values[h], out)
    return out


def _head_mask(rows_per_head, n_rows):
    return _div(_iota((n_rows, WIDTH), 0), rows_per_head) == _div(_iota((n_rows, WIDTH), 1), HEAD_DIM)


def _stack_heads(x, mask):
    return jnp.where(mask, jnp.concatenate([x] * N_HEADS, axis=0), 0.0)


def _unstack_heads(x_st, mask, c):
    x_st = jnp.where(mask, x_st, 0.0)
    out = x_st[0:c]
    for h in range(1, N_HEADS):
        out = out + x_st[h * c:(h + 1) * c]
    return out


def _ones_bd():
    return _same_head().astype(BF16)


def _same_head():
    return _div(_iota((WIDTH, WIDTH), 0), HEAD_DIM) == _div(_iota((WIDTH, WIDTH), 1), HEAD_DIM)


def _head_norm(x, ones_bd, g, b, eps):
    mu = _mm(x, ones_bd, pa=2) * (1.0 / HEAD_DIM)
    d = x - mu
    var = _mm(d * d, ones_bd, pa=2) * (1.0 / HEAD_DIM)
    return d * lax.rsqrt(var + eps) * g + b


def _layer_norm(x, g, b):
    mu = jnp.mean(x, axis=-1, keepdims=True)
    d = x - mu
    var = jnp.mean(d * d, axis=-1, keepdims=True)
    return d * lax.rsqrt(var + LN_EPS) * g + b


def _resident(shape):
    nd = len(shape)
    return pl.BlockSpec(shape, lambda *_: (0,) * nd, pipeline_mode=pl.Buffered(1))


def _proj_kernel(x_ref, w_ref, rw_ref, pool_ref, ret_ref, dil_ref):
    xb = x_ref[...].astype(BF16)
    for ref, lo, hi in ((rw_ref, 0, COL_POOL), (pool_ref, COL_POOL, COL_RET),
                        (ret_ref, COL_RET, COL_DIL), (dil_ref, COL_DIL, COL_GATE)):
        for s in range(lo, hi, 512):
            e = min(s + 512, hi)
            ref[:, s - lo:e - lo] = _mm(xb, w_ref[:, s:e])


def _layer_resident(w, l):
    nd = w.ndim - 1
    return pl.BlockSpec((None,) + w.shape[1:], lambda *_: (l,) + (0,) * nd, pipeline_mode=pl.Buffered(1))


def _project(x, w_in_b, l, tm):
    m = x.shape[0]
    widths = (COL_POOL, WIDTH, 4 * WIDTH, DIL_COLS)
    row = lambda w: pl.BlockSpec((tm, w), lambda i: (i, 0))
    return pl.pallas_call(
        _proj_kernel,
        grid=(m // tm,),
        in_specs=[row(D_MODEL), _layer_resident(w_in_b, l)],
        out_specs=[row(w) for w in widths],
        out_shape=[jax.ShapeDtypeStruct((m, w), F32) for w in widths],
        compiler_params=pltpu.CompilerParams(dimension_semantics=("parallel",), vmem_limit_bytes=VMEM_LIMIT),
        name="proj",
    )(x, w_in_b)


BNT = (((2,), (2,)), ((0,), (0,)))
BNN = (((2,), (1,)), ((0,), (0,)))
BTN = (((1,), (1,)), ((0,), (0,)))


def _rwkv_kernel(p_ref, prev_ref, s0_ref, mu_ref, w0_ref, w2_ref, a0_ref, a2_ref, g2_ref, kk_ref, ka_ref,
                 rk_ref, gng_ref, gnb_ref, o_ref, st_ref, s_scr, prev_scr, buf_scr, *, c, t_in, nb):
    ci = pl.program_id(1)

    @pl.when(ci == 0)
    def _():
        s_scr[...] = _embed_heads(s0_ref[...])
        for b in range(nb):
            prev_scr[b, 0:1, :] = prev_ref[b]

    first = _iota((c, RW_COLS), 0) == 0
    ps, shs = [], []
    for b in range(nb):
        if t_in == c:
            pb_ = p_ref[b]
        else:
            buf_scr[b] = jnp.zeros((c, RW_COLS), F32)
            buf_scr[b, 0:t_in, :] = p_ref[b]
            pb_ = buf_scr[b]
        shs.append(jnp.where(first, prev_scr[b, 0:1, :], pltpu.roll(pb_, 1, 0)))
        prev_scr[b, 0:1, :] = pb_[c - 1:c, :]
        ps.append(pb_)
    p, shifted = jnp.concatenate(ps, axis=0), jnp.concatenate(shs, axis=0)
    m = nb * c
    u = p + (shifted - p) * mu_ref[...]
    r, k, v, ul = u[:, 0:WIDTH], u[:, WIDTH:2 * WIDTH], u[:, 2 * WIDTH:3 * WIDTH], u[:, 3 * WIDTH:]

    w_pre = w0_ref[...] + _mm(jnp.tanh(ul), w2_ref[...])
    logw = -math.exp(-0.5) * _sigmoid(w_pre)
    a = _sigmoid(a0_ref[...] + _mm(ul, a2_ref[...]))
    g = _mm(_sigmoid(ul), g2_ref[...])

    ones_bd = _ones_bd()
    kk = k * kk_ref[...]
    kk = kk * lax.rsqrt(jnp.maximum(_mm(kk * kk, ones_bd, pa=2), 1e-24))
    k_mod = k * (1.0 + (a - 1.0) * ka_ref[...])
    a_vec, b_vec = -kk, kk * a
    if t_in < c:
        live = _mod(_iota((m, WIDTH), 0), c) < t_in
        logw = jnp.where(live, logw, 0.0)
        a_vec, b_vec = jnp.where(live, a_vec, 0.0), jnp.where(live, b_vec, 0.0)
        k_mod, v = jnp.where(live, k_mod, 0.0), jnp.where(live, v, 0.0)

    qi, qj = _iota((m, m), 0), _iota((m, m), 1)
    tri = ((_div(qi, c) == _div(qj, c)) & (qi >= qj)).astype(BF16)
    cum2 = _mm(tri, logw, pb=3)
    seq = lambda x: x.reshape(nb, c, WIDTH)
    cum, lw3 = seq(cum2), seq(logw)
    cum_end = cum[:, c - 1:c, :]
    e_neg, e_end = jnp.exp(-cum), jnp.exp(cum_end - cum)
    a_t = seq(a_vec) * jnp.exp(cum - lw3)
    r_t = seq(r) * jnp.exp(cum)
    b_t, k_t = seq(b_vec) * e_neg, seq(k_mod) * e_neg
    b_e, k_e = seq(b_vec) * e_end, seq(k_mod) * e_end
    v3 = seq(v)

    n = N_HEADS * c
    hm = _head_mask(c, n)
    stack = lambda x: jnp.where(hm, jnp.concatenate([x] * N_HEADS, axis=1), 0.0)
    a_st, r_st, b_st, v_st = stack(a_t), stack(r_t), stack(b_t), stack(v3)
    ri, rj = _iota((n, n), 0), _iota((n, n), 1)
    same = _div(ri, c) == _div(rj, c)
    strict_bd = same & (_mod(ri, c) > _mod(rj, c))
    incl_bd = same & (_mod(ri, c) >= _mod(rj, c))
    ti, tj = _mod(_iota((n, c), 0), c), _iota((n, c), 1)
    strict_st, incl_st = ti > tj, ti >= tj

    mm = functools.partial(_mm, pa=RW_P, pb=RW_P)
    a_ab = jnp.where(strict_bd, mm(a_st, b_st, BNT), 0.0)
    inv = jnp.where(ri == rj, 1.0, 0.0) + a_ab
    pw = a_ab
    for _ in range(int(math.log2(c)) - 1):
        pw = mm(pw, pw, BNN)
        inv = inv + mm(inv, pw, BNN)
    a_ak = jnp.where(strict_st, mm(a_st, k_t, BNT), 0.0)
    z_st = jnp.where(hm, mm(a_ak, v3, BNN), 0.0)
    wu = mm(inv, jnp.concatenate([a_st, z_st], axis=2), BNN)
    w_st, u0_st = wu[:, :, 0:WIDTH], wu[:, :, WIDTH:]

    s0 = s_scr[...]
    u_st = mm(w_st, s0, BNT) + u0_st
    lhs = jnp.concatenate([u_st, v_st], axis=1)
    rhs = jnp.concatenate([stack(b_e), stack(k_e)], axis=1)
    s_scr[...] = s0 * jnp.exp(cum_end) + mm(lhs, rhs, BTN)

    a_rb = jnp.where(incl_bd, mm(r_st, b_st, BNT), 0.0)
    a_rk = jnp.where(incl_st, mm(r_st, k_t, BNT), 0.0)
    o_st = jnp.where(hm, mm(r_st, s0, BNT) + mm(a_rb, u_st, BNN) + mm(a_rk, v3, BNN), 0.0)
    o3 = o_st[:, 0:c]
    for h in range(1, N_HEADS):
        o3 = o3 + o_st[:, h * c:(h + 1) * c]
    o = o3.reshape(m, WIDTH)

    o = _head_norm(o, ones_bd, gng_ref[...], gnb_ref[...], RW_GN_EPS)
    o = o + _mm(r * k_mod * rk_ref[...], ones_bd, pa=2) * v
    o_ref[...] = (o * g).reshape(nb, c, WIDTH)[:, 0:t_in]

    @pl.when(ci == pl.num_programs(1) - 1)
    def _():
        st_ref[...] = _extract_heads(s_scr[...])


def _embed_heads(s):
    return jnp.where(_same_head(), jnp.concatenate([s] * N_HEADS, axis=-1), 0.0)


def _extract_heads(s_bd):
    out = s_bd[..., 0:HEAD_DIM]
    for h in range(1, N_HEADS):
        out = out + s_bd[..., h * HEAD_DIM:(h + 1) * HEAD_DIM]
    return out


def _rwkv_branch(p, p_prev, wkv0, lw):
    b, t, _ = p.shape
    c = RW_CHUNK if t % RW_CHUNK == 0 else RW_CHUNK // 2
    assert t % c == 0 or t <= c
    t_in = c if t % c == 0 else t
    nc = t // c if t % c == 0 else 1
    vec = lambda x: x.reshape(1, -1)
    pad_rows = lambda w, lo: jnp.zeros((WIDTH, WIDTH), F32).at[lo:lo + w.shape[0]].set(w).astype(BF16)
    params = [vec(lw['rw_mu']), vec(lw['rw_w0']), pad_rows(lw['rw_w2'], 0), vec(lw['rw_a0']),
              pad_rows(lw['rw_a2'], 64), pad_rows(lw['rw_g2'], 128), vec(lw['rw_kk']), vec(lw['rw_ka']),
              vec(lw['rw_rk']), vec(lw['rw_gn_g']), vec(lw['rw_gn_b'])]
    nb = RW_BATCH
    assert b % nb == 0
    o, st = pl.pallas_call(
        functools.partial(_rwkv_kernel, c=c, t_in=t_in, nb=nb),
        grid=(b // nb, nc),
        in_specs=[pl.BlockSpec((nb, t_in, RW_COLS), lambda i, j: (i, j, 0)),
                  pl.BlockSpec((nb, 1, RW_COLS), lambda i, j: (i, 0, 0)),
                  pl.BlockSpec((nb, WIDTH, HEAD_DIM), lambda i, j: (i, 0, 0))]
                 + [_resident(x.shape) for x in params],
        out_specs=[pl.BlockSpec((nb, t_in, WIDTH), lambda i, j: (i, j, 0)),
                   pl.BlockSpec((nb, WIDTH, HEAD_DIM), lambda i, j: (i, 0, 0))],
        out_shape=[jax.ShapeDtypeStruct((b, t, WIDTH), F32), jax.ShapeDtypeStruct((b, WIDTH, HEAD_DIM), F32)],
        scratch_shapes=[pltpu.VMEM((nb, WIDTH, WIDTH), F32), pltpu.VMEM((nb, 8, RW_COLS), F32),
                        pltpu.VMEM((nb, c, RW_COLS), F32)],
        compiler_params=pltpu.CompilerParams(dimension_semantics=("parallel", "arbitrary"),
                                             vmem_limit_bytes=VMEM_LIMIT),
        name="rwkv7",
    )(p, p_prev.reshape(b, 1, RW_COLS), wkv0.reshape(b, WIDTH, HEAD_DIM), *params)
    return o, st.reshape(b, N_HEADS, HEAD_DIM, HEAD_DIM)


def _pool_rows(ext_scr, c, pos_first, w, scale):
    x = ext_scr[16:16 + c, :]
    sums, acc, off = [], x, 1
    for win in POOL_WINDOWS:
        while off < win:
            acc = acc + ext_scr[16 - off:16 - off + c, :]
            off += 1
        sums.append(acc)
    pos = pos_first + _iota((c, WIDTH), 0)
    grp = _div(_iota((c, WIDTH), 1), HEAD_DIM)
    mean = jnp.zeros((c, WIDTH), F32)
    for gi, win in enumerate(POOL_WINDOWS):
        cnt = jnp.minimum(win, pos + 1).astype(F32)
        mean = jnp.where(grp == gi, sums[gi] / cnt, mean)
    ext_scr[0:16, :] = ext_scr[c:c + 16, :]
    return _mm(mean - x, w) * scale


def _pool_kernel(u_ref, buf_ref, w_ref, scale_ref, o_ref, ext_scr, *, c, t_in, pos0):
    ci = pl.program_id(1)

    @pl.when(ci == 0)
    def _():
        ext_scr[0:16, :] = buf_ref[0]

    if t_in < c:
        ext_scr[16:16 + c, :] = jnp.zeros((c, WIDTH), F32)
    ext_scr[16:16 + t_in, :] = u_ref[0]
    o_ref[0] = _pool_rows(ext_scr, c, pos0 + ci * c, w_ref[...], scale_ref[...])[0:t_in]


def _pool_weights(lw):
    w_bd = jnp.einsum('gcd,gh->gchd', lw['pool_w'], jnp.eye(N_HEADS, dtype=F32)).reshape(WIDTH, WIDTH)
    return w_bd.astype(BF16), lw['pool_scale'].reshape(1, WIDTH)


def _pool_branch(u, buf, pos0, lw):
    b, t, _ = u.shape
    c = 512 if t % 512 == 0 else 8
    t_in = c if t % c == 0 else t
    nc = t // c if t % c == 0 else 1
    w_bd, scale = _pool_weights(lw)
    buf16 = jnp.pad(buf, ((0, 0), (1, 0), (0, 0)))
    return pl.pallas_call(
        functools.partial(_pool_kernel, c=c, t_in=t_in, pos0=pos0),
        grid=(b, nc),
        in_specs=[pl.BlockSpec((1, t_in, WIDTH), lambda i, j: (i, j, 0)),
                  pl.BlockSpec((1, 16, WIDTH), lambda i, j: (i, 0, 0)),
                  _resident((WIDTH, WIDTH)), _resident((1, WIDTH))],
        out_specs=pl.BlockSpec((1, t_in, WIDTH), lambda i, j: (i, j, 0)),
        out_shape=jax.ShapeDtypeStruct((b, t, WIDTH), F32),
        scratch_shapes=[pltpu.VMEM((16 + c, WIDTH), F32)],
        compiler_params=pltpu.CompilerParams(dimension_semantics=("parallel", "arbitrary")),
        name="pool",
    )(u, buf16, w_bd, scale)


def _rot_half(x):
    first = _mod(_iota(x.shape, 1), HEAD_DIM) < (HEAD_DIM // 2)
    return jnp.where(first, pltpu.roll(x, WIDTH - HEAD_DIM // 2, 1), pltpu.roll(x, HEAD_DIM // 2, 1))


RET_LOG_DECAY = tuple(math.log(1.0 - 2.0 ** (-5.0 - h)) for h in range(N_HEADS))
ALIBI_SLOPES = tuple(2.0 ** (-8.0 * (i + 1) / (DIL_GROUPS * N_HEADS)) for i in range(DIL_GROUPS * N_HEADS))


def _ret_kernel(p_ref, cos_ref, sin_ref, s0_ref, gng_ref, gnb_ref, o_ref, st_ref, s_scr, buf_scr,
                dmask_scr, qd_scr, kd_scr, *, c, t_in, nb):
    ci = pl.program_id(1)
    n = N_HEADS * c
    lg = _per_head(_div(_iota((1, WIDTH), 1), HEAD_DIM), RET_LOG_DECAY)

    @pl.when(ci == 0)
    def _():
        s_scr[...] = _embed_heads(s0_ref[...])
        idx = _iota((c, WIDTH), 0).astype(F32)
        qd_scr[...] = jnp.exp(lg * (idx + 1.0))
        kd_scr[...] = jnp.exp(lg * (t_in - 1.0 - idx))
        rel = _mod(_iota((n, c), 0), c) - _iota((n, c), 1)
        lg_rows = _per_head(_div(_iota((n, c), 0), c), RET_LOG_DECAY)
        dmask_scr[...] = jnp.where(rel >= 0, jnp.exp(lg_rows * jnp.maximum(rel, 0).astype(F32)), 0.0)

    hm = _head_mask(c, n)
    cos, sin = cos_ref[...], sin_ref[...]
    chunk_decay = jnp.exp(lg * float(t_in))
    same_head, ones_bd = _same_head(), _ones_bd()
    for b in range(nb):
        if t_in == c:
            p = p_ref[b]
        else:
            buf_scr[b] = jnp.zeros((c, 4 * WIDTH), F32)
            buf_scr[b, 0:t_in, :] = p_ref[b]
            p = buf_scr[b]
        q, k, v, g = (p[:, i * WIDTH:(i + 1) * WIDTH] for i in range(4))
        q = q * cos + _rot_half(q) * sin
        k = (k * cos + _rot_half(k) * sin) * HEAD_DIM ** -0.5
        if t_in < c:
            live = _iota((c, WIDTH), 0) < t_in
            k, v = jnp.where(live, k, 0.0), jnp.where(live, v, 0.0)
        inner = _mm(_stack_heads(q, hm), k, NT) * dmask_scr[...]
        s0 = s_scr[b]
        o = _unstack_heads(_mm(inner, v), hm, c) + _mm(q * qd_scr[...], s0)
        s_scr[b] = s0 * chunk_decay + jnp.where(same_head, _mm(k * kd_scr[...], v, TN), 0.0)
        o = _silu(g) * _head_norm(o, ones_bd, gng_ref[...], gnb_ref[...], LN_EPS)
        o_ref[b] = o[0:t_in]

    @pl.when(ci == pl.num_programs(1) - 1)
    def _():
        st_ref[...] = _extract_heads(s_scr[...])


def _rope_tables(pos0, t, rows):
    half = HEAD_DIM // 2
    inv = ROPE_BASE ** (-jnp.arange(half, dtype=F32) / half)
    ang = (pos0 + jnp.arange(t, dtype=jnp.int32)).astype(F32)[:, None] * inv[None, :]
    cos = jnp.tile(jnp.cos(ang), (1, 2 * N_HEADS))
    sin = jnp.tile(jnp.concatenate([-jnp.sin(ang), jnp.sin(ang)], axis=1), (1, N_HEADS))
    pad = ((0, rows - t), (0, 0))
    return jnp.pad(cos, pad), jnp.pad(sin, pad)


def _ret_branch(p, s0, pos0, lw):
    b, t, _ = p.shape
    c = RET_CHUNK
    t_in = c if t % c == 0 else t
    nc = t // c if t % c == 0 else 1
    cos, sin = _rope_tables(pos0, t, nc * c)
    nb = RW_BATCH
    assert b % nb == 0
    o, st = pl.pallas_call(
        functools.partial(_ret_kernel, c=c, t_in=t_in, nb=nb),
        grid=(b // nb, nc),
        in_specs=[pl.BlockSpec((nb, t_in, 4 * WIDTH), lambda i, j: (i, j, 0)),
                  pl.BlockSpec((c, WIDTH), lambda i, j: (j, 0)),
                  pl.BlockSpec((c, WIDTH), lambda i, j: (j, 0)),
                  pl.BlockSpec((nb, WIDTH, HEAD_DIM), lambda i, j: (i, 0, 0)),
                  _resident((1, WIDTH)), _resident((1, WIDTH))],
        out_specs=[pl.BlockSpec((nb, t_in, WIDTH), lambda i, j: (i, j, 0)),
                   pl.BlockSpec((nb, WIDTH, HEAD_DIM), lambda i, j: (i, 0, 0))],
        out_shape=[jax.ShapeDtypeStruct((b, t, WIDTH), F32), jax.ShapeDtypeStruct((b, WIDTH, HEAD_DIM), F32)],
        scratch_shapes=[pltpu.VMEM((nb, WIDTH, WIDTH), F32), pltpu.VMEM((nb, c, 4 * WIDTH), F32),
                        pltpu.VMEM((N_HEADS * c, c), F32), pltpu.VMEM((c, WIDTH), F32), pltpu.VMEM((c, WIDTH), F32)],
        compiler_params=pltpu.CompilerParams(dimension_semantics=("parallel", "arbitrary")),
        name="retention",
    )(p, cos, sin, s0.reshape(b, WIDTH, HEAD_DIM), lw['ret_gn_g'].reshape(1, WIDTH), lw['ret_gn_b'].reshape(1, WIDTH))
    return o, st.reshape(b, N_HEADS, HEAD_DIM, HEAD_DIM)


def _alibi_slope_rows(group, rows_per_head):
    head = _div(_iota((N_HEADS * rows_per_head, 1), 0), rows_per_head)
    return _per_head(head, ALIBI_SLOPES[group * N_HEADS:(group + 1) * N_HEADS])


def _dil_prompt_kernel(*refs, group, dil, span):
    q_refs, kc_refs, kp_refs, vc_refs, vp_refs = (refs[2 * i:2 * i + 2] for i in range(5))
    o_ref, lse_ref, k_scr, v_scr, o_scr, lse_scr = refs[10:]
    blk = DIL_BLOCK
    tail = blk * dil
    si = pl.program_id(1)
    for half in range(2):
        k_scr[half, 0:tail, :] = kp_refs[half][0]
        k_scr[half, tail:tail + span, :] = kc_refs[half][0]
        v_scr[half, 0:tail, :] = vp_refs[half][0]
        v_scr[half, tail:tail + span, :] = vc_refs[half][0]
    ki = _iota((blk, 2 * blk), 1)
    steps = blk + _iota((blk, 2 * blk), 0) - ki
    band = (steps >= 0) & (steps <= blk)
    masks = [jnp.where(band, (-ALIBI_SLOPES[group * N_HEADS + h] * dil) * steps.astype(F32), NEG_BIG)
             for h in range(N_HEADS)]
    sub_head = _div(_iota((blk, 128), 1), HEAD_DIM)
    for cc in range(span // tail):
        has_prev = (si > 0) | (ki >= blk)
        cc_masks = [jnp.where(has_prev, x, NEG_BIG) for x in masks] if cc == 0 else masks
        for r in range(dil):
            rows_q = pl.ds(cc * tail + r, blk, stride=dil) if dil > 1 else pl.ds(cc * tail, blk)
            rows_kv = pl.ds(cc * tail + r, 2 * blk, stride=dil) if dil > 1 else pl.ds(cc * tail, 2 * blk)
            for half in range(2):
                q2 = q_refs[half][0, rows_q, :] * HEAD_DIM ** -0.5
                k2 = k_scr[half, rows_kv, :].astype(BF16)
                v2 = v_scr[half, rows_kv, :].astype(BF16)
                o2 = lse2 = None
                for sub in range(2):
                    mine = sub_head == sub
                    s = _mm(jnp.where(mine, q2, 0.0), k2, NT) + cc_masks[2 * half + sub]
                    m = jnp.max(s, axis=1, keepdims=True)
                    e = jnp.exp(s - m)
                    l = jnp.sum(e, axis=1, keepdims=True)
                    o_h = _mm(e, v2) / l
                    lse_h = jnp.broadcast_to(m + jnp.log(l), (blk, 128))
                    o2 = o_h if o2 is None else jnp.where(mine, o_h, o2)
                    lse2 = lse_h if lse2 is None else jnp.where(mine, lse_h, lse2)
                o_scr[half, rows_q, :] = o2
                lse_scr[half, rows_q, :] = lse2
    o_ref[0] = jnp.concatenate([o_scr[0], o_scr[1]], axis=1)
    lse_ref[0] = jnp.concatenate([lse_scr[0], lse_scr[1]], axis=1)


def _dil_prompt_group(pd, group):
    b, s, _ = pd.shape
    win, dil = DIL_PATTERNS[group]
    span = DIL_SPAN
    tail = DIL_BLOCK * dil
    assert win // dil == DIL_BLOCK and span % tail == 0 and s % span == 0
    base = group * 3

    def cur(col):
        return [pl.BlockSpec((1, span, 128), lambda i, j, h=h: (i, j, 2 * (base + col) + h)) for h in range(2)]

    def prev(col):
        return [pl.BlockSpec((1, tail, 128),
                             lambda i, j, h=h: (i, jnp.maximum(j * (span // tail) - 1, 0), 2 * (base + col) + h))
                for h in range(2)]

    out_spec = pl.BlockSpec((1, span, WIDTH), lambda i, j: (i, j, 0))
    return pl.pallas_call(
        functools.partial(_dil_prompt_kernel, group=group, dil=dil, span=span),
        grid=(b, s // span),
        in_specs=cur(0) + cur(1) + prev(1) + cur(2) + prev(2),
        out_specs=[out_spec, out_spec],
        out_shape=[jax.ShapeDtypeStruct((b, s, WIDTH), F32)] * 2,
        scratch_shapes=[pltpu.VMEM((2, tail + span, 128), F32)] * 2 + [pltpu.VMEM((2, span, 128), F32)] * 2,
        compiler_params=pltpu.CompilerParams(dimension_semantics=("parallel", "arbitrary"),
                                             vmem_limit_bytes=VMEM_LIMIT),
        name=f"dil_prompt_g{group}",
    )(*([pd] * 10))


def _dil_combine(os_, ls):
    m = jnp.maximum(jnp.maximum(ls[0], ls[1]), ls[2])
    es = [jnp.exp(x - m) for x in ls]
    return (es[0] * os_[0] + es[1] * os_[1] + es[2] * os_[2]) / (es[0] + es[1] + es[2])


def _dil_prompt(pd):
    b, s, _ = pd.shape
    outs = [_dil_prompt_group(pd, g) for g in range(DIL_GROUPS)]
    return [x[0].reshape(b * s, WIDTH) for x in outs] + [x[1].reshape(b * s, WIDTH) for x in outs]


def _dil_step_kernel(pd_ref, c0_ref, c1_ref, c2_ref, o_ref, buf_scr, *, t, tp):
    buf_scr[...] = jnp.zeros_like(buf_scr)
    buf_scr[0:t, :] = pd_ref[0]
    pd = buf_scr[...]
    qt = _iota((tp, 1), 0)
    d_new = qt - _iota((tp, tp), 1)
    outs = [[None] * DIL_GROUPS for _ in range(N_HEADS)]
    lses = [[None] * DIL_GROUPS for _ in range(N_HEADS)]
    for g, cache_ref in enumerate((c0_ref, c1_ref, c2_ref)):
        win, dil = DIL_PATTERNS[g]
        length = cache_ref.shape[-1]
        d_old = length + qt - _iota((tp, length), 1)
        ok_old = (_mod(d_old, dil) == 0) & (d_old <= win)
        ok_new = (d_new >= 0) & (_mod(d_new, dil) == 0)
        for h in range(N_HEADS):
            slope = ALIBI_SLOPES[g * N_HEADS + h]
            lo = g * 3 * WIDTH + h * HEAD_DIM
            q = pd[:, lo:lo + HEAD_DIM] * HEAD_DIM ** -0.5
            k_new = pd[:, lo + WIDTH:lo + WIDTH + HEAD_DIM]
            v_new = pd[:, lo + 2 * WIDTH:lo + 2 * WIDTH + HEAD_DIM]
            k_t, v_t = cache_ref[0, 0, h], cache_ref[0, 1, h]
            s_old = jnp.where(ok_old, _mm(q, k_t) - slope * d_old.astype(F32), NEG_BIG)
            s_new = jnp.where(ok_new, _mm(q, k_new, NT) - slope * d_new.astype(F32), NEG_BIG)
            m = jnp.maximum(jnp.max(s_old, axis=1, keepdims=True), jnp.max(s_new, axis=1, keepdims=True))
            e_old, e_new = jnp.exp(s_old - m), jnp.exp(s_new - m)
            l = jnp.sum(e_old, axis=1, keepdims=True) + jnp.sum(e_new, axis=1, keepdims=True)
            outs[h][g] = (_mm(e_old, v_t, NT) + _mm(e_new, v_new)) / l
            lses[h][g] = m + jnp.log(l)
    o = jnp.concatenate([_dil_combine(outs[h], lses[h]) for h in range(N_HEADS)], axis=1)
    o_ref[0] = o[0:t]


def _dil_step(pd, caches, l):
    b, t, _ = pd.shape
    tp = 8
    flat = caches
    return pl.pallas_call(
        functools.partial(_dil_step_kernel, t=t, tp=tp),
        grid=(b,),
        in_specs=[pl.BlockSpec((1, t, DIL_COLS), lambda i: (i, 0, 0))]
                 + [pl.BlockSpec((None, 1) + c.shape[2:], lambda i: (l, i, 0, 0, 0, 0)) for c in flat],
        out_specs=pl.BlockSpec((1, t, WIDTH), lambda i: (i, 0, 0)),
        out_shape=jax.ShapeDtypeStruct((b, t, WIDTH), F32),
        scratch_shapes=[pltpu.VMEM((tp, DIL_COLS), F32)],
        compiler_params=pltpu.CompilerParams(dimension_semantics=("parallel",), vmem_limit_bytes=VMEM_LIMIT),
        name="dil_step",
    )(pd, *flat)


def _merge_kernel(h_ref, *refs):
    wg_ref, wb_ref, wo_ref, g_ref, b_ref, out_ref = refs[-6:]
    branch_refs = refs[:-6]
    branches = [r[...] for r in branch_refs[:3]]
    if len(branch_refs) == 4:
        branches.append(branch_refs[3][...])
    else:
        branches.append(_dil_combine([r[...] for r in branch_refs[3:6]], [r[...] for r in branch_refs[6:9]]))
    h = h_ref[...]
    hb = h.astype(BF16)
    z = None
    for n, o in enumerate(branches):
        gate = _sigmoid(_mm(hb, wg_ref[:, COL_GATE + n * D_MODEL:COL_GATE + (n + 1) * D_MODEL]))
        term = gate * _mm(o, wb_ref[n])
        z = term if z is None else z + term
    y = _mm(z, wo_ref[...])
    out_ref[...] = _layer_norm(DN_ALPHA * h + y, g_ref[...], b_ref[...])


def _merge(h, branches, w_in_b, w_branch_b, w_out_b, l, ln_g, ln_b, tm):
    m = h.shape[0]
    row = lambda w: pl.BlockSpec((tm, w), lambda i: (i, 0))
    return pl.pallas_call(
        _merge_kernel,
        grid=(m // tm,),
        in_specs=[row(D_MODEL)] + [row(WIDTH)] * len(branches)
                 + [_layer_resident(w_in_b, l), _layer_resident(w_branch_b, l), _layer_resident(w_out_b, l),
                    _resident((1, D_MODEL)), _resident((1, D_MODEL))],
        out_specs=row(D_MODEL),
        out_shape=jax.ShapeDtypeStruct((m, D_MODEL), F32),
        compiler_params=pltpu.CompilerParams(dimension_semantics=("parallel",), vmem_limit_bytes=VMEM_LIMIT),
        name="merge_ln",
    )(h, *branches, w_in_b, w_branch_b, w_out_b, ln_g.reshape(1, -1), ln_b.reshape(1, -1))


def _ffn_kernel(x_ref, wg_ref, wu_ref, wd_ref, g_ref, b_ref, out_ref):
    x = x_ref[...]
    xb = x.astype(BF16)
    act = _silu(_mm(xb, wg_ref[...])) * _mm(xb, wu_ref[...])
    out_ref[...] = _layer_norm(DN_ALPHA * x + _mm(act, wd_ref[...]), g_ref[...], b_ref[...])


def _ffn(x, wg, wu, wd, ln_g, ln_b, tm):
    m = x.shape[0]
    row = pl.BlockSpec((tm, D_MODEL), lambda i: (i, 0))
    return pl.pallas_call(
        _ffn_kernel,
        grid=(m // tm,),
        in_specs=[row, _resident(wg.shape), _resident(wu.shape), _resident(wd.shape),
                  _resident((1, D_MODEL)), _resident((1, D_MODEL))],
        out_specs=row,
        out_shape=jax.ShapeDtypeStruct((m, D_MODEL), F32),
        compiler_params=pltpu.CompilerParams(dimension_semantics=("parallel",), vmem_limit_bytes=VMEM_LIMIT),
        name="ffn_ln",
    )(x, wg, wu, wd, ln_g.reshape(1, -1), ln_b.reshape(1, -1))


def _route_kernel(x_ref, rt_ref, tri_ref, gate_ref, rank_ref, cnt_ref):
    logits = _mm(rt_ref[...], x_ref[...], NT, 2, 2)
    sub = _iota(logits.shape, 0).astype(F32)
    m1 = jnp.max(logits, axis=0, keepdims=True)
    i1 = jnp.min(jnp.where(logits == m1, sub, float(N_EXPERTS)), axis=0, keepdims=True)
    rest = jnp.where(sub == i1, NEG_BIG, logits)
    m2 = jnp.max(rest, axis=0, keepdims=True)
    i2 = jnp.min(jnp.where(rest == m2, sub, float(N_EXPERTS)), axis=0, keepdims=True)
    e2 = jnp.exp(m2 - m1)
    gate_ref[0] = jnp.where(sub == i1, 1.0 / (1.0 + e2), 0.0) + jnp.where(sub == i2, e2 / (1.0 + e2), 0.0)
    chosen = (sub == i1) | (sub == i2)
    sel = jnp.where(chosen, 1.0, 0.0)
    rank_ref[0] = jnp.where(chosen, _mm(sel, tri_ref[...]), -1.0)
    cnt_ref[0] = jnp.broadcast_to(jnp.sum(sel, axis=1, keepdims=True), cnt_ref.shape[1:])


def _moe_kernel(cnt_ref, x_ref, gate_ref, rank_ref, wg_ref, wu_ref, wd_ref, g_ref, b_ref, out_ref,
                xb_scr, xg_scr, yg_scr, *, ts, tb, nsb):
    i, e, f = pl.program_id(0), pl.program_id(1), pl.program_id(2)
    last_f = f == pl.num_programs(2) - 1

    @pl.when((e == 0) & (f == 0))
    def _():
        xb_scr[...] = x_ref[...].astype(BF16)
        out_ref[...] = jnp.zeros_like(out_ref)

    slot = _iota((ts, tb), 0)

    def expert(xg):
        return _mm(_silu(_mm(xg, wg_ref[0])) * _mm(xg, wu_ref[0]), wd_ref[0])

    def pick_of(sb, j):
        rank_row = rank_ref[sb, pl.ds(e, 1), :]
        return jnp.where(rank_row == (slot + j * ts).astype(F32), 1.0, 0.0)

    def scatter(sb, pick, y):
        w_slot = jnp.sum(pick * gate_ref[sb, pl.ds(e, 1), :], axis=1, keepdims=True)
        out_ref[sb * tb:(sb + 1) * tb, :] += _mm(pick, y * w_slot, TN)

    @pl.when(f == 0)
    def _():
        for sb in range(nsb):
            xg_scr[sb] = _mm(pick_of(sb, 0), xb_scr[sb * tb:(sb + 1) * tb, :]).astype(BF16)
        yg_scr[...] = jnp.zeros_like(yg_scr)

    yg_scr[...] += expert(xg_scr[...].reshape(nsb * ts, D_MODEL)).reshape(nsb, ts, D_MODEL)

    @pl.when(last_f)
    def _():
        for sb in range(nsb):
            scatter(sb, pick_of(sb, 0), yg_scr[sb])

    for sb in range(nsb):
        n_tiles = (cnt_ref[(i * nsb + sb) * N_EXPERTS + e] + (ts - 1)) // ts

        def overflow(j, carry, sb=sb):
            pick = pick_of(sb, j)
            scatter(sb, pick, expert(_mm(pick, xb_scr[sb * tb:(sb + 1) * tb, :]).astype(BF16)))
            return carry

        lax.fori_loop(1, n_tiles, overflow, 0)

    @pl.when((e == pl.num_programs(1) - 1) & last_f)
    def _():
        out_ref[...] = _layer_norm(DN_ALPHA * x_ref[...] + out_ref[...], g_ref[...], b_ref[...])


def _moe(x, router, wg, wu, wd, ln_g, ln_b, tm):
    m = x.shape[0]
    nblk = m // tm
    dff = wg.shape[2]
    nf = 4
    tf = dff // nf
    ts = min(MOE_TILE, tm)
    nsb = 2 if nblk % 2 == 0 else 1
    assert m % tm == 0 and tf % 128 == 0
    tri = (jnp.arange(tm)[:, None] < jnp.arange(tm)[None, :]).astype(BF16)
    gate, rank, cnt = pl.pallas_call(
        _route_kernel,
        grid=(nblk,),
        in_specs=[pl.BlockSpec((tm, D_MODEL), lambda i: (i, 0)), _resident((N_EXPERTS, D_MODEL)),
                  _resident((tm, tm))],
        out_specs=[pl.BlockSpec((1, N_EXPERTS, tm), lambda i: (i, 0, 0)),
                   pl.BlockSpec((1, N_EXPERTS, tm), lambda i: (i, 0, 0)),
                   pl.BlockSpec((1, N_EXPERTS, 128), lambda i: (i, 0, 0))],
        out_shape=[jax.ShapeDtypeStruct((nblk, N_EXPERTS, tm), F32), jax.ShapeDtypeStruct((nblk, N_EXPERTS, tm), F32),
                   jax.ShapeDtypeStruct((nblk, N_EXPERTS, 128), F32)],
        compiler_params=pltpu.CompilerParams(dimension_semantics=("parallel",), vmem_limit_bytes=VMEM_LIMIT),
        name="moe_route",
    )(x, router.T, tri)
    counts = cnt[:, :, 0].astype(jnp.int32).reshape(-1)
    rows = nsb * tm
    x_in = pl.BlockSpec((rows, D_MODEL), lambda i, e, f, c: (i, 0), pipeline_mode=pl.Buffered(1))
    meta = pl.BlockSpec((nsb, N_EXPERTS, tm), lambda i, e, f, c: (i, 0, 0))
    vec = pl.BlockSpec((1, D_MODEL), lambda i, e, f, c: (0, 0), pipeline_mode=pl.Buffered(1))
    return pl.pallas_call(
        functools.partial(_moe_kernel, ts=ts, tb=tm, nsb=nsb),
        grid_spec=pltpu.PrefetchScalarGridSpec(
            num_scalar_prefetch=1,
            grid=(nblk // nsb, N_EXPERTS, nf),
            in_specs=[x_in, meta, meta,
                      pl.BlockSpec((1, D_MODEL, tf), lambda i, e, f, c: (e, 0, f)),
                      pl.BlockSpec((1, D_MODEL, tf), lambda i, e, f, c: (e, 0, f)),
                      pl.BlockSpec((1, tf, D_MODEL), lambda i, e, f, c: (e, f, 0)),
                      vec, vec],
            out_specs=pl.BlockSpec((rows, D_MODEL), lambda i, e, f, c: (i, 0)),
            scratch_shapes=[pltpu.VMEM((rows, D_MODEL), BF16),
                            pltpu.VMEM((nsb, ts, D_MODEL), BF16), pltpu.VMEM((nsb, ts, D_MODEL), F32)]),
        out_shape=jax.ShapeDtypeStruct((m, D_MODEL), F32),
        compiler_params=pltpu.CompilerParams(dimension_semantics=("parallel", "arbitrary", "arbitrary"),
                                             vmem_limit_bytes=VMEM_LIMIT),
        name="moe_ln",
    )(counts, x, gate, rank, wg, wu, wd, ln_g.reshape(1, -1), ln_b.reshape(1, -1))


def _token_mix(h, pos0, rw_prev, wkv0, pool_buf, ret0, kv_bufs, lw, l, w_in_b, w_branch_b, w_out_b, ln_g, ln_b):
    b, t, _ = h.shape
    m = b * t
    tm = 512 if m % 512 == 0 else m
    hf = h.reshape(m, D_MODEL)
    p_rw, p_pool, p_ret, p_dil = _project(hf, w_in_b, l, tm)
    p_rw, p_pool = p_rw.reshape(b, t, -1), p_pool.reshape(b, t, -1)
    p_ret, p_dil = p_ret.reshape(b, t, -1), p_dil.reshape(b, t, -1)
    o_a, wkv_new = _rwkv_branch(p_rw, rw_prev, wkv0, lw)
    o_b = _pool_branch(p_pool, pool_buf, pos0, lw)
    pool_new = jnp.concatenate([pool_buf, p_pool], axis=1)[:, -POOL_BUF:]
    o_c, ret_new = _ret_branch(p_ret, ret0, pos0, lw)
    def kv_rows(g, keep):
        lo = (3 * g + 1) * WIDTH
        return p_dil[:, t - keep:, lo:lo + 2 * WIDTH].reshape(b, keep, 2, N_HEADS, HEAD_DIM)

    if kv_bufs is None:
        dil_parts = _dil_prompt(p_dil)
        kv_new = [kv_rows(g, min(win, t)) for g, (win, _) in enumerate(DIL_PATTERNS)]
    else:
        dil_parts = [_dil_step(p_dil, kv_bufs, l).reshape(m, WIDTH)]
        kv_new = [kv_rows(g, t) for g in range(DIL_GROUPS)]
    branches = [x.reshape(m, WIDTH) for x in (o_a, o_b, o_c)] + dil_parts
    x1 = _merge(hf, branches, w_in_b, w_branch_b, w_out_b, l, ln_g, ln_b, tm)
    return x1, (wkv_new, p_rw[:, -1], pool_new, ret_new, kv_new[0], kv_new[1], kv_new[2])


def kernel(x_prompt, x_sample, state_wkv, state_shift, state_pool, state_ret, cache_kv_w128, cache_kv_w512, cache_kv_w2048, w_in, rw_mu, rw_w0, rw_w2, rw_a0, rw_a2, rw_g2, rw_kk, rw_ka, rw_rk, rw_gn_g, rw_gn_b, pool_w, pool_scale, ret_gn_g, ret_gn_b, w_branch, w_out, ln_g, ln_b, ffn_w_gate, ffn_w_up, ffn_w_down, moe_router, moe_w_gate, moe_w_up, moe_w_down):
    hp, hs = x_prompt, x_sample
    bp, tp, _ = hp.shape
    bs, ts, _ = hs.shape
    names = ('rw_mu', 'rw_w0', 'rw_w2', 'rw_a0', 'rw_a2', 'rw_g2', 'rw_kk', 'rw_ka', 'rw_rk', 'rw_gn_g',
             'rw_gn_b', 'pool_w', 'pool_scale', 'ret_gn_g', 'ret_gn_b')
    stacked = (rw_mu, rw_w0, rw_w2, rw_a0, rw_a2, rw_g2, rw_kk, rw_ka, rw_rk, rw_gn_g, rw_gn_b, pool_w,
               pool_scale, ret_gn_g, ret_gn_b)
    new_p = [[] for _ in range(7)]
    new_s = [[] for _ in range(7)]
    zeros = lambda *shape: jnp.zeros(shape, F32)
    w_in_b, wb, wo = w_in.astype(BF16), w_branch.astype(BF16), w_out.astype(BF16)
    caches = [jnp.transpose(c, (0, 1, 3, 4, 5, 2)) for c in (cache_kv_w128, cache_kv_w512, cache_kv_w2048)]
    for l in range(DEPTH):
        lw = {k: v[l] for k, v in zip(names, stacked)}
        xp, st_p = _token_mix(hp, 0, zeros(bp, RW_COLS), zeros(bp, N_HEADS, HEAD_DIM, HEAD_DIM),
                              zeros(bp, POOL_BUF, WIDTH), zeros(bp, N_HEADS, HEAD_DIM, HEAD_DIM), None,
                              lw, l, w_in_b, wb, wo, ln_g[l, 0], ln_b[l, 0])
        xs, st_s = _token_mix(hs, 8192, state_shift[l], state_wkv[l], state_pool[l], state_ret[l], caches,
                              lw, l, w_in_b, wb, wo, ln_g[l, 0], ln_b[l, 0])
        j = l // 2
        if l % 2 == 0:
            ws = [w[j].astype(BF16) for w in (ffn_w_gate, ffn_w_up, ffn_w_down)]
            xp = _ffn(xp, *ws, ln_g[l, 1], ln_b[l, 1], 512)
            xs = _ffn(xs, *ws, ln_g[l, 1], ln_b[l, 1], xs.shape[0])
        else:
            ws = [w[j].astype(BF16) for w in (moe_w_gate, moe_w_up, moe_w_down)]
            xp = _moe(xp, moe_router[j], *ws, ln_g[l, 1], ln_b[l, 1], 1024)
            xs = _moe(xs, moe_router[j], *ws, ln_g[l, 1], ln_b[l, 1], xs.shape[0])
        hp, hs = xp.reshape(bp, tp, D_MODEL), xs.reshape(bs, ts, D_MODEL)
        for i in range(7):
            new_p[i].append(st_p[i])
            new_s[i].append(st_s[i])
    outs_p = [jnp.stack(x) for x in new_p]
    outs_s = [jnp.stack(x) for x in new_s]
    return (hp, hs, *outs_p, *outs_s)
```

```python
import functools
import math

import jax
import jax.numpy as jnp
from jax import lax
from jax.experimental import pallas as pl
from jax.experimental.pallas import tpu as pltpu

F32 = jnp.float32
BF16 = jnp.bfloat16

D_MODEL = 1024
DEPTH = 2
HEAD_DIM = 64
N_HEADS = 4
WIDTH = N_HEADS * HEAD_DIM
RW_COLS = 1024
RW_GN_EPS = 64e-5
POOL_WINDOWS = (2, 4, 8, 16)
POOL_BUF = 15
RET_CHUNK = 128
ROPE_BASE = 10000.0
DIL_PATTERNS = ((128, 1), (512, 4), (2048, 16))
DIL_GROUPS = 3
DIL_BLOCK = 128
DIL_SPAN = 2048
DIL_COLS = 3 * DIL_GROUPS * WIDTH
N_BRANCH = 4
COL_POOL = RW_COLS
COL_RET = COL_POOL + WIDTH
COL_DIL = COL_RET + 4 * WIDTH
COL_GATE = COL_DIL + DIL_COLS
N_EXPERTS = 8
DN_ALPHA = (2 * DEPTH) ** 0.25
LN_EPS = 1e-5
RW_CHUNK = 64
MOE_TILE = 288
RW_P = 1
RW_BATCH = 4
NEG_BIG = -1e30

NN = (((1,), (0,)), ((), ()))
NT = (((1,), (1,)), ((), ()))
TN = (((0,), (0,)), ((), ()))

VMEM_LIMIT = 56 * 1024 * 1024
MOE_VMEM_LIMIT = 61 * 1024 * 1024


def _split(x, n):
    if x.dtype == BF16:
        return [x]
    parts, rem = [], x
    for i in range(n):
        p = rem.astype(BF16)
        parts.append(p)
        if i + 1 < n:
            rem = rem - p.astype(F32)
    return parts


def _mm(a, b, dims=NN, pa=1, pb=1):
    a_parts, b_parts = _split(a, pa), _split(b, pb)
    depth = max(len(a_parts), len(b_parts))
    acc = None
    for i, ai in enumerate(a_parts):
        for j, bj in enumerate(b_parts):
            if i + j < depth:
                t = lax.dot_general(ai, bj, dims, preferred_element_type=F32)
                acc = t if acc is None else acc + t
    return acc


def _sigmoid(x):
    return 0.5 * jnp.tanh(0.5 * x) + 0.5


def _silu(x):
    return x * _sigmoid(x)


def _iota(shape, axis):
    return lax.broadcasted_iota(jnp.int32, shape, axis)


def _div(x, d):
    assert d & (d - 1) == 0
    return x >> (d.bit_length() - 1)


def _mod(x, d):
    assert d & (d - 1) == 0
    return x & (d - 1)


def _per_head(head, values):
    out = jnp.full(head.shape, values[-1], F32)
    for h in range(len(values) - 2, -1, -1):
        out = jnp.where(head == h, values[h], out)
    return out


def _head_mask(rows_per_head, n_rows):
    return _div(_iota((n_rows, WIDTH), 0), rows_per_head) == _div(_iota((n_rows, WIDTH), 1), HEAD_DIM)


def _stack_heads(x, mask):
    return jnp.where(mask, jnp.concatenate([x] * N_HEADS, axis=0), 0.0)


def _unstack_heads(x_st, mask, c):
    x_st = jnp.where(mask, x_st, 0.0)
    out = x_st[0:c]
    for h in range(1, N_HEADS):
        out = out + x_st[h * c:(h + 1) * c]
    return out


def _ones_bd():
    return _same_head().astype(BF16)


def _same_head():
    return _div(_iota((WIDTH, WIDTH), 0), HEAD_DIM) == _div(_iota((WIDTH, WIDTH), 1), HEAD_DIM)


def _head_norm(x, ones_bd, g, b, eps):
    mu = _mm(x, ones_bd, pa=2) * (1.0 / HEAD_DIM)
    d = x - mu
    var = _mm(d * d, ones_bd, pa=2) * (1.0 / HEAD_DIM)
    return d * lax.rsqrt(var + eps) * g + b


def _layer_norm(x, g, b):
    mu = jnp.mean(x, axis=-1, keepdims=True)
    d = x - mu
    var = jnp.mean(d * d, axis=-1, keepdims=True)
    return d * lax.rsqrt(var + LN_EPS) * g + b


def _resident(shape):
    nd = len(shape)
    return pl.BlockSpec(shape, lambda *_: (0,) * nd, pipeline_mode=pl.Buffered(1))


def _proj_kernel(x_ref, w_ref, rw_ref, pool_ref, ret_ref, dil_ref):
    xb = x_ref[...].astype(BF16)
    for ref, lo, hi in ((rw_ref, 0, COL_POOL), (pool_ref, COL_POOL, COL_RET),
                        (ret_ref, COL_RET, COL_DIL), (dil_ref, COL_DIL, COL_GATE)):
        for s in range(lo, hi, 512):
            e = min(s + 512, hi)
            ref[:, s - lo:e - lo] = _mm(xb, w_ref[:, s:e])


def _layer_resident(w, l):
    nd = w.ndim - 1
    return pl.BlockSpec((None,) + w.shape[1:], lambda *_: (l,) + (0,) * nd, pipeline_mode=pl.Buffered(1))


def _project(x, w_in_b, l, tm):
    m = x.shape[0]
    widths = (COL_POOL, WIDTH, 4 * WIDTH, DIL_COLS)
    row = lambda w: pl.BlockSpec((tm, w), lambda i: (i, 0))
    return pl.pallas_call(
        _proj_kernel,
        grid=(m // tm,),
        in_specs=[row(D_MODEL), _layer_resident(w_in_b, l)],
        out_specs=[row(w) for w in widths],
        out_shape=[jax.ShapeDtypeStruct((m, w), F32) for w in widths],
        compiler_params=pltpu.CompilerParams(dimension_semantics=("parallel",), vmem_limit_bytes=VMEM_LIMIT),
        name="proj",
    )(x, w_in_b)


BNT = (((2,), (2,)), ((0,), (0,)))
BNN = (((2,), (1,)), ((0,), (0,)))
BTN = (((1,), (1,)), ((0,), (0,)))


def _rwkv_kernel(p_ref, prev_ref, s0_ref, mu_ref, w0_ref, w2_ref, a0_ref, a2_ref, g2_ref, kk_ref, ka_ref,
                 rk_ref, gng_ref, gnb_ref, o_ref, st_ref, s_scr, prev_scr, buf_scr, *, c, t_in, nb):
    ci = pl.program_id(1)

    @pl.when(ci == 0)
    def _():
        s_scr[...] = _embed_heads(s0_ref[...])
        for b in range(nb):
            prev_scr[b, 0:1, :] = prev_ref[b]

    first = _iota((c, RW_COLS), 0) == 0
    ps, shs = [], []
    for b in range(nb):
        if t_in == c:
            pb_ = p_ref[b]
        else:
            buf_scr[b] = jnp.zeros((c, RW_COLS), F32)
            buf_scr[b, 0:t_in, :] = p_ref[b]
            pb_ = buf_scr[b]
        shs.append(jnp.where(first, prev_scr[b, 0:1, :], pltpu.roll(pb_, 1, 0)))
        prev_scr[b, 0:1, :] = pb_[c - 1:c, :]
        ps.append(pb_)
    p, shifted = jnp.concatenate(ps, axis=0), jnp.concatenate(shs, axis=0)
    m = nb * c
    u = p + (shifted - p) * mu_ref[...]
    r, k, v, ul = u[:, 0:WIDTH], u[:, WIDTH:2 * WIDTH], u[:, 2 * WIDTH:3 * WIDTH], u[:, 3 * WIDTH:]

    w_pre = w0_ref[...] + _mm(jnp.tanh(ul), w2_ref[...])
    logw = -math.exp(-0.5) * _sigmoid(w_pre)
    a = _sigmoid(a0_ref[...] + _mm(ul, a2_ref[...]))
    g = _mm(_sigmoid(ul), g2_ref[...])

    ones_bd = _ones_bd()
    kk = k * kk_ref[...]
    kk = kk * lax.rsqrt(jnp.maximum(_mm(kk * kk, ones_bd, pa=2), 1e-24))
    k_mod = k * (1.0 + (a - 1.0) * ka_ref[...])
    a_vec, b_vec = -kk, kk * a
    if t_in < c:
        live = _mod(_iota((m, WIDTH), 0), c) < t_in
        logw = jnp.where(live, logw, 0.0)
        a_vec, b_vec = jnp.where(live, a_vec, 0.0), jnp.where(live, b_vec, 0.0)
        k_mod, v = jnp.where(live, k_mod, 0.0), jnp.where(live, v, 0.0)

    qi, qj = _iota((m, m), 0), _iota((m, m), 1)
    tri = ((_div(qi, c) == _div(qj, c)) & (qi >= qj)).astype(BF16)
    cum2 = _mm(tri, logw, pb=3)
    seq = lambda x: x.reshape(nb, c, WIDTH)
    cum, lw3 = seq(cum2), seq(logw)
    cum_end = cum[:, c - 1:c, :]
    e_neg, e_end = jnp.exp(-cum), jnp.exp(cum_end - cum)
    a_t = seq(a_vec) * jnp.exp(cum - lw3)
    r_t = seq(r) * jnp.exp(cum)
    b_t, k_t = seq(b_vec) * e_neg, seq(k_mod) * e_neg
    b_e, k_e = seq(b_vec) * e_end, seq(k_mod) * e_end
    v3 = seq(v)

    n = N_HEADS * c
    hm = _head_mask(c, n)
    stack = lambda x: jnp.where(hm, jnp.concatenate([x] * N_HEADS, axis=1), 0.0)
    a_st, r_st, b_st, v_st = stack(a_t), stack(r_t), stack(b_t), stack(v3)
    ri, rj = _iota((n, n), 0), _iota((n, n), 1)
    same = _div(ri, c) == _div(rj, c)
    strict_bd = same & (_mod(ri, c) > _mod(rj, c))
    incl_bd = same & (_mod(ri, c) >= _mod(rj, c))
    ti, tj = _mod(_iota((n, c), 0), c), _iota((n, c), 1)
    strict_st, incl_st = ti > tj, ti >= tj

    mm = functools.partial(_mm, pa=RW_P, pb=RW_P)
    a_ab = jnp.where(strict_bd, mm(a_st, b_st, BNT), 0.0)
    inv = jnp.where(ri == rj, 1.0, 0.0) + a_ab
    pw = a_ab
    for _ in range(int(math.log2(c)) - 1):
        pw = mm(pw, pw, BNN)
        inv = inv + mm(inv, pw, BNN)
    a_ak = jnp.where(strict_st, mm(a_st, k_t, BNT), 0.0)
    z_st = jnp.where(hm, mm(a_ak, v3, BNN), 0.0)
    wu = mm(inv, jnp.concatenate([a_st, z_st], axis=2), BNN)
    w_st, u0_st = wu[:, :, 0:WIDTH], wu[:, :, WIDTH:]

    s0 = s_scr[...]
    u_st = mm(w_st, s0, BNT) + u0_st
    lhs = jnp.concatenate([u_st, v_st], axis=1)
    rhs = jnp.concatenate([stack(b_e), stack(k_e)], axis=1)
    s_scr[...] = s0 * jnp.exp(cum_end) + mm(lhs, rhs, BTN)

    a_rb = jnp.where(incl_bd, mm(r_st, b_st, BNT), 0.0)
    a_rk = jnp.where(incl_st, mm(r_st, k_t, BNT), 0.0)
    o_st = jnp.where(hm, mm(r_st, s0, BNT) + mm(a_rb, u_st, BNN) + mm(a_rk, v3, BNN), 0.0)
    o3 = o_st[:, 0:c]
    for h in range(1, N_HEADS):
        o3 = o3 + o_st[:, h * c:(h + 1) * c]
    o = o3.reshape(m, WIDTH)

    o = _head_norm(o, ones_bd, gng_ref[...], gnb_ref[...], RW_GN_EPS)
    o = o + _mm(r * k_mod * rk_ref[...], ones_bd, pa=2) * v
    o_ref[...] = (o * g).reshape(nb, c, WIDTH)[:, 0:t_in]

    @pl.when(ci == pl.num_programs(1) - 1)
    def _():
        st_ref[...] = _extract_heads(s_scr[...])


def _embed_heads(s):
    return jnp.where(_same_head(), jnp.concatenate([s] * N_HEADS, axis=-1), 0.0)


def _extract_heads(s_bd):
    out = s_bd[..., 0:HEAD_DIM]
    for h in range(1, N_HEADS):
        out = out + s_bd[..., h * HEAD_DIM:(h + 1) * HEAD_DIM]
    return out


def _rwkv_branch(p, p_prev, wkv0, lw):
    b, t, _ = p.shape
    c = RW_CHUNK if t % RW_CHUNK == 0 else RW_CHUNK // 2
    assert t % c == 0 or t <= c
    t_in = c if t % c == 0 else t
    nc = t // c if t % c == 0 else 1
    vec = lambda x: x.reshape(1, -1)
    pad_rows = lambda w, lo: jnp.zeros((WIDTH, WIDTH), F32).at[lo:lo + w.shape[0]].set(w).astype(BF16)
    params = [vec(lw['rw_mu']), vec(lw['rw_w0']), pad_rows(lw['rw_w2'], 0), vec(lw['rw_a0']),
              pad_rows(lw['rw_a2'], 64), pad_rows(lw['rw_g2'], 128), vec(lw['rw_kk']), vec(lw['rw_ka']),
              vec(lw['rw_rk']), vec(lw['rw_gn_g']), vec(lw['rw_gn_b'])]
    nb = RW_BATCH
    assert b % nb == 0
    o, st = pl.pallas_call(
        functools.partial(_rwkv_kernel, c=c, t_in=t_in, nb=nb),
        grid=(b // nb, nc),
        in_specs=[pl.BlockSpec((nb, t_in, RW_COLS), lambda i, j: (i, j, 0)),
                  pl.BlockSpec((nb, 1, RW_COLS), lambda i, j: (i, 0, 0)),
                  pl.BlockSpec((nb, WIDTH, HEAD_DIM), lambda i, j: (i, 0, 0))]
                 + [_resident(x.shape) for x in params],
        out_specs=[pl.BlockSpec((nb, t_in, WIDTH), lambda i, j: (i, j, 0)),
                   pl.BlockSpec((nb, WIDTH, HEAD_DIM), lambda i, j: (i, 0, 0))],
        out_shape=[jax.ShapeDtypeStruct((b, t, WIDTH), F32), jax.ShapeDtypeStruct((b, WIDTH, HEAD_DIM), F32)],
        scratch_shapes=[pltpu.VMEM((nb, WIDTH, WIDTH), F32), pltpu.VMEM((nb, 8, RW_COLS), F32),
                        pltpu.VMEM((nb, c, RW_COLS), F32)],
        compiler_params=pltpu.CompilerParams(dimension_semantics=("parallel", "arbitrary"),
                                             vmem_limit_bytes=VMEM_LIMIT),
        name="rwkv7",
    )(p, p_prev.reshape(b, 1, RW_COLS), wkv0.reshape(b, WIDTH, HEAD_DIM), *params)
    return o, st.reshape(b, N_HEADS, HEAD_DIM, HEAD_DIM)


def _pool_rows(ext_scr, c, pos_first, w, scale):
    x = ext_scr[16:16 + c, :]
    sums, acc, off = [], x, 1
    for win in POOL_WINDOWS:
        while off < win:
            acc = acc + ext_scr[16 - off:16 - off + c, :]
            off += 1
        sums.append(acc)
    pos = pos_first + _iota((c, WIDTH), 0)
    grp = _div(_iota((c, WIDTH), 1), HEAD_DIM)
    mean = jnp.zeros((c, WIDTH), F32)
    for gi, win in enumerate(POOL_WINDOWS):
        cnt = jnp.minimum(win, pos + 1).astype(F32)
        mean = jnp.where(grp == gi, sums[gi] / cnt, mean)
    ext_scr[0:16, :] = ext_scr[c:c + 16, :]
    return _mm(mean - x, w) * scale


def _pool_kernel(u_ref, buf_ref, w_ref, scale_ref, o_ref, ext_scr, *, c, t_in, pos0):
    ci = pl.program_id(1)

    @pl.when(ci == 0)
    def _():
        ext_scr[0:16, :] = buf_ref[0]

    if t_in < c:
        ext_scr[16:16 + c, :] = jnp.zeros((c, WIDTH), F32)
    ext_scr[16:16 + t_in, :] = u_ref[0]
    o_ref[0] = _pool_rows(ext_scr, c, pos0 + ci * c, w_ref[...], scale_ref[...])[0:t_in]


def _pool_weights(lw):
    w_bd = jnp.einsum('gcd,gh->gchd', lw['pool_w'], jnp.eye(N_HEADS, dtype=F32)).reshape(WIDTH, WIDTH)
    return w_bd.astype(BF16), lw['pool_scale'].reshape(1, WIDTH)


def _pool_branch(u, buf, pos0, lw):
    b, t, _ = u.shape
    c = 512 if t % 512 == 0 else 8
    t_in = c if t % c == 0 else t
    nc = t // c if t % c == 0 else 1
    w_bd, scale = _pool_weights(lw)
    buf16 = jnp.pad(buf, ((0, 0), (1, 0), (0, 0)))
    return pl.pallas_call(
        functools.partial(_pool_kernel, c=c, t_in=t_in, pos0=pos0),
        grid=(b, nc),
        in_specs=[pl.BlockSpec((1, t_in, WIDTH), lambda i, j: (i, j, 0)),
                  pl.BlockSpec((1, 16, WIDTH), lambda i, j: (i, 0, 0)),
                  _resident((WIDTH, WIDTH)), _resident((1, WIDTH))],
        out_specs=pl.BlockSpec((1, t_in, WIDTH), lambda i, j: (i, j, 0)),
        out_shape=jax.ShapeDtypeStruct((b, t, WIDTH), F32),
        scratch_shapes=[pltpu.VMEM((16 + c, WIDTH), F32)],
        compiler_params=pltpu.CompilerParams(dimension_semantics=("parallel", "arbitrary")),
        name="pool",
    )(u, buf16, w_bd, scale)


def _rot_half(x):
    first = _mod(_iota(x.shape, 1), HEAD_DIM) < (HEAD_DIM // 2)
    return jnp.where(first, pltpu.roll(x, WIDTH - HEAD_DIM // 2, 1), pltpu.roll(x, HEAD_DIM // 2, 1))


RET_LOG_DECAY = tuple(math.log(1.0 - 2.0 ** (-5.0 - h)) for h in range(N_HEADS))
ALIBI_SLOPES = tuple(2.0 ** (-8.0 * (i + 1) / (DIL_GROUPS * N_HEADS)) for i in range(DIL_GROUPS * N_HEADS))


def _ret_kernel(p_ref, cos_ref, sin_ref, s0_ref, gng_ref, gnb_ref, o_ref, st_ref, s_scr, buf_scr,
                dmask_scr, qd_scr, kd_scr, *, c, t_in, nb):
    ci = pl.program_id(1)
    n = N_HEADS * c
    lg = _per_head(_div(_iota((1, WIDTH), 1), HEAD_DIM), RET_LOG_DECAY)

    @pl.when(ci == 0)
    def _():
        s_scr[...] = _embed_heads(s0_ref[...])
        idx = _iota((c, WIDTH), 0).astype(F32)
        qd_scr[...] = jnp.exp(lg * (idx + 1.0))
        kd_scr[...] = jnp.exp(lg * (t_in - 1.0 - idx))
        rel = _mod(_iota((n, c), 0), c) - _iota((n, c), 1)
        lg_rows = _per_head(_div(_iota((n, c), 0), c), RET_LOG_DECAY)
        dmask_scr[...] = jnp.where(rel >= 0, jnp.exp(lg_rows * jnp.maximum(rel, 0).astype(F32)), 0.0)

    hm = _head_mask(c, n)
    cos, sin = cos_ref[...], sin_ref[...]
    chunk_decay = jnp.exp(lg * float(t_in))
    same_head, ones_bd = _same_head(), _ones_bd()
    for b in range(nb):
        if t_in == c:
            p = p_ref[b]
        else:
            buf_scr[b] = jnp.zeros((c, 4 * WIDTH), F32)
            buf_scr[b, 0:t_in, :] = p_ref[b]
            p = buf_scr[b]
        q, k, v, g = (p[:, i * WIDTH:(i + 1) * WIDTH] for i in range(4))
        q = q * cos + _rot_half(q) * sin
        k = (k * cos + _rot_half(k) * sin) * HEAD_DIM ** -0.5
        if t_in < c:
            live = _iota((c, WIDTH), 0) < t_in
            k, v = jnp.where(live, k, 0.0), jnp.where(live, v, 0.0)
        inner = _mm(_stack_heads(q, hm), k, NT) * dmask_scr[...]
        s0 = s_scr[b]
        o = _unstack_heads(_mm(inner, v), hm, c) + _mm(q * qd_scr[...], s0)
        s_scr[b] = s0 * chunk_decay + jnp.where(same_head, _mm(k * kd_scr[...], v, TN), 0.0)
        o = _silu(g) * _head_norm(o, ones_bd, gng_ref[...], gnb_ref[...], LN_EPS)
        o_ref[b] = o[0:t_in]

    @pl.when(ci == pl.num_programs(1) - 1)
    def _():
        st_ref[...] = _extract_heads(s_scr[...])


def _rope_tables(pos0, t, rows):
    half = HEAD_DIM // 2
    inv = ROPE_BASE ** (-jnp.arange(half, dtype=F32) / half)
    ang = (pos0 + jnp.arange(t, dtype=jnp.int32)).astype(F32)[:, None] * inv[None, :]
    cos = jnp.tile(jnp.cos(ang), (1, 2 * N_HEADS))
    sin = jnp.tile(jnp.concatenate([-jnp.sin(ang), jnp.sin(ang)], axis=1), (1, N_HEADS))
    pad = ((0, rows - t), (0, 0))
    return jnp.pad(cos, pad), jnp.pad(sin, pad)


def _ret_branch(p, s0, pos0, lw):
    b, t, _ = p.shape
    c = RET_CHUNK
    t_in = c if t % c == 0 else t
    nc = t // c if t % c == 0 else 1
    cos, sin = _rope_tables(pos0, t, nc * c)
    nb = RW_BATCH
    assert b % nb == 0
    o, st = pl.pallas_call(
        functools.partial(_ret_kernel, c=c, t_in=t_in, nb=nb),
        grid=(b // nb, nc),
        in_specs=[pl.BlockSpec((nb, t_in, 4 * WIDTH), lambda i, j: (i, j, 0)),
                  pl.BlockSpec((c, WIDTH), lambda i, j: (j, 0)),
                  pl.BlockSpec((c, WIDTH), lambda i, j: (j, 0)),
                  pl.BlockSpec((nb, WIDTH, HEAD_DIM), lambda i, j: (i, 0, 0)),
                  _resident((1, WIDTH)), _resident((1, WIDTH))],
        out_specs=[pl.BlockSpec((nb, t_in, WIDTH), lambda i, j: (i, j, 0)),
                   pl.BlockSpec((nb, WIDTH, HEAD_DIM), lambda i, j: (i, 0, 0))],
        out_shape=[jax.ShapeDtypeStruct((b, t, WIDTH), F32), jax.ShapeDtypeStruct((b, WIDTH, HEAD_DIM), F32)],
        scratch_shapes=[pltpu.VMEM((nb, WIDTH, WIDTH), F32), pltpu.VMEM((nb, c, 4 * WIDTH), F32),
                        pltpu.VMEM((N_HEADS * c, c), F32), pltpu.VMEM((c, WIDTH), F32), pltpu.VMEM((c, WIDTH), F32)],
        compiler_params=pltpu.CompilerParams(dimension_semantics=("parallel", "arbitrary")),
        name="retention",
    )(p, cos, sin, s0.reshape(b, WIDTH, HEAD_DIM), lw['ret_gn_g'].reshape(1, WIDTH), lw['ret_gn_b'].reshape(1, WIDTH))
    return o, st.reshape(b, N_HEADS, HEAD_DIM, HEAD_DIM)


def _alibi_slope_rows(group, rows_per_head):
    head = _div(_iota((N_HEADS * rows_per_head, 1), 0), rows_per_head)
    return _per_head(head, ALIBI_SLOPES[group * N_HEADS:(group + 1) * N_HEADS])


def _dil_prompt_kernel(*refs, group, dil, span):
    q_refs, kc_refs, kp_refs, vc_refs, vp_refs = (refs[2 * i:2 * i + 2] for i in range(5))
    o_ref, lse_ref, k_scr, v_scr, o_scr, lse_scr = refs[10:]
    blk = DIL_BLOCK
    tail = blk * dil
    si = pl.program_id(1)
    for half in range(2):
        k_scr[half, 0:tail, :] = kp_refs[half][0]
        k_scr[half, tail:tail + span, :] = kc_refs[half][0]
        v_scr[half, 0:tail, :] = vp_refs[half][0]
        v_scr[half, tail:tail + span, :] = vc_refs[half][0]
    ki = _iota((blk, 2 * blk), 1)
    steps = blk + _iota((blk, 2 * blk), 0) - ki
    band = (steps >= 0) & (steps <= blk)
    masks = [jnp.where(band, (-ALIBI_SLOPES[group * N_HEADS + h] * dil) * steps.astype(F32), NEG_BIG)
             for h in range(N_HEADS)]
    sub_head = _div(_iota((blk, 128), 1), HEAD_DIM)
    for cc in range(span // tail):
        has_prev = (si > 0) | (ki >= blk)
        cc_masks = [jnp.where(has_prev, x, NEG_BIG) for x in masks] if cc == 0 else masks
        for r in range(dil):
            rows_q = pl.ds(cc * tail + r, blk, stride=dil) if dil > 1 else pl.ds(cc * tail, blk)
            rows_kv = pl.ds(cc * tail + r, 2 * blk, stride=dil) if dil > 1 else pl.ds(cc * tail, 2 * blk)
            for half in range(2):
                q2 = q_refs[half][0, rows_q, :] * HEAD_DIM ** -0.5
                k2 = k_scr[half, rows_kv, :].astype(BF16)
                v2 = v_scr[half, rows_kv, :].astype(BF16)
                o2 = lse2 = None
                for sub in range(2):
                    mine = sub_head == sub
                    s = _mm(jnp.where(mine, q2, 0.0), k2, NT) + cc_masks[2 * half + sub]
                    m = jnp.max(s, axis=1, keepdims=True)
                    e = jnp.exp(s - m)
                    l = jnp.sum(e, axis=1, keepdims=True)
                    o_h = _mm(e, v2) / l
                    lse_h = jnp.broadcast_to(m + jnp.log(l), (blk, 128))
                    o2 = o_h if o2 is None else jnp.where(mine, o_h, o2)
                    lse2 = lse_h if lse2 is None else jnp.where(mine, lse_h, lse2)
                o_scr[half, rows_q, :] = o2
                lse_scr[half, rows_q, :] = lse2
    o_ref[0] = jnp.concatenate([o_scr[0], o_scr[1]], axis=1)
    lse_ref[0] = jnp.concatenate([lse_scr[0], lse_scr[1]], axis=1)


def _dil_prompt_group(pd, group):
    b, s, _ = pd.shape
    win, dil = DIL_PATTERNS[group]
    span = DIL_SPAN
    tail = DIL_BLOCK * dil
    assert win // dil == DIL_BLOCK and span % tail == 0 and s % span == 0
    base = group * 3

    def cur(col):
        return [pl.BlockSpec((1, span, 128), lambda i, j, h=h: (i, j, 2 * (base + col) + h)) for h in range(2)]

    def prev(col):
        return [pl.BlockSpec((1, tail, 128),
                             lambda i, j, h=h: (i, jnp.maximum(j * (span // tail) - 1, 0), 2 * (base + col) + h))
                for h in range(2)]

    out_spec = pl.BlockSpec((1, span, WIDTH), lambda i, j: (i, j, 0))
    return pl.pallas_call(
        functools.partial(_dil_prompt_kernel, group=group, dil=dil, span=span),
        grid=(b, s // span),
        in_specs=cur(0) + cur(1) + prev(1) + cur(2) + prev(2),
        out_specs=[out_spec, out_spec],
        out_shape=[jax.ShapeDtypeStruct((b, s, WIDTH), F32)] * 2,
        scratch_shapes=[pltpu.VMEM((2, tail + span, 128), F32)] * 2 + [pltpu.VMEM((2, span, 128), F32)] * 2,
        compiler_params=pltpu.CompilerParams(dimension_semantics=("parallel", "arbitrary"),
                                             vmem_limit_bytes=VMEM_LIMIT),
        name=f"dil_prompt_g{group}",
    )(*([pd] * 10))


def _dil_combine(os_, ls):
    m = jnp.maximum(jnp.maximum(ls[0], ls[1]), ls[2])
    es = [jnp.exp(x - m) for x in ls]
    return (es[0] * os_[0] + es[1] * os_[1] + es[2] * os_[2]) / (es[0] + es[1] + es[2])


def _dil_prompt(pd):
    b, s, _ = pd.shape
    outs = [_dil_prompt_group(pd, g) for g in range(DIL_GROUPS)]
    return [x[0].reshape(b * s, WIDTH) for x in outs] + [x[1].reshape(b * s, WIDTH) for x in outs]


def _dil_step_kernel(pd_ref, c0_ref, c1_ref, c2_ref, o_ref, buf_scr, *, t, tp):
    buf_scr[...] = jnp.zeros_like(buf_scr)
    buf_scr[0:t, :] = pd_ref[0]
    pd = buf_scr[...]
    qt = _iota((tp, 1), 0)
    d_new = qt - _iota((tp, tp), 1)
    outs = [[None] * DIL_GROUPS for _ in range(N_HEADS)]
    lses = [[None] * DIL_GROUPS for _ in range(N_HEADS)]
    for g, cache_ref in enumerate((c0_ref, c1_ref, c2_ref)):
        win, dil = DIL_PATTERNS[g]
        length = cache_ref.shape[-1]
        d_old = length + qt - _iota((tp, length), 1)
        ok_old = (_mod(d_old, dil) == 0) & (d_old <= win)
        ok_new = (d_new >= 0) & (_mod(d_new, dil) == 0)
        for h in range(N_HEADS):
            slope = ALIBI_SLOPES[g * N_HEADS + h]
            lo = g * 3 * WIDTH + h * HEAD_DIM
            q = pd[:, lo:lo + HEAD_DIM] * HEAD_DIM ** -0.5
            k_new = pd[:, lo + WIDTH:lo + WIDTH + HEAD_DIM]
            v_new = pd[:, lo + 2 * WIDTH:lo + 2 * WIDTH + HEAD_DIM]
            k_t, v_t = cache_ref[0, 0, h], cache_ref[0, 1, h]
            s_old = jnp.where(ok_old, _mm(q, k_t) - slope * d_old.astype(F32), NEG_BIG)
            s_new = jnp.where(ok_new, _mm(q, k_new, NT) - slope * d_new.astype(F32), NEG_BIG)
            m = jnp.maximum(jnp.max(s_old, axis=1, keepdims=True), jnp.max(s_new, axis=1, keepdims=True))
            e_old, e_new = jnp.exp(s_old - m), jnp.exp(s_new - m)
            l = jnp.sum(e_old, axis=1, keepdims=True) + jnp.sum(e_new, axis=1, keepdims=True)
            outs[h][g] = (_mm(e_old, v_t, NT) + _mm(e_new, v_new)) / l
            lses[h][g] = m + jnp.log(l)
    o = jnp.concatenate([_dil_combine(outs[h], lses[h]) for h in range(N_HEADS)], axis=1)
    o_ref[0] = o[0:t]


def _dil_step(pd, caches, l):
    b, t, _ = pd.shape
    tp = 8
    flat = caches
    return pl.pallas_call(
        functools.partial(_dil_step_kernel, t=t, tp=tp),
        grid=(b,),
        in_specs=[pl.BlockSpec((1, t, DIL_COLS), lambda i: (i, 0, 0))]
                 + [pl.BlockSpec((None, 1) + c.shape[2:], lambda i: (l, i, 0, 0, 0, 0)) for c in flat],
        out_specs=pl.BlockSpec((1, t, WIDTH), lambda i: (i, 0, 0)),
        out_shape=jax.ShapeDtypeStruct((b, t, WIDTH), F32),
        scratch_shapes=[pltpu.VMEM((tp, DIL_COLS), F32)],
        compiler_params=pltpu.CompilerParams(dimension_semantics=("parallel",), vmem_limit_bytes=VMEM_LIMIT),
        name="dil_step",
    )(pd, *flat)


def _merge_kernel(h_ref, *refs):
    wg_ref, wb_ref, wo_ref, g_ref, b_ref, out_ref = refs[-6:]
    branch_refs = refs[:-6]
    branches = [r[...] for r in branch_refs[:3]]
    if len(branch_refs) == 4:
        branches.append(branch_refs[3][...])
    else:
        branches.append(_dil_combine([r[...] for r in branch_refs[3:6]], [r[...] for r in branch_refs[6:9]]))
    h = h_ref[...]
    hb = h.astype(BF16)
    z = None
    for n, o in enumerate(branches):
        gate = _sigmoid(_mm(hb, wg_ref[:, COL_GATE + n * D_MODEL:COL_GATE + (n + 1) * D_MODEL]))
        term = gate * _mm(o, wb_ref[n])
        z = term if z is None else z + term
    y = _mm(z, wo_ref[...])
    out_ref[...] = _layer_norm(DN_ALPHA * h + y, g_ref[...], b_ref[...])


def _merge(h, branches, w_in_b, w_branch_b, w_out_b, l, ln_g, ln_b, tm):
    m = h.shape[0]
    row = lambda w: pl.BlockSpec((tm, w), lambda i: (i, 0))
    return pl.pallas_call(
        _merge_kernel,
        grid=(m // tm,),
        in_specs=[row(D_MODEL)] + [row(WIDTH)] * len(branches)
                 + [_layer_resident(w_in_b, l), _layer_resident(w_branch_b, l), _layer_resident(w_out_b, l),
                    _resident((1, D_MODEL)), _resident((1, D_MODEL))],
        out_specs=row(D_MODEL),
        out_shape=jax.ShapeDtypeStruct((m, D_MODEL), F32),
        compiler_params=pltpu.CompilerParams(dimension_semantics=("parallel",), vmem_limit_bytes=VMEM_LIMIT),
        name="merge_ln",
    )(h, *branches, w_in_b, w_branch_b, w_out_b, ln_g.reshape(1, -1), ln_b.reshape(1, -1))


def _ffn_kernel(x_ref, wg_ref, wu_ref, wd_ref, g_ref, b_ref, out_ref):
    x = x_ref[...]
    xb = x.astype(BF16)
    act = _silu(_mm(xb, wg_ref[...])) * _mm(xb, wu_ref[...])
    out_ref[...] = _layer_norm(DN_ALPHA * x + _mm(act, wd_ref[...]), g_ref[...], b_ref[...])


def _ffn(x, wg, wu, wd, ln_g, ln_b, tm):
    m = x.shape[0]
    row = pl.BlockSpec((tm, D_MODEL), lambda i: (i, 0))
    return pl.pallas_call(
        _ffn_kernel,
        grid=(m // tm,),
        in_specs=[row, _resident(wg.shape), _resident(wu.shape), _resident(wd.shape),
                  _resident((1, D_MODEL)), _resident((1, D_MODEL))],
        out_specs=row,
        out_shape=jax.ShapeDtypeStruct((m, D_MODEL), F32),
        compiler_params=pltpu.CompilerParams(dimension_semantics=("parallel",), vmem_limit_bytes=VMEM_LIMIT),
        name="ffn_ln",
    )(x, wg, wu, wd, ln_g.reshape(1, -1), ln_b.reshape(1, -1))


def _route_kernel(x_ref, rt_ref, tri_ref, gate_ref, rank_ref, cnt_ref):
    logits = _mm(rt_ref[...], x_ref[...], NT, 2, 2)
    sub = _iota(logits.shape, 0).astype(F32)
    m1 = jnp.max(logits, axis=0, keepdims=True)
    i1 = jnp.min(jnp.where(logits == m1, sub, float(N_EXPERTS)), axis=0, keepdims=True)
    rest = jnp.where(sub == i1, NEG_BIG, logits)
    m2 = jnp.max(rest, axis=0, keepdims=True)
    i2 = jnp.min(jnp.where(rest == m2, sub, float(N_EXPERTS)), axis=0, keepdims=True)
    e2 = jnp.exp(m2 - m1)
    gate_ref[0] = jnp.where(sub == i1, 1.0 / (1.0 + e2), 0.0) + jnp.where(sub == i2, e2 / (1.0 + e2), 0.0)
    chosen = (sub == i1) | (sub == i2)
    sel = jnp.where(chosen, 1.0, 0.0)
    rank_ref[0] = jnp.where(chosen, _mm(sel, tri_ref[...]), -1.0)
    cnt_ref[0] = jnp.broadcast_to(jnp.sum(sel, axis=1, keepdims=True), cnt_ref.shape[1:])


def _moe_kernel(cnt_ref, x_ref, gate_ref, rank_ref, wg_ref, wu_ref, wd_ref, g_ref, b_ref, out_ref,
                xb_scr, xg_scr, yg_scr, *, ts, tb, nsb):
    i, e, f = pl.program_id(0), pl.program_id(1), pl.program_id(2)
    last_f = f == pl.num_programs(2) - 1

    @pl.when((e == 0) & (f == 0))
    def _():
        xb_scr[...] = x_ref[...].astype(BF16)
        out_ref[...] = jnp.zeros_like(out_ref)

    slot = _iota((ts, tb), 0)

    def expert(xg):
        return _mm(_silu(_mm(xg, wg_ref[0])) * _mm(xg, wu_ref[0]), wd_ref[0])

    def pick_of(sb, j):
        rank_row = rank_ref[sb, pl.ds(e, 1), :]
        return jnp.where(rank_row == (slot + j * ts).astype(F32), 1.0, 0.0)

    def scatter(sb, pick, y):
        w_slot = jnp.sum(pick * gate_ref[sb, pl.ds(e, 1), :], axis=1, keepdims=True)
        out_ref[sb * tb:(sb + 1) * tb, :] += _mm(pick, y * w_slot, TN)

    @pl.when(f == 0)
    def _():
        for sb in range(nsb):
            xg_scr[sb] = _mm(pick_of(sb, 0), xb_scr[sb * tb:(sb + 1) * tb, :]).astype(BF16)
        yg_scr[...] = jnp.zeros_like(yg_scr)

    yg_scr[...] += expert(xg_scr[...].reshape(nsb * ts, D_MODEL)).reshape(nsb, ts, D_MODEL)

    @pl.when(last_f)
    def _():
        for sb in range(nsb):
            scatter(sb, pick_of(sb, 0), yg_scr[sb])

    for sb in range(nsb):
        n_tiles = (cnt_ref[(i * nsb + sb) * N_EXPERTS + e] + (ts - 1)) // ts

        def overflow(j, carry, sb=sb):
            pick = pick_of(sb, j)
            scatter(sb, pick, expert(_mm(pick, xb_scr[sb * tb:(sb + 1) * tb, :]).astype(BF16)))
            return carry

        lax.fori_loop(1, n_tiles, overflow, 0)

    @pl.when((e == pl.num_programs(1) - 1) & last_f)
    def _():
        out_ref[...] = _layer_norm(DN_ALPHA * x_ref[...] + out_ref[...], g_ref[...], b_ref[...])


def _moe(x, router, wg, wu, wd, ln_g, ln_b, tm):
    m = x.shape[0]
    nblk = m // tm
    dff = wg.shape[2]
    nf = 2
    tf = dff // nf
    ts = min(MOE_TILE, tm)
    nsb = 2 if nblk % 2 == 0 else 1
    assert m % tm == 0 and tf % 128 == 0
    tri = (jnp.arange(tm)[:, None] < jnp.arange(tm)[None, :]).astype(BF16)
    gate, rank, cnt = pl.pallas_call(
        _route_kernel,
        grid=(nblk,),
        in_specs=[pl.BlockSpec((tm, D_MODEL), lambda i: (i, 0)), _resident((N_EXPERTS, D_MODEL)),
                  _resident((tm, tm))],
        out_specs=[pl.BlockSpec((1, N_EXPERTS, tm), lambda i: (i, 0, 0)),
                   pl.BlockSpec((1, N_EXPERTS, tm), lambda i: (i, 0, 0)),
                   pl.BlockSpec((1, N_EXPERTS, 128), lambda i: (i, 0, 0))],
        out_shape=[jax.ShapeDtypeStruct((nblk, N_EXPERTS, tm), F32), jax.ShapeDtypeStruct((nblk, N_EXPERTS, tm), F32),
                   jax.ShapeDtypeStruct((nblk, N_EXPERTS, 128), F32)],
        compiler_params=pltpu.CompilerParams(dimension_semantics=("parallel",), vmem_limit_bytes=VMEM_LIMIT),
        name="moe_route",
    )(x, router.T, tri)
    counts = cnt[:, :, 0].astype(jnp.int32).reshape(-1)
    rows = nsb * tm
    x_in = pl.BlockSpec((rows, D_MODEL), lambda i, e, f, c: (i, 0), pipeline_mode=pl.Buffered(1))
    meta = pl.BlockSpec((nsb, N_EXPERTS, tm), lambda i, e, f, c: (i, 0, 0))
    vec = pl.BlockSpec((1, D_MODEL), lambda i, e, f, c: (0, 0), pipeline_mode=pl.Buffered(1))
    return pl.pallas_call(
        functools.partial(_moe_kernel, ts=ts, tb=tm, nsb=nsb),
        grid_spec=pltpu.PrefetchScalarGridSpec(
            num_scalar_prefetch=1,
            grid=(nblk // nsb, N_EXPERTS, nf),
            in_specs=[x_in, meta, meta,
                      pl.BlockSpec((1, D_MODEL, tf), lambda i, e, f, c: (e, 0, f)),
                      pl.BlockSpec((1, D_MODEL, tf), lambda i, e, f, c: (e, 0, f)),
                      pl.BlockSpec((1, tf, D_MODEL), lambda i, e, f, c: (e, f, 0)),
                      vec, vec],
            out_specs=pl.BlockSpec((rows, D_MODEL), lambda i, e, f, c: (i, 0), pipeline_mode=pl.Buffered(1)),
            scratch_shapes=[pltpu.VMEM((rows, D_MODEL), BF16),
                            pltpu.VMEM((nsb, ts, D_MODEL), BF16), pltpu.VMEM((nsb, ts, D_MODEL), F32)]),
        out_shape=jax.ShapeDtypeStruct((m, D_MODEL), F32),
        compiler_params=pltpu.CompilerParams(dimension_semantics=("parallel", "arbitrary", "arbitrary"),
                                             vmem_limit_bytes=MOE_VMEM_LIMIT),
        name="moe_ln",
    )(counts, x, gate, rank, wg, wu, wd, ln_g.reshape(1, -1), ln_b.reshape(1, -1))


def _token_mix(h, pos0, rw_prev, wkv0, pool_buf, ret0, kv_bufs, lw, l, w_in_b, w_branch_b, w_out_b, ln_g, ln_b):
    b, t, _ = h.shape
    m = b * t
    tm = 512 if m % 512 == 0 else m
    hf = h.reshape(m, D_MODEL)
    p_rw, p_pool, p_ret, p_dil = _project(hf, w_in_b, l, tm)
    p_rw, p_pool = p_rw.reshape(b, t, -1), p_pool.reshape(b, t, -1)
    p_ret, p_dil = p_ret.reshape(b, t, -1), p_dil.reshape(b, t, -1)
    o_a, wkv_new = _rwkv_branch(p_rw, rw_prev, wkv0, lw)
    o_b = _pool_branch(p_pool, pool_buf, pos0, lw)
    pool_new = jnp.concatenate([pool_buf, p_pool], axis=1)[:, -POOL_BUF:]
    o_c, ret_new = _ret_branch(p_ret, ret0, pos0, lw)
    def kv_rows(g, keep):
        lo = (3 * g + 1) * WIDTH
        return p_dil[:, t - keep:, lo:lo + 2 * WIDTH].reshape(b, keep, 2, N_HEADS, HEAD_DIM)

    if kv_bufs is None:
        dil_parts = _dil_prompt(p_dil)
        kv_new = [kv_rows(g, min(win, t)) for g, (win, _) in enumerate(DIL_PATTERNS)]
    else:
        dil_parts = [_dil_step(p_dil, kv_bufs, l).reshape(m, WIDTH)]
        kv_new = [kv_rows(g, t) for g in range(DIL_GROUPS)]
    branches = [x.reshape(m, WIDTH) for x in (o_a, o_b, o_c)] + dil_parts
    x1 = _merge(hf, branches, w_in_b, w_branch_b, w_out_b, l, ln_g, ln_b, tm)
    return x1, (wkv_new, p_rw[:, -1], pool_new, ret_new, kv_new[0], kv_new[1], kv_new[2])


def kernel(x_prompt, x_sample, state_wkv, state_shift, state_pool, state_ret, cache_kv_w128, cache_kv_w512, cache_kv_w2048, w_in, rw_mu, rw_w0, rw_w2, rw_a0, rw_a2, rw_g2, rw_kk, rw_ka, rw_rk, rw_gn_g, rw_gn_b, pool_w, pool_scale, ret_gn_g, ret_gn_b, w_branch, w_out, ln_g, ln_b, ffn_w_gate, ffn_w_up, ffn_w_down, moe_router, moe_w_gate, moe_w_up, moe_w_down):
    hp, hs = x_prompt, x_sample
    bp, tp, _ = hp.shape
    bs, ts, _ = hs.shape
    names = ('rw_mu', 'rw_w0', 'rw_w2', 'rw_a0', 'rw_a2', 'rw_g2', 'rw_kk', 'rw_ka', 'rw_rk', 'rw_gn_g',
             'rw_gn_b', 'pool_w', 'pool_scale', 'ret_gn_g', 'ret_gn_b')
    stacked = (rw_mu, rw_w0, rw_w2, rw_a0, rw_a2, rw_g2, rw_kk, rw_ka, rw_rk, rw_gn_g, rw_gn_b, pool_w,
               pool_scale, ret_gn_g, ret_gn_b)
    new_p = [[] for _ in range(7)]
    new_s = [[] for _ in range(7)]
    zeros = lambda *shape: jnp.zeros(shape, F32)
    w_in_b, wb, wo = w_in.astype(BF16), w_branch.astype(BF16), w_out.astype(BF16)
    caches = [jnp.transpose(c, (0, 1, 3, 4, 5, 2)) for c in (cache_kv_w128, cache_kv_w512, cache_kv_w2048)]
    for l in range(DEPTH):
        lw = {k: v[l] for k, v in zip(names, stacked)}
        xp, st_p = _token_mix(hp, 0, zeros(bp, RW_COLS), zeros(bp, N_HEADS, HEAD_DIM, HEAD_DIM),
                              zeros(bp, POOL_BUF, WIDTH), zeros(bp, N_HEADS, HEAD_DIM, HEAD_DIM), None,
                              lw, l, w_in_b, wb, wo, ln_g[l, 0], ln_b[l, 0])
        xs, st_s = _token_mix(hs, 8192, state_shift[l], state_wkv[l], state_pool[l], state_ret[l], caches,
                              lw, l, w_in_b, wb, wo, ln_g[l, 0], ln_b[l, 0])
        j = l // 2
        if l % 2 == 0:
            ws = [w[j].astype(BF16) for w in (ffn_w_gate, ffn_w_up, ffn_w_down)]
            xp = _ffn(xp, *ws, ln_g[l, 1], ln_b[l, 1], 512)
            xs = _ffn(xs, *ws, ln_g[l, 1], ln_b[l, 1], xs.shape[0])
        else:
            ws = [w[j].astype(BF16) for w in (moe_w_gate, moe_w_up, moe_w_down)]
            xp = _moe(xp, moe_router[j], *ws, ln_g[l, 1], ln_b[l, 1], 1024)
            xs = _moe(xs, moe_router[j], *ws, ln_g[l, 1], ln_b[l, 1], xs.shape[0])
        hp, hs = xp.reshape(bp, tp, D_MODEL), xs.reshape(bs, ts, D_MODEL)
        for i in range(7):
            new_p[i].append(st_p[i])
            new_s[i].append(st_s[i])
    outs_p = [jnp.stack(x) for x in new_p]
    outs_s = [jnp.stack(x) for x in new_s]
    return (hp, hs, *outs_p, *outs_s)
```

```python
import functools
import math

import jax
import jax.numpy as jnp
from jax import lax
from jax.experimental import pallas as pl
from jax.experimental.pallas import tpu as pltpu

F32 = jnp.float32
BF16 = jnp.bfloat16

D_MODEL = 1024
DEPTH = 2
HEAD_DIM = 64
N_HEADS = 4
WIDTH = N_HEADS * HEAD_DIM
RW_COLS = 1024
RW_GN_EPS = 64e-5
POOL_WINDOWS = (2, 4, 8, 16)
POOL_BUF = 15
RET_CHUNK = 128
ROPE_BASE = 10000.0
DIL_PATTERNS = ((128, 1), (512, 4), (2048, 16))
DIL_GROUPS = 3
DIL_BLOCK = 128
DIL_SPAN = 2048
DIL_COLS = 3 * DIL_GROUPS * WIDTH
N_BRANCH = 4
COL_POOL = RW_COLS
COL_RET = COL_POOL + WIDTH
COL_DIL = COL_RET + 4 * WIDTH
COL_GATE = COL_DIL + DIL_COLS
N_EXPERTS = 8
DN_ALPHA = (2 * DEPTH) ** 0.25
LN_EPS = 1e-5
RW_CHUNK = 64
MOE_TILE = 288
RW_P = 1
RW_BATCH = 4
RW_LANE_GROUPS = 2
NEG_BIG = -1e30

NN = (((1,), (0,)), ((), ()))
NT = (((1,), (1,)), ((), ()))
TN = (((0,), (0,)), ((), ()))

VMEM_LIMIT = 56 * 1024 * 1024
MOE_VMEM_LIMIT = 61 * 1024 * 1024


def _split(x, n):
    if x.dtype == BF16:
        return [x]
    parts, rem = [], x
    for i in range(n):
        p = rem.astype(BF16)
        parts.append(p)
        if i + 1 < n:
            rem = rem - p.astype(F32)
    return parts


def _mm(a, b, dims=NN, pa=1, pb=1):
    a_parts, b_parts = _split(a, pa), _split(b, pb)
    depth = max(len(a_parts), len(b_parts))
    acc = None
    for i, ai in enumerate(a_parts):
        for j, bj in enumerate(b_parts):
            if i + j < depth:
                t = lax.dot_general(ai, bj, dims, preferred_element_type=F32)
                acc = t if acc is None else acc + t
    return acc


def _sigmoid(x):
    return 0.5 * jnp.tanh(0.5 * x) + 0.5


def _silu(x):
    return x * _sigmoid(x)


def _iota(shape, axis):
    return lax.broadcasted_iota(jnp.int32, shape, axis)


def _div(x, d):
    assert d & (d - 1) == 0
    return x >> (d.bit_length() - 1)


def _mod(x, d):
    assert d & (d - 1) == 0
    return x & (d - 1)


def _per_head(head, values):
    out = jnp.full(head.shape, values[-1], F32)
    for h in range(len(values) - 2, -1, -1):
        out = jnp.where(head == h, values[h], out)
    return out


def _head_mask(rows_per_head, n_rows):
    return _div(_iota((n_rows, WIDTH), 0), rows_per_head) == _div(_iota((n_rows, WIDTH), 1), HEAD_DIM)


def _stack_heads(x, mask):
    return jnp.where(mask, jnp.concatenate([x] * N_HEADS, axis=0), 0.0)


def _unstack_heads(x_st, mask, c):
    x_st = jnp.where(mask, x_st, 0.0)
    out = x_st[0:c]
    for h in range(1, N_HEADS):
        out = out + x_st[h * c:(h + 1) * c]
    return out


def _ones_bd():
    return _same_head().astype(BF16)


def _same_head():
    return _div(_iota((WIDTH, WIDTH), 0), HEAD_DIM) == _div(_iota((WIDTH, WIDTH), 1), HEAD_DIM)


def _head_norm(x, ones_bd, g, b, eps):
    mu = _mm(x, ones_bd, pa=2) * (1.0 / HEAD_DIM)
    d = x - mu
    var = _mm(d * d, ones_bd, pa=2) * (1.0 / HEAD_DIM)
    return d * lax.rsqrt(var + eps) * g + b


def _layer_norm(x, g, b):
    mu = jnp.mean(x, axis=-1, keepdims=True)
    d = x - mu
    var = jnp.mean(d * d, axis=-1, keepdims=True)
    return d * lax.rsqrt(var + LN_EPS) * g + b


def _resident(shape):
    nd = len(shape)
    return pl.BlockSpec(shape, lambda *_: (0,) * nd, pipeline_mode=pl.Buffered(1))


def _proj_kernel(x_ref, w_ref, rw_ref, pool_ref, ret_ref, dil_ref):
    xb = x_ref[...].astype(BF16)
    for ref, lo, hi in ((rw_ref, 0, COL_POOL), (pool_ref, COL_POOL, COL_RET),
                        (ret_ref, COL_RET, COL_DIL), (dil_ref, COL_DIL, COL_GATE)):
        for s in range(lo, hi, 512):
            e = min(s + 512, hi)
            ref[:, s - lo:e - lo] = _mm(xb, w_ref[:, s:e])


def _layer_resident(w, l):
    nd = w.ndim - 1
    return pl.BlockSpec((None,) + w.shape[1:], lambda *_: (l,) + (0,) * nd, pipeline_mode=pl.Buffered(1))


def _project(x, w_in_b, l, tm):
    m = x.shape[0]
    widths = (COL_POOL, WIDTH, 4 * WIDTH, DIL_COLS)
    row = lambda w: pl.BlockSpec((tm, w), lambda i: (i, 0))
    return pl.pallas_call(
        _proj_kernel,
        grid=(m // tm,),
        in_specs=[row(D_MODEL), _layer_resident(w_in_b, l)],
        out_specs=[row(w) for w in widths],
        out_shape=[jax.ShapeDtypeStruct((m, w), F32) for w in widths],
        compiler_params=pltpu.CompilerParams(dimension_semantics=("parallel",), vmem_limit_bytes=VMEM_LIMIT),
        name="proj",
    )(x, w_in_b)


BNT = (((2,), (2,)), ((0,), (0,)))
BNN = (((2,), (1,)), ((0,), (0,)))
BTN = (((1,), (1,)), ((0,), (0,)))


def _rwkv_kernel(p_ref, prev_ref, s0_ref, mu_ref, w0_ref, w2_ref, a0_ref, a2_ref, g2_ref, kk_ref, ka_ref,
                 rk_ref, gng_ref, gnb_ref, o_ref, st_ref, s_scr, prev_scr, buf_scr, *, c, t_in, nb):
    ci = pl.program_id(1)

    halves = RW_LANE_GROUPS
    hw, hpg = WIDTH // halves, N_HEADS // halves
    n = hpg * c
    same_grp = _div(_iota((hw, hw), 0), HEAD_DIM) == _div(_iota((hw, hw), 1), HEAD_DIM)

    @pl.when(ci == 0)
    def _():
        for g in range(halves):
            part = s0_ref[:, g * hw:(g + 1) * hw, :]
            s_scr[g * nb:(g + 1) * nb] = jnp.where(same_grp, jnp.concatenate([part] * hpg, axis=-1), 0.0)
        for b in range(nb):
            prev_scr[b, 0:1, :] = prev_ref[b]

    first = _iota((c, RW_COLS), 0) == 0
    ps, shs = [], []
    for b in range(nb):
        if t_in == c:
            pb_ = p_ref[b]
        else:
            buf_scr[b] = jnp.zeros((c, RW_COLS), F32)
            buf_scr[b, 0:t_in, :] = p_ref[b]
            pb_ = buf_scr[b]
        shs.append(jnp.where(first, prev_scr[b, 0:1, :], pltpu.roll(pb_, 1, 0)))
        prev_scr[b, 0:1, :] = pb_[c - 1:c, :]
        ps.append(pb_)
    p, shifted = jnp.concatenate(ps, axis=0), jnp.concatenate(shs, axis=0)
    m = nb * c
    u = p + (shifted - p) * mu_ref[...]
    r, k, v, ul = u[:, 0:WIDTH], u[:, WIDTH:2 * WIDTH], u[:, 2 * WIDTH:3 * WIDTH], u[:, 3 * WIDTH:]

    w_pre = w0_ref[...] + _mm(jnp.tanh(ul), w2_ref[...])
    logw = -math.exp(-0.5) * _sigmoid(w_pre)
    a = _sigmoid(a0_ref[...] + _mm(ul, a2_ref[...]))
    g = _mm(_sigmoid(ul), g2_ref[...])

    ones_bd = _ones_bd()
    kk = k * kk_ref[...]
    kk = kk * lax.rsqrt(jnp.maximum(_mm(kk * kk, ones_bd, pa=2), 1e-24))
    k_mod = k * (1.0 + (a - 1.0) * ka_ref[...])
    a_vec, b_vec = -kk, kk * a
    if t_in < c:
        live = _mod(_iota((m, WIDTH), 0), c) < t_in
        logw = jnp.where(live, logw, 0.0)
        a_vec, b_vec = jnp.where(live, a_vec, 0.0), jnp.where(live, b_vec, 0.0)
        k_mod, v = jnp.where(live, k_mod, 0.0), jnp.where(live, v, 0.0)

    qi, qj = _iota((m, m), 0), _iota((m, m), 1)
    tri = ((_div(qi, c) == _div(qj, c)) & (qi >= qj)).astype(BF16)
    cum2 = _mm(tri, logw, pb=3)
    def seq(x):
        x3 = x.reshape(nb, c, WIDTH)
        return jnp.concatenate([x3[:, :, g * hw:(g + 1) * hw] for g in range(halves)], axis=0)

    cum, lw3 = seq(cum2), seq(logw)
    cum_end = cum[:, c - 1:c, :]
    e_neg, e_end = jnp.exp(-cum), jnp.exp(cum_end - cum)
    a_t = seq(a_vec) * jnp.exp(cum - lw3)
    r_t = seq(r) * jnp.exp(cum)
    b_t, k_t = seq(b_vec) * e_neg, seq(k_mod) * e_neg
    b_e, k_e = seq(b_vec) * e_end, seq(k_mod) * e_end
    v3 = seq(v)

    hm = _div(_iota((n, hw), 0), c) == _div(_iota((n, hw), 1), HEAD_DIM)
    stack = lambda x: jnp.where(hm, jnp.concatenate([x] * hpg, axis=1), 0.0)
    a_st, r_st, b_st, v_st = stack(a_t), stack(r_t), stack(b_t), stack(v3)
    ri, rj = _iota((n, n), 0), _iota((n, n), 1)
    same = _div(ri, c) == _div(rj, c)
    strict_bd = same & (_mod(ri, c) > _mod(rj, c))
    incl_bd = same & (_mod(ri, c) >= _mod(rj, c))
    ti, tj = _mod(_iota((n, c), 0), c), _iota((n, c), 1)
    strict_st, incl_st = ti > tj, ti >= tj

    mm = functools.partial(_mm, pa=RW_P, pb=RW_P)
    a_ab = jnp.where(strict_bd, mm(a_st, b_st, BNT), 0.0)
    inv = jnp.where(ri == rj, 1.0, 0.0) + a_ab
    pw = a_ab
    for _ in range(int(math.log2(c)) - 1):
        pw = mm(pw, pw, BNN)
        inv = inv + mm(inv, pw, BNN)
    a_ak = jnp.where(strict_st, mm(a_st, k_t, BNT), 0.0)
    z_st = jnp.where(hm, mm(a_ak, v3, BNN), 0.0)
    wu = mm(inv, jnp.concatenate([a_st, z_st], axis=2), BNN)
    w_st, u0_st = wu[:, :, 0:hw], wu[:, :, hw:]

    s0 = s_scr[...]
    u_st = mm(w_st, s0, BNT) + u0_st
    lhs = jnp.concatenate([u_st, v_st], axis=1)
    rhs = jnp.concatenate([stack(b_e), stack(k_e)], axis=1)
    s_scr[...] = s0 * jnp.exp(cum_end) + mm(lhs, rhs, BTN)

    a_rb = jnp.where(incl_bd, mm(r_st, b_st, BNT), 0.0)
    a_rk = jnp.where(incl_st, mm(r_st, k_t, BNT), 0.0)
    o_st = jnp.where(hm, mm(r_st, s0, BNT) + mm(a_rb, u_st, BNN) + mm(a_rk, v3, BNN), 0.0)
    o3 = o_st[:, 0:c]
    for h in range(1, hpg):
        o3 = o3 + o_st[:, h * c:(h + 1) * c]
    o = jnp.concatenate([o3[g * nb:(g + 1) * nb] for g in range(halves)], axis=2).reshape(m, WIDTH)

    o = _head_norm(o, ones_bd, gng_ref[...], gnb_ref[...], RW_GN_EPS)
    o = o + _mm(r * k_mod * rk_ref[...], ones_bd, pa=2) * v
    o_ref[...] = (o * g).reshape(nb, c, WIDTH)[:, 0:t_in]

    @pl.when(ci == pl.num_programs(1) - 1)
    def _():
        for g in range(halves):
            blk = s_scr[g * nb:(g + 1) * nb]
            part = blk[..., 0:HEAD_DIM]
            for h in range(1, hpg):
                part = part + blk[..., h * HEAD_DIM:(h + 1) * HEAD_DIM]
            st_ref[:, g * hw:(g + 1) * hw, :] = part


def _embed_heads(s):
    return jnp.where(_same_head(), jnp.concatenate([s] * N_HEADS, axis=-1), 0.0)


def _extract_heads(s_bd):
    out = s_bd[..., 0:HEAD_DIM]
    for h in range(1, N_HEADS):
        out = out + s_bd[..., h * HEAD_DIM:(h + 1) * HEAD_DIM]
    return out


def _rwkv_branch(p, p_prev, wkv0, lw):
    b, t, _ = p.shape
    c = RW_CHUNK if t % RW_CHUNK == 0 else RW_CHUNK // 2
    assert t % c == 0 or t <= c
    t_in = c if t % c == 0 else t
    nc = t // c if t % c == 0 else 1
    vec = lambda x: x.reshape(1, -1)
    pad_rows = lambda w, lo: jnp.zeros((WIDTH, WIDTH), F32).at[lo:lo + w.shape[0]].set(w).astype(BF16)
    params = [vec(lw['rw_mu']), vec(lw['rw_w0']), pad_rows(lw['rw_w2'], 0), vec(lw['rw_a0']),
              pad_rows(lw['rw_a2'], 64), pad_rows(lw['rw_g2'], 128), vec(lw['rw_kk']), vec(lw['rw_ka']),
              vec(lw['rw_rk']), vec(lw['rw_gn_g']), vec(lw['rw_gn_b'])]
    nb = RW_BATCH
    assert b % nb == 0
    o, st = pl.pallas_call(
        functools.partial(_rwkv_kernel, c=c, t_in=t_in, nb=nb),
        grid=(b // nb, nc),
        in_specs=[pl.BlockSpec((nb, t_in, RW_COLS), lambda i, j: (i, j, 0)),
                  pl.BlockSpec((nb, 1, RW_COLS), lambda i, j: (i, 0, 0)),
                  pl.BlockSpec((nb, WIDTH, HEAD_DIM), lambda i, j: (i, 0, 0))]
                 + [_resident(x.shape) for x in params],
        out_specs=[pl.BlockSpec((nb, t_in, WIDTH), lambda i, j: (i, j, 0)),
                   pl.BlockSpec((nb, WIDTH, HEAD_DIM), lambda i, j: (i, 0, 0))],
        out_shape=[jax.ShapeDtypeStruct((b, t, WIDTH), F32), jax.ShapeDtypeStruct((b, WIDTH, HEAD_DIM), F32)],
        scratch_shapes=[pltpu.VMEM((RW_LANE_GROUPS * nb, WIDTH // RW_LANE_GROUPS, WIDTH // RW_LANE_GROUPS), F32),
                        pltpu.VMEM((nb, 8, RW_COLS), F32),
                        pltpu.VMEM((nb, c, RW_COLS), F32)],
        compiler_params=pltpu.CompilerParams(dimension_semantics=("parallel", "arbitrary"),
                                             vmem_limit_bytes=VMEM_LIMIT),
        name="rwkv7",
    )(p, p_prev.reshape(b, 1, RW_COLS), wkv0.reshape(b, WIDTH, HEAD_DIM), *params)
    return o, st.reshape(b, N_HEADS, HEAD_DIM, HEAD_DIM)


def _pool_rows(ext_scr, c, pos_first, w, scale):
    x = ext_scr[16:16 + c, :]
    sums, acc, off = [], x, 1
    for win in POOL_WINDOWS:
        while off < win:
            acc = acc + ext_scr[16 - off:16 - off + c, :]
            off += 1
        sums.append(acc)
    pos = pos_first + _iota((c, WIDTH), 0)
    grp = _div(_iota((c, WIDTH), 1), HEAD_DIM)
    mean = jnp.zeros((c, WIDTH), F32)
    for gi, win in enumerate(POOL_WINDOWS):
        cnt = jnp.minimum(win, pos + 1).astype(F32)
        mean = jnp.where(grp == gi, sums[gi] / cnt, mean)
    ext_scr[0:16, :] = ext_scr[c:c + 16, :]
    return _mm(mean - x, w) * scale


def _pool_kernel(u_ref, buf_ref, w_ref, scale_ref, o_ref, ext_scr, *, c, t_in, pos0):
    ci = pl.program_id(1)

    @pl.when(ci == 0)
    def _():
        ext_scr[0:16, :] = buf_ref[0]

    if t_in < c:
        ext_scr[16:16 + c, :] = jnp.zeros((c, WIDTH), F32)
    ext_scr[16:16 + t_in, :] = u_ref[0]
    o_ref[0] = _pool_rows(ext_scr, c, pos0 + ci * c, w_ref[...], scale_ref[...])[0:t_in]


def _pool_weights(lw):
    w_bd = jnp.einsum('gcd,gh->gchd', lw['pool_w'], jnp.eye(N_HEADS, dtype=F32)).reshape(WIDTH, WIDTH)
    return w_bd.astype(BF16), lw['pool_scale'].reshape(1, WIDTH)


def _pool_branch(u, buf, pos0, lw):
    b, t, _ = u.shape
    c = 512 if t % 512 == 0 else 8
    t_in = c if t % c == 0 else t
    nc = t // c if t % c == 0 else 1
    w_bd, scale = _pool_weights(lw)
    buf16 = jnp.pad(buf, ((0, 0), (1, 0), (0, 0)))
    return pl.pallas_call(
        functools.partial(_pool_kernel, c=c, t_in=t_in, pos0=pos0),
        grid=(b, nc),
        in_specs=[pl.BlockSpec((1, t_in, WIDTH), lambda i, j: (i, j, 0)),
                  pl.BlockSpec((1, 16, WIDTH), lambda i, j: (i, 0, 0)),
                  _resident((WIDTH, WIDTH)), _resident((1, WIDTH))],
        out_specs=pl.BlockSpec((1, t_in, WIDTH), lambda i, j: (i, j, 0)),
        out_shape=jax.ShapeDtypeStruct((b, t, WIDTH), F32),
        scratch_shapes=[pltpu.VMEM((16 + c, WIDTH), F32)],
        compiler_params=pltpu.CompilerParams(dimension_semantics=("parallel", "arbitrary")),
        name="pool",
    )(u, buf16, w_bd, scale)


def _rot_half(x):
    first = _mod(_iota(x.shape, 1), HEAD_DIM) < (HEAD_DIM // 2)
    return jnp.where(first, pltpu.roll(x, WIDTH - HEAD_DIM // 2, 1), pltpu.roll(x, HEAD_DIM // 2, 1))


RET_LOG_DECAY = tuple(math.log(1.0 - 2.0 ** (-5.0 - h)) for h in range(N_HEADS))
ALIBI_SLOPES = tuple(2.0 ** (-8.0 * (i + 1) / (DIL_GROUPS * N_HEADS)) for i in range(DIL_GROUPS * N_HEADS))


def _ret_kernel(p_ref, cos_ref, sin_ref, s0_ref, gng_ref, gnb_ref, o_ref, st_ref, s_scr, buf_scr,
                dmask_scr, qd_scr, kd_scr, *, c, t_in, nb):
    ci = pl.program_id(1)
    n = N_HEADS * c
    lg = _per_head(_div(_iota((1, WIDTH), 1), HEAD_DIM), RET_LOG_DECAY)

    @pl.when(ci == 0)
    def _():
        s_scr[...] = _embed_heads(s0_ref[...])
        idx = _iota((c, WIDTH), 0).astype(F32)
        qd_scr[...] = jnp.exp(lg * (idx + 1.0))
        kd_scr[...] = jnp.exp(lg * (t_in - 1.0 - idx))
        rel = _mod(_iota((n, c), 0), c) - _iota((n, c), 1)
        lg_rows = _per_head(_div(_iota((n, c), 0), c), RET_LOG_DECAY)
        dmask_scr[...] = jnp.where(rel >= 0, jnp.exp(lg_rows * jnp.maximum(rel, 0).astype(F32)), 0.0)

    hm = _head_mask(c, n)
    cos, sin = cos_ref[...], sin_ref[...]
    chunk_decay = jnp.exp(lg * float(t_in))
    same_head, ones_bd = _same_head(), _ones_bd()
    for b in range(nb):
        if t_in == c:
            p = p_ref[b]
        else:
            buf_scr[b] = jnp.zeros((c, 4 * WIDTH), F32)
            buf_scr[b, 0:t_in, :] = p_ref[b]
            p = buf_scr[b]
        q, k, v, g = (p[:, i * WIDTH:(i + 1) * WIDTH] for i in range(4))
        q = q * cos + _rot_half(q) * sin
        k = (k * cos + _rot_half(k) * sin) * HEAD_DIM ** -0.5
        if t_in < c:
            live = _iota((c, WIDTH), 0) < t_in
            k, v = jnp.where(live, k, 0.0), jnp.where(live, v, 0.0)
        inner = _mm(_stack_heads(q, hm), k, NT) * dmask_scr[...]
        s0 = s_scr[b]
        o = _unstack_heads(_mm(inner, v), hm, c) + _mm(q * qd_scr[...], s0)
        s_scr[b] = s0 * chunk_decay + jnp.where(same_head, _mm(k * kd_scr[...], v, TN), 0.0)
        o = _silu(g) * _head_norm(o, ones_bd, gng_ref[...], gnb_ref[...], LN_EPS)
        o_ref[b] = o[0:t_in]

    @pl.when(ci == pl.num_programs(1) - 1)
    def _():
        st_ref[...] = _extract_heads(s_scr[...])


def _rope_tables(pos0, t, rows):
    half = HEAD_DIM // 2
    inv = ROPE_BASE ** (-jnp.arange(half, dtype=F32) / half)
    ang = (pos0 + jnp.arange(t, dtype=jnp.int32)).astype(F32)[:, None] * inv[None, :]
    cos = jnp.tile(jnp.cos(ang), (1, 2 * N_HEADS))
    sin = jnp.tile(jnp.concatenate([-jnp.sin(ang), jnp.sin(ang)], axis=1), (1, N_HEADS))
    pad = ((0, rows - t), (0, 0))
    return jnp.pad(cos, pad), jnp.pad(sin, pad)


def _ret_branch(p, s0, pos0, lw):
    b, t, _ = p.shape
    c = RET_CHUNK
    t_in = c if t % c == 0 else t
    nc = t // c if t % c == 0 else 1
    cos, sin = _rope_tables(pos0, t, nc * c)
    nb = RW_BATCH
    assert b % nb == 0
    o, st = pl.pallas_call(
        functools.partial(_ret_kernel, c=c, t_in=t_in, nb=nb),
        grid=(b // nb, nc),
        in_specs=[pl.BlockSpec((nb, t_in, 4 * WIDTH), lambda i, j: (i, j, 0)),
                  pl.BlockSpec((c, WIDTH), lambda i, j: (j, 0)),
                  pl.BlockSpec((c, WIDTH), lambda i, j: (j, 0)),
                  pl.BlockSpec((nb, WIDTH, HEAD_DIM), lambda i, j: (i, 0, 0)),
                  _resident((1, WIDTH)), _resident((1, WIDTH))],
        out_specs=[pl.BlockSpec((nb, t_in, WIDTH), lambda i, j: (i, j, 0)),
                   pl.BlockSpec((nb, WIDTH, HEAD_DIM), lambda i, j: (i, 0, 0))],
        out_shape=[jax.ShapeDtypeStruct((b, t, WIDTH), F32), jax.ShapeDtypeStruct((b, WIDTH, HEAD_DIM), F32)],
        scratch_shapes=[pltpu.VMEM((nb, WIDTH, WIDTH), F32), pltpu.VMEM((nb, c, 4 * WIDTH), F32),
                        pltpu.VMEM((N_HEADS * c, c), F32), pltpu.VMEM((c, WIDTH), F32), pltpu.VMEM((c, WIDTH), F32)],
        compiler_params=pltpu.CompilerParams(dimension_semantics=("parallel", "arbitrary")),
        name="retention",
    )(p, cos, sin, s0.reshape(b, WIDTH, HEAD_DIM), lw['ret_gn_g'].reshape(1, WIDTH), lw['ret_gn_b'].reshape(1, WIDTH))
    return o, st.reshape(b, N_HEADS, HEAD_DIM, HEAD_DIM)


def _alibi_slope_rows(group, rows_per_head):
    head = _div(_iota((N_HEADS * rows_per_head, 1), 0), rows_per_head)
    return _per_head(head, ALIBI_SLOPES[group * N_HEADS:(group + 1) * N_HEADS])


def _dil_prompt_kernel(*refs, group, dil, span):
    q_refs, kc_refs, kp_refs, vc_refs, vp_refs = (refs[2 * i:2 * i + 2] for i in range(5))
    o_ref, lse_ref, k_scr, v_scr, o_scr, lse_scr = refs[10:]
    blk = DIL_BLOCK
    tail = blk * dil
    si = pl.program_id(1)
    for half in range(2):
        k_scr[half, 0:tail, :] = kp_refs[half][0]
        k_scr[half, tail:tail + span, :] = kc_refs[half][0]
        v_scr[half, 0:tail, :] = vp_refs[half][0]
        v_scr[half, tail:tail + span, :] = vc_refs[half][0]
    ki = _iota((blk, 2 * blk), 1)
    steps = blk + _iota((blk, 2 * blk), 0) - ki
    band = (steps >= 0) & (steps <= blk)
    masks = [jnp.where(band, (-ALIBI_SLOPES[group * N_HEADS + h] * dil) * steps.astype(F32), NEG_BIG)
             for h in range(N_HEADS)]
    sub_head = _div(_iota((blk, 128), 1), HEAD_DIM)
    for cc in range(span // tail):
        has_prev = (si > 0) | (ki >= blk)
        cc_masks = [jnp.where(has_prev, x, NEG_BIG) for x in masks] if cc == 0 else masks
        for r in range(dil):
            rows_q = pl.ds(cc * tail + r, blk, stride=dil) if dil > 1 else pl.ds(cc * tail, blk)
            rows_kv = pl.ds(cc * tail + r, 2 * blk, stride=dil) if dil > 1 else pl.ds(cc * tail, 2 * blk)
            for half in range(2):
                q2 = q_refs[half][0, rows_q, :] * HEAD_DIM ** -0.5
                k2 = k_scr[half, rows_kv, :].astype(BF16)
                v2 = v_scr[half, rows_kv, :].astype(BF16)
                o2 = lse2 = None
                for sub in range(2):
                    mine = sub_head == sub
                    s = _mm(jnp.where(mine, q2, 0.0), k2, NT) + cc_masks[2 * half + sub]
                    m = jnp.max(s, axis=1, keepdims=True)
                    e = jnp.exp(s - m)
                    l = jnp.sum(e, axis=1, keepdims=True)
                    o_h = _mm(e, v2) / l
                    lse_h = jnp.broadcast_to(m + jnp.log(l), (blk, 128))
                    o2 = o_h if o2 is None else jnp.where(mine, o_h, o2)
                    lse2 = lse_h if lse2 is None else jnp.where(mine, lse_h, lse2)
                o_scr[half, rows_q, :] = o2
                lse_scr[half, rows_q, :] = lse2
    o_ref[0] = jnp.concatenate([o_scr[0], o_scr[1]], axis=1)
    lse_ref[0] = jnp.concatenate([lse_scr[0], lse_scr[1]], axis=1)


def _dil_prompt_group(pd, group):
    b, s, _ = pd.shape
    win, dil = DIL_PATTERNS[group]
    span = DIL_SPAN
    tail = DIL_BLOCK * dil
    assert win // dil == DIL_BLOCK and span % tail == 0 and s % span == 0
    base = group * 3

    def cur(col):
        return [pl.BlockSpec((1, span, 128), lambda i, j, h=h: (i, j, 2 * (base + col) + h)) for h in range(2)]

    def prev(col):
        return [pl.BlockSpec((1, tail, 128),
                             lambda i, j, h=h: (i, jnp.maximum(j * (span // tail) - 1, 0), 2 * (base + col) + h))
                for h in range(2)]

    out_spec = pl.BlockSpec((1, span, WIDTH), lambda i, j: (i, j, 0))
    return pl.pallas_call(
        functools.partial(_dil_prompt_kernel, group=group, dil=dil, span=span),
        grid=(b, s // span),
        in_specs=cur(0) + cur(1) + prev(1) + cur(2) + prev(2),
        out_specs=[out_spec, out_spec],
        out_shape=[jax.ShapeDtypeStruct((b, s, WIDTH), F32)] * 2,
        scratch_shapes=[pltpu.VMEM((2, tail + span, 128), F32)] * 2 + [pltpu.VMEM((2, span, 128), F32)] * 2,
        compiler_params=pltpu.CompilerParams(dimension_semantics=("parallel", "arbitrary"),
                                             vmem_limit_bytes=VMEM_LIMIT),
        name=f"dil_prompt_g{group}",
    )(*([pd] * 10))


def _dil_combine(os_, ls):
    m = jnp.maximum(jnp.maximum(ls[0], ls[1]), ls[2])
    es = [jnp.exp(x - m) for x in ls]
    return (es[0] * os_[0] + es[1] * os_[1] + es[2] * os_[2]) / (es[0] + es[1] + es[2])


def _dil_prompt(pd):
    b, s, _ = pd.shape
    outs = [_dil_prompt_group(pd, g) for g in range(DIL_GROUPS)]
    return [x[0].reshape(b * s, WIDTH) for x in outs] + [x[1].reshape(b * s, WIDTH) for x in outs]


def _dil_step_kernel(pd_ref, c0_ref, c1_ref, c2_ref, o_ref, buf_scr, *, t, tp):
    buf_scr[...] = jnp.zeros_like(buf_scr)
    buf_scr[0:t, :] = pd_ref[0]
    pd = buf_scr[...]
    qt = _iota((tp, 1), 0)
    d_new = qt - _iota((tp, tp), 1)
    outs = [[None] * DIL_GROUPS for _ in range(N_HEADS)]
    lses = [[None] * DIL_GROUPS for _ in range(N_HEADS)]
    for g, cache_ref in enumerate((c0_ref, c1_ref, c2_ref)):
        win, dil = DIL_PATTERNS[g]
        length = cache_ref.shape[-1]
        d_old = length + qt - _iota((tp, length), 1)
        ok_old = (_mod(d_old, dil) == 0) & (d_old <= win)
        ok_new = (d_new >= 0) & (_mod(d_new, dil) == 0)
        for h in range(N_HEADS):
            slope = ALIBI_SLOPES[g * N_HEADS + h]
            lo = g * 3 * WIDTH + h * HEAD_DIM
            q = pd[:, lo:lo + HEAD_DIM] * HEAD_DIM ** -0.5
            k_new = pd[:, lo + WIDTH:lo + WIDTH + HEAD_DIM]
            v_new = pd[:, lo + 2 * WIDTH:lo + 2 * WIDTH + HEAD_DIM]
            k_t, v_t = cache_ref[0, 0, h], cache_ref[0, 1, h]
            s_old = jnp.where(ok_old, _mm(q, k_t) - slope * d_old.astype(F32), NEG_BIG)
            s_new = jnp.where(ok_new, _mm(q, k_new, NT) - slope * d_new.astype(F32), NEG_BIG)
            m = jnp.maximum(jnp.max(s_old, axis=1, keepdims=True), jnp.max(s_new, axis=1, keepdims=True))
            e_old, e_new = jnp.exp(s_old - m), jnp.exp(s_new - m)
            l = jnp.sum(e_old, axis=1, keepdims=True) + jnp.sum(e_new, axis=1, keepdims=True)
            outs[h][g] = (_mm(e_old, v_t, NT) + _mm(e_new, v_new)) / l
            lses[h][g] = m + jnp.log(l)
    o = jnp.concatenate([_dil_combine(outs[h], lses[h]) for h in range(N_HEADS)], axis=1)
    o_ref[0] = o[0:t]


def _dil_step(pd, caches, l):
    b, t, _ = pd.shape
    tp = 8
    flat = caches
    return pl.pallas_call(
        functools.partial(_dil_step_kernel, t=t, tp=tp),
        grid=(b,),
        in_specs=[pl.BlockSpec((1, t, DIL_COLS), lambda i: (i, 0, 0))]
                 + [pl.BlockSpec((None, 1) + c.shape[2:], lambda i: (l, i, 0, 0, 0, 0)) for c in flat],
        out_specs=pl.BlockSpec((1, t, WIDTH), lambda i: (i, 0, 0)),
        out_shape=jax.ShapeDtypeStruct((b, t, WIDTH), F32),
        scratch_shapes=[pltpu.VMEM((tp, DIL_COLS), F32)],
        compiler_params=pltpu.CompilerParams(dimension_semantics=("parallel",), vmem_limit_bytes=VMEM_LIMIT),
        name="dil_step",
    )(pd, *flat)


def _merge_kernel(h_ref, *refs):
    wg_ref, wb_ref, wo_ref, g_ref, b_ref, out_ref = refs[-6:]
    branch_refs = refs[:-6]
    branches = [r[...] for r in branch_refs[:3]]
    if len(branch_refs) == 4:
        branches.append(branch_refs[3][...])
    else:
        branches.append(_dil_combine([r[...] for r in branch_refs[3:6]], [r[...] for r in branch_refs[6:9]]))
    h = h_ref[...]
    hb = h.astype(BF16)
    z = None
    for n, o in enumerate(branches):
        gate = _sigmoid(_mm(hb, wg_ref[:, COL_GATE + n * D_MODEL:COL_GATE + (n + 1) * D_MODEL]))
        term = gate * _mm(o, wb_ref[n])
        z = term if z is None else z + term
    y = _mm(z, wo_ref[...])
    out_ref[...] = _layer_norm(DN_ALPHA * h + y, g_ref[...], b_ref[...])


def _merge(h, branches, w_in_b, w_branch_b, w_out_b, l, ln_g, ln_b, tm):
    m = h.shape[0]
    row = lambda w: pl.BlockSpec((tm, w), lambda i: (i, 0))
    return pl.pallas_call(
        _merge_kernel,
        grid=(m // tm,),
        in_specs=[row(D_MODEL)] + [row(WIDTH)] * len(branches)
                 + [_layer_resident(w_in_b, l), _layer_resident(w_branch_b, l), _layer_resident(w_out_b, l),
                    _resident((1, D_MODEL)), _resident((1, D_MODEL))],
        out_specs=row(D_MODEL),
        out_shape=jax.ShapeDtypeStruct((m, D_MODEL), F32),
        compiler_params=pltpu.CompilerParams(dimension_semantics=("parallel",), vmem_limit_bytes=VMEM_LIMIT),
        name="merge_ln",
    )(h, *branches, w_in_b, w_branch_b, w_out_b, ln_g.reshape(1, -1), ln_b.reshape(1, -1))


def _ffn_kernel(x_ref, wg_ref, wu_ref, wd_ref, g_ref, b_ref, out_ref):
    x = x_ref[...]
    xb = x.astype(BF16)
    act = _silu(_mm(xb, wg_ref[...])) * _mm(xb, wu_ref[...])
    out_ref[...] = _layer_norm(DN_ALPHA * x + _mm(act, wd_ref[...]), g_ref[...], b_ref[...])


def _ffn(x, wg, wu, wd, ln_g, ln_b, tm):
    m = x.shape[0]
    row = pl.BlockSpec((tm, D_MODEL), lambda i: (i, 0))
    return pl.pallas_call(
        _ffn_kernel,
        grid=(m // tm,),
        in_specs=[row, _resident(wg.shape), _resident(wu.shape), _resident(wd.shape),
                  _resident((1, D_MODEL)), _resident((1, D_MODEL))],
        out_specs=row,
        out_shape=jax.ShapeDtypeStruct((m, D_MODEL), F32),
        compiler_params=pltpu.CompilerParams(dimension_semantics=("parallel",), vmem_limit_bytes=VMEM_LIMIT),
        name="ffn_ln",
    )(x, wg, wu, wd, ln_g.reshape(1, -1), ln_b.reshape(1, -1))


def _route_kernel(x_ref, rt_ref, tri_ref, gate_ref, rank_ref, cnt_ref):
    logits = _mm(rt_ref[...], x_ref[...], NT, 2, 2)
    sub = _iota(logits.shape, 0).astype(F32)
    m1 = jnp.max(logits, axis=0, keepdims=True)
    i1 = jnp.min(jnp.where(logits == m1, sub, float(N_EXPERTS)), axis=0, keepdims=True)
    rest = jnp.where(sub == i1, NEG_BIG, logits)
    m2 = jnp.max(rest, axis=0, keepdims=True)
    i2 = jnp.min(jnp.where(rest == m2, sub, float(N_EXPERTS)), axis=0, keepdims=True)
    e2 = jnp.exp(m2 - m1)
    gate_ref[0] = jnp.where(sub == i1, 1.0 / (1.0 + e2), 0.0) + jnp.where(sub == i2, e2 / (1.0 + e2), 0.0)
    chosen = (sub == i1) | (sub == i2)
    sel = jnp.where(chosen, 1.0, 0.0)
    rank_ref[0] = jnp.where(chosen, _mm(sel, tri_ref[...]), -1.0)
    cnt_ref[0] = jnp.broadcast_to(jnp.sum(sel, axis=1, keepdims=True), cnt_ref.shape[1:])


def _moe_kernel(cnt_ref, x_ref, gate_ref, rank_ref, wg_ref, wu_ref, wd_ref, g_ref, b_ref, out_ref,
                xb_scr, xg_scr, yg_scr, *, ts, tb, nsb):
    i, e, f = pl.program_id(0), pl.program_id(1), pl.program_id(2)
    last_f = f == pl.num_programs(2) - 1

    @pl.when((e == 0) & (f == 0))
    def _():
        xb_scr[...] = x_ref[...].astype(BF16)
        out_ref[...] = jnp.zeros_like(out_ref)

    slot = _iota((ts, tb), 0)

    def expert(xg):
        return _mm(_silu(_mm(xg, wg_ref[0])) * _mm(xg, wu_ref[0]), wd_ref[0])

    def pick_of(sb, j):
        rank_row = rank_ref[sb, pl.ds(e, 1), :]
        return jnp.where(rank_row == (slot + j * ts).astype(F32), 1.0, 0.0)

    def scatter(sb, pick, y):
        w_slot = jnp.sum(pick * gate_ref[sb, pl.ds(e, 1), :], axis=1, keepdims=True)
        out_ref[sb * tb:(sb + 1) * tb, :] += _mm(pick, y * w_slot, TN)

    @pl.when(f == 0)
    def _():
        for sb in range(nsb):
            xg_scr[sb] = _mm(pick_of(sb, 0), xb_scr[sb * tb:(sb + 1) * tb, :]).astype(BF16)
        yg_scr[...] = jnp.zeros_like(yg_scr)

    yg_scr[...] += expert(xg_scr[...].reshape(nsb * ts, D_MODEL)).reshape(nsb, ts, D_MODEL)

    @pl.when(last_f)
    def _():
        for sb in range(nsb):
            scatter(sb, pick_of(sb, 0), yg_scr[sb])

    for sb in range(nsb):
        n_tiles = (cnt_ref[(i * nsb + sb) * N_EXPERTS + e] + (ts - 1)) // ts

        def overflow(j, carry, sb=sb):
            pick = pick_of(sb, j)
            scatter(sb, pick, expert(_mm(pick, xb_scr[sb * tb:(sb + 1) * tb, :]).astype(BF16)))
            return carry

        lax.fori_loop(1, n_tiles, overflow, 0)

    @pl.when((e == pl.num_programs(1) - 1) & last_f)
    def _():
        out_ref[...] = _layer_norm(DN_ALPHA * x_ref[...] + out_ref[...], g_ref[...], b_ref[...])


def _moe(x, router, wg, wu, wd, ln_g, ln_b, tm):
    m = x.shape[0]
    nblk = m // tm
    dff = wg.shape[2]
    nf = 2
    tf = dff // nf
    ts = min(MOE_TILE, tm)
    nsb = 2 if nblk % 2 == 0 else 1
    assert m % tm == 0 and tf % 128 == 0
    tri = (jnp.arange(tm)[:, None] < jnp.arange(tm)[None, :]).astype(BF16)
    gate, rank, cnt = pl.pallas_call(
        _route_kernel,
        grid=(nblk,),
        in_specs=[pl.BlockSpec((tm, D_MODEL), lambda i: (i, 0)), _resident((N_EXPERTS, D_MODEL)),
                  _resident((tm, tm))],
        out_specs=[pl.BlockSpec((1, N_EXPERTS, tm), lambda i: (i, 0, 0)),
                   pl.BlockSpec((1, N_EXPERTS, tm), lambda i: (i, 0, 0)),
                   pl.BlockSpec((1, N_EXPERTS, 128), lambda i: (i, 0, 0))],
        out_shape=[jax.ShapeDtypeStruct((nblk, N_EXPERTS, tm), F32), jax.ShapeDtypeStruct((nblk, N_EXPERTS, tm), F32),
                   jax.ShapeDtypeStruct((nblk, N_EXPERTS, 128), F32)],
        compiler_params=pltpu.CompilerParams(dimension_semantics=("parallel",), vmem_limit_bytes=VMEM_LIMIT),
        name="moe_route",
    )(x, router.T, tri)
    counts = cnt[:, :, 0].astype(jnp.int32).reshape(-1)
    rows = nsb * tm
    x_in = pl.BlockSpec((rows, D_MODEL), lambda i, e, f, c: (i, 0), pipeline_mode=pl.Buffered(1))
    meta = pl.BlockSpec((nsb, N_EXPERTS, tm), lambda i, e, f, c: (i, 0, 0))
    vec = pl.BlockSpec((1, D_MODEL), lambda i, e, f, c: (0, 0), pipeline_mode=pl.Buffered(1))
    return pl.pallas_call(
        functools.partial(_moe_kernel, ts=ts, tb=tm, nsb=nsb),
        grid_spec=pltpu.PrefetchScalarGridSpec(
            num_scalar_prefetch=1,
            grid=(nblk // nsb, N_EXPERTS, nf),
            in_specs=[x_in, meta, meta,
                      pl.BlockSpec((1, D_MODEL, tf), lambda i, e, f, c: (e, 0, f)),
                      pl.BlockSpec((1, D_MODEL, tf), lambda i, e, f, c: (e, 0, f)),
                      pl.BlockSpec((1, tf, D_MODEL), lambda i, e, f, c: (e, f, 0)),
                      vec, vec],
            out_specs=pl.BlockSpec((rows, D_MODEL), lambda i, e, f, c: (i, 0), pipeline_mode=pl.Buffered(1)),
            scratch_shapes=[pltpu.VMEM((rows, D_MODEL), BF16),
                            pltpu.VMEM((nsb, ts, D_MODEL), BF16), pltpu.VMEM((nsb, ts, D_MODEL), F32)]),
        out_shape=jax.ShapeDtypeStruct((m, D_MODEL), F32),
        compiler_params=pltpu.CompilerParams(dimension_semantics=("parallel", "arbitrary", "arbitrary"),
                                             vmem_limit_bytes=MOE_VMEM_LIMIT),
        name="moe_ln",
    )(counts, x, gate, rank, wg, wu, wd, ln_g.reshape(1, -1), ln_b.reshape(1, -1))


def _token_mix(h, pos0, rw_prev, wkv0, pool_buf, ret0, kv_bufs, lw, l, w_in_b, w_branch_b, w_out_b, ln_g, ln_b):
    b, t, _ = h.shape
    m = b * t
    tm = 512 if m % 512 == 0 else m
    hf = h.reshape(m, D_MODEL)
    p_rw, p_pool, p_ret, p_dil = _project(hf, w_in_b, l, tm)
    p_rw, p_pool = p_rw.reshape(b, t, -1), p_pool.reshape(b, t, -1)
    p_ret, p_dil = p_ret.reshape(b, t, -1), p_dil.reshape(b, t, -1)
    o_a, wkv_new = _rwkv_branch(p_rw, rw_prev, wkv0, lw)
    o_b = _pool_branch(p_pool, pool_buf, pos0, lw)
    pool_new = jnp.concatenate([pool_buf, p_pool], axis=1)[:, -POOL_BUF:]
    o_c, ret_new = _ret_branch(p_ret, ret0, pos0, lw)
    def kv_rows(g, keep):
        lo = (3 * g + 1) * WIDTH
        return p_dil[:, t - keep:, lo:lo + 2 * WIDTH].reshape(b, keep, 2, N_HEADS, HEAD_DIM)

    if kv_bufs is None:
        dil_parts = _dil_prompt(p_dil)
        kv_new = [kv_rows(g, min(win, t)) for g, (win, _) in enumerate(DIL_PATTERNS)]
    else:
        dil_parts = [_dil_step(p_dil, kv_bufs, l).reshape(m, WIDTH)]
        kv_new = [kv_rows(g, t) for g in range(DIL_GROUPS)]
    branches = [x.reshape(m, WIDTH) for x in (o_a, o_b, o_c)] + dil_parts
    x1 = _merge(hf, branches, w_in_b, w_branch_b, w_out_b, l, ln_g, ln_b, tm)
    return x1, (wkv_new, p_rw[:, -1], pool_new, ret_new, kv_new[0], kv_new[1], kv_new[2])


def kernel(x_prompt, x_sample, state_wkv, state_shift, state_pool, state_ret, cache_kv_w128, cache_kv_w512, cache_kv_w2048, w_in, rw_mu, rw_w0, rw_w2, rw_a0, rw_a2, rw_g2, rw_kk, rw_ka, rw_rk, rw_gn_g, rw_gn_b, pool_w, pool_scale, ret_gn_g, ret_gn_b, w_branch, w_out, ln_g, ln_b, ffn_w_gate, ffn_w_up, ffn_w_down, moe_router, moe_w_gate, moe_w_up, moe_w_down):
    hp, hs = x_prompt, x_sample
    bp, tp, _ = hp.shape
    bs, ts, _ = hs.shape
    names = ('rw_mu', 'rw_w0', 'rw_w2', 'rw_a0', 'rw_a2', 'rw_g2', 'rw_kk', 'rw_ka', 'rw_rk', 'rw_gn_g',
             'rw_gn_b', 'pool_w', 'pool_scale', 'ret_gn_g', 'ret_gn_b')
    stacked = (rw_mu, rw_w0, rw_w2, rw_a0, rw_a2, rw_g2, rw_kk, rw_ka, rw_rk, rw_gn_g, rw_gn_b, pool_w,
               pool_scale, ret_gn_g, ret_gn_b)
    new_p = [[] for _ in range(7)]
    new_s = [[] for _ in range(7)]
    zeros = lambda *shape: jnp.zeros(shape, F32)
    w_in_b, wb, wo = w_in.astype(BF16), w_branch.astype(BF16), w_out.astype(BF16)
    caches = [jnp.transpose(c, (0, 1, 3, 4, 5, 2)) for c in (cache_kv_w128, cache_kv_w512, cache_kv_w2048)]
    for l in range(DEPTH):
        lw = {k: v[l] for k, v in zip(names, stacked)}
        xp, st_p = _token_mix(hp, 0, zeros(bp, RW_COLS), zeros(bp, N_HEADS, HEAD_DIM, HEAD_DIM),
                              zeros(bp, POOL_BUF, WIDTH), zeros(bp, N_HEADS, HEAD_DIM, HEAD_DIM), None,
                              lw, l, w_in_b, wb, wo, ln_g[l, 0], ln_b[l, 0])
        xs, st_s = _token_mix(hs, 8192, state_shift[l], state_wkv[l], state_pool[l], state_ret[l], caches,
                              lw, l, w_in_b, wb, wo, ln_g[l, 0], ln_b[l, 0])
        j = l // 2
        if l % 2 == 0:
            ws = [w[j].astype(BF16) for w in (ffn_w_gate, ffn_w_up, ffn_w_down)]
            xp = _ffn(xp, *ws, ln_g[l, 1], ln_b[l, 1], 512)
            xs = _ffn(xs, *ws, ln_g[l, 1], ln_b[l, 1], xs.shape[0])
        else:
            ws = [w[j].astype(BF16) for w in (moe_w_gate, moe_w_up, moe_w_down)]
            xp = _moe(xp, moe_router[j], *ws, ln_g[l, 1], ln_b[l, 1], 1024)
            xs = _moe(xs, moe_router[j], *ws, ln_g[l, 1], ln_b[l, 1], xs.shape[0])
        hp, hs = xp.reshape(bp, tp, D_MODEL), xs.reshape(bs, ts, D_MODEL)
        for i in range(7):
            new_p[i].append(st_p[i])
            new_s[i].append(st_s[i])
    outs_p = [jnp.stack(x) for x in new_p]
    outs_s = [jnp.stack(x) for x in new_s]
    return (hp, hs, *outs_p, *outs_s)
```

```python
import functools
import math

import jax
import jax.numpy as jnp
from jax import lax
from jax.experimental import pallas as pl
from jax.experimental.pallas import tpu as pltpu

F32 = jnp.float32
BF16 = jnp.bfloat16

D_MODEL = 1024
DEPTH = 2
HEAD_DIM = 64
N_HEADS = 4
WIDTH = N_HEADS * HEAD_DIM
RW_COLS = 1024
RW_GN_EPS = 64e-5
POOL_WINDOWS = (2, 4, 8, 16)
POOL_BUF = 15
RET_CHUNK = 128
ROPE_BASE = 10000.0
DIL_PATTERNS = ((128, 1), (512, 4), (2048, 16))
DIL_GROUPS = 3
DIL_BLOCK = 128
DIL_SPAN = 2048
DIL_COLS = 3 * DIL_GROUPS * WIDTH
COL_POOL = RW_COLS
COL_RET = COL_POOL + WIDTH
COL_DIL = COL_RET + 4 * WIDTH
COL_GATE = COL_DIL + DIL_COLS
N_EXPERTS = 8
DN_ALPHA = (2 * DEPTH) ** 0.25
LN_EPS = 1e-5
RW_CHUNK = 64
MOE_TILE = 288
RW_P = 1
RW_BATCH = 4
RW_CHUNKS_PER_STEP = 2
RW_LANE_GROUPS = 2
NEG_BIG = -1e30

NN = (((1,), (0,)), ((), ()))
NT = (((1,), (1,)), ((), ()))
TN = (((0,), (0,)), ((), ()))

VMEM_LIMIT = 56 * 1024 * 1024
MOE_VMEM_LIMIT = 61 * 1024 * 1024


def _split(x, n):
    if x.dtype == BF16:
        return [x]
    parts, rem = [], x
    for i in range(n):
        p = rem.astype(BF16)
        parts.append(p)
        if i + 1 < n:
            rem = rem - p.astype(F32)
    return parts


def _mm(a, b, dims=NN, pa=1, pb=1):
    a_parts, b_parts = _split(a, pa), _split(b, pb)
    depth = max(len(a_parts), len(b_parts))
    acc = None
    for i, ai in enumerate(a_parts):
        for j, bj in enumerate(b_parts):
            if i + j < depth:
                t = lax.dot_general(ai, bj, dims, preferred_element_type=F32)
                acc = t if acc is None else acc + t
    return acc


def _sigmoid(x):
    return 0.5 * jnp.tanh(0.5 * x) + 0.5


def _silu(x):
    return x * _sigmoid(x)


def _iota(shape, axis):
    return lax.broadcasted_iota(jnp.int32, shape, axis)


def _div(x, d):
    assert d & (d - 1) == 0
    return x >> (d.bit_length() - 1)


def _mod(x, d):
    assert d & (d - 1) == 0
    return x & (d - 1)


def _per_head(head, values):
    out = jnp.full(head.shape, values[-1], F32)
    for h in range(len(values) - 2, -1, -1):
        out = jnp.where(head == h, values[h], out)
    return out


def _head_mask(rows_per_head, n_rows):
    return _div(_iota((n_rows, WIDTH), 0), rows_per_head) == _div(_iota((n_rows, WIDTH), 1), HEAD_DIM)


def _stack_heads(x, mask):
    return jnp.where(mask, jnp.concatenate([x] * N_HEADS, axis=0), 0.0)


def _unstack_heads(x_st, mask, c):
    x_st = jnp.where(mask, x_st, 0.0)
    out = x_st[0:c]
    for h in range(1, N_HEADS):
        out = out + x_st[h * c:(h + 1) * c]
    return out


def _ones_bd():
    return _same_head().astype(BF16)


def _same_head():
    return _div(_iota((WIDTH, WIDTH), 0), HEAD_DIM) == _div(_iota((WIDTH, WIDTH), 1), HEAD_DIM)


def _head_norm(x, ones_bd, g, b, eps):
    mu = _mm(x, ones_bd, pa=2) * (1.0 / HEAD_DIM)
    d = x - mu
    var = _mm(d * d, ones_bd, pa=2) * (1.0 / HEAD_DIM)
    return d * lax.rsqrt(var + eps) * g + b


def _layer_norm(x, g, b):
    mu = jnp.mean(x, axis=-1, keepdims=True)
    d = x - mu
    var = jnp.mean(d * d, axis=-1, keepdims=True)
    return d * lax.rsqrt(var + LN_EPS) * g + b


def _resident(shape):
    nd = len(shape)
    return pl.BlockSpec(shape, lambda *_: (0,) * nd, pipeline_mode=pl.Buffered(1))


def _proj_kernel(x_ref, w_ref, rw_ref, pool_ref, ret_ref, dil_ref):
    xb = x_ref[...].astype(BF16)
    for ref, lo, hi in ((rw_ref, 0, COL_POOL), (pool_ref, COL_POOL, COL_RET),
                        (ret_ref, COL_RET, COL_DIL), (dil_ref, COL_DIL, COL_GATE)):
        for s in range(lo, hi, 512):
            e = min(s + 512, hi)
            ref[:, s - lo:e - lo] = _mm(xb, w_ref[:, s:e])


def _layer_resident(w, l):
    nd = w.ndim - 1
    return pl.BlockSpec((None,) + w.shape[1:], lambda *_: (l,) + (0,) * nd, pipeline_mode=pl.Buffered(1))


def _project(x, w_in_b, l, tm):
    m = x.shape[0]
    widths = (COL_POOL, WIDTH, 4 * WIDTH, DIL_COLS)
    row = lambda w: pl.BlockSpec((tm, w), lambda i: (i, 0))
    return pl.pallas_call(
        _proj_kernel,
        grid=(m // tm,),
        in_specs=[row(D_MODEL), _layer_resident(w_in_b, l)],
        out_specs=[row(w) for w in widths],
        out_shape=[jax.ShapeDtypeStruct((m, w), F32) for w in widths],
        compiler_params=pltpu.CompilerParams(dimension_semantics=("parallel",), vmem_limit_bytes=VMEM_LIMIT),
        name="proj",
    )(x, w_in_b)


BNT = (((2,), (2,)), ((0,), (0,)))
BNN = (((2,), (1,)), ((0,), (0,)))
BTN = (((1,), (1,)), ((0,), (0,)))


def _rwkv_kernel(p_ref, prev_ref, s0_ref, mu_ref, w0_ref, w2_ref, a0_ref, a2_ref, g2_ref, kk_ref, ka_ref,
                 rk_ref, gng_ref, gnb_ref, o_ref, st_ref, s_scr, prev_scr, buf_scr, *, c, t_in, nb, nsub):
    ci = pl.program_id(1)
    rows = nsub * c

    halves = RW_LANE_GROUPS
    hw, hpg = WIDTH // halves, N_HEADS // halves
    n = hpg * c
    same_grp = _div(_iota((hw, hw), 0), HEAD_DIM) == _div(_iota((hw, hw), 1), HEAD_DIM)

    @pl.when(ci == 0)
    def _():
        for g in range(halves):
            part = s0_ref[:, g * hw:(g + 1) * hw, :]
            s_scr[g * nb:(g + 1) * nb] = jnp.where(same_grp, jnp.concatenate([part] * hpg, axis=-1), 0.0)
        for b in range(nb):
            prev_scr[b, 0:1, :] = prev_ref[b]

    first = _iota((rows, RW_COLS), 0) == 0
    ps, shs = [], []
    for b in range(nb):
        if t_in == c:
            pb_ = p_ref[b]
        else:
            buf_scr[b] = jnp.zeros((c, RW_COLS), F32)
            buf_scr[b, 0:t_in, :] = p_ref[b]
            pb_ = buf_scr[b]
        shs.append(jnp.where(first, prev_scr[b, 0:1, :], pltpu.roll(pb_, 1, 0)))
        prev_scr[b, 0:1, :] = pb_[rows - 1:rows, :]
        ps.append(pb_)
    p, shifted = jnp.concatenate(ps, axis=0), jnp.concatenate(shs, axis=0)
    m = nb * rows
    u = p + (shifted - p) * mu_ref[...]
    r, k, v, ul = u[:, 0:WIDTH], u[:, WIDTH:2 * WIDTH], u[:, 2 * WIDTH:3 * WIDTH], u[:, 3 * WIDTH:]

    w_pre = w0_ref[...] + _mm(jnp.tanh(ul), w2_ref[...])
    logw = -math.exp(-0.5) * _sigmoid(w_pre)
    a = _sigmoid(a0_ref[...] + _mm(ul, a2_ref[...]))
    g = _mm(_sigmoid(ul), g2_ref[...])

    ones_bd = _ones_bd()
    kk = k * kk_ref[...]
    kk = kk * lax.rsqrt(jnp.maximum(_mm(kk * kk, ones_bd, pa=2), 1e-24))
    k_mod = k * (1.0 + (a - 1.0) * ka_ref[...])
    a_vec, b_vec = -kk, kk * a
    if t_in < c:
        live = _mod(_iota((m, WIDTH), 0), c) < t_in
        logw = jnp.where(live, logw, 0.0)
        a_vec, b_vec = jnp.where(live, a_vec, 0.0), jnp.where(live, b_vec, 0.0)
        k_mod, v = jnp.where(live, k_mod, 0.0), jnp.where(live, v, 0.0)

    qi, qj = _iota((m, m), 0), _iota((m, m), 1)
    tri = ((_div(qi, c) == _div(qj, c)) & (qi >= qj)).astype(BF16)
    cum2 = _mm(tri, logw, pb=3)
    def chunk_rows(x, s):
        return x if nsub == 1 else x.reshape(nb, nsub, c, x.shape[-1])[:, s].reshape(nb * c, x.shape[-1])

    def seq(x):
        parts = []
        for s in range(nsub):
            x3 = chunk_rows(x, s).reshape(nb, c, WIDTH)
            parts += [x3[:, :, g * hw:(g + 1) * hw] for g in range(halves)]
        return jnp.concatenate(parts, axis=0)

    cum, lw3 = seq(cum2), seq(logw)
    cum_end = cum[:, c - 1:c, :]
    e_neg, e_end = jnp.exp(-cum), jnp.exp(cum_end - cum)
    a_t = seq(a_vec) * jnp.exp(cum - lw3)
    r_t = seq(r) * jnp.exp(cum)
    b_t, k_t = seq(b_vec) * e_neg, seq(k_mod) * e_neg
    b_e, k_e = seq(b_vec) * e_end, seq(k_mod) * e_end
    v3 = seq(v)

    hm = _div(_iota((n, hw), 0), c) == _div(_iota((n, hw), 1), HEAD_DIM)
    stack = lambda x: jnp.where(hm, jnp.concatenate([x] * hpg, axis=1), 0.0)
    a_st, r_st, b_st, v_st = stack(a_t), stack(r_t), stack(b_t), stack(v3)
    ri, rj = _iota((n, n), 0), _iota((n, n), 1)
    same = _div(ri, c) == _div(rj, c)
    strict_bd = same & (_mod(ri, c) > _mod(rj, c))
    incl_bd = same & (_mod(ri, c) >= _mod(rj, c))
    ti, tj = _mod(_iota((n, c), 0), c), _iota((n, c), 1)
    strict_st, incl_st = ti > tj, ti >= tj

    mm = functools.partial(_mm, pa=RW_P, pb=RW_P)
    a_ab = jnp.where(strict_bd, mm(a_st, b_st, BNT), 0.0)
    inv = jnp.where(ri == rj, 1.0, 0.0) + a_ab
    pw = a_ab
    for _ in range(int(math.log2(c)) - 1):
        pw = mm(pw, pw, BNN)
        inv = inv + mm(inv, pw, BNN)
    a_ak = jnp.where(strict_st, mm(a_st, k_t, BNT), 0.0)
    z_st = jnp.where(hm, mm(a_ak, v3, BNN), 0.0)
    wu = mm(inv, jnp.concatenate([a_st, z_st], axis=2), BNN)
    w_st, u0_st = wu[:, :, 0:hw], wu[:, :, hw:]

    rhs = jnp.concatenate([stack(b_e), stack(k_e)], axis=1)
    a_rb = jnp.where(incl_bd, mm(r_st, b_st, BNT), 0.0)
    a_rk = jnp.where(incl_st, mm(r_st, k_t, BNT), 0.0)
    o_in = mm(a_rk, v3, BNN)
    decay_end = jnp.exp(cum_end)

    per = halves * nb
    s0 = s_scr[...]
    for s in range(nsub):
        sl = slice(s * per, (s + 1) * per)
        u_st = mm(w_st[sl], s0, BNT) + u0_st[sl]
        o_st = jnp.where(hm, mm(r_st[sl], s0, BNT) + mm(a_rb[sl], u_st, BNN) + o_in[sl], 0.0)
        s0 = s0 * decay_end[sl] + mm(jnp.concatenate([u_st, v_st[sl]], axis=1), rhs[sl], BTN)
        o3 = o_st[:, 0:c]
        for h in range(1, hpg):
            o3 = o3 + o_st[:, h * c:(h + 1) * c]
        o = jnp.concatenate([o3[g * nb:(g + 1) * nb] for g in range(halves)], axis=2).reshape(nb * c, WIDTH)
        o = _head_norm(o, ones_bd, gng_ref[...], gnb_ref[...], RW_GN_EPS)
        v_s = chunk_rows(v, s)
        o = o + _mm(chunk_rows(r, s) * chunk_rows(k_mod, s) * rk_ref[...], ones_bd, pa=2) * v_s
        o = (o * chunk_rows(g, s)).reshape(nb, c, WIDTH)
        if t_in == c:
            o_ref[:, s * c:(s + 1) * c, :] = o
        else:
            o_ref[...] = o[:, 0:t_in]
    s_scr[...] = s0

    @pl.when(ci == pl.num_programs(1) - 1)
    def _():
        for g in range(halves):
            blk = s_scr[g * nb:(g + 1) * nb]
            part = blk[..., 0:HEAD_DIM]
            for h in range(1, hpg):
                part = part + blk[..., h * HEAD_DIM:(h + 1) * HEAD_DIM]
            st_ref[:, g * hw:(g + 1) * hw, :] = part


def _embed_heads(s):
    return jnp.where(_same_head(), jnp.concatenate([s] * N_HEADS, axis=-1), 0.0)


def _extract_heads(s_bd):
    out = s_bd[..., 0:HEAD_DIM]
    for h in range(1, N_HEADS):
        out = out + s_bd[..., h * HEAD_DIM:(h + 1) * HEAD_DIM]
    return out


def _rwkv_branch(p, p_prev, wkv0, lw):
    b, t, _ = p.shape
    c = RW_CHUNK if t % RW_CHUNK == 0 else RW_CHUNK // 2
    assert t % c == 0 or t <= c
    t_in = c if t % c == 0 else t
    nc = t // c if t % c == 0 else 1
    vec = lambda x: x.reshape(1, -1)
    pad_rows = lambda w, lo: jnp.zeros((WIDTH, WIDTH), F32).at[lo:lo + w.shape[0]].set(w).astype(BF16)
    params = [vec(lw['rw_mu']), vec(lw['rw_w0']), pad_rows(lw['rw_w2'], 0), vec(lw['rw_a0']),
              pad_rows(lw['rw_a2'], 64), pad_rows(lw['rw_g2'], 128), vec(lw['rw_kk']), vec(lw['rw_ka']),
              vec(lw['rw_rk']), vec(lw['rw_gn_g']), vec(lw['rw_gn_b'])]
    nb = RW_BATCH
    nsub = RW_CHUNKS_PER_STEP if nc % RW_CHUNKS_PER_STEP == 0 else 1
    rows = nsub * t_in
    assert b % nb == 0
    o, st = pl.pallas_call(
        functools.partial(_rwkv_kernel, c=c, t_in=t_in, nb=nb, nsub=nsub),
        grid=(b // nb, nc // nsub),
        in_specs=[pl.BlockSpec((nb, rows, RW_COLS), lambda i, j: (i, j, 0)),
                  pl.BlockSpec((nb, 1, RW_COLS), lambda i, j: (i, 0, 0)),
                  pl.BlockSpec((nb, WIDTH, HEAD_DIM), lambda i, j: (i, 0, 0))]
                 + [_resident(x.shape) for x in params],
        out_specs=[pl.BlockSpec((nb, rows, WIDTH), lambda i, j: (i, j, 0)),
                   pl.BlockSpec((nb, WIDTH, HEAD_DIM), lambda i, j: (i, 0, 0))],
        out_shape=[jax.ShapeDtypeStruct((b, t, WIDTH), F32), jax.ShapeDtypeStruct((b, WIDTH, HEAD_DIM), F32)],
        scratch_shapes=[pltpu.VMEM((RW_LANE_GROUPS * nb, WIDTH // RW_LANE_GROUPS, WIDTH // RW_LANE_GROUPS), F32),
                        pltpu.VMEM((nb, 8, RW_COLS), F32),
                        pltpu.VMEM((nb, c, RW_COLS), F32)],
        compiler_params=pltpu.CompilerParams(dimension_semantics=("parallel", "arbitrary"),
                                             vmem_limit_bytes=VMEM_LIMIT),
        name="rwkv7",
    )(p, p_prev.reshape(b, 1, RW_COLS), wkv0.reshape(b, WIDTH, HEAD_DIM), *params)
    return o, st.reshape(b, N_HEADS, HEAD_DIM, HEAD_DIM)


def _pool_rows(ext_scr, c, pos_first, w, scale):
    x = ext_scr[16:16 + c, :]
    sums, acc, off = [], x, 1
    for win in POOL_WINDOWS:
        while off < win:
            acc = acc + ext_scr[16 - off:16 - off + c, :]
            off += 1
        sums.append(acc)
    pos = pos_first + _iota((c, WIDTH), 0)
    grp = _div(_iota((c, WIDTH), 1), HEAD_DIM)
    mean = jnp.zeros((c, WIDTH), F32)
    for gi, win in enumerate(POOL_WINDOWS):
        cnt = jnp.minimum(win, pos + 1).astype(F32)
        mean = jnp.where(grp == gi, sums[gi] / cnt, mean)
    ext_scr[0:16, :] = ext_scr[c:c + 16, :]
    return _mm(mean - x, w) * scale


def _pool_kernel(u_ref, buf_ref, w_ref, scale_ref, o_ref, ext_scr, *, c, t_in, pos0):
    ci = pl.program_id(1)

    @pl.when(ci == 0)
    def _():
        ext_scr[0:16, :] = buf_ref[0]

    if t_in < c:
        ext_scr[16:16 + c, :] = jnp.zeros((c, WIDTH), F32)
    ext_scr[16:16 + t_in, :] = u_ref[0]
    o_ref[0] = _pool_rows(ext_scr, c, pos0 + ci * c, w_ref[...], scale_ref[...])[0:t_in]


def _pool_weights(lw):
    w_bd = jnp.einsum('gcd,gh->gchd', lw['pool_w'], jnp.eye(N_HEADS, dtype=F32)).reshape(WIDTH, WIDTH)
    return w_bd.astype(BF16), lw['pool_scale'].reshape(1, WIDTH)


def _pool_branch(u, buf, pos0, lw):
    b, t, _ = u.shape
    c = 512 if t % 512 == 0 else 8
    t_in = c if t % c == 0 else t
    nc = t // c if t % c == 0 else 1
    w_bd, scale = _pool_weights(lw)
    buf16 = jnp.pad(buf, ((0, 0), (1, 0), (0, 0)))
    return pl.pallas_call(
        functools.partial(_pool_kernel, c=c, t_in=t_in, pos0=pos0),
        grid=(b, nc),
        in_specs=[pl.BlockSpec((1, t_in, WIDTH), lambda i, j: (i, j, 0)),
                  pl.BlockSpec((1, 16, WIDTH), lambda i, j: (i, 0, 0)),
                  _resident((WIDTH, WIDTH)), _resident((1, WIDTH))],
        out_specs=pl.BlockSpec((1, t_in, WIDTH), lambda i, j: (i, j, 0)),
        out_shape=jax.ShapeDtypeStruct((b, t, WIDTH), F32),
        scratch_shapes=[pltpu.VMEM((16 + c, WIDTH), F32)],
        compiler_params=pltpu.CompilerParams(dimension_semantics=("parallel", "arbitrary")),
        name="pool",
    )(u, buf16, w_bd, scale)


def _rot_half(x):
    first = _mod(_iota(x.shape, 1), HEAD_DIM) < (HEAD_DIM // 2)
    return jnp.where(first, pltpu.roll(x, WIDTH - HEAD_DIM // 2, 1), pltpu.roll(x, HEAD_DIM // 2, 1))


RET_LOG_DECAY = tuple(math.log(1.0 - 2.0 ** (-5.0 - h)) for h in range(N_HEADS))
ALIBI_SLOPES = tuple(2.0 ** (-8.0 * (i + 1) / (DIL_GROUPS * N_HEADS)) for i in range(DIL_GROUPS * N_HEADS))


def _ret_kernel(p_ref, cos_ref, sin_ref, s0_ref, gng_ref, gnb_ref, o_ref, st_ref, s_scr, buf_scr,
                dmask_scr, qd_scr, kd_scr, *, c, t_in, nb):
    ci = pl.program_id(1)
    n = N_HEADS * c
    lg = _per_head(_div(_iota((1, WIDTH), 1), HEAD_DIM), RET_LOG_DECAY)

    @pl.when(ci == 0)
    def _():
        s_scr[...] = _embed_heads(s0_ref[...])
        idx = _iota((c, WIDTH), 0).astype(F32)
        qd_scr[...] = jnp.exp(lg * (idx + 1.0))
        kd_scr[...] = jnp.exp(lg * (t_in - 1.0 - idx))
        rel = _mod(_iota((n, c), 0), c) - _iota((n, c), 1)
        lg_rows = _per_head(_div(_iota((n, c), 0), c), RET_LOG_DECAY)
        dmask_scr[...] = jnp.where(rel >= 0, jnp.exp(lg_rows * jnp.maximum(rel, 0).astype(F32)), 0.0)

    hm = _head_mask(c, n)
    cos, sin = cos_ref[...], sin_ref[...]
    chunk_decay = jnp.exp(lg * float(t_in))
    same_head, ones_bd = _same_head(), _ones_bd()
    for b in range(nb):
        if t_in == c:
            p = p_ref[b]
        else:
            buf_scr[b] = jnp.zeros((c, 4 * WIDTH), F32)
            buf_scr[b, 0:t_in, :] = p_ref[b]
            p = buf_scr[b]
        q, k, v, g = (p[:, i * WIDTH:(i + 1) * WIDTH] for i in range(4))
        q = q * cos + _rot_half(q) * sin
        k = (k * cos + _rot_half(k) * sin) * HEAD_DIM ** -0.5
        if t_in < c:
            live = _iota((c, WIDTH), 0) < t_in
            k, v = jnp.where(live, k, 0.0), jnp.where(live, v, 0.0)
        inner = _mm(_stack_heads(q, hm), k, NT) * dmask_scr[...]
        s0 = s_scr[b]
        o = _unstack_heads(_mm(inner, v), hm, c) + _mm(q * qd_scr[...], s0)
        s_scr[b] = s0 * chunk_decay + jnp.where(same_head, _mm(k * kd_scr[...], v, TN), 0.0)
        o = _silu(g) * _head_norm(o, ones_bd, gng_ref[...], gnb_ref[...], LN_EPS)
        o_ref[b] = o[0:t_in]

    @pl.when(ci == pl.num_programs(1) - 1)
    def _():
        st_ref[...] = _extract_heads(s_scr[...])


def _rope_tables(pos0, t, rows):
    half = HEAD_DIM // 2
    inv = ROPE_BASE ** (-jnp.arange(half, dtype=F32) / half)
    ang = (pos0 + jnp.arange(t, dtype=jnp.int32)).astype(F32)[:, None] * inv[None, :]
    cos = jnp.tile(jnp.cos(ang), (1, 2 * N_HEADS))
    sin = jnp.tile(jnp.concatenate([-jnp.sin(ang), jnp.sin(ang)], axis=1), (1, N_HEADS))
    pad = ((0, rows - t), (0, 0))
    return jnp.pad(cos, pad), jnp.pad(sin, pad)


def _ret_branch(p, s0, pos0, lw):
    b, t, _ = p.shape
    c = RET_CHUNK
    t_in = c if t % c == 0 else t
    nc = t // c if t % c == 0 else 1
    cos, sin = _rope_tables(pos0, t, nc * c)
    nb = RW_BATCH
    assert b % nb == 0
    o, st = pl.pallas_call(
        functools.partial(_ret_kernel, c=c, t_in=t_in, nb=nb),
        grid=(b // nb, nc),
        in_specs=[pl.BlockSpec((nb, t_in, 4 * WIDTH), lambda i, j: (i, j, 0)),
                  pl.BlockSpec((c, WIDTH), lambda i, j: (j, 0)),
                  pl.BlockSpec((c, WIDTH), lambda i, j: (j, 0)),
                  pl.BlockSpec((nb, WIDTH, HEAD_DIM), lambda i, j: (i, 0, 0)),
                  _resident((1, WIDTH)), _resident((1, WIDTH))],
        out_specs=[pl.BlockSpec((nb, t_in, WIDTH), lambda i, j: (i, j, 0)),
                   pl.BlockSpec((nb, WIDTH, HEAD_DIM), lambda i, j: (i, 0, 0))],
        out_shape=[jax.ShapeDtypeStruct((b, t, WIDTH), F32), jax.ShapeDtypeStruct((b, WIDTH, HEAD_DIM), F32)],
        scratch_shapes=[pltpu.VMEM((nb, WIDTH, WIDTH), F32), pltpu.VMEM((nb, c, 4 * WIDTH), F32),
                        pltpu.VMEM((N_HEADS * c, c), F32), pltpu.VMEM((c, WIDTH), F32), pltpu.VMEM((c, WIDTH), F32)],
        compiler_params=pltpu.CompilerParams(dimension_semantics=("parallel", "arbitrary")),
        name="retention",
    )(p, cos, sin, s0.reshape(b, WIDTH, HEAD_DIM), lw['ret_gn_g'].reshape(1, WIDTH), lw['ret_gn_b'].reshape(1, WIDTH))
    return o, st.reshape(b, N_HEADS, HEAD_DIM, HEAD_DIM)


def _dil_prompt_kernel(*refs, group, dil, span):
    q_refs, kc_refs, kp_refs, vc_refs, vp_refs = (refs[2 * i:2 * i + 2] for i in range(5))
    o_ref, lse_ref, k_scr, v_scr, o_scr, lse_scr = refs[10:]
    blk = DIL_BLOCK
    tail = blk * dil
    si = pl.program_id(1)
    for half in range(2):
        k_scr[half, 0:tail, :] = kp_refs[half][0]
        k_scr[half, tail:tail + span, :] = kc_refs[half][0]
        v_scr[half, 0:tail, :] = vp_refs[half][0]
        v_scr[half, tail:tail + span, :] = vc_refs[half][0]
    ki = _iota((blk, 2 * blk), 1)
    steps = blk + _iota((blk, 2 * blk), 0) - ki
    band = (steps >= 0) & (steps <= blk)
    masks = [jnp.where(band, (-ALIBI_SLOPES[group * N_HEADS + h] * dil) * steps.astype(F32), NEG_BIG)
             for h in range(N_HEADS)]
    sub_head = _div(_iota((blk, 128), 1), HEAD_DIM)
    for cc in range(span // tail):
        has_prev = (si > 0) | (ki >= blk)
        cc_masks = [jnp.where(has_prev, x, NEG_BIG) for x in masks] if cc == 0 else masks
        for r in range(dil):
            rows_q = pl.ds(cc * tail + r, blk, stride=dil) if dil > 1 else pl.ds(cc * tail, blk)
            rows_kv = pl.ds(cc * tail + r, 2 * blk, stride=dil) if dil > 1 else pl.ds(cc * tail, 2 * blk)
            for half in range(2):
                q2 = q_refs[half][0, rows_q, :] * HEAD_DIM ** -0.5
                k2 = k_scr[half, rows_kv, :].astype(BF16)
                v2 = v_scr[half, rows_kv, :].astype(BF16)
                o2 = lse2 = None
                for sub in range(2):
                    mine = sub_head == sub
                    s = _mm(jnp.where(mine, q2, 0.0), k2, NT) + cc_masks[2 * half + sub]
                    m = jnp.max(s, axis=1, keepdims=True)
                    e = jnp.exp(s - m)
                    l = jnp.sum(e, axis=1, keepdims=True)
                    o_h = _mm(e, v2) / l
                    lse_h = jnp.broadcast_to(m + jnp.log(l), (blk, 128))
                    o2 = o_h if o2 is None else jnp.where(mine, o_h, o2)
                    lse2 = lse_h if lse2 is None else jnp.where(mine, lse_h, lse2)
                o_scr[half, rows_q, :] = o2
                lse_scr[half, rows_q, :] = lse2
    o_ref[0] = jnp.concatenate([o_scr[0], o_scr[1]], axis=1)
    lse_ref[0] = jnp.concatenate([lse_scr[0], lse_scr[1]], axis=1)


def _dil_prompt_group(pd, group):
    b, s, _ = pd.shape
    win, dil = DIL_PATTERNS[group]
    span = DIL_SPAN
    tail = DIL_BLOCK * dil
    assert win // dil == DIL_BLOCK and span % tail == 0 and s % span == 0
    base = group * 3

    def cur(col):
        return [pl.BlockSpec((1, span, 128), lambda i, j, h=h: (i, j, 2 * (base + col) + h)) for h in range(2)]

    def prev(col):
        return [pl.BlockSpec((1, tail, 128),
                             lambda i, j, h=h: (i, jnp.maximum(j * (span // tail) - 1, 0), 2 * (base + col) + h))
                for h in range(2)]

    out_spec = pl.BlockSpec((1, span, WIDTH), lambda i, j: (i, j, 0))
    return pl.pallas_call(
        functools.partial(_dil_prompt_kernel, group=group, dil=dil, span=span),
        grid=(b, s // span),
        in_specs=cur(0) + cur(1) + prev(1) + cur(2) + prev(2),
        out_specs=[out_spec, out_spec],
        out_shape=[jax.ShapeDtypeStruct((b, s, WIDTH), F32)] * 2,
        scratch_shapes=[pltpu.VMEM((2, tail + span, 128), F32)] * 2 + [pltpu.VMEM((2, span, 128), F32)] * 2,
        compiler_params=pltpu.CompilerParams(dimension_semantics=("parallel", "arbitrary"),
                                             vmem_limit_bytes=VMEM_LIMIT),
        name=f"dil_prompt_g{group}",
    )(*([pd] * 10))


def _dil_combine(os_, ls):
    m = jnp.maximum(jnp.maximum(ls[0], ls[1]), ls[2])
    es = [jnp.exp(x - m) for x in ls]
    return (es[0] * os_[0] + es[1] * os_[1] + es[2] * os_[2]) / (es[0] + es[1] + es[2])


def _dil_prompt(pd):
    b, s, _ = pd.shape
    outs = [_dil_prompt_group(pd, g) for g in range(DIL_GROUPS)]
    return [x[0].reshape(b * s, WIDTH) for x in outs] + [x[1].reshape(b * s, WIDTH) for x in outs]


def _dil_step_kernel(pd_ref, c0_ref, c1_ref, c2_ref, o_ref, buf_scr, *, t, tp):
    buf_scr[...] = jnp.zeros_like(buf_scr)
    buf_scr[0:t, :] = pd_ref[0]
    pd = buf_scr[...]
    qt = _iota((tp, 1), 0)
    d_new = qt - _iota((tp, tp), 1)
    outs = [[None] * DIL_GROUPS for _ in range(N_HEADS)]
    lses = [[None] * DIL_GROUPS for _ in range(N_HEADS)]
    for g, cache_ref in enumerate((c0_ref, c1_ref, c2_ref)):
        win, dil = DIL_PATTERNS[g]
        length = cache_ref.shape[-1]
        d_old = length + qt - _iota((tp, length), 1)
        ok_old = (_mod(d_old, dil) == 0) & (d_old <= win)
        ok_new = (d_new >= 0) & (_mod(d_new, dil) == 0)
        for h in range(N_HEADS):
            slope = ALIBI_SLOPES[g * N_HEADS + h]
            lo = g * 3 * WIDTH + h * HEAD_DIM
            q = pd[:, lo:lo + HEAD_DIM] * HEAD_DIM ** -0.5
            k_new = pd[:, lo + WIDTH:lo + WIDTH + HEAD_DIM]
            v_new = pd[:, lo + 2 * WIDTH:lo + 2 * WIDTH + HEAD_DIM]
            k_t, v_t = cache_ref[0, 0, h], cache_ref[0, 1, h]
            s_old = jnp.where(ok_old, _mm(q, k_t) - slope * d_old.astype(F32), NEG_BIG)
            s_new = jnp.where(ok_new, _mm(q, k_new, NT) - slope * d_new.astype(F32), NEG_BIG)
            m = jnp.maximum(jnp.max(s_old, axis=1, keepdims=True), jnp.max(s_new, axis=1, keepdims=True))
            e_old, e_new = jnp.exp(s_old - m), jnp.exp(s_new - m)
            l = jnp.sum(e_old, axis=1, keepdims=True) + jnp.sum(e_new, axis=1, keepdims=True)
            outs[h][g] = (_mm(e_old, v_t, NT) + _mm(e_new, v_new)) / l
            lses[h][g] = m + jnp.log(l)
    o = jnp.concatenate([_dil_combine(outs[h], lses[h]) for h in range(N_HEADS)], axis=1)
    o_ref[0] = o[0:t]


def _dil_step(pd, caches, l):
    b, t, _ = pd.shape
    tp = 8
    flat = caches
    return pl.pallas_call(
        functools.partial(_dil_step_kernel, t=t, tp=tp),
        grid=(b,),
        in_specs=[pl.BlockSpec((1, t, DIL_COLS), lambda i: (i, 0, 0))]
                 + [pl.BlockSpec((None, 1) + c.shape[2:], lambda i: (l, i, 0, 0, 0, 0)) for c in flat],
        out_specs=pl.BlockSpec((1, t, WIDTH), lambda i: (i, 0, 0)),
        out_shape=jax.ShapeDtypeStruct((b, t, WIDTH), F32),
        scratch_shapes=[pltpu.VMEM((tp, DIL_COLS), F32)],
        compiler_params=pltpu.CompilerParams(dimension_semantics=("parallel",), vmem_limit_bytes=VMEM_LIMIT),
        name="dil_step",
    )(pd, *flat)


def _merge_kernel(h_ref, *refs):
    wg_ref, wb_ref, wo_ref, g_ref, b_ref, out_ref = refs[-6:]
    branch_refs = refs[:-6]
    branches = [r[...] for r in branch_refs[:3]]
    if len(branch_refs) == 4:
        branches.append(branch_refs[3][...])
    else:
        branches.append(_dil_combine([r[...] for r in branch_refs[3:6]], [r[...] for r in branch_refs[6:9]]))
    h = h_ref[...]
    hb = h.astype(BF16)
    z = None
    for n, o in enumerate(branches):
        gate = _sigmoid(_mm(hb, wg_ref[:, COL_GATE + n * D_MODEL:COL_GATE + (n + 1) * D_MODEL]))
        term = gate * _mm(o, wb_ref[n])
        z = term if z is None else z + term
    y = _mm(z, wo_ref[...])
    out_ref[...] = _layer_norm(DN_ALPHA * h + y, g_ref[...], b_ref[...])


def _merge(h, branches, w_in_b, w_branch_b, w_out_b, l, ln_g, ln_b, tm):
    m = h.shape[0]
    row = lambda w: pl.BlockSpec((tm, w), lambda i: (i, 0))
    return pl.pallas_call(
        _merge_kernel,
        grid=(m // tm,),
        in_specs=[row(D_MODEL)] + [row(WIDTH)] * len(branches)
                 + [_layer_resident(w_in_b, l), _layer_resident(w_branch_b, l), _layer_resident(w_out_b, l),
                    _resident((1, D_MODEL)), _resident((1, D_MODEL))],
        out_specs=row(D_MODEL),
        out_shape=jax.ShapeDtypeStruct((m, D_MODEL), F32),
        compiler_params=pltpu.CompilerParams(dimension_semantics=("parallel",), vmem_limit_bytes=VMEM_LIMIT),
        name="merge_ln",
    )(h, *branches, w_in_b, w_branch_b, w_out_b, ln_g.reshape(1, -1), ln_b.reshape(1, -1))


def _ffn_kernel(x_ref, wg_ref, wu_ref, wd_ref, g_ref, b_ref, out_ref):
    x = x_ref[...]
    xb = x.astype(BF16)
    act = _silu(_mm(xb, wg_ref[...])) * _mm(xb, wu_ref[...])
    out_ref[...] = _layer_norm(DN_ALPHA * x + _mm(act, wd_ref[...]), g_ref[...], b_ref[...])


def _ffn(x, wg, wu, wd, ln_g, ln_b, tm):
    m = x.shape[0]
    row = pl.BlockSpec((tm, D_MODEL), lambda i: (i, 0))
    return pl.pallas_call(
        _ffn_kernel,
        grid=(m // tm,),
        in_specs=[row, _resident(wg.shape), _resident(wu.shape), _resident(wd.shape),
                  _resident((1, D_MODEL)), _resident((1, D_MODEL))],
        out_specs=row,
        out_shape=jax.ShapeDtypeStruct((m, D_MODEL), F32),
        compiler_params=pltpu.CompilerParams(dimension_semantics=("parallel",), vmem_limit_bytes=VMEM_LIMIT),
        name="ffn_ln",
    )(x, wg, wu, wd, ln_g.reshape(1, -1), ln_b.reshape(1, -1))


def _route_kernel(x_ref, rt_ref, tri_ref, gate_ref, rank_ref, cnt_ref):
    logits = _mm(rt_ref[...], x_ref[...], NT, 2, 2)
    sub = _iota(logits.shape, 0).astype(F32)
    m1 = jnp.max(logits, axis=0, keepdims=True)
    i1 = jnp.min(jnp.where(logits == m1, sub, float(N_EXPERTS)), axis=0, keepdims=True)
    rest = jnp.where(sub == i1, NEG_BIG, logits)
    m2 = jnp.max(rest, axis=0, keepdims=True)
    i2 = jnp.min(jnp.where(rest == m2, sub, float(N_EXPERTS)), axis=0, keepdims=True)
    e2 = jnp.exp(m2 - m1)
    gate_ref[0] = jnp.where(sub == i1, 1.0 / (1.0 + e2), 0.0) + jnp.where(sub == i2, e2 / (1.0 + e2), 0.0)
    chosen = (sub == i1) | (sub == i2)
    sel = jnp.where(chosen, 1.0, 0.0)
    rank_ref[0] = jnp.where(chosen, _mm(sel, tri_ref[...]), -1.0)
    cnt_ref[0] = jnp.broadcast_to(jnp.sum(sel, axis=1, keepdims=True), cnt_ref.shape[1:])


def _moe_kernel(cnt_ref, x_ref, gate_ref, rank_ref, wg_ref, wu_ref, wd_ref, g_ref, b_ref, out_ref,
                xb_scr, xg_scr, yg_scr, *, ts, tb, nsb):
    i, e, f = pl.program_id(0), pl.program_id(1), pl.program_id(2)
    last_f = f == pl.num_programs(2) - 1

    @pl.when((e == 0) & (f == 0))
    def _():
        xb_scr[...] = x_ref[...].astype(BF16)
        out_ref[...] = jnp.zeros_like(out_ref)

    slot = _iota((ts, tb), 0)

    def expert(xg):
        return _mm(_silu(_mm(xg, wg_ref[0])) * _mm(xg, wu_ref[0]), wd_ref[0])

    def pick_of(sb, j):
        rank_row = rank_ref[sb, pl.ds(e, 1), :]
        return jnp.where(rank_row == (slot + j * ts).astype(F32), 1.0, 0.0)

    def scatter(sb, pick, y):
        w_slot = jnp.sum(pick * gate_ref[sb, pl.ds(e, 1), :], axis=1, keepdims=True)
        out_ref[sb * tb:(sb + 1) * tb, :] += _mm(pick, y * w_slot, TN)

    @pl.when(f == 0)
    def _():
        for sb in range(nsb):
            xg_scr[sb] = _mm(pick_of(sb, 0), xb_scr[sb * tb:(sb + 1) * tb, :]).astype(BF16)
        yg_scr[...] = jnp.zeros_like(yg_scr)

    yg_scr[...] += expert(xg_scr[...].reshape(nsb * ts, D_MODEL)).reshape(nsb, ts, D_MODEL)

    @pl.when(last_f)
    def _():
        for sb in range(nsb):
            scatter(sb, pick_of(sb, 0), yg_scr[sb])

    for sb in range(nsb):
        n_tiles = (cnt_ref[(i * nsb + sb) * N_EXPERTS + e] + (ts - 1)) // ts

        def overflow(j, carry, sb=sb):
            pick = pick_of(sb, j)
            scatter(sb, pick, expert(_mm(pick, xb_scr[sb * tb:(sb + 1) * tb, :]).astype(BF16)))
            return carry

        lax.fori_loop(1, n_tiles, overflow, 0)

    @pl.when((e == pl.num_programs(1) - 1) & last_f)
    def _():
        out_ref[...] = _layer_norm(DN_ALPHA * x_ref[...] + out_ref[...], g_ref[...], b_ref[...])


def _moe(x, router, wg, wu, wd, ln_g, ln_b, tm):
    m = x.shape[0]
    nblk = m // tm
    dff = wg.shape[2]
    nf = 2
    tf = dff // nf
    ts = min(MOE_TILE, tm)
    nsb = 2 if nblk % 2 == 0 else 1
    assert m % tm == 0 and tf % 128 == 0
    tri = (jnp.arange(tm)[:, None] < jnp.arange(tm)[None, :]).astype(BF16)
    gate, rank, cnt = pl.pallas_call(
        _route_kernel,
        grid=(nblk,),
        in_specs=[pl.BlockSpec((tm, D_MODEL), lambda i: (i, 0)), _resident((N_EXPERTS, D_MODEL)),
                  _resident((tm, tm))],
        out_specs=[pl.BlockSpec((1, N_EXPERTS, tm), lambda i: (i, 0, 0)),
                   pl.BlockSpec((1, N_EXPERTS, tm), lambda i: (i, 0, 0)),
                   pl.BlockSpec((1, N_EXPERTS, 128), lambda i: (i, 0, 0))],
        out_shape=[jax.ShapeDtypeStruct((nblk, N_EXPERTS, tm), F32), jax.ShapeDtypeStruct((nblk, N_EXPERTS, tm), F32),
                   jax.ShapeDtypeStruct((nblk, N_EXPERTS, 128), F32)],
        compiler_params=pltpu.CompilerParams(dimension_semantics=("parallel",), vmem_limit_bytes=VMEM_LIMIT),
        name="moe_route",
    )(x, router.T, tri)
    counts = cnt[:, :, 0].astype(jnp.int32).reshape(-1)
    rows = nsb * tm
    x_in = pl.BlockSpec((rows, D_MODEL), lambda i, e, f, c: (i, 0), pipeline_mode=pl.Buffered(1))
    meta = pl.BlockSpec((nsb, N_EXPERTS, tm), lambda i, e, f, c: (i, 0, 0))
    vec = pl.BlockSpec((1, D_MODEL), lambda i, e, f, c: (0, 0), pipeline_mode=pl.Buffered(1))
    return pl.pallas_call(
        functools.partial(_moe_kernel, ts=ts, tb=tm, nsb=nsb),
        grid_spec=pltpu.PrefetchScalarGridSpec(
            num_scalar_prefetch=1,
            grid=(nblk // nsb, N_EXPERTS, nf),
            in_specs=[x_in, meta, meta,
                      pl.BlockSpec((1, D_MODEL, tf), lambda i, e, f, c: (e, 0, f)),
                      pl.BlockSpec((1, D_MODEL, tf), lambda i, e, f, c: (e, 0, f)),
                      pl.BlockSpec((1, tf, D_MODEL), lambda i, e, f, c: (e, f, 0)),
                      vec, vec],
            out_specs=pl.BlockSpec((rows, D_MODEL), lambda i, e, f, c: (i, 0), pipeline_mode=pl.Buffered(1)),
            scratch_shapes=[pltpu.VMEM((rows, D_MODEL), BF16),
                            pltpu.VMEM((nsb, ts, D_MODEL), BF16), pltpu.VMEM((nsb, ts, D_MODEL), F32)]),
        out_shape=jax.ShapeDtypeStruct((m, D_MODEL), F32),
        compiler_params=pltpu.CompilerParams(dimension_semantics=("parallel", "arbitrary", "arbitrary"),
                                             vmem_limit_bytes=MOE_VMEM_LIMIT),
        name="moe_ln",
    )(counts, x, gate, rank, wg, wu, wd, ln_g.reshape(1, -1), ln_b.reshape(1, -1))


def _token_mix(h, pos0, rw_prev, wkv0, pool_buf, ret0, kv_bufs, lw, l, w_in_b, w_branch_b, w_out_b, ln_g, ln_b):
    b, t, _ = h.shape
    m = b * t
    tm = 512 if m % 512 == 0 else m
    hf = h.reshape(m, D_MODEL)
    p_rw, p_pool, p_ret, p_dil = _project(hf, w_in_b, l, tm)
    p_rw, p_pool = p_rw.reshape(b, t, -1), p_pool.reshape(b, t, -1)
    p_ret, p_dil = p_ret.reshape(b, t, -1), p_dil.reshape(b, t, -1)
    o_a, wkv_new = _rwkv_branch(p_rw, rw_prev, wkv0, lw)
    o_b = _pool_branch(p_pool, pool_buf, pos0, lw)
    pool_new = jnp.concatenate([pool_buf, p_pool], axis=1)[:, -POOL_BUF:]
    o_c, ret_new = _ret_branch(p_ret, ret0, pos0, lw)
    def kv_rows(g, keep):
        lo = (3 * g + 1) * WIDTH
        return p_dil[:, t - keep:, lo:lo + 2 * WIDTH].reshape(b, keep, 2, N_HEADS, HEAD_DIM)

    if kv_bufs is None:
        dil_parts = _dil_prompt(p_dil)
        kv_new = [kv_rows(g, min(win, t)) for g, (win, _) in enumerate(DIL_PATTERNS)]
    else:
        dil_parts = [_dil_step(p_dil, kv_bufs, l).reshape(m, WIDTH)]
        kv_new = [kv_rows(g, t) for g in range(DIL_GROUPS)]
    branches = [x.reshape(m, WIDTH) for x in (o_a, o_b, o_c)] + dil_parts
    x1 = _merge(hf, branches, w_in_b, w_branch_b, w_out_b, l, ln_g, ln_b, tm)
    return x1, (wkv_new, p_rw[:, -1], pool_new, ret_new, kv_new[0], kv_new[1], kv_new[2])


def kernel(x_prompt, x_sample, state_wkv, state_shift, state_pool, state_ret, cache_kv_w128, cache_kv_w512, cache_kv_w2048, w_in, rw_mu, rw_w0, rw_w2, rw_a0, rw_a2, rw_g2, rw_kk, rw_ka, rw_rk, rw_gn_g, rw_gn_b, pool_w, pool_scale, ret_gn_g, ret_gn_b, w_branch, w_out, ln_g, ln_b, ffn_w_gate, ffn_w_up, ffn_w_down, moe_router, moe_w_gate, moe_w_up, moe_w_down):
    hp, hs = x_prompt, x_sample
    bp, tp, _ = hp.shape
    bs, ts, _ = hs.shape
    names = ('rw_mu', 'rw_w0', 'rw_w2', 'rw_a0', 'rw_a2', 'rw_g2', 'rw_kk', 'rw_ka', 'rw_rk', 'rw_gn_g',
             'rw_gn_b', 'pool_w', 'pool_scale', 'ret_gn_g', 'ret_gn_b')
    stacked = (rw_mu, rw_w0, rw_w2, rw_a0, rw_a2, rw_g2, rw_kk, rw_ka, rw_rk, rw_gn_g, rw_gn_b, pool_w,
               pool_scale, ret_gn_g, ret_gn_b)
    new_p = [[] for _ in range(7)]
    new_s = [[] for _ in range(7)]
    zeros = lambda *shape: jnp.zeros(shape, F32)
    w_in_b, wb, wo = w_in.astype(BF16), w_branch.astype(BF16), w_out.astype(BF16)
    caches = [jnp.transpose(c, (0, 1, 3, 4, 5, 2)) for c in (cache_kv_w128, cache_kv_w512, cache_kv_w2048)]
    for l in range(DEPTH):
        lw = {k: v[l] for k, v in zip(names, stacked)}
        xp, st_p = _token_mix(hp, 0, zeros(bp, RW_COLS), zeros(bp, N_HEADS, HEAD_DIM, HEAD_DIM),
                              zeros(bp, POOL_BUF, WIDTH), zeros(bp, N_HEADS, HEAD_DIM, HEAD_DIM), None,
                              lw, l, w_in_b, wb, wo, ln_g[l, 0], ln_b[l, 0])
        xs, st_s = _token_mix(hs, 8192, state_shift[l], state_wkv[l], state_pool[l], state_ret[l], caches,
                              lw, l, w_in_b, wb, wo, ln_g[l, 0], ln_b[l, 0])
        j = l // 2
        if l % 2 == 0:
            ws = [w[j].astype(BF16) for w in (ffn_w_gate, ffn_w_up, ffn_w_down)]
            xp = _ffn(xp, *ws, ln_g[l, 1], ln_b[l, 1], 512)
            xs = _ffn(xs, *ws, ln_g[l, 1], ln_b[l, 1], xs.shape[0])
        else:
            ws = [w[j].astype(BF16) for w in (moe_w_gate, moe_w_up, moe_w_down)]
            xp = _moe(xp, moe_router[j], *ws, ln_g[l, 1], ln_b[l, 1], 1024)
            xs = _moe(xs, moe_router[j], *ws, ln_g[l, 1], ln_b[l, 1], xs.shape[0])
        hp, hs = xp.reshape(bp, tp, D_MODEL), xs.reshape(bs, ts, D_MODEL)
        for i in range(7):
            new_p[i].append(st_p[i])
            new_s[i].append(st_s[i])
    outs_p = [jnp.stack(x) for x in new_p]
    outs_s = [jnp.stack(x) for x in new_s]
    return (hp, hs, *outs_p, *outs_s)
```

```python
import functools
import math

import jax
import jax.numpy as jnp
from jax import lax
from jax.experimental import pallas as pl
from jax.experimental.pallas import tpu as pltpu

F32 = jnp.float32
BF16 = jnp.bfloat16

D_MODEL = 1024
DEPTH = 2
HEAD_DIM = 64
N_HEADS = 4
WIDTH = N_HEADS * HEAD_DIM
RW_COLS = 1024
RW_GN_EPS = 64e-5
POOL_WINDOWS = (2, 4, 8, 16)
POOL_BUF = 15
RET_CHUNK = 128
ROPE_BASE = 10000.0
DIL_PATTERNS = ((128, 1), (512, 4), (2048, 16))
DIL_GROUPS = 3
DIL_BLOCK = 128
DIL_SPAN = 2048
DIL_COLS = 3 * DIL_GROUPS * WIDTH
COL_POOL = RW_COLS
COL_RET = COL_POOL + WIDTH
COL_DIL = COL_RET + 4 * WIDTH
COL_GATE = COL_DIL + DIL_COLS
N_EXPERTS = 8
DN_ALPHA = (2 * DEPTH) ** 0.25
LN_EPS = 1e-5
RW_CHUNK = 64
MOE_FF_SLICES = 2
MOE_TILE = 288
RW_P = 1
RW_BATCH = 4
RW_CHUNKS_PER_STEP = 2
RW_LANE_GROUPS = 2
NEG_BIG = -1e30

NN = (((1,), (0,)), ((), ()))
NT = (((1,), (1,)), ((), ()))
TN = (((0,), (0,)), ((), ()))

VMEM_LIMIT = 56 * 1024 * 1024
MOE_VMEM_LIMIT = 61 * 1024 * 1024


def _split(x, n):
    if x.dtype == BF16:
        return [x]
    parts, rem = [], x
    for i in range(n):
        p = rem.astype(BF16)
        parts.append(p)
        if i + 1 < n:
            rem = rem - p.astype(F32)
    return parts


def _mm(a, b, dims=NN, pa=1, pb=1):
    a_parts, b_parts = _split(a, pa), _split(b, pb)
    depth = max(len(a_parts), len(b_parts))
    acc = None
    for i, ai in enumerate(a_parts):
        for j, bj in enumerate(b_parts):
            if i + j < depth:
                t = lax.dot_general(ai, bj, dims, preferred_element_type=F32)
                acc = t if acc is None else acc + t
    return acc


def _sigmoid(x):
    return 0.5 * jnp.tanh(0.5 * x) + 0.5


def _silu(x):
    return x * _sigmoid(x)


def _iota(shape, axis):
    return lax.broadcasted_iota(jnp.int32, shape, axis)


def _div(x, d):
    assert d & (d - 1) == 0
    return x >> (d.bit_length() - 1)


def _mod(x, d):
    assert d & (d - 1) == 0
    return x & (d - 1)


def _per_head(head, values):
    out = jnp.full(head.shape, values[-1], F32)
    for h in range(len(values) - 2, -1, -1):
        out = jnp.where(head == h, values[h], out)
    return out


def _head_mask(rows_per_head, n_rows):
    return _div(_iota((n_rows, WIDTH), 0), rows_per_head) == _div(_iota((n_rows, WIDTH), 1), HEAD_DIM)


def _stack_heads(x, mask):
    return jnp.where(mask, jnp.concatenate([x] * N_HEADS, axis=0), 0.0)


def _unstack_heads(x_st, mask, c):
    x_st = jnp.where(mask, x_st, 0.0)
    out = x_st[0:c]
    for h in range(1, N_HEADS):
        out = out + x_st[h * c:(h + 1) * c]
    return out


def _ones_bd():
    return _same_head().astype(BF16)


def _same_head():
    return _div(_iota((WIDTH, WIDTH), 0), HEAD_DIM) == _div(_iota((WIDTH, WIDTH), 1), HEAD_DIM)


def _head_norm(x, ones_bd, g, b, eps):
    mu = _mm(x, ones_bd, pa=2) * (1.0 / HEAD_DIM)
    d = x - mu
    var = _mm(d * d, ones_bd, pa=2) * (1.0 / HEAD_DIM)
    return d * lax.rsqrt(var + eps) * g + b


def _layer_norm(x, g, b):
    mu = jnp.mean(x, axis=-1, keepdims=True)
    d = x - mu
    var = jnp.mean(d * d, axis=-1, keepdims=True)
    return d * lax.rsqrt(var + LN_EPS) * g + b


def _resident(shape):
    nd = len(shape)
    return pl.BlockSpec(shape, lambda *_: (0,) * nd, pipeline_mode=pl.Buffered(1))


def _proj_kernel(x_ref, w_ref, rw_ref, pool_ref, ret_ref, dil_ref):
    xb = x_ref[...].astype(BF16)
    for ref, lo, hi in ((rw_ref, 0, COL_POOL), (pool_ref, COL_POOL, COL_RET),
                        (ret_ref, COL_RET, COL_DIL), (dil_ref, COL_DIL, COL_GATE)):
        for s in range(lo, hi, 512):
            e = min(s + 512, hi)
            ref[:, s - lo:e - lo] = _mm(xb, w_ref[:, s:e])


def _layer_resident(w, l):
    nd = w.ndim - 1
    return pl.BlockSpec((None,) + w.shape[1:], lambda *_: (l,) + (0,) * nd, pipeline_mode=pl.Buffered(1))


def _project(x, w_in_b, l, tm):
    m = x.shape[0]
    widths = (COL_POOL, WIDTH, 4 * WIDTH, DIL_COLS)
    row = lambda w: pl.BlockSpec((tm, w), lambda i: (i, 0))
    return pl.pallas_call(
        _proj_kernel,
        grid=(m // tm,),
        in_specs=[row(D_MODEL), _layer_resident(w_in_b, l)],
        out_specs=[row(w) for w in widths],
        out_shape=[jax.ShapeDtypeStruct((m, w), F32) for w in widths],
        compiler_params=pltpu.CompilerParams(dimension_semantics=("parallel",), vmem_limit_bytes=VMEM_LIMIT),
        name="proj",
    )(x, w_in_b)


BNT = (((2,), (2,)), ((0,), (0,)))
BNN = (((2,), (1,)), ((0,), (0,)))
BTN = (((1,), (1,)), ((0,), (0,)))


def _rwkv_kernel(p_ref, prev_ref, s0_ref, mu_ref, w0_ref, w2_ref, a0_ref, a2_ref, g2_ref, kk_ref, ka_ref,
                 rk_ref, gng_ref, gnb_ref, o_ref, st_ref, s_scr, prev_scr, buf_scr, *, c, t_in, nb, nsub):
    ci = pl.program_id(1)
    rows = nsub * c

    halves = RW_LANE_GROUPS
    hw, hpg = WIDTH // halves, N_HEADS // halves
    n = hpg * c
    same_grp = _div(_iota((hw, hw), 0), HEAD_DIM) == _div(_iota((hw, hw), 1), HEAD_DIM)

    @pl.when(ci == 0)
    def _():
        for g in range(halves):
            part = s0_ref[:, g * hw:(g + 1) * hw, :]
            s_scr[g * nb:(g + 1) * nb] = jnp.where(same_grp, jnp.concatenate([part] * hpg, axis=-1), 0.0)
        for b in range(nb):
            prev_scr[b, 0:1, :] = prev_ref[b]

    first = _iota((rows, RW_COLS), 0) == 0
    ps, shs = [], []
    for b in range(nb):
        if t_in == c:
            pb_ = p_ref[b]
        else:
            buf_scr[b] = jnp.zeros((c, RW_COLS), F32)
            buf_scr[b, 0:t_in, :] = p_ref[b]
            pb_ = buf_scr[b]
        shs.append(jnp.where(first, prev_scr[b, 0:1, :], pltpu.roll(pb_, 1, 0)))
        prev_scr[b, 0:1, :] = pb_[rows - 1:rows, :]
        ps.append(pb_)
    p, shifted = jnp.concatenate(ps, axis=0), jnp.concatenate(shs, axis=0)
    m = nb * rows
    u = p + (shifted - p) * mu_ref[...]
    r, k, v, ul = u[:, 0:WIDTH], u[:, WIDTH:2 * WIDTH], u[:, 2 * WIDTH:3 * WIDTH], u[:, 3 * WIDTH:]

    w_pre = w0_ref[...] + _mm(jnp.tanh(ul), w2_ref[...])
    logw = -math.exp(-0.5) * _sigmoid(w_pre)
    a = _sigmoid(a0_ref[...] + _mm(ul, a2_ref[...]))
    g = _mm(_sigmoid(ul), g2_ref[...])

    ones_bd = _ones_bd()
    kk = k * kk_ref[...]
    kk = kk * lax.rsqrt(jnp.maximum(_mm(kk * kk, ones_bd, pa=2), 1e-24))
    k_mod = k * (1.0 + (a - 1.0) * ka_ref[...])
    a_vec, b_vec = -kk, kk * a
    if t_in < c:
        live = _mod(_iota((m, WIDTH), 0), c) < t_in
        logw = jnp.where(live, logw, 0.0)
        a_vec, b_vec = jnp.where(live, a_vec, 0.0), jnp.where(live, b_vec, 0.0)
        k_mod, v = jnp.where(live, k_mod, 0.0), jnp.where(live, v, 0.0)

    qi, qj = _iota((m, m), 0), _iota((m, m), 1)
    tri = ((_div(qi, c) == _div(qj, c)) & (qi >= qj)).astype(BF16)
    cum2 = _mm(tri, logw, pb=3)
    def chunk_rows(x, s):
        return x if nsub == 1 else x.reshape(nb, nsub, c, x.shape[-1])[:, s].reshape(nb * c, x.shape[-1])

    def seq(x):
        parts = []
        for s in range(nsub):
            x3 = chunk_rows(x, s).reshape(nb, c, WIDTH)
            parts += [x3[:, :, g * hw:(g + 1) * hw] for g in range(halves)]
        return jnp.concatenate(parts, axis=0)

    cum, lw3 = seq(cum2), seq(logw)
    cum_end = cum[:, c - 1:c, :]
    e_neg, e_end = jnp.exp(-cum), jnp.exp(cum_end - cum)
    a_t = seq(a_vec) * jnp.exp(cum - lw3)
    r_t = seq(r) * jnp.exp(cum)
    b_t, k_t = seq(b_vec) * e_neg, seq(k_mod) * e_neg
    b_e, k_e = seq(b_vec) * e_end, seq(k_mod) * e_end
    v3 = seq(v)

    hm = _div(_iota((n, hw), 0), c) == _div(_iota((n, hw), 1), HEAD_DIM)
    stack = lambda x: jnp.where(hm, jnp.concatenate([x] * hpg, axis=1), 0.0)
    a_st, r_st, b_st, v_st = stack(a_t), stack(r_t), stack(b_t), stack(v3)
    ri, rj = _iota((n, n), 0), _iota((n, n), 1)
    same = _div(ri, c) == _div(rj, c)
    strict_bd = same & (_mod(ri, c) > _mod(rj, c))
    incl_bd = same & (_mod(ri, c) >= _mod(rj, c))
    ti, tj = _mod(_iota((n, c), 0), c), _iota((n, c), 1)
    strict_st, incl_st = ti > tj, ti >= tj

    mm = functools.partial(_mm, pa=RW_P, pb=RW_P)
    a_ab = jnp.where(strict_bd, mm(a_st, b_st, BNT), 0.0)
    inv = jnp.where(ri == rj, 1.0, 0.0) + a_ab
    pw = a_ab
    for _ in range(int(math.log2(c)) - 1):
        pw = mm(pw, pw, BNN)
        inv = inv + mm(inv, pw, BNN)
    a_ak = jnp.where(strict_st, mm(a_st, k_t, BNT), 0.0)
    z_st = jnp.where(hm, mm(a_ak, v3, BNN), 0.0)
    wu = mm(inv, jnp.concatenate([a_st, z_st], axis=2), BNN)
    w_st, u0_st = wu[:, :, 0:hw], wu[:, :, hw:]

    rhs = jnp.concatenate([stack(b_e), stack(k_e)], axis=1)
    a_rb = jnp.where(incl_bd, mm(r_st, b_st, BNT), 0.0)
    a_rk = jnp.where(incl_st, mm(r_st, k_t, BNT), 0.0)
    o_in = mm(a_rk, v3, BNN)
    decay_end = jnp.exp(cum_end)

    per = halves * nb
    s0 = s_scr[...]
    for s in range(nsub):
        sl = slice(s * per, (s + 1) * per)
        u_st = mm(w_st[sl], s0, BNT) + u0_st[sl]
        o_st = jnp.where(hm, mm(r_st[sl], s0, BNT) + mm(a_rb[sl], u_st, BNN) + o_in[sl], 0.0)
        s0 = s0 * decay_end[sl] + mm(jnp.concatenate([u_st, v_st[sl]], axis=1), rhs[sl], BTN)
        o3 = o_st[:, 0:c]
        for h in range(1, hpg):
            o3 = o3 + o_st[:, h * c:(h + 1) * c]
        o = jnp.concatenate([o3[g * nb:(g + 1) * nb] for g in range(halves)], axis=2).reshape(nb * c, WIDTH)
        o = _head_norm(o, ones_bd, gng_ref[...], gnb_ref[...], RW_GN_EPS)
        v_s = chunk_rows(v, s)
        o = o + _mm(chunk_rows(r, s) * chunk_rows(k_mod, s) * rk_ref[...], ones_bd, pa=2) * v_s
        o = (o * chunk_rows(g, s)).reshape(nb, c, WIDTH)
        if t_in == c:
            o_ref[:, s * c:(s + 1) * c, :] = o
        else:
            o_ref[...] = o[:, 0:t_in]
    s_scr[...] = s0

    @pl.when(ci == pl.num_programs(1) - 1)
    def _():
        for g in range(halves):
            blk = s_scr[g * nb:(g + 1) * nb]
            part = blk[..., 0:HEAD_DIM]
            for h in range(1, hpg):
                part = part + blk[..., h * HEAD_DIM:(h + 1) * HEAD_DIM]
            st_ref[:, g * hw:(g + 1) * hw, :] = part


def _embed_heads(s):
    return jnp.where(_same_head(), jnp.concatenate([s] * N_HEADS, axis=-1), 0.0)


def _extract_heads(s_bd):
    out = s_bd[..., 0:HEAD_DIM]
    for h in range(1, N_HEADS):
        out = out + s_bd[..., h * HEAD_DIM:(h + 1) * HEAD_DIM]
    return out


def _rwkv_branch(p, p_prev, wkv0, lw):
    b, t, _ = p.shape
    c = RW_CHUNK if t % RW_CHUNK == 0 else RW_CHUNK // 2
    assert t % c == 0 or t <= c
    t_in = c if t % c == 0 else t
    nc = t // c if t % c == 0 else 1
    vec = lambda x: x.reshape(1, -1)
    pad_rows = lambda w, lo: jnp.zeros((WIDTH, WIDTH), F32).at[lo:lo + w.shape[0]].set(w).astype(BF16)
    params = [vec(lw['rw_mu']), vec(lw['rw_w0']), pad_rows(lw['rw_w2'], 0), vec(lw['rw_a0']),
              pad_rows(lw['rw_a2'], 64), pad_rows(lw['rw_g2'], 128), vec(lw['rw_kk']), vec(lw['rw_ka']),
              vec(lw['rw_rk']), vec(lw['rw_gn_g']), vec(lw['rw_gn_b'])]
    nb = RW_BATCH
    nsub = RW_CHUNKS_PER_STEP if nc % RW_CHUNKS_PER_STEP == 0 else 1
    rows = nsub * t_in
    assert b % nb == 0
    o, st = pl.pallas_call(
        functools.partial(_rwkv_kernel, c=c, t_in=t_in, nb=nb, nsub=nsub),
        grid=(b // nb, nc // nsub),
        in_specs=[pl.BlockSpec((nb, rows, RW_COLS), lambda i, j: (i, j, 0)),
                  pl.BlockSpec((nb, 1, RW_COLS), lambda i, j: (i, 0, 0)),
                  pl.BlockSpec((nb, WIDTH, HEAD_DIM), lambda i, j: (i, 0, 0))]
                 + [_resident(x.shape) for x in params],
        out_specs=[pl.BlockSpec((nb, rows, WIDTH), lambda i, j: (i, j, 0)),
                   pl.BlockSpec((nb, WIDTH, HEAD_DIM), lambda i, j: (i, 0, 0))],
        out_shape=[jax.ShapeDtypeStruct((b, t, WIDTH), F32), jax.ShapeDtypeStruct((b, WIDTH, HEAD_DIM), F32)],
        scratch_shapes=[pltpu.VMEM((RW_LANE_GROUPS * nb, WIDTH // RW_LANE_GROUPS, WIDTH // RW_LANE_GROUPS), F32),
                        pltpu.VMEM((nb, 8, RW_COLS), F32),
                        pltpu.VMEM((nb, c, RW_COLS), F32)],
        compiler_params=pltpu.CompilerParams(dimension_semantics=("parallel", "arbitrary"),
                                             vmem_limit_bytes=VMEM_LIMIT),
        name="rwkv7",
    )(p, p_prev.reshape(b, 1, RW_COLS), wkv0.reshape(b, WIDTH, HEAD_DIM), *params)
    return o, st.reshape(b, N_HEADS, HEAD_DIM, HEAD_DIM)


def _pool_rows(ext_scr, c, pos_first, w, scale):
    x = ext_scr[16:16 + c, :]
    sums, acc, off = [], x, 1
    for win in POOL_WINDOWS:
        while off < win:
            acc = acc + ext_scr[16 - off:16 - off + c, :]
            off += 1
        sums.append(acc)
    pos = pos_first + _iota((c, WIDTH), 0)
    grp = _div(_iota((c, WIDTH), 1), HEAD_DIM)
    mean = jnp.zeros((c, WIDTH), F32)
    for gi, win in enumerate(POOL_WINDOWS):
        cnt = jnp.minimum(win, pos + 1).astype(F32)
        mean = jnp.where(grp == gi, sums[gi] / cnt, mean)
    ext_scr[0:16, :] = ext_scr[c:c + 16, :]
    return _mm(mean - x, w) * scale


def _pool_kernel(u_ref, buf_ref, w_ref, scale_ref, o_ref, ext_scr, *, c, t_in, pos0):
    ci = pl.program_id(1)

    @pl.when(ci == 0)
    def _():
        ext_scr[0:16, :] = buf_ref[0]

    if t_in < c:
        ext_scr[16:16 + c, :] = jnp.zeros((c, WIDTH), F32)
    ext_scr[16:16 + t_in, :] = u_ref[0]
    o_ref[0] = _pool_rows(ext_scr, c, pos0 + ci * c, w_ref[...], scale_ref[...])[0:t_in]


def _pool_weights(lw):
    w_bd = jnp.einsum('gcd,gh->gchd', lw['pool_w'], jnp.eye(N_HEADS, dtype=F32)).reshape(WIDTH, WIDTH)
    return w_bd.astype(BF16), lw['pool_scale'].reshape(1, WIDTH)


def _pool_branch(u, buf, pos0, lw):
    b, t, _ = u.shape
    c = 512 if t % 512 == 0 else 8
    t_in = c if t % c == 0 else t
    nc = t // c if t % c == 0 else 1
    w_bd, scale = _pool_weights(lw)
    buf16 = jnp.pad(buf, ((0, 0), (1, 0), (0, 0)))
    return pl.pallas_call(
        functools.partial(_pool_kernel, c=c, t_in=t_in, pos0=pos0),
        grid=(b, nc),
        in_specs=[pl.BlockSpec((1, t_in, WIDTH), lambda i, j: (i, j, 0)),
                  pl.BlockSpec((1, 16, WIDTH), lambda i, j: (i, 0, 0)),
                  _resident((WIDTH, WIDTH)), _resident((1, WIDTH))],
        out_specs=pl.BlockSpec((1, t_in, WIDTH), lambda i, j: (i, j, 0)),
        out_shape=jax.ShapeDtypeStruct((b, t, WIDTH), F32),
        scratch_shapes=[pltpu.VMEM((16 + c, WIDTH), F32)],
        compiler_params=pltpu.CompilerParams(dimension_semantics=("parallel", "arbitrary")),
        name="pool",
    )(u, buf16, w_bd, scale)


def _rot_half(x):
    first = _mod(_iota(x.shape, 1), HEAD_DIM) < (HEAD_DIM // 2)
    return jnp.where(first, pltpu.roll(x, WIDTH - HEAD_DIM // 2, 1), pltpu.roll(x, HEAD_DIM // 2, 1))


RET_LOG_DECAY = tuple(math.log(1.0 - 2.0 ** (-5.0 - h)) for h in range(N_HEADS))
ALIBI_SLOPES = tuple(2.0 ** (-8.0 * (i + 1) / (DIL_GROUPS * N_HEADS)) for i in range(DIL_GROUPS * N_HEADS))


def _ret_kernel(p_ref, cos_ref, sin_ref, s0_ref, gng_ref, gnb_ref, o_ref, st_ref, s_scr, buf_scr,
                dmask_scr, qd_scr, kd_scr, *, c, t_in, nb):
    ci = pl.program_id(1)
    n = N_HEADS * c
    lg = _per_head(_div(_iota((1, WIDTH), 1), HEAD_DIM), RET_LOG_DECAY)

    @pl.when(ci == 0)
    def _():
        s_scr[...] = _embed_heads(s0_ref[...])
        idx = _iota((c, WIDTH), 0).astype(F32)
        qd_scr[...] = jnp.exp(lg * (idx + 1.0))
        kd_scr[...] = jnp.exp(lg * (t_in - 1.0 - idx))
        rel = _mod(_iota((n, c), 0), c) - _iota((n, c), 1)
        lg_rows = _per_head(_div(_iota((n, c), 0), c), RET_LOG_DECAY)
        dmask_scr[...] = jnp.where(rel >= 0, jnp.exp(lg_rows * jnp.maximum(rel, 0).astype(F32)), 0.0)

    hm = _head_mask(c, n)
    cos, sin = cos_ref[...], sin_ref[...]
    chunk_decay = jnp.exp(lg * float(t_in))
    same_head, ones_bd = _same_head(), _ones_bd()
    for b in range(nb):
        if t_in == c:
            p = p_ref[b]
        else:
            buf_scr[b] = jnp.zeros((c, 4 * WIDTH), F32)
            buf_scr[b, 0:t_in, :] = p_ref[b]
            p = buf_scr[b]
        q, k, v, g = (p[:, i * WIDTH:(i + 1) * WIDTH] for i in range(4))
        q = q * cos + _rot_half(q) * sin
        k = (k * cos + _rot_half(k) * sin) * HEAD_DIM ** -0.5
        if t_in < c:
            live = _iota((c, WIDTH), 0) < t_in
            k, v = jnp.where(live, k, 0.0), jnp.where(live, v, 0.0)
        inner = _mm(_stack_heads(q, hm), k, NT) * dmask_scr[...]
        s0 = s_scr[b]
        o = _unstack_heads(_mm(inner, v), hm, c) + _mm(q * qd_scr[...], s0)
        s_scr[b] = s0 * chunk_decay + jnp.where(same_head, _mm(k * kd_scr[...], v, TN), 0.0)
        o = _silu(g) * _head_norm(o, ones_bd, gng_ref[...], gnb_ref[...], LN_EPS)
        o_ref[b] = o[0:t_in]

    @pl.when(ci == pl.num_programs(1) - 1)
    def _():
        st_ref[...] = _extract_heads(s_scr[...])


def _rope_tables(pos0, t, rows):
    half = HEAD_DIM // 2
    inv = ROPE_BASE ** (-jnp.arange(half, dtype=F32) / half)
    ang = (pos0 + jnp.arange(t, dtype=jnp.int32)).astype(F32)[:, None] * inv[None, :]
    cos = jnp.tile(jnp.cos(ang), (1, 2 * N_HEADS))
    sin = jnp.tile(jnp.concatenate([-jnp.sin(ang), jnp.sin(ang)], axis=1), (1, N_HEADS))
    pad = ((0, rows - t), (0, 0))
    return jnp.pad(cos, pad), jnp.pad(sin, pad)


def _ret_branch(p, s0, pos0, lw):
    b, t, _ = p.shape
    c = RET_CHUNK
    t_in = c if t % c == 0 else t
    nc = t // c if t % c == 0 else 1
    cos, sin = _rope_tables(pos0, t, nc * c)
    nb = RW_BATCH
    assert b % nb == 0
    o, st = pl.pallas_call(
        functools.partial(_ret_kernel, c=c, t_in=t_in, nb=nb),
        grid=(b // nb, nc),
        in_specs=[pl.BlockSpec((nb, t_in, 4 * WIDTH), lambda i, j: (i, j, 0)),
                  pl.BlockSpec((c, WIDTH), lambda i, j: (j, 0)),
                  pl.BlockSpec((c, WIDTH), lambda i, j: (j, 0)),
                  pl.BlockSpec((nb, WIDTH, HEAD_DIM), lambda i, j: (i, 0, 0)),
                  _resident((1, WIDTH)), _resident((1, WIDTH))],
        out_specs=[pl.BlockSpec((nb, t_in, WIDTH), lambda i, j: (i, j, 0)),
                   pl.BlockSpec((nb, WIDTH, HEAD_DIM), lambda i, j: (i, 0, 0))],
        out_shape=[jax.ShapeDtypeStruct((b, t, WIDTH), F32), jax.ShapeDtypeStruct((b, WIDTH, HEAD_DIM), F32)],
        scratch_shapes=[pltpu.VMEM((nb, WIDTH, WIDTH), F32), pltpu.VMEM((nb, c, 4 * WIDTH), F32),
                        pltpu.VMEM((N_HEADS * c, c), F32), pltpu.VMEM((c, WIDTH), F32), pltpu.VMEM((c, WIDTH), F32)],
        compiler_params=pltpu.CompilerParams(dimension_semantics=("parallel", "arbitrary")),
        name="retention",
    )(p, cos, sin, s0.reshape(b, WIDTH, HEAD_DIM), lw['ret_gn_g'].reshape(1, WIDTH), lw['ret_gn_b'].reshape(1, WIDTH))
    return o, st.reshape(b, N_HEADS, HEAD_DIM, HEAD_DIM)


def _dil_prompt_kernel(*refs, group, dil, span):
    q_refs, kc_refs, kp_refs, vc_refs, vp_refs = (refs[2 * i:2 * i + 2] for i in range(5))
    o_ref, lse_ref, k_scr, v_scr, o_scr, lse_scr = refs[10:]
    blk = DIL_BLOCK
    tail = blk * dil
    si = pl.program_id(1)
    for half in range(2):
        k_scr[half, 0:tail, :] = kp_refs[half][0]
        k_scr[half, tail:tail + span, :] = kc_refs[half][0]
        v_scr[half, 0:tail, :] = vp_refs[half][0]
        v_scr[half, tail:tail + span, :] = vc_refs[half][0]
    ki = _iota((blk, 2 * blk), 1)
    steps = blk + _iota((blk, 2 * blk), 0) - ki
    band = (steps >= 0) & (steps <= blk)
    masks = [jnp.where(band, (-ALIBI_SLOPES[group * N_HEADS + h] * dil) * steps.astype(F32), NEG_BIG)
             for h in range(N_HEADS)]
    sub_head = _div(_iota((blk, 128), 1), HEAD_DIM)
    for cc in range(span // tail):
        has_prev = (si > 0) | (ki >= blk)
        cc_masks = [jnp.where(has_prev, x, NEG_BIG) for x in masks] if cc == 0 else masks
        for r in range(dil):
            rows_q = pl.ds(cc * tail + r, blk, stride=dil) if dil > 1 else pl.ds(cc * tail, blk)
            rows_kv = pl.ds(cc * tail + r, 2 * blk, stride=dil) if dil > 1 else pl.ds(cc * tail, 2 * blk)
            for half in range(2):
                q2 = q_refs[half][0, rows_q, :] * HEAD_DIM ** -0.5
                k2 = k_scr[half, rows_kv, :].astype(BF16)
                v2 = v_scr[half, rows_kv, :].astype(BF16)
                o2 = lse2 = None
                for sub in range(2):
                    mine = sub_head == sub
                    s = _mm(jnp.where(mine, q2, 0.0), k2, NT) + cc_masks[2 * half + sub]
                    m = jnp.max(s, axis=1, keepdims=True)
                    e = jnp.exp(s - m)
                    l = jnp.sum(e, axis=1, keepdims=True)
                    o_h = _mm(e, v2) / l
                    lse_h = jnp.broadcast_to(m + jnp.log(l), (blk, 128))
                    o2 = o_h if o2 is None else jnp.where(mine, o_h, o2)
                    lse2 = lse_h if lse2 is None else jnp.where(mine, lse_h, lse2)
                o_scr[half, rows_q, :] = o2
                lse_scr[half, rows_q, :] = lse2
    o_ref[0] = jnp.concatenate([o_scr[0], o_scr[1]], axis=1)
    lse_ref[0] = jnp.concatenate([lse_scr[0], lse_scr[1]], axis=1)


def _dil_prompt_group(pd, group):
    b, s, _ = pd.shape
    win, dil = DIL_PATTERNS[group]
    span = DIL_SPAN
    tail = DIL_BLOCK * dil
    assert win // dil == DIL_BLOCK and span % tail == 0 and s % span == 0
    base = group * 3

    def cur(col):
        return [pl.BlockSpec((1, span, 128), lambda i, j, h=h: (i, j, 2 * (base + col) + h)) for h in range(2)]

    def prev(col):
        return [pl.BlockSpec((1, tail, 128),
                             lambda i, j, h=h: (i, jnp.maximum(j * (span // tail) - 1, 0), 2 * (base + col) + h))
                for h in range(2)]

    out_spec = pl.BlockSpec((1, span, WIDTH), lambda i, j: (i, j, 0))
    return pl.pallas_call(
        functools.partial(_dil_prompt_kernel, group=group, dil=dil, span=span),
        grid=(b, s // span),
        in_specs=cur(0) + cur(1) + prev(1) + cur(2) + prev(2),
        out_specs=[out_spec, out_spec],
        out_shape=[jax.ShapeDtypeStruct((b, s, WIDTH), F32)] * 2,
        scratch_shapes=[pltpu.VMEM((2, tail + span, 128), F32)] * 2 + [pltpu.VMEM((2, span, 128), F32)] * 2,
        compiler_params=pltpu.CompilerParams(dimension_semantics=("parallel", "arbitrary"),
                                             vmem_limit_bytes=VMEM_LIMIT),
        name=f"dil_prompt_g{group}",
    )(*([pd] * 10))


def _dil_combine(os_, ls):
    m = jnp.maximum(jnp.maximum(ls[0], ls[1]), ls[2])
    es = [jnp.exp(x - m) for x in ls]
    return (es[0] * os_[0] + es[1] * os_[1] + es[2] * os_[2]) / (es[0] + es[1] + es[2])


def _dil_prompt(pd):
    b, s, _ = pd.shape
    outs = [_dil_prompt_group(pd, g) for g in range(DIL_GROUPS)]
    return [x[0].reshape(b * s, WIDTH) for x in outs] + [x[1].reshape(b * s, WIDTH) for x in outs]


def _dil_step_kernel(pd_ref, c0_ref, c1_ref, c2_ref, o_ref, buf_scr, *, t, tp):
    buf_scr[...] = jnp.zeros_like(buf_scr)
    buf_scr[0:t, :] = pd_ref[0]
    pd = buf_scr[...]
    qt = _iota((tp, 1), 0)
    d_new = qt - _iota((tp, tp), 1)
    outs = [[None] * DIL_GROUPS for _ in range(N_HEADS)]
    lses = [[None] * DIL_GROUPS for _ in range(N_HEADS)]
    for g, cache_ref in enumerate((c0_ref, c1_ref, c2_ref)):
        win, dil = DIL_PATTERNS[g]
        length = cache_ref.shape[-1]
        d_old = length + qt - _iota((tp, length), 1)
        ok_old = (_mod(d_old, dil) == 0) & (d_old <= win)
        ok_new = (d_new >= 0) & (_mod(d_new, dil) == 0)
        for h in range(N_HEADS):
            slope = ALIBI_SLOPES[g * N_HEADS + h]
            lo = g * 3 * WIDTH + h * HEAD_DIM
            q = pd[:, lo:lo + HEAD_DIM] * HEAD_DIM ** -0.5
            k_new = pd[:, lo + WIDTH:lo + WIDTH + HEAD_DIM]
            v_new = pd[:, lo + 2 * WIDTH:lo + 2 * WIDTH + HEAD_DIM]
            k_t, v_t = cache_ref[0, 0, h], cache_ref[0, 1, h]
            s_old = jnp.where(ok_old, _mm(q, k_t) - slope * d_old.astype(F32), NEG_BIG)
            s_new = jnp.where(ok_new, _mm(q, k_new, NT) - slope * d_new.astype(F32), NEG_BIG)
            m = jnp.maximum(jnp.max(s_old, axis=1, keepdims=True), jnp.max(s_new, axis=1, keepdims=True))
            e_old, e_new = jnp.exp(s_old - m), jnp.exp(s_new - m)
            l = jnp.sum(e_old, axis=1, keepdims=True) + jnp.sum(e_new, axis=1, keepdims=True)
            outs[h][g] = (_mm(e_old, v_t, NT) + _mm(e_new, v_new)) / l
            lses[h][g] = m + jnp.log(l)
    o = jnp.concatenate([_dil_combine(outs[h], lses[h]) for h in range(N_HEADS)], axis=1)
    o_ref[0] = o[0:t]


def _dil_step(pd, caches, l):
    b, t, _ = pd.shape
    tp = 8
    flat = caches
    return pl.pallas_call(
        functools.partial(_dil_step_kernel, t=t, tp=tp),
        grid=(b,),
        in_specs=[pl.BlockSpec((1, t, DIL_COLS), lambda i: (i, 0, 0))]
                 + [pl.BlockSpec((None, 1) + c.shape[2:], lambda i: (l, i, 0, 0, 0, 0)) for c in flat],
        out_specs=pl.BlockSpec((1, t, WIDTH), lambda i: (i, 0, 0)),
        out_shape=jax.ShapeDtypeStruct((b, t, WIDTH), F32),
        scratch_shapes=[pltpu.VMEM((tp, DIL_COLS), F32)],
        compiler_params=pltpu.CompilerParams(dimension_semantics=("parallel",), vmem_limit_bytes=VMEM_LIMIT),
        name="dil_step",
    )(pd, *flat)


def _merge_kernel(h_ref, *refs):
    wg_ref, wb_ref, wo_ref, g_ref, b_ref, out_ref = refs[-6:]
    branch_refs = refs[:-6]
    branches = [r[...] for r in branch_refs[:3]]
    if len(branch_refs) == 4:
        branches.append(branch_refs[3][...])
    else:
        branches.append(_dil_combine([r[...] for r in branch_refs[3:6]], [r[...] for r in branch_refs[6:9]]))
    h = h_ref[...]
    hb = h.astype(BF16)
    z = None
    for n, o in enumerate(branches):
        gate = _sigmoid(_mm(hb, wg_ref[:, COL_GATE + n * D_MODEL:COL_GATE + (n + 1) * D_MODEL]))
        term = gate * _mm(o, wb_ref[n])
        z = term if z is None else z + term
    y = _mm(z, wo_ref[...])
    out_ref[...] = _layer_norm(DN_ALPHA * h + y, g_ref[...], b_ref[...])


def _merge(h, branches, w_in_b, w_branch_b, w_out_b, l, ln_g, ln_b, tm):
    m = h.shape[0]
    row = lambda w: pl.BlockSpec((tm, w), lambda i: (i, 0))
    return pl.pallas_call(
        _merge_kernel,
        grid=(m // tm,),
        in_specs=[row(D_MODEL)] + [row(WIDTH)] * len(branches)
                 + [_layer_resident(w_in_b, l), _layer_resident(w_branch_b, l), _layer_resident(w_out_b, l),
                    _resident((1, D_MODEL)), _resident((1, D_MODEL))],
        out_specs=row(D_MODEL),
        out_shape=jax.ShapeDtypeStruct((m, D_MODEL), F32),
        compiler_params=pltpu.CompilerParams(dimension_semantics=("parallel",), vmem_limit_bytes=VMEM_LIMIT),
        name="merge_ln",
    )(h, *branches, w_in_b, w_branch_b, w_out_b, ln_g.reshape(1, -1), ln_b.reshape(1, -1))


def _ffn_kernel(x_ref, wg_ref, wu_ref, wd_ref, g_ref, b_ref, out_ref):
    x = x_ref[...]
    xb = x.astype(BF16)
    act = _silu(_mm(xb, wg_ref[...])) * _mm(xb, wu_ref[...])
    out_ref[...] = _layer_norm(DN_ALPHA * x + _mm(act, wd_ref[...]), g_ref[...], b_ref[...])


def _ffn(x, wg, wu, wd, ln_g, ln_b, tm):
    m = x.shape[0]
    row = pl.BlockSpec((tm, D_MODEL), lambda i: (i, 0))
    return pl.pallas_call(
        _ffn_kernel,
        grid=(m // tm,),
        in_specs=[row, _resident(wg.shape), _resident(wu.shape), _resident(wd.shape),
                  _resident((1, D_MODEL)), _resident((1, D_MODEL))],
        out_specs=row,
        out_shape=jax.ShapeDtypeStruct((m, D_MODEL), F32),
        compiler_params=pltpu.CompilerParams(dimension_semantics=("parallel",), vmem_limit_bytes=VMEM_LIMIT),
        name="ffn_ln",
    )(x, wg, wu, wd, ln_g.reshape(1, -1), ln_b.reshape(1, -1))


def _route_kernel(x_ref, rt_ref, tri_ref, gate_ref, rank_ref, cnt_ref):
    logits = _mm(rt_ref[...], x_ref[...], NT, 2, 2)
    sub = _iota(logits.shape, 0).astype(F32)
    m1 = jnp.max(logits, axis=0, keepdims=True)
    i1 = jnp.min(jnp.where(logits == m1, sub, float(N_EXPERTS)), axis=0, keepdims=True)
    rest = jnp.where(sub == i1, NEG_BIG, logits)
    m2 = jnp.max(rest, axis=0, keepdims=True)
    i2 = jnp.min(jnp.where(rest == m2, sub, float(N_EXPERTS)), axis=0, keepdims=True)
    e2 = jnp.exp(m2 - m1)
    gate_ref[0] = jnp.where(sub == i1, 1.0 / (1.0 + e2), 0.0) + jnp.where(sub == i2, e2 / (1.0 + e2), 0.0)
    chosen = (sub == i1) | (sub == i2)
    sel = jnp.where(chosen, 1.0, 0.0)
    rank_ref[0] = jnp.where(chosen, _mm(sel, tri_ref[...]), -1.0)
    cnt_ref[0] = jnp.broadcast_to(jnp.sum(sel, axis=1, keepdims=True), cnt_ref.shape[1:])


def _moe_kernel(cnt_ref, x_ref, gate_ref, rank_ref, wg_ref, wu_ref, wd_ref, g_ref, b_ref, out_ref,
                xb_scr, xg_scr, yg_scr, *, ts, tb, nsb):
    i, e, f = pl.program_id(0), pl.program_id(1), pl.program_id(2)
    last_f = f == pl.num_programs(2) - 1

    @pl.when((e == 0) & (f == 0))
    def _():
        xb_scr[...] = x_ref[...].astype(BF16)
        out_ref[...] = jnp.zeros_like(out_ref)

    slot = _iota((ts, tb), 0)

    def expert(xg):
        return _mm(_silu(_mm(xg, wg_ref[0, 0])) * _mm(xg, wu_ref[0, 0]), wd_ref[0])

    def pick_of(sb, j):
        rank_row = rank_ref[sb, pl.ds(e, 1), :]
        return jnp.where(rank_row == (slot + j * ts).astype(F32), 1.0, 0.0)

    def scatter(sb, pick, y):
        w_slot = jnp.sum(pick * gate_ref[sb, pl.ds(e, 1), :], axis=1, keepdims=True)
        out_ref[sb * tb:(sb + 1) * tb, :] += _mm(pick, y * w_slot, TN)

    @pl.when(f == 0)
    def _():
        for sb in range(nsb):
            xg_scr[sb] = _mm(pick_of(sb, 0), xb_scr[sb * tb:(sb + 1) * tb, :]).astype(BF16)
        yg_scr[...] = jnp.zeros_like(yg_scr)

    yg_scr[...] += expert(xg_scr[...].reshape(nsb * ts, D_MODEL)).reshape(nsb, ts, D_MODEL)

    @pl.when(last_f)
    def _():
        for sb in range(nsb):
            scatter(sb, pick_of(sb, 0), yg_scr[sb])

    for sb in range(nsb):
        n_tiles = (cnt_ref[(i * nsb + sb) * N_EXPERTS + e] + (ts - 1)) // ts

        def overflow(j, carry, sb=sb):
            pick = pick_of(sb, j)
            scatter(sb, pick, expert(_mm(pick, xb_scr[sb * tb:(sb + 1) * tb, :]).astype(BF16)))
            return carry

        lax.fori_loop(1, n_tiles, overflow, 0)

    @pl.when((e == pl.num_programs(1) - 1) & last_f)
    def _():
        out_ref[...] = _layer_norm(DN_ALPHA * x_ref[...] + out_ref[...], g_ref[...], b_ref[...])


def _moe(x, router, wg, wu, wd, ln_g, ln_b, tm):
    m = x.shape[0]
    nblk = m // tm
    nf, tf = wg.shape[1], wg.shape[3]
    ts = min(MOE_TILE, tm)
    nsb = 2 if nblk % 2 == 0 else 1
    assert m % tm == 0 and tf % 128 == 0
    tri = (jnp.arange(tm)[:, None] < jnp.arange(tm)[None, :]).astype(BF16)
    gate, rank, cnt = pl.pallas_call(
        _route_kernel,
        grid=(nblk,),
        in_specs=[pl.BlockSpec((tm, D_MODEL), lambda i: (i, 0)), _resident((N_EXPERTS, D_MODEL)),
                  _resident((tm, tm))],
        out_specs=[pl.BlockSpec((1, N_EXPERTS, tm), lambda i: (i, 0, 0)),
                   pl.BlockSpec((1, N_EXPERTS, tm), lambda i: (i, 0, 0)),
                   pl.BlockSpec((1, N_EXPERTS, 128), lambda i: (i, 0, 0))],
        out_shape=[jax.ShapeDtypeStruct((nblk, N_EXPERTS, tm), F32), jax.ShapeDtypeStruct((nblk, N_EXPERTS, tm), F32),
                   jax.ShapeDtypeStruct((nblk, N_EXPERTS, 128), F32)],
        compiler_params=pltpu.CompilerParams(dimension_semantics=("parallel",), vmem_limit_bytes=VMEM_LIMIT),
        name="moe_route",
    )(x, router.T, tri)
    counts = cnt[:, :, 0].astype(jnp.int32).reshape(-1)
    rows = nsb * tm
    x_in = pl.BlockSpec((rows, D_MODEL), lambda i, e, f, c: (i, 0), pipeline_mode=pl.Buffered(1))
    meta = pl.BlockSpec((nsb, N_EXPERTS, tm), lambda i, e, f, c: (i, 0, 0))
    vec = pl.BlockSpec((1, D_MODEL), lambda i, e, f, c: (0, 0), pipeline_mode=pl.Buffered(1))
    return pl.pallas_call(
        functools.partial(_moe_kernel, ts=ts, tb=tm, nsb=nsb),
        grid_spec=pltpu.PrefetchScalarGridSpec(
            num_scalar_prefetch=1,
            grid=(nblk // nsb, N_EXPERTS, nf),
            in_specs=[x_in, meta, meta,
                      pl.BlockSpec((1, 1, D_MODEL, tf), lambda i, e, f, c: (e, f, 0, 0)),
                      pl.BlockSpec((1, 1, D_MODEL, tf), lambda i, e, f, c: (e, f, 0, 0)),
                      pl.BlockSpec((1, tf, D_MODEL), lambda i, e, f, c: (e, f, 0)),
                      vec, vec],
            out_specs=pl.BlockSpec((rows, D_MODEL), lambda i, e, f, c: (i, 0), pipeline_mode=pl.Buffered(1)),
            scratch_shapes=[pltpu.VMEM((rows, D_MODEL), BF16),
                            pltpu.VMEM((nsb, ts, D_MODEL), BF16), pltpu.VMEM((nsb, ts, D_MODEL), F32)]),
        out_shape=jax.ShapeDtypeStruct((m, D_MODEL), F32),
        compiler_params=pltpu.CompilerParams(dimension_semantics=("parallel", "arbitrary", "arbitrary"),
                                             vmem_limit_bytes=MOE_VMEM_LIMIT),
        name="moe_ln",
    )(counts, x, gate, rank, wg, wu, wd, ln_g.reshape(1, -1), ln_b.reshape(1, -1))


def _moe_weight_slices(w):
    e, d, dff = w.shape
    tf = dff // MOE_FF_SLICES
    assert tf % 256 == 0
    return w.astype(BF16).reshape(e, d, MOE_FF_SLICES, tf).transpose(0, 2, 1, 3)


def _token_mix(h, pos0, rw_prev, wkv0, pool_buf, ret0, kv_bufs, lw, l, w_in_b, w_branch_b, w_out_b, ln_g, ln_b):
    b, t, _ = h.shape
    m = b * t
    tm = 512 if m % 512 == 0 else m
    hf = h.reshape(m, D_MODEL)
    p_rw, p_pool, p_ret, p_dil = _project(hf, w_in_b, l, tm)
    p_rw, p_pool = p_rw.reshape(b, t, -1), p_pool.reshape(b, t, -1)
    p_ret, p_dil = p_ret.reshape(b, t, -1), p_dil.reshape(b, t, -1)
    o_a, wkv_new = _rwkv_branch(p_rw, rw_prev, wkv0, lw)
    o_b = _pool_branch(p_pool, pool_buf, pos0, lw)
    pool_new = jnp.concatenate([pool_buf, p_pool], axis=1)[:, -POOL_BUF:]
    o_c, ret_new = _ret_branch(p_ret, ret0, pos0, lw)
    def kv_rows(g, keep):
        lo = (3 * g + 1) * WIDTH
        return p_dil[:, t - keep:, lo:lo + 2 * WIDTH].reshape(b, keep, 2, N_HEADS, HEAD_DIM)

    if kv_bufs is None:
        dil_parts = _dil_prompt(p_dil)
        kv_new = [kv_rows(g, min(win, t)) for g, (win, _) in enumerate(DIL_PATTERNS)]
    else:
        dil_parts = [_dil_step(p_dil, kv_bufs, l).reshape(m, WIDTH)]
        kv_new = [kv_rows(g, t) for g in range(DIL_GROUPS)]
    branches = [x.reshape(m, WIDTH) for x in (o_a, o_b, o_c)] + dil_parts
    x1 = _merge(hf, branches, w_in_b, w_branch_b, w_out_b, l, ln_g, ln_b, tm)
    return x1, (wkv_new, p_rw[:, -1], pool_new, ret_new, kv_new[0], kv_new[1], kv_new[2])


def kernel(x_prompt, x_sample, state_wkv, state_shift, state_pool, state_ret, cache_kv_w128, cache_kv_w512, cache_kv_w2048, w_in, rw_mu, rw_w0, rw_w2, rw_a0, rw_a2, rw_g2, rw_kk, rw_ka, rw_rk, rw_gn_g, rw_gn_b, pool_w, pool_scale, ret_gn_g, ret_gn_b, w_branch, w_out, ln_g, ln_b, ffn_w_gate, ffn_w_up, ffn_w_down, moe_router, moe_w_gate, moe_w_up, moe_w_down):
    hp, hs = x_prompt, x_sample
    bp, tp, _ = hp.shape
    bs, ts, _ = hs.shape
    names = ('rw_mu', 'rw_w0', 'rw_w2', 'rw_a0', 'rw_a2', 'rw_g2', 'rw_kk', 'rw_ka', 'rw_rk', 'rw_gn_g',
             'rw_gn_b', 'pool_w', 'pool_scale', 'ret_gn_g', 'ret_gn_b')
    stacked = (rw_mu, rw_w0, rw_w2, rw_a0, rw_a2, rw_g2, rw_kk, rw_ka, rw_rk, rw_gn_g, rw_gn_b, pool_w,
               pool_scale, ret_gn_g, ret_gn_b)
    new_p = [[] for _ in range(7)]
    new_s = [[] for _ in range(7)]
    zeros = lambda *shape: jnp.zeros(shape, F32)
    w_in_b, wb, wo = w_in.astype(BF16), w_branch.astype(BF16), w_out.astype(BF16)
    caches = [jnp.transpose(c, (0, 1, 3, 4, 5, 2)) for c in (cache_kv_w128, cache_kv_w512, cache_kv_w2048)]
    for l in range(DEPTH):
        lw = {k: v[l] for k, v in zip(names, stacked)}
        xp, st_p = _token_mix(hp, 0, zeros(bp, RW_COLS), zeros(bp, N_HEADS, HEAD_DIM, HEAD_DIM),
                              zeros(bp, POOL_BUF, WIDTH), zeros(bp, N_HEADS, HEAD_DIM, HEAD_DIM), None,
                              lw, l, w_in_b, wb, wo, ln_g[l, 0], ln_b[l, 0])
        xs, st_s = _token_mix(hs, 8192, state_shift[l], state_wkv[l], state_pool[l], state_ret[l], caches,
                              lw, l, w_in_b, wb, wo, ln_g[l, 0], ln_b[l, 0])
        j = l // 2
        if l % 2 == 0:
            ws = [w[j].astype(BF16) for w in (ffn_w_gate, ffn_w_up, ffn_w_down)]
            xp = _ffn(xp, *ws, ln_g[l, 1], ln_b[l, 1], 512)
            xs = _ffn(xs, *ws, ln_g[l, 1], ln_b[l, 1], xs.shape[0])
        else:
            ws = [_moe_weight_slices(moe_w_gate[j]), _moe_weight_slices(moe_w_up[j]), moe_w_down[j].astype(BF16)]
            xp = _moe(xp, moe_router[j], *ws, ln_g[l, 1], ln_b[l, 1], 1024)
            xs = _moe(xs, moe_router[j], *ws, ln_g[l, 1], ln_b[l, 1], xs.shape[0])
        hp, hs = xp.reshape(bp, tp, D_MODEL), xs.reshape(bs, ts, D_MODEL)
        for i in range(7):
            new_p[i].append(st_p[i])
            new_s[i].append(st_s[i])
    outs_p = [jnp.stack(x) for x in new_p]
    outs_s = [jnp.stack(x) for x in new_s]
    return (hp, hs, *outs_p, *outs_s)
```

```python
import functools
import math

import jax
import jax.numpy as jnp
from jax import lax
from jax.experimental import pallas as pl
from jax.experimental.pallas import tpu as pltpu

F32 = jnp.float32
BF16 = jnp.bfloat16

D_MODEL = 1024
DEPTH = 2
HEAD_DIM = 64
N_HEADS = 4
WIDTH = N_HEADS * HEAD_DIM
RW_COLS = 1024
RW_GN_EPS = 64e-5
POOL_WINDOWS = (2, 4, 8, 16)
POOL_BUF = 15
POOL_HALO = max(POOL_WINDOWS)
RET_CHUNK = 128
ROPE_BASE = 10000.0
DIL_PATTERNS = ((128, 1), (512, 4), (2048, 16))
DIL_GROUPS = 3
DIL_BLOCK = 128
DIL_SPAN = 2048
DIL_COLS = 3 * DIL_GROUPS * WIDTH
COL_POOL = RW_COLS
COL_RET = COL_POOL + WIDTH
COL_DIL = COL_RET + 4 * WIDTH
COL_GATE = COL_DIL + DIL_COLS
N_EXPERTS = 8
DN_ALPHA = (2 * DEPTH) ** 0.25
LN_EPS = 1e-5
RW_CHUNK = 64
PAST_LEN = 8192
SUBLANES = 8
ROW_TILE = 512
PROJ_COL_TILE = 512
MOE_BLOCK = 1024
MOE_TILE = 288
RW_P = 1
RW_BATCH = 4
RW_CHUNKS_PER_STEP = 2
RW_LANE_GROUPS = 2
NEG_BIG = -1e30

NN = (((1,), (0,)), ((), ()))
NT = (((1,), (1,)), ((), ()))
TN = (((0,), (0,)), ((), ()))

VMEM_LIMIT = 56 * 1024 * 1024
MOE_VMEM_LIMIT = 61 * 1024 * 1024


def _split(x, n):
    if x.dtype == BF16:
        return [x]
    parts, rem = [], x
    for i in range(n):
        p = rem.astype(BF16)
        parts.append(p)
        if i + 1 < n:
            rem = rem - p.astype(F32)
    return parts


def _mm(a, b, dims=NN, pa=1, pb=1):
    a_parts, b_parts = _split(a, pa), _split(b, pb)
    depth = max(len(a_parts), len(b_parts))
    acc = None
    for i, ai in enumerate(a_parts):
        for j, bj in enumerate(b_parts):
            if i + j < depth:
                t = lax.dot_general(ai, bj, dims, preferred_element_type=F32)
                acc = t if acc is None else acc + t
    return acc


def _sigmoid(x):
    return 0.5 * jnp.tanh(0.5 * x) + 0.5


def _silu(x):
    return x * _sigmoid(x)


def _iota(shape, axis):
    return lax.broadcasted_iota(jnp.int32, shape, axis)


def _div(x, d):
    assert d & (d - 1) == 0
    return x >> (d.bit_length() - 1)


def _mod(x, d):
    assert d & (d - 1) == 0
    return x & (d - 1)


def _per_head(head, values):
    out = jnp.full(head.shape, values[-1], F32)
    for h in range(len(values) - 2, -1, -1):
        out = jnp.where(head == h, values[h], out)
    return out


def _head_mask(rows_per_head, n_rows):
    return _div(_iota((n_rows, WIDTH), 0), rows_per_head) == _div(_iota((n_rows, WIDTH), 1), HEAD_DIM)


def _stack_heads(x, mask):
    return jnp.where(mask, jnp.concatenate([x] * N_HEADS, axis=0), 0.0)


def _unstack_heads(x_st, mask, c):
    x_st = jnp.where(mask, x_st, 0.0)
    out = x_st[0:c]
    for h in range(1, N_HEADS):
        out = out + x_st[h * c:(h + 1) * c]
    return out


def _ones_bd():
    return _same_head().astype(BF16)


def _same_head():
    return _div(_iota((WIDTH, WIDTH), 0), HEAD_DIM) == _div(_iota((WIDTH, WIDTH), 1), HEAD_DIM)


def _head_norm(x, ones_bd, g, b, eps):
    mu = _mm(x, ones_bd, pa=2) * (1.0 / HEAD_DIM)
    d = x - mu
    var = _mm(d * d, ones_bd, pa=2) * (1.0 / HEAD_DIM)
    return d * lax.rsqrt(var + eps) * g + b


def _layer_norm(x, g, b):
    mu = jnp.mean(x, axis=-1, keepdims=True)
    d = x - mu
    var = jnp.mean(d * d, axis=-1, keepdims=True)
    return d * lax.rsqrt(var + LN_EPS) * g + b


def _resident(shape):
    nd = len(shape)
    return pl.BlockSpec(shape, lambda *_: (0,) * nd, pipeline_mode=pl.Buffered(1))


def _proj_kernel(x_ref, w_ref, rw_ref, pool_ref, ret_ref, dil_ref):
    xb = x_ref[...].astype(BF16)
    for ref, lo, hi in ((rw_ref, 0, COL_POOL), (pool_ref, COL_POOL, COL_RET),
                        (ret_ref, COL_RET, COL_DIL), (dil_ref, COL_DIL, COL_GATE)):
        for s in range(lo, hi, PROJ_COL_TILE):
            e = min(s + PROJ_COL_TILE, hi)
            ref[:, s - lo:e - lo] = _mm(xb, w_ref[:, s:e])


def _layer_resident(w, l):
    nd = w.ndim - 1
    return pl.BlockSpec((None,) + w.shape[1:], lambda *_: (l,) + (0,) * nd, pipeline_mode=pl.Buffered(1))


def _project(x, w_in_b, l, tm):
    m = x.shape[0]
    widths = (COL_POOL, WIDTH, 4 * WIDTH, DIL_COLS)
    row = lambda w: pl.BlockSpec((tm, w), lambda i: (i, 0))
    return pl.pallas_call(
        _proj_kernel,
        grid=(m // tm,),
        in_specs=[row(D_MODEL), _layer_resident(w_in_b, l)],
        out_specs=[row(w) for w in widths],
        out_shape=[jax.ShapeDtypeStruct((m, w), F32) for w in widths],
        compiler_params=pltpu.CompilerParams(dimension_semantics=("parallel",), vmem_limit_bytes=VMEM_LIMIT),
        name="proj",
    )(x, w_in_b)


BNT = (((2,), (2,)), ((0,), (0,)))
BNN = (((2,), (1,)), ((0,), (0,)))
BTN = (((1,), (1,)), ((0,), (0,)))


def _rwkv_kernel(p_ref, prev_ref, s0_ref, mu_ref, w0_ref, w2_ref, a0_ref, a2_ref, g2_ref, kk_ref, ka_ref,
                 rk_ref, gng_ref, gnb_ref, o_ref, st_ref, s_scr, prev_scr, buf_scr, *, c, t_in, nb, nsub):
    ci = pl.program_id(1)
    rows = nsub * c

    halves = RW_LANE_GROUPS
    hw, hpg = WIDTH // halves, N_HEADS // halves
    n = hpg * c
    same_grp = _div(_iota((hw, hw), 0), HEAD_DIM) == _div(_iota((hw, hw), 1), HEAD_DIM)

    @pl.when(ci == 0)
    def _():
        for g in range(halves):
            part = s0_ref[:, g * hw:(g + 1) * hw, :]
            s_scr[g * nb:(g + 1) * nb] = jnp.where(same_grp, jnp.concatenate([part] * hpg, axis=-1), 0.0)
        for b in range(nb):
            prev_scr[b, 0:1, :] = prev_ref[b]

    first = _iota((rows, RW_COLS), 0) == 0
    ps, shs = [], []
    for b in range(nb):
        if t_in == c:
            pb_ = p_ref[b]
        else:
            buf_scr[b] = jnp.zeros((c, RW_COLS), F32)
            buf_scr[b, 0:t_in, :] = p_ref[b]
            pb_ = buf_scr[b]
        shs.append(jnp.where(first, prev_scr[b, 0:1, :], pltpu.roll(pb_, 1, 0)))
        prev_scr[b, 0:1, :] = pb_[rows - 1:rows, :]
        ps.append(pb_)
    p, shifted = jnp.concatenate(ps, axis=0), jnp.concatenate(shs, axis=0)
    m = nb * rows
    u = p + (shifted - p) * mu_ref[...]
    r, k, v, ul = u[:, 0:WIDTH], u[:, WIDTH:2 * WIDTH], u[:, 2 * WIDTH:3 * WIDTH], u[:, 3 * WIDTH:]

    w_pre = w0_ref[...] + _mm(jnp.tanh(ul), w2_ref[...])
    logw = -math.exp(-0.5) * _sigmoid(w_pre)
    a = _sigmoid(a0_ref[...] + _mm(ul, a2_ref[...]))
    g = _mm(_sigmoid(ul), g2_ref[...])

    ones_bd = _ones_bd()
    kk = k * kk_ref[...]
    kk = kk * lax.rsqrt(jnp.maximum(_mm(kk * kk, ones_bd, pa=2), 1e-24))
    k_mod = k * (1.0 + (a - 1.0) * ka_ref[...])
    a_vec, b_vec = -kk, kk * a
    if t_in < c:
        live = _mod(_iota((m, WIDTH), 0), c) < t_in
        logw = jnp.where(live, logw, 0.0)
        a_vec, b_vec = jnp.where(live, a_vec, 0.0), jnp.where(live, b_vec, 0.0)
        k_mod, v = jnp.where(live, k_mod, 0.0), jnp.where(live, v, 0.0)

    qi, qj = _iota((m, m), 0), _iota((m, m), 1)
    tri = ((_div(qi, c) == _div(qj, c)) & (qi >= qj)).astype(BF16)
    cum2 = _mm(tri, logw, pb=3)
    def chunk_rows(x, s):
        return x if nsub == 1 else x.reshape(nb, nsub, c, x.shape[-1])[:, s].reshape(nb * c, x.shape[-1])

    def seq(x):
        parts = []
        for s in range(nsub):
            x3 = chunk_rows(x, s).reshape(nb, c, WIDTH)
            parts += [x3[:, :, g * hw:(g + 1) * hw] for g in range(halves)]
        return jnp.concatenate(parts, axis=0)

    cum, lw3 = seq(cum2), seq(logw)
    cum_end = cum[:, c - 1:c, :]
    e_neg, e_end = jnp.exp(-cum), jnp.exp(cum_end - cum)
    a_t = seq(a_vec) * jnp.exp(cum - lw3)
    r_t = seq(r) * jnp.exp(cum)
    b_t, k_t = seq(b_vec) * e_neg, seq(k_mod) * e_neg
    b_e, k_e = seq(b_vec) * e_end, seq(k_mod) * e_end
    v3 = seq(v)

    hm = _div(_iota((n, hw), 0), c) == _div(_iota((n, hw), 1), HEAD_DIM)
    stack = lambda x: jnp.where(hm, jnp.concatenate([x] * hpg, axis=1), 0.0)
    a_st, r_st, b_st, v_st = stack(a_t), stack(r_t), stack(b_t), stack(v3)
    ri, rj = _iota((n, n), 0), _iota((n, n), 1)
    same = _div(ri, c) == _div(rj, c)
    strict_bd = same & (_mod(ri, c) > _mod(rj, c))
    incl_bd = same & (_mod(ri, c) >= _mod(rj, c))
    ti, tj = _mod(_iota((n, c), 0), c), _iota((n, c), 1)
    strict_st, incl_st = ti > tj, ti >= tj

    mm = functools.partial(_mm, pa=RW_P, pb=RW_P)
    a_ab = jnp.where(strict_bd, mm(a_st, b_st, BNT), 0.0)
    inv = jnp.where(ri == rj, 1.0, 0.0) + a_ab
    pw = a_ab
    for _ in range(int(math.log2(c)) - 1):
        pw = mm(pw, pw, BNN)
        inv = inv + mm(inv, pw, BNN)
    a_ak = jnp.where(strict_st, mm(a_st, k_t, BNT), 0.0)
    z_st = jnp.where(hm, mm(a_ak, v3, BNN), 0.0)
    wu = mm(inv, jnp.concatenate([a_st, z_st], axis=2), BNN)
    w_st, u0_st = wu[:, :, 0:hw], wu[:, :, hw:]

    rhs = jnp.concatenate([stack(b_e), stack(k_e)], axis=1)
    a_rb = jnp.where(incl_bd, mm(r_st, b_st, BNT), 0.0)
    a_rk = jnp.where(incl_st, mm(r_st, k_t, BNT), 0.0)
    o_in = mm(a_rk, v3, BNN)
    decay_end = jnp.exp(cum_end)

    per = halves * nb
    s0 = s_scr[...]
    for s in range(nsub):
        sl = slice(s * per, (s + 1) * per)
        u_st = mm(w_st[sl], s0, BNT) + u0_st[sl]
        o_st = jnp.where(hm, mm(r_st[sl], s0, BNT) + mm(a_rb[sl], u_st, BNN) + o_in[sl], 0.0)
        s0 = s0 * decay_end[sl] + mm(jnp.concatenate([u_st, v_st[sl]], axis=1), rhs[sl], BTN)
        o3 = o_st[:, 0:c]
        for h in range(1, hpg):
            o3 = o3 + o_st[:, h * c:(h + 1) * c]
        o = jnp.concatenate([o3[g * nb:(g + 1) * nb] for g in range(halves)], axis=2).reshape(nb * c, WIDTH)
        o = _head_norm(o, ones_bd, gng_ref[...], gnb_ref[...], RW_GN_EPS)
        v_s = chunk_rows(v, s)
        o = o + _mm(chunk_rows(r, s) * chunk_rows(k_mod, s) * rk_ref[...], ones_bd, pa=2) * v_s
        o = (o * chunk_rows(g, s)).reshape(nb, c, WIDTH)
        if t_in == c:
            o_ref[:, s * c:(s + 1) * c, :] = o
        else:
            o_ref[...] = o[:, 0:t_in]
    s_scr[...] = s0

    @pl.when(ci == pl.num_programs(1) - 1)
    def _():
        for g in range(halves):
            blk = s_scr[g * nb:(g + 1) * nb]
            part = blk[..., 0:HEAD_DIM]
            for h in range(1, hpg):
                part = part + blk[..., h * HEAD_DIM:(h + 1) * HEAD_DIM]
            st_ref[:, g * hw:(g + 1) * hw, :] = part


def _embed_heads(s):
    return jnp.where(_same_head(), jnp.concatenate([s] * N_HEADS, axis=-1), 0.0)


def _extract_heads(s_bd):
    out = s_bd[..., 0:HEAD_DIM]
    for h in range(1, N_HEADS):
        out = out + s_bd[..., h * HEAD_DIM:(h + 1) * HEAD_DIM]
    return out


def _rwkv_branch(p, p_prev, wkv0, lw):
    b, t, _ = p.shape
    c = RW_CHUNK if t % RW_CHUNK == 0 else RW_CHUNK // 2
    assert t % c == 0 or t <= c
    t_in = c if t % c == 0 else t
    nc = t // c if t % c == 0 else 1
    vec = lambda x: x.reshape(1, -1)
    pad_rows = lambda w, lo: jnp.zeros((WIDTH, WIDTH), F32).at[lo:lo + w.shape[0]].set(w).astype(BF16)
    params = [vec(lw['rw_mu']), vec(lw['rw_w0']), pad_rows(lw['rw_w2'], 0), vec(lw['rw_a0']),
              pad_rows(lw['rw_a2'], 64), pad_rows(lw['rw_g2'], 128), vec(lw['rw_kk']), vec(lw['rw_ka']),
              vec(lw['rw_rk']), vec(lw['rw_gn_g']), vec(lw['rw_gn_b'])]
    nb = RW_BATCH
    nsub = RW_CHUNKS_PER_STEP if nc % RW_CHUNKS_PER_STEP == 0 else 1
    rows = nsub * t_in
    assert b % nb == 0
    o, st = pl.pallas_call(
        functools.partial(_rwkv_kernel, c=c, t_in=t_in, nb=nb, nsub=nsub),
        grid=(b // nb, nc // nsub),
        in_specs=[pl.BlockSpec((nb, rows, RW_COLS), lambda i, j: (i, j, 0)),
                  pl.BlockSpec((nb, 1, RW_COLS), lambda i, j: (i, 0, 0)),
                  pl.BlockSpec((nb, WIDTH, HEAD_DIM), lambda i, j: (i, 0, 0))]
                 + [_resident(x.shape) for x in params],
        out_specs=[pl.BlockSpec((nb, rows, WIDTH), lambda i, j: (i, j, 0)),
                   pl.BlockSpec((nb, WIDTH, HEAD_DIM), lambda i, j: (i, 0, 0))],
        out_shape=[jax.ShapeDtypeStruct((b, t, WIDTH), F32), jax.ShapeDtypeStruct((b, WIDTH, HEAD_DIM), F32)],
        scratch_shapes=[pltpu.VMEM((RW_LANE_GROUPS * nb, WIDTH // RW_LANE_GROUPS, WIDTH // RW_LANE_GROUPS), F32),
                        pltpu.VMEM((nb, 8, RW_COLS), F32),
                        pltpu.VMEM((nb, c, RW_COLS), F32)],
        compiler_params=pltpu.CompilerParams(dimension_semantics=("parallel", "arbitrary"),
                                             vmem_limit_bytes=VMEM_LIMIT),
        name="rwkv7",
    )(p, p_prev.reshape(b, 1, RW_COLS), wkv0.reshape(b, WIDTH, HEAD_DIM), *params)
    return o, st.reshape(b, N_HEADS, HEAD_DIM, HEAD_DIM)


def _pool_rows(ext_scr, c, pos_first, w, scale):
    x = ext_scr[POOL_HALO:POOL_HALO + c, :]
    sums, acc, off = [], x, 1
    for win in POOL_WINDOWS:
        while off < win:
            acc = acc + ext_scr[POOL_HALO - off:POOL_HALO - off + c, :]
            off += 1
        sums.append(acc)
    pos = pos_first + _iota((c, WIDTH), 0)
    grp = _div(_iota((c, WIDTH), 1), HEAD_DIM)
    mean = jnp.zeros((c, WIDTH), F32)
    for gi, win in enumerate(POOL_WINDOWS):
        cnt = jnp.minimum(win, pos + 1).astype(F32)
        mean = jnp.where(grp == gi, sums[gi] / cnt, mean)
    ext_scr[0:POOL_HALO, :] = ext_scr[c:c + POOL_HALO, :]
    return _mm(mean - x, w) * scale


def _pool_kernel(u_ref, buf_ref, w_ref, scale_ref, o_ref, ext_scr, *, c, t_in, pos0):
    ci = pl.program_id(1)

    @pl.when(ci == 0)
    def _():
        ext_scr[0:POOL_HALO, :] = buf_ref[0]

    if t_in < c:
        ext_scr[POOL_HALO:POOL_HALO + c, :] = jnp.zeros((c, WIDTH), F32)
    ext_scr[POOL_HALO:POOL_HALO + t_in, :] = u_ref[0]
    o_ref[0] = _pool_rows(ext_scr, c, pos0 + ci * c, w_ref[...], scale_ref[...])[0:t_in]


def _pool_weights(lw):
    w_bd = jnp.einsum('gcd,gh->gchd', lw['pool_w'], jnp.eye(N_HEADS, dtype=F32)).reshape(WIDTH, WIDTH)
    return w_bd.astype(BF16), lw['pool_scale'].reshape(1, WIDTH)


def _pool_branch(u, buf, pos0, lw):
    b, t, _ = u.shape
    c = ROW_TILE if t % ROW_TILE == 0 else SUBLANES
    t_in = c if t % c == 0 else t
    nc = t // c if t % c == 0 else 1
    w_bd, scale = _pool_weights(lw)
    buf16 = jnp.pad(buf, ((0, 0), (1, 0), (0, 0)))
    return pl.pallas_call(
        functools.partial(_pool_kernel, c=c, t_in=t_in, pos0=pos0),
        grid=(b, nc),
        in_specs=[pl.BlockSpec((1, t_in, WIDTH), lambda i, j: (i, j, 0)),
                  pl.BlockSpec((1, POOL_HALO, WIDTH), lambda i, j: (i, 0, 0)),
                  _resident((WIDTH, WIDTH)), _resident((1, WIDTH))],
        out_specs=pl.BlockSpec((1, t_in, WIDTH), lambda i, j: (i, j, 0)),
        out_shape=jax.ShapeDtypeStruct((b, t, WIDTH), F32),
        scratch_shapes=[pltpu.VMEM((POOL_HALO + c, WIDTH), F32)],
        compiler_params=pltpu.CompilerParams(dimension_semantics=("parallel", "arbitrary")),
        name="pool",
    )(u, buf16, w_bd, scale)


def _rot_half(x):
    first = _mod(_iota(x.shape, 1), HEAD_DIM) < (HEAD_DIM // 2)
    return jnp.where(first, pltpu.roll(x, WIDTH - HEAD_DIM // 2, 1), pltpu.roll(x, HEAD_DIM // 2, 1))


RET_LOG_DECAY = tuple(math.log(1.0 - 2.0 ** (-5.0 - h)) for h in range(N_HEADS))
ALIBI_SLOPES = tuple(2.0 ** (-8.0 * (i + 1) / (DIL_GROUPS * N_HEADS)) for i in range(DIL_GROUPS * N_HEADS))


def _ret_kernel(p_ref, cos_ref, sin_ref, s0_ref, gng_ref, gnb_ref, o_ref, st_ref, s_scr, buf_scr,
                dmask_scr, qd_scr, kd_scr, *, c, t_in, nb):
    ci = pl.program_id(1)
    n = N_HEADS * c
    lg = _per_head(_div(_iota((1, WIDTH), 1), HEAD_DIM), RET_LOG_DECAY)

    @pl.when(ci == 0)
    def _():
        s_scr[...] = _embed_heads(s0_ref[...])
        idx = _iota((c, WIDTH), 0).astype(F32)
        qd_scr[...] = jnp.exp(lg * (idx + 1.0))
        kd_scr[...] = jnp.exp(lg * (t_in - 1.0 - idx))
        rel = _mod(_iota((n, c), 0), c) - _iota((n, c), 1)
        lg_rows = _per_head(_div(_iota((n, c), 0), c), RET_LOG_DECAY)
        dmask_scr[...] = jnp.where(rel >= 0, jnp.exp(lg_rows * jnp.maximum(rel, 0).astype(F32)), 0.0)

    hm = _head_mask(c, n)
    cos, sin = cos_ref[...], sin_ref[...]
    chunk_decay = jnp.exp(lg * float(t_in))
    same_head, ones_bd = _same_head(), _ones_bd()
    for b in range(nb):
        if t_in == c:
            p = p_ref[b]
        else:
            buf_scr[b] = jnp.zeros((c, 4 * WIDTH), F32)
            buf_scr[b, 0:t_in, :] = p_ref[b]
            p = buf_scr[b]
        q, k, v, g = (p[:, i * WIDTH:(i + 1) * WIDTH] for i in range(4))
        q = q * cos + _rot_half(q) * sin
        k = (k * cos + _rot_half(k) * sin) * HEAD_DIM ** -0.5
        if t_in < c:
            live = _iota((c, WIDTH), 0) < t_in
            k, v = jnp.where(live, k, 0.0), jnp.where(live, v, 0.0)
        inner = _mm(_stack_heads(q, hm), k, NT) * dmask_scr[...]
        s0 = s_scr[b]
        o = _unstack_heads(_mm(inner, v), hm, c) + _mm(q * qd_scr[...], s0)
        s_scr[b] = s0 * chunk_decay + jnp.where(same_head, _mm(k * kd_scr[...], v, TN), 0.0)
        o = _silu(g) * _head_norm(o, ones_bd, gng_ref[...], gnb_ref[...], LN_EPS)
        o_ref[b] = o[0:t_in]

    @pl.when(ci == pl.num_programs(1) - 1)
    def _():
        st_ref[...] = _extract_heads(s_scr[...])


def _rope_tables(pos0, t, rows):
    half = HEAD_DIM // 2
    inv = ROPE_BASE ** (-jnp.arange(half, dtype=F32) / half)
    ang = (pos0 + jnp.arange(t, dtype=jnp.int32)).astype(F32)[:, None] * inv[None, :]
    cos = jnp.tile(jnp.cos(ang), (1, 2 * N_HEADS))
    sin = jnp.tile(jnp.concatenate([-jnp.sin(ang), jnp.sin(ang)], axis=1), (1, N_HEADS))
    pad = ((0, rows - t), (0, 0))
    return jnp.pad(cos, pad), jnp.pad(sin, pad)


def _ret_branch(p, s0, pos0, lw):
    b, t, _ = p.shape
    c = RET_CHUNK
    t_in = c if t % c == 0 else t
    nc = t // c if t % c == 0 else 1
    cos, sin = _rope_tables(pos0, t, nc * c)
    nb = RW_BATCH
    assert b % nb == 0
    o, st = pl.pallas_call(
        functools.partial(_ret_kernel, c=c, t_in=t_in, nb=nb),
        grid=(b // nb, nc),
        in_specs=[pl.BlockSpec((nb, t_in, 4 * WIDTH), lambda i, j: (i, j, 0)),
                  pl.BlockSpec((c, WIDTH), lambda i, j: (j, 0)),
                  pl.BlockSpec((c, WIDTH), lambda i, j: (j, 0)),
                  pl.BlockSpec((nb, WIDTH, HEAD_DIM), lambda i, j: (i, 0, 0)),
                  _resident((1, WIDTH)), _resident((1, WIDTH))],
        out_specs=[pl.BlockSpec((nb, t_in, WIDTH), lambda i, j: (i, j, 0)),
                   pl.BlockSpec((nb, WIDTH, HEAD_DIM), lambda i, j: (i, 0, 0))],
        out_shape=[jax.ShapeDtypeStruct((b, t, WIDTH), F32), jax.ShapeDtypeStruct((b, WIDTH, HEAD_DIM), F32)],
        scratch_shapes=[pltpu.VMEM((nb, WIDTH, WIDTH), F32), pltpu.VMEM((nb, c, 4 * WIDTH), F32),
                        pltpu.VMEM((N_HEADS * c, c), F32), pltpu.VMEM((c, WIDTH), F32), pltpu.VMEM((c, WIDTH), F32)],
        compiler_params=pltpu.CompilerParams(dimension_semantics=("parallel", "arbitrary")),
        name="retention",
    )(p, cos, sin, s0.reshape(b, WIDTH, HEAD_DIM), lw['ret_gn_g'].reshape(1, WIDTH), lw['ret_gn_b'].reshape(1, WIDTH))
    return o, st.reshape(b, N_HEADS, HEAD_DIM, HEAD_DIM)


def _dil_prompt_kernel(*refs, group, dil, span):
    q_refs, kc_refs, kp_refs, vc_refs, vp_refs = (refs[2 * i:2 * i + 2] for i in range(5))
    o_ref, lse_ref, k_scr, v_scr, o_scr, lse_scr = refs[10:]
    blk = DIL_BLOCK
    tail = blk * dil
    si = pl.program_id(1)
    for half in range(2):
        k_scr[half, 0:tail, :] = kp_refs[half][0]
        k_scr[half, tail:tail + span, :] = kc_refs[half][0]
        v_scr[half, 0:tail, :] = vp_refs[half][0]
        v_scr[half, tail:tail + span, :] = vc_refs[half][0]
    ki = _iota((blk, 2 * blk), 1)
    steps = blk + _iota((blk, 2 * blk), 0) - ki
    band = (steps >= 0) & (steps <= blk)
    masks = [jnp.where(band, (-ALIBI_SLOPES[group * N_HEADS + h] * dil) * steps.astype(F32), NEG_BIG)
             for h in range(N_HEADS)]
    sub_head = _div(_iota((blk, 128), 1), HEAD_DIM)
    for cc in range(span // tail):
        has_prev = (si > 0) | (ki >= blk)
        cc_masks = [jnp.where(has_prev, x, NEG_BIG) for x in masks] if cc == 0 else masks
        for r in range(dil):
            rows_q = pl.ds(cc * tail + r, blk, stride=dil) if dil > 1 else pl.ds(cc * tail, blk)
            rows_kv = pl.ds(cc * tail + r, 2 * blk, stride=dil) if dil > 1 else pl.ds(cc * tail, 2 * blk)
            for half in range(2):
                q2 = q_refs[half][0, rows_q, :] * HEAD_DIM ** -0.5
                k2 = k_scr[half, rows_kv, :].astype(BF16)
                v2 = v_scr[half, rows_kv, :].astype(BF16)
                o2 = lse2 = None
                for sub in range(2):
                    mine = sub_head == sub
                    s = _mm(jnp.where(mine, q2, 0.0), k2, NT) + cc_masks[2 * half + sub]
                    m = jnp.max(s, axis=1, keepdims=True)
                    e = jnp.exp(s - m)
                    l = jnp.sum(e, axis=1, keepdims=True)
                    o_h = _mm(e, v2) / l
                    lse_h = jnp.broadcast_to(m + jnp.log(l), (blk, 128))
                    o2 = o_h if o2 is None else jnp.where(mine, o_h, o2)
                    lse2 = lse_h if lse2 is None else jnp.where(mine, lse_h, lse2)
                o_scr[half, rows_q, :] = o2
                lse_scr[half, rows_q, :] = lse2
    o_ref[0] = jnp.concatenate([o_scr[0], o_scr[1]], axis=1)
    lse_ref[0] = jnp.concatenate([lse_scr[0], lse_scr[1]], axis=1)


def _dil_prompt_group(pd, group):
    b, s, _ = pd.shape
    win, dil = DIL_PATTERNS[group]
    span = DIL_SPAN
    tail = DIL_BLOCK * dil
    assert win // dil == DIL_BLOCK and span % tail == 0 and s % span == 0
    base = group * 3

    def cur(col):
        return [pl.BlockSpec((1, span, 128), lambda i, j, h=h: (i, j, 2 * (base + col) + h)) for h in range(2)]

    def prev(col):
        return [pl.BlockSpec((1, tail, 128),
                             lambda i, j, h=h: (i, jnp.maximum(j * (span // tail) - 1, 0), 2 * (base + col) + h))
                for h in range(2)]

    out_spec = pl.BlockSpec((1, span, WIDTH), lambda i, j: (i, j, 0))
    return pl.pallas_call(
        functools.partial(_dil_prompt_kernel, group=group, dil=dil, span=span),
        grid=(b, s // span),
        in_specs=cur(0) + cur(1) + prev(1) + cur(2) + prev(2),
        out_specs=[out_spec, out_spec],
        out_shape=[jax.ShapeDtypeStruct((b, s, WIDTH), F32)] * 2,
        scratch_shapes=[pltpu.VMEM((2, tail + span, 128), F32)] * 2 + [pltpu.VMEM((2, span, 128), F32)] * 2,
        compiler_params=pltpu.CompilerParams(dimension_semantics=("parallel", "arbitrary"),
                                             vmem_limit_bytes=VMEM_LIMIT),
        name=f"dil_prompt_g{group}",
    )(*([pd] * 10))


def _dil_combine(os_, ls):
    m = jnp.maximum(jnp.maximum(ls[0], ls[1]), ls[2])
    es = [jnp.exp(x - m) for x in ls]
    return (es[0] * os_[0] + es[1] * os_[1] + es[2] * os_[2]) / (es[0] + es[1] + es[2])


def _dil_prompt(pd):
    b, s, _ = pd.shape
    outs = [_dil_prompt_group(pd, g) for g in range(DIL_GROUPS)]
    return [x[0].reshape(b * s, WIDTH) for x in outs] + [x[1].reshape(b * s, WIDTH) for x in outs]


def _dil_step_kernel(pd_ref, c0_ref, c1_ref, c2_ref, o_ref, buf_scr, *, t, tp):
    buf_scr[...] = jnp.zeros_like(buf_scr)
    buf_scr[0:t, :] = pd_ref[0]
    pd = buf_scr[...]
    qt = _iota((tp, 1), 0)
    d_new = qt - _iota((tp, tp), 1)
    outs = [[None] * DIL_GROUPS for _ in range(N_HEADS)]
    lses = [[None] * DIL_GROUPS for _ in range(N_HEADS)]
    for g, cache_ref in enumerate((c0_ref, c1_ref, c2_ref)):
        win, dil = DIL_PATTERNS[g]
        length = cache_ref.shape[-1]
        d_old = length + qt - _iota((tp, length), 1)
        ok_old = (_mod(d_old, dil) == 0) & (d_old <= win)
        ok_new = (d_new >= 0) & (_mod(d_new, dil) == 0)
        for h in range(N_HEADS):
            slope = ALIBI_SLOPES[g * N_HEADS + h]
            lo = g * 3 * WIDTH + h * HEAD_DIM
            q = pd[:, lo:lo + HEAD_DIM] * HEAD_DIM ** -0.5
            k_new = pd[:, lo + WIDTH:lo + WIDTH + HEAD_DIM]
            v_new = pd[:, lo + 2 * WIDTH:lo + 2 * WIDTH + HEAD_DIM]
            k_t, v_t = cache_ref[0, 0, h], cache_ref[0, 1, h]
            s_old = jnp.where(ok_old, _mm(q, k_t) - slope * d_old.astype(F32), NEG_BIG)
            s_new = jnp.where(ok_new, _mm(q, k_new, NT) - slope * d_new.astype(F32), NEG_BIG)
            m = jnp.maximum(jnp.max(s_old, axis=1, keepdims=True), jnp.max(s_new, axis=1, keepdims=True))
            e_old, e_new = jnp.exp(s_old - m), jnp.exp(s_new - m)
            l = jnp.sum(e_old, axis=1, keepdims=True) + jnp.sum(e_new, axis=1, keepdims=True)
            outs[h][g] = (_mm(e_old, v_t, NT) + _mm(e_new, v_new)) / l
            lses[h][g] = m + jnp.log(l)
    o = jnp.concatenate([_dil_combine(outs[h], lses[h]) for h in range(N_HEADS)], axis=1)
    o_ref[0] = o[0:t]


def _dil_step(pd, caches, l):
    b, t, _ = pd.shape
    tp = SUBLANES
    flat = caches
    return pl.pallas_call(
        functools.partial(_dil_step_kernel, t=t, tp=tp),
        grid=(b,),
        in_specs=[pl.BlockSpec((1, t, DIL_COLS), lambda i: (i, 0, 0))]
                 + [pl.BlockSpec((None, 1) + c.shape[2:], lambda i: (l, i, 0, 0, 0, 0)) for c in flat],
        out_specs=pl.BlockSpec((1, t, WIDTH), lambda i: (i, 0, 0)),
        out_shape=jax.ShapeDtypeStruct((b, t, WIDTH), F32),
        scratch_shapes=[pltpu.VMEM((tp, DIL_COLS), F32)],
        compiler_params=pltpu.CompilerParams(dimension_semantics=("parallel",), vmem_limit_bytes=VMEM_LIMIT),
        name="dil_step",
    )(pd, *flat)


def _merge_kernel(h_ref, *refs):
    wg_ref, wb_ref, wo_ref, g_ref, b_ref, out_ref = refs[-6:]
    branch_refs = refs[:-6]
    branches = [r[...] for r in branch_refs[:3]]
    if len(branch_refs) == 4:
        branches.append(branch_refs[3][...])
    else:
        branches.append(_dil_combine([r[...] for r in branch_refs[3:6]], [r[...] for r in branch_refs[6:9]]))
    h = h_ref[...]
    hb = h.astype(BF16)
    z = None
    for n, o in enumerate(branches):
        gate = _sigmoid(_mm(hb, wg_ref[:, COL_GATE + n * D_MODEL:COL_GATE + (n + 1) * D_MODEL]))
        term = gate * _mm(o, wb_ref[n])
        z = term if z is None else z + term
    y = _mm(z, wo_ref[...])
    out_ref[...] = _layer_norm(DN_ALPHA * h + y, g_ref[...], b_ref[...])


def _merge(h, branches, w_in_b, w_branch_b, w_out_b, l, ln_g, ln_b, tm):
    m = h.shape[0]
    row = lambda w: pl.BlockSpec((tm, w), lambda i: (i, 0))
    return pl.pallas_call(
        _merge_kernel,
        grid=(m // tm,),
        in_specs=[row(D_MODEL)] + [row(WIDTH)] * len(branches)
                 + [_layer_resident(w_in_b, l), _layer_resident(w_branch_b, l), _layer_resident(w_out_b, l),
                    _resident((1, D_MODEL)), _resident((1, D_MODEL))],
        out_specs=row(D_MODEL),
        out_shape=jax.ShapeDtypeStruct((m, D_MODEL), F32),
        compiler_params=pltpu.CompilerParams(dimension_semantics=("parallel",), vmem_limit_bytes=VMEM_LIMIT),
        name="merge_ln",
    )(h, *branches, w_in_b, w_branch_b, w_out_b, ln_g.reshape(1, -1), ln_b.reshape(1, -1))


def _ffn_kernel(x_ref, wg_ref, wu_ref, wd_ref, g_ref, b_ref, out_ref):
    x = x_ref[...]
    xb = x.astype(BF16)
    act = _silu(_mm(xb, wg_ref[...])) * _mm(xb, wu_ref[...])
    out_ref[...] = _layer_norm(DN_ALPHA * x + _mm(act, wd_ref[...]), g_ref[...], b_ref[...])


def _ffn(x, wg, wu, wd, ln_g, ln_b, tm):
    m = x.shape[0]
    row = pl.BlockSpec((tm, D_MODEL), lambda i: (i, 0))
    return pl.pallas_call(
        _ffn_kernel,
        grid=(m // tm,),
        in_specs=[row, _resident(wg.shape), _resident(wu.shape), _resident(wd.shape),
                  _resident((1, D_MODEL)), _resident((1, D_MODEL))],
        out_specs=row,
        out_shape=jax.ShapeDtypeStruct((m, D_MODEL), F32),
        compiler_params=pltpu.CompilerParams(dimension_semantics=("parallel",), vmem_limit_bytes=VMEM_LIMIT),
        name="ffn_ln",
    )(x, wg, wu, wd, ln_g.reshape(1, -1), ln_b.reshape(1, -1))


def _route_kernel(x_ref, rt_ref, tri_ref, gate_ref, rank_ref, cnt_ref):
    logits = _mm(rt_ref[...], x_ref[...], NT, 2, 2)
    sub = _iota(logits.shape, 0).astype(F32)
    m1 = jnp.max(logits, axis=0, keepdims=True)
    i1 = jnp.min(jnp.where(logits == m1, sub, float(N_EXPERTS)), axis=0, keepdims=True)
    rest = jnp.where(sub == i1, NEG_BIG, logits)
    m2 = jnp.max(rest, axis=0, keepdims=True)
    i2 = jnp.min(jnp.where(rest == m2, sub, float(N_EXPERTS)), axis=0, keepdims=True)
    e2 = jnp.exp(m2 - m1)
    gate_ref[0] = jnp.where(sub == i1, 1.0 / (1.0 + e2), 0.0) + jnp.where(sub == i2, e2 / (1.0 + e2), 0.0)
    chosen = (sub == i1) | (sub == i2)
    sel = jnp.where(chosen, 1.0, 0.0)
    rank_ref[0] = jnp.where(chosen, _mm(sel, tri_ref[...]), -1.0)
    cnt_ref[0] = jnp.broadcast_to(jnp.sum(sel, axis=1, keepdims=True), cnt_ref.shape[1:])


def _moe_kernel(cnt_ref, x_ref, gate_ref, rank_ref, wg_ref, wu_ref, wd_ref, g_ref, b_ref, out_ref,
                xb_scr, xg_scr, yg_scr, *, ts, tb, nsb):
    i, e, f = pl.program_id(0), pl.program_id(1), pl.program_id(2)
    last_f = f == pl.num_programs(2) - 1

    @pl.when((e == 0) & (f == 0))
    def _():
        xb_scr[...] = x_ref[...].astype(BF16)
        out_ref[...] = jnp.zeros_like(out_ref)

    slot = _iota((ts, tb), 0)

    def expert(xg):
        return _mm(_silu(_mm(xg, wg_ref[0])) * _mm(xg, wu_ref[0]), wd_ref[0])

    def pick_of(sb, j):
        rank_row = rank_ref[sb, pl.ds(e, 1), :]
        return jnp.where(rank_row == (slot + j * ts).astype(F32), 1.0, 0.0)

    def scatter(sb, pick, y):
        w_slot = jnp.sum(pick * gate_ref[sb, pl.ds(e, 1), :], axis=1, keepdims=True)
        out_ref[sb * tb:(sb + 1) * tb, :] += _mm(pick, y * w_slot, TN)

    @pl.when(f == 0)
    def _():
        for sb in range(nsb):
            xg_scr[sb] = _mm(pick_of(sb, 0), xb_scr[sb * tb:(sb + 1) * tb, :]).astype(BF16)
        yg_scr[...] = jnp.zeros_like(yg_scr)

    yg_scr[...] += expert(xg_scr[...].reshape(nsb * ts, D_MODEL)).reshape(nsb, ts, D_MODEL)

    @pl.when(last_f)
    def _():
        for sb in range(nsb):
            scatter(sb, pick_of(sb, 0), yg_scr[sb])

    for sb in range(nsb):
        n_tiles = (cnt_ref[(i * nsb + sb) * N_EXPERTS + e] + (ts - 1)) // ts

        def overflow(j, carry, sb=sb):
            pick = pick_of(sb, j)
            scatter(sb, pick, expert(_mm(pick, xb_scr[sb * tb:(sb + 1) * tb, :]).astype(BF16)))
            return carry

        lax.fori_loop(1, n_tiles, overflow, 0)

    @pl.when((e == pl.num_programs(1) - 1) & last_f)
    def _():
        out_ref[...] = _layer_norm(DN_ALPHA * x_ref[...] + out_ref[...], g_ref[...], b_ref[...])


def _moe(x, router, wg, wu, wd, ln_g, ln_b, tm):
    m = x.shape[0]
    nblk = m // tm
    dff = wg.shape[2]
    nf = 2
    tf = dff // nf
    ts = min(MOE_TILE, tm)
    nsb = 2 if nblk % 2 == 0 else 1
    assert m % tm == 0 and tf % 128 == 0
    tri = (jnp.arange(tm)[:, None] < jnp.arange(tm)[None, :]).astype(BF16)
    gate, rank, cnt = pl.pallas_call(
        _route_kernel,
        grid=(nblk,),
        in_specs=[pl.BlockSpec((tm, D_MODEL), lambda i: (i, 0)), _resident((N_EXPERTS, D_MODEL)),
                  _resident((tm, tm))],
        out_specs=[pl.BlockSpec((1, N_EXPERTS, tm), lambda i: (i, 0, 0)),
                   pl.BlockSpec((1, N_EXPERTS, tm), lambda i: (i, 0, 0)),
                   pl.BlockSpec((1, N_EXPERTS, 128), lambda i: (i, 0, 0))],
        out_shape=[jax.ShapeDtypeStruct((nblk, N_EXPERTS, tm), F32), jax.ShapeDtypeStruct((nblk, N_EXPERTS, tm), F32),
                   jax.ShapeDtypeStruct((nblk, N_EXPERTS, 128), F32)],
        compiler_params=pltpu.CompilerParams(dimension_semantics=("parallel",), vmem_limit_bytes=VMEM_LIMIT),
        name="moe_route",
    )(x, router.T, tri)
    counts = cnt[:, :, 0].astype(jnp.int32).reshape(-1)
    rows = nsb * tm
    x_in = pl.BlockSpec((rows, D_MODEL), lambda i, e, f, c: (i, 0), pipeline_mode=pl.Buffered(1))
    meta = pl.BlockSpec((nsb, N_EXPERTS, tm), lambda i, e, f, c: (i, 0, 0))
    vec = pl.BlockSpec((1, D_MODEL), lambda i, e, f, c: (0, 0), pipeline_mode=pl.Buffered(1))
    return pl.pallas_call(
        functools.partial(_moe_kernel, ts=ts, tb=tm, nsb=nsb),
        grid_spec=pltpu.PrefetchScalarGridSpec(
            num_scalar_prefetch=1,
            grid=(nblk // nsb, N_EXPERTS, nf),
            in_specs=[x_in, meta, meta,
                      pl.BlockSpec((1, D_MODEL, tf), lambda i, e, f, c: (e, 0, f)),
                      pl.BlockSpec((1, D_MODEL, tf), lambda i, e, f, c: (e, 0, f)),
                      pl.BlockSpec((1, tf, D_MODEL), lambda i, e, f, c: (e, f, 0)),
                      vec, vec],
            out_specs=pl.BlockSpec((rows, D_MODEL), lambda i, e, f, c: (i, 0), pipeline_mode=pl.Buffered(1)),
            scratch_shapes=[pltpu.VMEM((rows, D_MODEL), BF16),
                            pltpu.VMEM((nsb, ts, D_MODEL), BF16), pltpu.VMEM((nsb, ts, D_MODEL), F32)]),
        out_shape=jax.ShapeDtypeStruct((m, D_MODEL), F32),
        compiler_params=pltpu.CompilerParams(dimension_semantics=("parallel", "arbitrary", "arbitrary"),
                                             vmem_limit_bytes=MOE_VMEM_LIMIT),
        name="moe_ln",
    )(counts, x, gate, rank, wg, wu, wd, ln_g.reshape(1, -1), ln_b.reshape(1, -1))


def _token_mix(h, pos0, rw_prev, wkv0, pool_buf, ret0, kv_bufs, lw, l, w_in_b, w_branch_b, w_out_b, ln_g, ln_b):
    b, t, _ = h.shape
    m = b * t
    tm = ROW_TILE if m % ROW_TILE == 0 else m
    hf = h.reshape(m, D_MODEL)
    p_rw, p_pool, p_ret, p_dil = _project(hf, w_in_b, l, tm)
    p_rw, p_pool = p_rw.reshape(b, t, -1), p_pool.reshape(b, t, -1)
    p_ret, p_dil = p_ret.reshape(b, t, -1), p_dil.reshape(b, t, -1)
    o_a, wkv_new = _rwkv_branch(p_rw, rw_prev, wkv0, lw)
    o_b = _pool_branch(p_pool, pool_buf, pos0, lw)
    pool_new = jnp.concatenate([pool_buf, p_pool], axis=1)[:, -POOL_BUF:]
    o_c, ret_new = _ret_branch(p_ret, ret0, pos0, lw)
    def kv_rows(g, keep):
        lo = (3 * g + 1) * WIDTH
        return p_dil[:, t - keep:, lo:lo + 2 * WIDTH].reshape(b, keep, 2, N_HEADS, HEAD_DIM)

    if kv_bufs is None:
        dil_parts = _dil_prompt(p_dil)
        kv_new = [kv_rows(g, min(win, t)) for g, (win, _) in enumerate(DIL_PATTERNS)]
    else:
        dil_parts = [_dil_step(p_dil, kv_bufs, l).reshape(m, WIDTH)]
        kv_new = [kv_rows(g, t) for g in range(DIL_GROUPS)]
    branches = [x.reshape(m, WIDTH) for x in (o_a, o_b, o_c)] + dil_parts
    x1 = _merge(hf, branches, w_in_b, w_branch_b, w_out_b, l, ln_g, ln_b, tm)
    return x1, (wkv_new, p_rw[:, -1], pool_new, ret_new, kv_new[0], kv_new[1], kv_new[2])


def kernel(x_prompt, x_sample, state_wkv, state_shift, state_pool, state_ret, cache_kv_w128, cache_kv_w512, cache_kv_w2048, w_in, rw_mu, rw_w0, rw_w2, rw_a0, rw_a2, rw_g2, rw_kk, rw_ka, rw_rk, rw_gn_g, rw_gn_b, pool_w, pool_scale, ret_gn_g, ret_gn_b, w_branch, w_out, ln_g, ln_b, ffn_w_gate, ffn_w_up, ffn_w_down, moe_router, moe_w_gate, moe_w_up, moe_w_down):
    hp, hs = x_prompt, x_sample
    bp, tp, _ = hp.shape
    bs, ts, _ = hs.shape
    names = ('rw_mu', 'rw_w0', 'rw_w2', 'rw_a0', 'rw_a2', 'rw_g2', 'rw_kk', 'rw_ka', 'rw_rk', 'rw_gn_g',
             'rw_gn_b', 'pool_w', 'pool_scale', 'ret_gn_g', 'ret_gn_b')
    stacked = (rw_mu, rw_w0, rw_w2, rw_a0, rw_a2, rw_g2, rw_kk, rw_ka, rw_rk, rw_gn_g, rw_gn_b, pool_w,
               pool_scale, ret_gn_g, ret_gn_b)
    new_p = [[] for _ in range(7)]
    new_s = [[] for _ in range(7)]
    zeros = lambda *shape: jnp.zeros(shape, F32)
    w_in_b, wb, wo = w_in.astype(BF16), w_branch.astype(BF16), w_out.astype(BF16)
    caches = [jnp.transpose(c, (0, 1, 3, 4, 5, 2)) for c in (cache_kv_w128, cache_kv_w512, cache_kv_w2048)]
    for l in range(DEPTH):
        lw = {k: v[l] for k, v in zip(names, stacked)}
        xp, st_p = _token_mix(hp, 0, zeros(bp, RW_COLS), zeros(bp, N_HEADS, HEAD_DIM, HEAD_DIM),
                              zeros(bp, POOL_BUF, WIDTH), zeros(bp, N_HEADS, HEAD_DIM, HEAD_DIM), None,
                              lw, l, w_in_b, wb, wo, ln_g[l, 0], ln_b[l, 0])
        xs, st_s = _token_mix(hs, PAST_LEN, state_shift[l], state_wkv[l], state_pool[l], state_ret[l], caches,
                              lw, l, w_in_b, wb, wo, ln_g[l, 0], ln_b[l, 0])
        j = l // 2
        if l % 2 == 0:
            ws = [w[j].astype(BF16) for w in (ffn_w_gate, ffn_w_up, ffn_w_down)]
            xp = _ffn(xp, *ws, ln_g[l, 1], ln_b[l, 1], ROW_TILE)
            xs = _ffn(xs, *ws, ln_g[l, 1], ln_b[l, 1], xs.shape[0])
        else:
            ws = [w[j].astype(BF16) for w in (moe_w_gate, moe_w_up, moe_w_down)]
            xp = _moe(xp, moe_router[j], *ws, ln_g[l, 1], ln_b[l, 1], MOE_BLOCK)
            xs = _moe(xs, moe_router[j], *ws, ln_g[l, 1], ln_b[l, 1], xs.shape[0])
        hp, hs = xp.reshape(bp, tp, D_MODEL), xs.reshape(bs, ts, D_MODEL)
        for i in range(7):
            new_p[i].append(st_p[i])
            new_s[i].append(st_s[i])
    outs_p = [jnp.stack(x) for x in new_p]
    outs_s = [jnp.stack(x) for x in new_s]
    return (hp, hs, *outs_p, *outs_s)
```

```python
import functools
import math

import jax
import jax.numpy as jnp
from jax import lax
from jax.experimental import pallas as pl
from jax.experimental.pallas import tpu as pltpu

F32 = jnp.float32
BF16 = jnp.bfloat16

D_MODEL = 1024
DEPTH = 2
HEAD_DIM = 64
N_HEADS = 4
WIDTH = N_HEADS * HEAD_DIM
RW_COLS = 1024
RW_GN_EPS = 64e-5
POOL_WINDOWS = (2, 4, 8, 16)
POOL_BUF = 15
POOL_HALO = max(POOL_WINDOWS)
RET_CHUNK = 128
ROPE_BASE = 10000.0
DIL_PATTERNS = ((128, 1), (512, 4), (2048, 16))
DIL_GROUPS = 3
DIL_BLOCK = 128
DIL_SPAN = 2048
DIL_COLS = 3 * DIL_GROUPS * WIDTH
COL_POOL = RW_COLS
COL_RET = COL_POOL + WIDTH
COL_DIL = COL_RET + 4 * WIDTH
COL_GATE = COL_DIL + DIL_COLS
N_EXPERTS = 8
DN_ALPHA = (2 * DEPTH) ** 0.25
LN_EPS = 1e-5
RW_CHUNK = 64
PAST_LEN = 8192
SUBLANES = 8
ROW_TILE = 512
PROJ_COL_TILE = 512
MOE_BLOCK = 1024
MOE_TILE = 288
RW_P = 1
RW_BATCH = 4
RW_CHUNKS_PER_STEP = 2
RW_LANE_GROUPS = 2
NEG_BIG = -1e30

NN = (((1,), (0,)), ((), ()))
NT = (((1,), (1,)), ((), ()))
TN = (((0,), (0,)), ((), ()))

VMEM_LIMIT = 56 * 1024 * 1024
MOE_VMEM_LIMIT = 61 * 1024 * 1024


def _split(x, n):
    if x.dtype == BF16:
        return [x]
    parts, rem = [], x
    for i in range(n):
        p = rem.astype(BF16)
        parts.append(p)
        if i + 1 < n:
            rem = rem - p.astype(F32)
    return parts


def _mm(a, b, dims=NN, pa=1, pb=1):
    a_parts, b_parts = _split(a, pa), _split(b, pb)
    depth = max(len(a_parts), len(b_parts))
    acc = None
    for i, ai in enumerate(a_parts):
        for j, bj in enumerate(b_parts):
            if i + j < depth:
                t = lax.dot_general(ai, bj, dims, preferred_element_type=F32)
                acc = t if acc is None else acc + t
    return acc


def _sigmoid(x):
    return 0.5 * jnp.tanh(0.5 * x) + 0.5


def _silu(x):
    return x * _sigmoid(x)


def _iota(shape, axis):
    return lax.broadcasted_iota(jnp.int32, shape, axis)


def _div(x, d):
    assert d & (d - 1) == 0
    return x >> (d.bit_length() - 1)


def _mod(x, d):
    assert d & (d - 1) == 0
    return x & (d - 1)


def _per_head(head, values):
    out = jnp.full(head.shape, values[-1], F32)
    for h in range(len(values) - 2, -1, -1):
        out = jnp.where(head == h, values[h], out)
    return out


def _head_mask(rows_per_head, n_rows):
    return _div(_iota((n_rows, WIDTH), 0), rows_per_head) == _div(_iota((n_rows, WIDTH), 1), HEAD_DIM)


def _stack_heads(x, mask):
    return jnp.where(mask, jnp.concatenate([x] * N_HEADS, axis=0), 0.0)


def _unstack_heads(x_st, mask, c):
    x_st = jnp.where(mask, x_st, 0.0)
    out = x_st[0:c]
    for h in range(1, N_HEADS):
        out = out + x_st[h * c:(h + 1) * c]
    return out


def _ones_bd():
    return _same_head().astype(BF16)


def _same_head():
    return _div(_iota((WIDTH, WIDTH), 0), HEAD_DIM) == _div(_iota((WIDTH, WIDTH), 1), HEAD_DIM)


def _head_norm(x, ones_bd, g, b, eps):
    mu = _mm(x, ones_bd, pa=2) * (1.0 / HEAD_DIM)
    d = x - mu
    var = _mm(d * d, ones_bd, pa=2) * (1.0 / HEAD_DIM)
    return d * lax.rsqrt(var + eps) * g + b


def _layer_norm(x, g, b):
    mu = jnp.mean(x, axis=-1, keepdims=True)
    d = x - mu
    var = jnp.mean(d * d, axis=-1, keepdims=True)
    return d * lax.rsqrt(var + LN_EPS) * g + b


def _resident(shape):
    nd = len(shape)
    return pl.BlockSpec(shape, lambda *_: (0,) * nd, pipeline_mode=pl.Buffered(1))


def _proj_kernel(x_ref, w_ref, rw_ref, pool_ref, ret_ref, dil_ref):
    xb = x_ref[...].astype(BF16)
    for ref, lo, hi in ((rw_ref, 0, COL_POOL), (pool_ref, COL_POOL, COL_RET),
                        (ret_ref, COL_RET, COL_DIL), (dil_ref, COL_DIL, COL_GATE)):
        for s in range(lo, hi, PROJ_COL_TILE):
            e = min(s + PROJ_COL_TILE, hi)
            ref[:, s - lo:e - lo] = _mm(xb, w_ref[:, s:e])


def _layer_resident(w, l):
    nd = w.ndim - 1
    return pl.BlockSpec((None,) + w.shape[1:], lambda *_: (l,) + (0,) * nd, pipeline_mode=pl.Buffered(1))


def _project(x, w_in_b, l, tm):
    m = x.shape[0]
    widths = (COL_POOL, WIDTH, 4 * WIDTH, DIL_COLS)
    row = lambda w: pl.BlockSpec((tm, w), lambda i: (i, 0))
    return pl.pallas_call(
        _proj_kernel,
        grid=(m // tm,),
        in_specs=[row(D_MODEL), _layer_resident(w_in_b, l)],
        out_specs=[row(w) for w in widths],
        out_shape=[jax.ShapeDtypeStruct((m, w), F32) for w in widths],
        compiler_params=pltpu.CompilerParams(dimension_semantics=("parallel",), vmem_limit_bytes=VMEM_LIMIT),
        name="proj",
    )(x, w_in_b)


BNT = (((2,), (2,)), ((0,), (0,)))
BNN = (((2,), (1,)), ((0,), (0,)))
BTN = (((1,), (1,)), ((0,), (0,)))


def _rwkv_kernel(p_ref, prev_ref, s0_ref, mu_ref, w0_ref, w2_ref, a0_ref, a2_ref, g2_ref, kk_ref, ka_ref,
                 rk_ref, gng_ref, gnb_ref, o_ref, st_ref, s_scr, prev_scr, buf_scr, *, c, t_in, nb, nsub):
    ci = pl.program_id(1)
    rows = nsub * c

    halves = RW_LANE_GROUPS
    hw, hpg = WIDTH // halves, N_HEADS // halves
    n = hpg * c
    same_grp = _div(_iota((hw, hw), 0), HEAD_DIM) == _div(_iota((hw, hw), 1), HEAD_DIM)

    @pl.when(ci == 0)
    def _():
        for g in range(halves):
            part = s0_ref[:, g * hw:(g + 1) * hw, :]
            s_scr[g * nb:(g + 1) * nb] = jnp.where(same_grp, jnp.concatenate([part] * hpg, axis=-1), 0.0)
        for b in range(nb):
            prev_scr[b, 0:1, :] = prev_ref[b]

    first = _iota((rows, RW_COLS), 0) == 0
    ps, shs = [], []
    for b in range(nb):
        if t_in == c:
            pb_ = p_ref[b]
        else:
            buf_scr[b] = jnp.zeros((c, RW_COLS), F32)
            buf_scr[b, 0:t_in, :] = p_ref[b]
            pb_ = buf_scr[b]
        shs.append(jnp.where(first, prev_scr[b, 0:1, :], pltpu.roll(pb_, 1, 0)))
        prev_scr[b, 0:1, :] = pb_[rows - 1:rows, :]
        ps.append(pb_)
    p, shifted = jnp.concatenate(ps, axis=0), jnp.concatenate(shs, axis=0)
    m = nb * rows
    u = p + (shifted - p) * mu_ref[...]
    r, k, v, ul = u[:, 0:WIDTH], u[:, WIDTH:2 * WIDTH], u[:, 2 * WIDTH:3 * WIDTH], u[:, 3 * WIDTH:]

    w_pre = w0_ref[...] + _mm(jnp.tanh(ul), w2_ref[...])
    logw = -math.exp(-0.5) * _sigmoid(w_pre)
    a = _sigmoid(a0_ref[...] + _mm(ul, a2_ref[...]))
    g = _mm(_sigmoid(ul), g2_ref[...])

    ones_bd = _ones_bd()
    kk = k * kk_ref[...]
    kk = kk * lax.rsqrt(jnp.maximum(_mm(kk * kk, ones_bd, pa=2), 1e-24))
    k_mod = k * (1.0 + (a - 1.0) * ka_ref[...])
    a_vec, b_vec = -kk, kk * a
    if t_in < c:
        live = _mod(_iota((m, WIDTH), 0), c) < t_in
        logw = jnp.where(live, logw, 0.0)
        a_vec, b_vec = jnp.where(live, a_vec, 0.0), jnp.where(live, b_vec, 0.0)
        k_mod, v = jnp.where(live, k_mod, 0.0), jnp.where(live, v, 0.0)

    qi, qj = _iota((m, m), 0), _iota((m, m), 1)
    tri = ((_div(qi, c) == _div(qj, c)) & (qi >= qj)).astype(BF16)
    cum2 = _mm(tri, logw, pb=3)
    def chunk_rows(x, s):
        return x if nsub == 1 else x.reshape(nb, nsub, c, x.shape[-1])[:, s].reshape(nb * c, x.shape[-1])

    def seq(x):
        parts = []
        for s in range(nsub):
            x3 = chunk_rows(x, s).reshape(nb, c, WIDTH)
            parts += [x3[:, :, g * hw:(g + 1) * hw] for g in range(halves)]
        return jnp.concatenate(parts, axis=0)

    cum, lw3 = seq(cum2), seq(logw)
    cum_end = cum[:, c - 1:c, :]
    e_neg, e_end = jnp.exp(-cum), jnp.exp(cum_end - cum)
    a_t = seq(a_vec) * jnp.exp(cum - lw3)
    r_t = seq(r) * jnp.exp(cum)
    b_t, k_t = seq(b_vec) * e_neg, seq(k_mod) * e_neg
    b_e, k_e = seq(b_vec) * e_end, seq(k_mod) * e_end
    v3 = seq(v)

    hm = _div(_iota((n, hw), 0), c) == _div(_iota((n, hw), 1), HEAD_DIM)
    stack = lambda x: jnp.where(hm, jnp.concatenate([x] * hpg, axis=1), 0.0)
    a_st, r_st, b_st, v_st = stack(a_t), stack(r_t), stack(b_t), stack(v3)
    ri, rj = _iota((n, n), 0), _iota((n, n), 1)
    same = _div(ri, c) == _div(rj, c)
    strict_bd = same & (_mod(ri, c) > _mod(rj, c))
    incl_bd = same & (_mod(ri, c) >= _mod(rj, c))
    ti, tj = _mod(_iota((n, c), 0), c), _iota((n, c), 1)
    strict_st, incl_st = ti > tj, ti >= tj

    mm = functools.partial(_mm, pa=RW_P, pb=RW_P)
    a_ab = jnp.where(strict_bd, mm(a_st, b_st, BNT), 0.0)
    inv = jnp.where(ri == rj, 1.0, 0.0) + a_ab
    pw = a_ab
    for _ in range(int(math.log2(c)) - 1):
        pw = mm(pw, pw, BNN)
        inv = inv + mm(inv, pw, BNN)
    a_ak = jnp.where(strict_st, mm(a_st, k_t, BNT), 0.0)
    z_st = jnp.where(hm, mm(a_ak, v3, BNN), 0.0)
    wu = mm(inv, jnp.concatenate([a_st, z_st], axis=2), BNN)
    w_st, u0_st = wu[:, :, 0:hw], wu[:, :, hw:]

    rhs = jnp.concatenate([stack(b_e), stack(k_e)], axis=1)
    a_rb = jnp.where(incl_bd, mm(r_st, b_st, BNT), 0.0)
    a_rk = jnp.where(incl_st, mm(r_st, k_t, BNT), 0.0)
    o_in = mm(a_rk, v3, BNN)
    decay_end = jnp.exp(cum_end)

    per = halves * nb
    s0 = s_scr[...]
    for s in range(nsub):
        sl = slice(s * per, (s + 1) * per)
        u_st = mm(w_st[sl], s0, BNT) + u0_st[sl]
        o_st = jnp.where(hm, mm(r_st[sl], s0, BNT) + mm(a_rb[sl], u_st, BNN) + o_in[sl], 0.0)
        s0 = s0 * decay_end[sl] + mm(jnp.concatenate([u_st, v_st[sl]], axis=1), rhs[sl], BTN)
        o3 = o_st[:, 0:c]
        for h in range(1, hpg):
            o3 = o3 + o_st[:, h * c:(h + 1) * c]
        o = jnp.concatenate([o3[g * nb:(g + 1) * nb] for g in range(halves)], axis=2).reshape(nb * c, WIDTH)
        o = _head_norm(o, ones_bd, gng_ref[...], gnb_ref[...], RW_GN_EPS)
        v_s = chunk_rows(v, s)
        o = o + _mm(chunk_rows(r, s) * chunk_rows(k_mod, s) * rk_ref[...], ones_bd, pa=2) * v_s
        o = (o * chunk_rows(g, s)).reshape(nb, c, WIDTH)
        if t_in == c:
            o_ref[:, s * c:(s + 1) * c, :] = o
        else:
            o_ref[...] = o[:, 0:t_in]
    s_scr[...] = s0

    @pl.when(ci == pl.num_programs(1) - 1)
    def _():
        for g in range(halves):
            blk = s_scr[g * nb:(g + 1) * nb]
            part = blk[..., 0:HEAD_DIM]
            for h in range(1, hpg):
                part = part + blk[..., h * HEAD_DIM:(h + 1) * HEAD_DIM]
            st_ref[:, g * hw:(g + 1) * hw, :] = part


def _embed_heads(s):
    return jnp.where(_same_head(), jnp.concatenate([s] * N_HEADS, axis=-1), 0.0)


def _extract_heads(s_bd):
    out = s_bd[..., 0:HEAD_DIM]
    for h in range(1, N_HEADS):
        out = out + s_bd[..., h * HEAD_DIM:(h + 1) * HEAD_DIM]
    return out


def _rwkv_branch(p, p_prev, wkv0, lw):
    b, t, _ = p.shape
    c = RW_CHUNK if t % RW_CHUNK == 0 else RW_CHUNK // 2
    assert t % c == 0 or t <= c
    t_in = c if t % c == 0 else t
    nc = t // c if t % c == 0 else 1
    vec = lambda x: x.reshape(1, -1)
    pad_rows = lambda w, lo: jnp.zeros((WIDTH, WIDTH), F32).at[lo:lo + w.shape[0]].set(w).astype(BF16)
    params = [vec(lw['rw_mu']), vec(lw['rw_w0']), pad_rows(lw['rw_w2'], 0), vec(lw['rw_a0']),
              pad_rows(lw['rw_a2'], 64), pad_rows(lw['rw_g2'], 128), vec(lw['rw_kk']), vec(lw['rw_ka']),
              vec(lw['rw_rk']), vec(lw['rw_gn_g']), vec(lw['rw_gn_b'])]
    nb = 2 * RW_BATCH if b % (2 * RW_BATCH) == 0 else RW_BATCH
    nsub = RW_CHUNKS_PER_STEP if nc % RW_CHUNKS_PER_STEP == 0 else 1
    rows = nsub * t_in
    assert b % nb == 0
    o, st = pl.pallas_call(
        functools.partial(_rwkv_kernel, c=c, t_in=t_in, nb=nb, nsub=nsub),
        grid=(b // nb, nc // nsub),
        in_specs=[pl.BlockSpec((nb, rows, RW_COLS), lambda i, j: (i, j, 0)),
                  pl.BlockSpec((nb, 1, RW_COLS), lambda i, j: (i, 0, 0)),
                  pl.BlockSpec((nb, WIDTH, HEAD_DIM), lambda i, j: (i, 0, 0))]
                 + [_resident(x.shape) for x in params],
        out_specs=[pl.BlockSpec((nb, rows, WIDTH), lambda i, j: (i, j, 0)),
                   pl.BlockSpec((nb, WIDTH, HEAD_DIM), lambda i, j: (i, 0, 0))],
        out_shape=[jax.ShapeDtypeStruct((b, t, WIDTH), F32), jax.ShapeDtypeStruct((b, WIDTH, HEAD_DIM), F32)],
        scratch_shapes=[pltpu.VMEM((RW_LANE_GROUPS * nb, WIDTH // RW_LANE_GROUPS, WIDTH // RW_LANE_GROUPS), F32),
                        pltpu.VMEM((nb, 8, RW_COLS), F32),
                        pltpu.VMEM((nb, c, RW_COLS), F32)],
        compiler_params=pltpu.CompilerParams(dimension_semantics=("parallel", "arbitrary"),
                                             vmem_limit_bytes=VMEM_LIMIT),
        name="rwkv7",
    )(p, p_prev.reshape(b, 1, RW_COLS), wkv0.reshape(b, WIDTH, HEAD_DIM), *params)
    return o, st.reshape(b, N_HEADS, HEAD_DIM, HEAD_DIM)


def _pool_rows(ext_scr, c, pos_first, w, scale):
    x = ext_scr[POOL_HALO:POOL_HALO + c, :]
    sums, acc, off = [], x, 1
    for win in POOL_WINDOWS:
        while off < win:
            acc = acc + ext_scr[POOL_HALO - off:POOL_HALO - off + c, :]
            off += 1
        sums.append(acc)
    pos = pos_first + _iota((c, WIDTH), 0)
    grp = _div(_iota((c, WIDTH), 1), HEAD_DIM)
    mean = jnp.zeros((c, WIDTH), F32)
    for gi, win in enumerate(POOL_WINDOWS):
        cnt = jnp.minimum(win, pos + 1).astype(F32)
        mean = jnp.where(grp == gi, sums[gi] / cnt, mean)
    ext_scr[0:POOL_HALO, :] = ext_scr[c:c + POOL_HALO, :]
    return _mm(mean - x, w) * scale


def _pool_kernel(u_ref, buf_ref, w_ref, scale_ref, o_ref, ext_scr, *, c, t_in, pos0):
    ci = pl.program_id(1)

    @pl.when(ci == 0)
    def _():
        ext_scr[0:POOL_HALO, :] = buf_ref[0]

    if t_in < c:
        ext_scr[POOL_HALO:POOL_HALO + c, :] = jnp.zeros((c, WIDTH), F32)
    ext_scr[POOL_HALO:POOL_HALO + t_in, :] = u_ref[0]
    o_ref[0] = _pool_rows(ext_scr, c, pos0 + ci * c, w_ref[...], scale_ref[...])[0:t_in]


def _pool_weights(lw):
    w_bd = jnp.einsum('gcd,gh->gchd', lw['pool_w'], jnp.eye(N_HEADS, dtype=F32)).reshape(WIDTH, WIDTH)
    return w_bd.astype(BF16), lw['pool_scale'].reshape(1, WIDTH)


def _pool_branch(u, buf, pos0, lw):
    b, t, _ = u.shape
    c = ROW_TILE if t % ROW_TILE == 0 else SUBLANES
    t_in = c if t % c == 0 else t
    nc = t // c if t % c == 0 else 1
    w_bd, scale = _pool_weights(lw)
    buf16 = jnp.pad(buf, ((0, 0), (1, 0), (0, 0)))
    return pl.pallas_call(
        functools.partial(_pool_kernel, c=c, t_in=t_in, pos0=pos0),
        grid=(b, nc),
        in_specs=[pl.BlockSpec((1, t_in, WIDTH), lambda i, j: (i, j, 0)),
                  pl.BlockSpec((1, POOL_HALO, WIDTH), lambda i, j: (i, 0, 0)),
                  _resident((WIDTH, WIDTH)), _resident((1, WIDTH))],
        out_specs=pl.BlockSpec((1, t_in, WIDTH), lambda i, j: (i, j, 0)),
        out_shape=jax.ShapeDtypeStruct((b, t, WIDTH), F32),
        scratch_shapes=[pltpu.VMEM((POOL_HALO + c, WIDTH), F32)],
        compiler_params=pltpu.CompilerParams(dimension_semantics=("parallel", "arbitrary")),
        name="pool",
    )(u, buf16, w_bd, scale)


def _rot_half(x):
    first = _mod(_iota(x.shape, 1), HEAD_DIM) < (HEAD_DIM // 2)
    return jnp.where(first, pltpu.roll(x, WIDTH - HEAD_DIM // 2, 1), pltpu.roll(x, HEAD_DIM // 2, 1))


RET_LOG_DECAY = tuple(math.log(1.0 - 2.0 ** (-5.0 - h)) for h in range(N_HEADS))
ALIBI_SLOPES = tuple(2.0 ** (-8.0 * (i + 1) / (DIL_GROUPS * N_HEADS)) for i in range(DIL_GROUPS * N_HEADS))


def _ret_kernel(p_ref, cos_ref, sin_ref, s0_ref, gng_ref, gnb_ref, o_ref, st_ref, s_scr, buf_scr,
                dmask_scr, qd_scr, kd_scr, *, c, t_in, nb):
    ci = pl.program_id(1)
    n = N_HEADS * c
    lg = _per_head(_div(_iota((1, WIDTH), 1), HEAD_DIM), RET_LOG_DECAY)

    @pl.when(ci == 0)
    def _():
        s_scr[...] = _embed_heads(s0_ref[...])
        idx = _iota((c, WIDTH), 0).astype(F32)
        qd_scr[...] = jnp.exp(lg * (idx + 1.0))
        kd_scr[...] = jnp.exp(lg * (t_in - 1.0 - idx))
        rel = _mod(_iota((n, c), 0), c) - _iota((n, c), 1)
        lg_rows = _per_head(_div(_iota((n, c), 0), c), RET_LOG_DECAY)
        dmask_scr[...] = jnp.where(rel >= 0, jnp.exp(lg_rows * jnp.maximum(rel, 0).astype(F32)), 0.0)

    hm = _head_mask(c, n)
    cos, sin = cos_ref[...], sin_ref[...]
    chunk_decay = jnp.exp(lg * float(t_in))
    same_head, ones_bd = _same_head(), _ones_bd()
    for b in range(nb):
        if t_in == c:
            p = p_ref[b]
        else:
            buf_scr[b] = jnp.zeros((c, 4 * WIDTH), F32)
            buf_scr[b, 0:t_in, :] = p_ref[b]
            p = buf_scr[b]
        q, k, v, g = (p[:, i * WIDTH:(i + 1) * WIDTH] for i in range(4))
        q = q * cos + _rot_half(q) * sin
        k = (k * cos + _rot_half(k) * sin) * HEAD_DIM ** -0.5
        if t_in < c:
            live = _iota((c, WIDTH), 0) < t_in
            k, v = jnp.where(live, k, 0.0), jnp.where(live, v, 0.0)
        inner = _mm(_stack_heads(q, hm), k, NT) * dmask_scr[...]
        s0 = s_scr[b]
        o = _unstack_heads(_mm(inner, v), hm, c) + _mm(q * qd_scr[...], s0)
        s_scr[b] = s0 * chunk_decay + jnp.where(same_head, _mm(k * kd_scr[...], v, TN), 0.0)
        o = _silu(g) * _head_norm(o, ones_bd, gng_ref[...], gnb_ref[...], LN_EPS)
        o_ref[b] = o[0:t_in]

    @pl.when(ci == pl.num_programs(1) - 1)
    def _():
        st_ref[...] = _extract_heads(s_scr[...])


def _rope_tables(pos0, t, rows):
    half = HEAD_DIM // 2
    inv = ROPE_BASE ** (-jnp.arange(half, dtype=F32) / half)
    ang = (pos0 + jnp.arange(t, dtype=jnp.int32)).astype(F32)[:, None] * inv[None, :]
    cos = jnp.tile(jnp.cos(ang), (1, 2 * N_HEADS))
    sin = jnp.tile(jnp.concatenate([-jnp.sin(ang), jnp.sin(ang)], axis=1), (1, N_HEADS))
    pad = ((0, rows - t), (0, 0))
    return jnp.pad(cos, pad), jnp.pad(sin, pad)


def _ret_branch(p, s0, pos0, lw):
    b, t, _ = p.shape
    c = RET_CHUNK
    t_in = c if t % c == 0 else t
    nc = t // c if t % c == 0 else 1
    cos, sin = _rope_tables(pos0, t, nc * c)
    nb = 2 * RW_BATCH if b % (2 * RW_BATCH) == 0 else RW_BATCH
    assert b % nb == 0
    o, st = pl.pallas_call(
        functools.partial(_ret_kernel, c=c, t_in=t_in, nb=nb),
        grid=(b // nb, nc),
        in_specs=[pl.BlockSpec((nb, t_in, 4 * WIDTH), lambda i, j: (i, j, 0)),
                  pl.BlockSpec((c, WIDTH), lambda i, j: (j, 0)),
                  pl.BlockSpec((c, WIDTH), lambda i, j: (j, 0)),
                  pl.BlockSpec((nb, WIDTH, HEAD_DIM), lambda i, j: (i, 0, 0)),
                  _resident((1, WIDTH)), _resident((1, WIDTH))],
        out_specs=[pl.BlockSpec((nb, t_in, WIDTH), lambda i, j: (i, j, 0)),
                   pl.BlockSpec((nb, WIDTH, HEAD_DIM), lambda i, j: (i, 0, 0))],
        out_shape=[jax.ShapeDtypeStruct((b, t, WIDTH), F32), jax.ShapeDtypeStruct((b, WIDTH, HEAD_DIM), F32)],
        scratch_shapes=[pltpu.VMEM((nb, WIDTH, WIDTH), F32), pltpu.VMEM((nb, c, 4 * WIDTH), F32),
                        pltpu.VMEM((N_HEADS * c, c), F32), pltpu.VMEM((c, WIDTH), F32), pltpu.VMEM((c, WIDTH), F32)],
        compiler_params=pltpu.CompilerParams(dimension_semantics=("parallel", "arbitrary")),
        name="retention",
    )(p, cos, sin, s0.reshape(b, WIDTH, HEAD_DIM), lw['ret_gn_g'].reshape(1, WIDTH), lw['ret_gn_b'].reshape(1, WIDTH))
    return o, st.reshape(b, N_HEADS, HEAD_DIM, HEAD_DIM)


def _dil_prompt_kernel(*refs, group, dil, span):
    q_refs, kc_refs, kp_refs, vc_refs, vp_refs = (refs[2 * i:2 * i + 2] for i in range(5))
    o_ref, lse_ref, k_scr, v_scr, o_scr, lse_scr = refs[10:]
    blk = DIL_BLOCK
    tail = blk * dil
    si = pl.program_id(1)
    for half in range(2):
        k_scr[half, 0:tail, :] = kp_refs[half][0]
        k_scr[half, tail:tail + span, :] = kc_refs[half][0]
        v_scr[half, 0:tail, :] = vp_refs[half][0]
        v_scr[half, tail:tail + span, :] = vc_refs[half][0]
    ki = _iota((blk, 2 * blk), 1)
    steps = blk + _iota((blk, 2 * blk), 0) - ki
    band = (steps >= 0) & (steps <= blk)
    masks = [jnp.where(band, (-ALIBI_SLOPES[group * N_HEADS + h] * dil) * steps.astype(F32), NEG_BIG)
             for h in range(N_HEADS)]
    sub_head = _div(_iota((blk, 128), 1), HEAD_DIM)
    for cc in range(span // tail):
        has_prev = (si > 0) | (ki >= blk)
        cc_masks = [jnp.where(has_prev, x, NEG_BIG) for x in masks] if cc == 0 else masks
        for r in range(dil):
            rows_q = pl.ds(cc * tail + r, blk, stride=dil) if dil > 1 else pl.ds(cc * tail, blk)
            rows_kv = pl.ds(cc * tail + r, 2 * blk, stride=dil) if dil > 1 else pl.ds(cc * tail, 2 * blk)
            for half in range(2):
                q2 = q_refs[half][0, rows_q, :] * HEAD_DIM ** -0.5
                k2 = k_scr[half, rows_kv, :].astype(BF16)
                v2 = v_scr[half, rows_kv, :].astype(BF16)
                o2 = lse2 = None
                for sub in range(2):
                    mine = sub_head == sub
                    s = _mm(jnp.where(mine, q2, 0.0), k2, NT) + cc_masks[2 * half + sub]
                    m = jnp.max(s, axis=1, keepdims=True)
                    e = jnp.exp(s - m)
                    l = jnp.sum(e, axis=1, keepdims=True)
                    o_h = _mm(e, v2) / l
                    lse_h = jnp.broadcast_to(m + jnp.log(l), (blk, 128))
                    o2 = o_h if o2 is None else jnp.where(mine, o_h, o2)
                    lse2 = lse_h if lse2 is None else jnp.where(mine, lse_h, lse2)
                o_scr[half, rows_q, :] = o2
                lse_scr[half, rows_q, :] = lse2
    o_ref[0] = jnp.concatenate([o_scr[0], o_scr[1]], axis=1)
    lse_ref[0] = jnp.concatenate([lse_scr[0], lse_scr[1]], axis=1)


def _dil_prompt_group(pd, group):
    b, s, _ = pd.shape
    win, dil = DIL_PATTERNS[group]
    span = DIL_SPAN
    tail = DIL_BLOCK * dil
    assert win // dil == DIL_BLOCK and span % tail == 0 and s % span == 0
    base = group * 3

    def cur(col):
        return [pl.BlockSpec((1, span, 128), lambda i, j, h=h: (i, j, 2 * (base + col) + h)) for h in range(2)]

    def prev(col):
        return [pl.BlockSpec((1, tail, 128),
                             lambda i, j, h=h: (i, jnp.maximum(j * (span // tail) - 1, 0), 2 * (base + col) + h))
                for h in range(2)]

    out_spec = pl.BlockSpec((1, span, WIDTH), lambda i, j: (i, j, 0))
    return pl.pallas_call(
        functools.partial(_dil_prompt_kernel, group=group, dil=dil, span=span),
        grid=(b, s // span),
        in_specs=cur(0) + cur(1) + prev(1) + cur(2) + prev(2),
        out_specs=[out_spec, out_spec],
        out_shape=[jax.ShapeDtypeStruct((b, s, WIDTH), F32)] * 2,
        scratch_shapes=[pltpu.VMEM((2, tail + span, 128), F32)] * 2 + [pltpu.VMEM((2, span, 128), F32)] * 2,
        compiler_params=pltpu.CompilerParams(dimension_semantics=("parallel", "arbitrary"),
                                             vmem_limit_bytes=VMEM_LIMIT),
        name=f"dil_prompt_g{group}",
    )(*([pd] * 10))


def _dil_combine(os_, ls):
    m = jnp.maximum(jnp.maximum(ls[0], ls[1]), ls[2])
    es = [jnp.exp(x - m) for x in ls]
    return (es[0] * os_[0] + es[1] * os_[1] + es[2] * os_[2]) / (es[0] + es[1] + es[2])


def _dil_prompt(pd):
    b, s, _ = pd.shape
    outs = [_dil_prompt_group(pd, g) for g in range(DIL_GROUPS)]
    return [x[0].reshape(b * s, WIDTH) for x in outs] + [x[1].reshape(b * s, WIDTH) for x in outs]


def _dil_step_kernel(pd_ref, c0_ref, c1_ref, c2_ref, o_ref, buf_scr, *, t, tp):
    buf_scr[...] = jnp.zeros_like(buf_scr)
    buf_scr[0:t, :] = pd_ref[0]
    pd = buf_scr[...]
    qt = _iota((tp, 1), 0)
    d_new = qt - _iota((tp, tp), 1)
    outs = [[None] * DIL_GROUPS for _ in range(N_HEADS)]
    lses = [[None] * DIL_GROUPS for _ in range(N_HEADS)]
    for g, cache_ref in enumerate((c0_ref, c1_ref, c2_ref)):
        win, dil = DIL_PATTERNS[g]
        length = cache_ref.shape[-1]
        d_old = length + qt - _iota((tp, length), 1)
        ok_old = (_mod(d_old, dil) == 0) & (d_old <= win)
        ok_new = (d_new >= 0) & (_mod(d_new, dil) == 0)
        for h in range(N_HEADS):
            slope = ALIBI_SLOPES[g * N_HEADS + h]
            lo = g * 3 * WIDTH + h * HEAD_DIM
            q = pd[:, lo:lo + HEAD_DIM] * HEAD_DIM ** -0.5
            k_new = pd[:, lo + WIDTH:lo + WIDTH + HEAD_DIM]
            v_new = pd[:, lo + 2 * WIDTH:lo + 2 * WIDTH + HEAD_DIM]
            k_t, v_t = cache_ref[0, 0, h], cache_ref[0, 1, h]
            s_old = jnp.where(ok_old, _mm(q, k_t) - slope * d_old.astype(F32), NEG_BIG)
            s_new = jnp.where(ok_new, _mm(q, k_new, NT) - slope * d_new.astype(F32), NEG_BIG)
            m = jnp.maximum(jnp.max(s_old, axis=1, keepdims=True), jnp.max(s_new, axis=1, keepdims=True))
            e_old, e_new = jnp.exp(s_old - m), jnp.exp(s_new - m)
            l = jnp.sum(e_old, axis=1, keepdims=True) + jnp.sum(e_new, axis=1, keepdims=True)
            outs[h][g] = (_mm(e_old, v_t, NT) + _mm(e_new, v_new)) / l
            lses[h][g] = m + jnp.log(l)
    o = jnp.concatenate([_dil_combine(outs[h], lses[h]) for h in range(N_HEADS)], axis=1)
    o_ref[0] = o[0:t]


def _dil_step(pd, caches, l):
    b, t, _ = pd.shape
    tp = SUBLANES
    flat = caches
    return pl.pallas_call(
        functools.partial(_dil_step_kernel, t=t, tp=tp),
        grid=(b,),
        in_specs=[pl.BlockSpec((1, t, DIL_COLS), lambda i: (i, 0, 0))]
                 + [pl.BlockSpec((None, 1) + c.shape[2:], lambda i: (l, i, 0, 0, 0, 0)) for c in flat],
        out_specs=pl.BlockSpec((1, t, WIDTH), lambda i: (i, 0, 0)),
        out_shape=jax.ShapeDtypeStruct((b, t, WIDTH), F32),
        scratch_shapes=[pltpu.VMEM((tp, DIL_COLS), F32)],
        compiler_params=pltpu.CompilerParams(dimension_semantics=("parallel",), vmem_limit_bytes=VMEM_LIMIT),
        name="dil_step",
    )(pd, *flat)


def _merge_kernel(h_ref, *refs):
    wg_ref, wb_ref, wo_ref, g_ref, b_ref, out_ref = refs[-6:]
    branch_refs = refs[:-6]
    branches = [r[...] for r in branch_refs[:3]]
    if len(branch_refs) == 4:
        branches.append(branch_refs[3][...])
    else:
        branches.append(_dil_combine([r[...] for r in branch_refs[3:6]], [r[...] for r in branch_refs[6:9]]))
    h = h_ref[...]
    hb = h.astype(BF16)
    z = None
    for n, o in enumerate(branches):
        gate = _sigmoid(_mm(hb, wg_ref[:, COL_GATE + n * D_MODEL:COL_GATE + (n + 1) * D_MODEL]))
        term = gate * _mm(o, wb_ref[n])
        z = term if z is None else z + term
    y = _mm(z, wo_ref[...])
    out_ref[...] = _layer_norm(DN_ALPHA * h + y, g_ref[...], b_ref[...])


def _merge(h, branches, w_in_b, w_branch_b, w_out_b, l, ln_g, ln_b, tm):
    m = h.shape[0]
    row = lambda w: pl.BlockSpec((tm, w), lambda i: (i, 0))
    return pl.pallas_call(
        _merge_kernel,
        grid=(m // tm,),
        in_specs=[row(D_MODEL)] + [row(WIDTH)] * len(branches)
                 + [_layer_resident(w_in_b, l), _layer_resident(w_branch_b, l), _layer_resident(w_out_b, l),
                    _resident((1, D_MODEL)), _resident((1, D_MODEL))],
        out_specs=row(D_MODEL),
        out_shape=jax.ShapeDtypeStruct((m, D_MODEL), F32),
        compiler_params=pltpu.CompilerParams(dimension_semantics=("parallel",), vmem_limit_bytes=VMEM_LIMIT),
        name="merge_ln",
    )(h, *branches, w_in_b, w_branch_b, w_out_b, ln_g.reshape(1, -1), ln_b.reshape(1, -1))


def _ffn_kernel(x_ref, wg_ref, wu_ref, wd_ref, g_ref, b_ref, out_ref):
    x = x_ref[...]
    xb = x.astype(BF16)
    act = _silu(_mm(xb, wg_ref[...])) * _mm(xb, wu_ref[...])
    out_ref[...] = _layer_norm(DN_ALPHA * x + _mm(act, wd_ref[...]), g_ref[...], b_ref[...])


def _ffn(x, wg, wu, wd, ln_g, ln_b, tm):
    m = x.shape[0]
    row = pl.BlockSpec((tm, D_MODEL), lambda i: (i, 0))
    return pl.pallas_call(
        _ffn_kernel,
        grid=(m // tm,),
        in_specs=[row, _resident(wg.shape), _resident(wu.shape), _resident(wd.shape),
                  _resident((1, D_MODEL)), _resident((1, D_MODEL))],
        out_specs=row,
        out_shape=jax.ShapeDtypeStruct((m, D_MODEL), F32),
        compiler_params=pltpu.CompilerParams(dimension_semantics=("parallel",), vmem_limit_bytes=VMEM_LIMIT),
        name="ffn_ln",
    )(x, wg, wu, wd, ln_g.reshape(1, -1), ln_b.reshape(1, -1))


def _route_kernel(x_ref, rt_ref, tri_ref, gate_ref, rank_ref, cnt_ref):
    logits = _mm(rt_ref[...], x_ref[...], NT, 2, 2)
    sub = _iota(logits.shape, 0).astype(F32)
    m1 = jnp.max(logits, axis=0, keepdims=True)
    i1 = jnp.min(jnp.where(logits == m1, sub, float(N_EXPERTS)), axis=0, keepdims=True)
    rest = jnp.where(sub == i1, NEG_BIG, logits)
    m2 = jnp.max(rest, axis=0, keepdims=True)
    i2 = jnp.min(jnp.where(rest == m2, sub, float(N_EXPERTS)), axis=0, keepdims=True)
    e2 = jnp.exp(m2 - m1)
    gate_ref[0] = jnp.where(sub == i1, 1.0 / (1.0 + e2), 0.0) + jnp.where(sub == i2, e2 / (1.0 + e2), 0.0)
    chosen = (sub == i1) | (sub == i2)
    sel = jnp.where(chosen, 1.0, 0.0)
    rank_ref[0] = jnp.where(chosen, _mm(sel, tri_ref[...]), -1.0)
    cnt_ref[0] = jnp.broadcast_to(jnp.sum(sel, axis=1, keepdims=True), cnt_ref.shape[1:])


def _moe_kernel(cnt_ref, x_ref, gate_ref, rank_ref, wg_ref, wu_ref, wd_ref, g_ref, b_ref, out_ref,
                xb_scr, xg_scr, yg_scr, *, ts, tb, nsb):
    i, e, f = pl.program_id(0), pl.program_id(1), pl.program_id(2)
    last_f = f == pl.num_programs(2) - 1

    @pl.when((e == 0) & (f == 0))
    def _():
        xb_scr[...] = x_ref[...].astype(BF16)
        out_ref[...] = jnp.zeros_like(out_ref)

    slot = _iota((ts, tb), 0)

    def expert(xg):
        return _mm(_silu(_mm(xg, wg_ref[0])) * _mm(xg, wu_ref[0]), wd_ref[0])

    def pick_of(sb, j):
        rank_row = rank_ref[sb, pl.ds(e, 1), :]
        return jnp.where(rank_row == (slot + j * ts).astype(F32), 1.0, 0.0)

    def scatter(sb, pick, y):
        w_slot = jnp.sum(pick * gate_ref[sb, pl.ds(e, 1), :], axis=1, keepdims=True)
        out_ref[sb * tb:(sb + 1) * tb, :] += _mm(pick, y * w_slot, TN)

    @pl.when(f == 0)
    def _():
        for sb in range(nsb):
            xg_scr[sb] = _mm(pick_of(sb, 0), xb_scr[sb * tb:(sb + 1) * tb, :]).astype(BF16)
        yg_scr[...] = jnp.zeros_like(yg_scr)

    yg_scr[...] += expert(xg_scr[...].reshape(nsb * ts, D_MODEL)).reshape(nsb, ts, D_MODEL)

    @pl.when(last_f)
    def _():
        for sb in range(nsb):
            scatter(sb, pick_of(sb, 0), yg_scr[sb])

    for sb in range(nsb):
        n_tiles = (cnt_ref[(i * nsb + sb) * N_EXPERTS + e] + (ts - 1)) // ts

        def overflow(j, carry, sb=sb):
            pick = pick_of(sb, j)
            scatter(sb, pick, expert(_mm(pick, xb_scr[sb * tb:(sb + 1) * tb, :]).astype(BF16)))
            return carry

        lax.fori_loop(1, n_tiles, overflow, 0)

    @pl.when((e == pl.num_programs(1) - 1) & last_f)
    def _():
        out_ref[...] = _layer_norm(DN_ALPHA * x_ref[...] + out_ref[...], g_ref[...], b_ref[...])


def _moe(x, router, wg, wu, wd, ln_g, ln_b, tm):
    m = x.shape[0]
    nblk = m // tm
    dff = wg.shape[2]
    nf = 2
    tf = dff // nf
    ts = min(MOE_TILE, tm)
    nsb = 2 if nblk % 2 == 0 else 1
    assert m % tm == 0 and tf % 128 == 0
    tri = (jnp.arange(tm)[:, None] < jnp.arange(tm)[None, :]).astype(BF16)
    gate, rank, cnt = pl.pallas_call(
        _route_kernel,
        grid=(nblk,),
        in_specs=[pl.BlockSpec((tm, D_MODEL), lambda i: (i, 0)), _resident((N_EXPERTS, D_MODEL)),
                  _resident((tm, tm))],
        out_specs=[pl.BlockSpec((1, N_EXPERTS, tm), lambda i: (i, 0, 0)),
                   pl.BlockSpec((1, N_EXPERTS, tm), lambda i: (i, 0, 0)),
                   pl.BlockSpec((1, N_EXPERTS, 128), lambda i: (i, 0, 0))],
        out_shape=[jax.ShapeDtypeStruct((nblk, N_EXPERTS, tm), F32), jax.ShapeDtypeStruct((nblk, N_EXPERTS, tm), F32),
                   jax.ShapeDtypeStruct((nblk, N_EXPERTS, 128), F32)],
        compiler_params=pltpu.CompilerParams(dimension_semantics=("parallel",), vmem_limit_bytes=VMEM_LIMIT),
        name="moe_route",
    )(x, router.T, tri)
    counts = cnt[:, :, 0].astype(jnp.int32).reshape(-1)
    rows = nsb * tm
    x_in = pl.BlockSpec((rows, D_MODEL), lambda i, e, f, c: (i, 0), pipeline_mode=pl.Buffered(1))
    meta = pl.BlockSpec((nsb, N_EXPERTS, tm), lambda i, e, f, c: (i, 0, 0))
    vec = pl.BlockSpec((1, D_MODEL), lambda i, e, f, c: (0, 0), pipeline_mode=pl.Buffered(1))
    return pl.pallas_call(
        functools.partial(_moe_kernel, ts=ts, tb=tm, nsb=nsb),
        grid_spec=pltpu.PrefetchScalarGridSpec(
            num_scalar_prefetch=1,
            grid=(nblk // nsb, N_EXPERTS, nf),
            in_specs=[x_in, meta, meta,
                      pl.BlockSpec((1, D_MODEL, tf), lambda i, e, f, c: (e, 0, f)),
                      pl.BlockSpec((1, D_MODEL, tf), lambda i, e, f, c: (e, 0, f)),
                      pl.BlockSpec((1, tf, D_MODEL), lambda i, e, f, c: (e, f, 0)),
                      vec, vec],
            out_specs=pl.BlockSpec((rows, D_MODEL), lambda i, e, f, c: (i, 0), pipeline_mode=pl.Buffered(1)),
            scratch_shapes=[pltpu.VMEM((rows, D_MODEL), BF16),
                            pltpu.VMEM((nsb, ts, D_MODEL), BF16), pltpu.VMEM((nsb, ts, D_MODEL), F32)]),
        out_shape=jax.ShapeDtypeStruct((m, D_MODEL), F32),
        compiler_params=pltpu.CompilerParams(dimension_semantics=("parallel", "arbitrary", "arbitrary"),
                                             vmem_limit_bytes=MOE_VMEM_LIMIT),
        name="moe_ln",
    )(counts, x, gate, rank, wg, wu, wd, ln_g.reshape(1, -1), ln_b.reshape(1, -1))


def _token_mix(h, pos0, rw_prev, wkv0, pool_buf, ret0, kv_bufs, lw, l, w_in_b, w_branch_b, w_out_b, ln_g, ln_b):
    b, t, _ = h.shape
    m = b * t
    tm = ROW_TILE if m % ROW_TILE == 0 else m
    hf = h.reshape(m, D_MODEL)
    p_rw, p_pool, p_ret, p_dil = _project(hf, w_in_b, l, tm)
    p_rw, p_pool = p_rw.reshape(b, t, -1), p_pool.reshape(b, t, -1)
    p_ret, p_dil = p_ret.reshape(b, t, -1), p_dil.reshape(b, t, -1)
    o_a, wkv_new = _rwkv_branch(p_rw, rw_prev, wkv0, lw)
    o_b = _pool_branch(p_pool, pool_buf, pos0, lw)
    pool_new = jnp.concatenate([pool_buf, p_pool], axis=1)[:, -POOL_BUF:]
    o_c, ret_new = _ret_branch(p_ret, ret0, pos0, lw)
    def kv_rows(g, keep):
        lo = (3 * g + 1) * WIDTH
        return p_dil[:, t - keep:, lo:lo + 2 * WIDTH].reshape(b, keep, 2, N_HEADS, HEAD_DIM)

    if kv_bufs is None:
        dil_parts = _dil_prompt(p_dil)
        kv_new = [kv_rows(g, min(win, t)) for g, (win, _) in enumerate(DIL_PATTERNS)]
    else:
        dil_parts = [_dil_step(p_dil, kv_bufs, l).reshape(m, WIDTH)]
        kv_new = [kv_rows(g, t) for g in range(DIL_GROUPS)]
    branches = [x.reshape(m, WIDTH) for x in (o_a, o_b, o_c)] + dil_parts
    x1 = _merge(hf, branches, w_in_b, w_branch_b, w_out_b, l, ln_g, ln_b, tm)
    return x1, (wkv_new, p_rw[:, -1], pool_new, ret_new, kv_new[0], kv_new[1], kv_new[2])


def kernel(x_prompt, x_sample, state_wkv, state_shift, state_pool, state_ret, cache_kv_w128, cache_kv_w512, cache_kv_w2048, w_in, rw_mu, rw_w0, rw_w2, rw_a0, rw_a2, rw_g2, rw_kk, rw_ka, rw_rk, rw_gn_g, rw_gn_b, pool_w, pool_scale, ret_gn_g, ret_gn_b, w_branch, w_out, ln_g, ln_b, ffn_w_gate, ffn_w_up, ffn_w_down, moe_router, moe_w_gate, moe_w_up, moe_w_down):
    hp, hs = x_prompt, x_sample
    bp, tp, _ = hp.shape
    bs, ts, _ = hs.shape
    names = ('rw_mu', 'rw_w0', 'rw_w2', 'rw_a0', 'rw_a2', 'rw_g2', 'rw_kk', 'rw_ka', 'rw_rk', 'rw_gn_g',
             'rw_gn_b', 'pool_w', 'pool_scale', 'ret_gn_g', 'ret_gn_b')
    stacked = (rw_mu, rw_w0, rw_w2, rw_a0, rw_a2, rw_g2, rw_kk, rw_ka, rw_rk, rw_gn_g, rw_gn_b, pool_w,
               pool_scale, ret_gn_g, ret_gn_b)
    new_p = [[] for _ in range(7)]
    new_s = [[] for _ in range(7)]
    zeros = lambda *shape: jnp.zeros(shape, F32)
    w_in_b, wb, wo = w_in.astype(BF16), w_branch.astype(BF16), w_out.astype(BF16)
    caches = [jnp.transpose(c, (0, 1, 3, 4, 5, 2)) for c in (cache_kv_w128, cache_kv_w512, cache_kv_w2048)]
    for l in range(DEPTH):
        lw = {k: v[l] for k, v in zip(names, stacked)}
        xp, st_p = _token_mix(hp, 0, zeros(bp, RW_COLS), zeros(bp, N_HEADS, HEAD_DIM, HEAD_DIM),
                              zeros(bp, POOL_BUF, WIDTH), zeros(bp, N_HEADS, HEAD_DIM, HEAD_DIM), None,
                              lw, l, w_in_b, wb, wo, ln_g[l, 0], ln_b[l, 0])
        xs, st_s = _token_mix(hs, PAST_LEN, state_shift[l], state_wkv[l], state_pool[l], state_ret[l], caches,
                              lw, l, w_in_b, wb, wo, ln_g[l, 0], ln_b[l, 0])
        j = l // 2
        if l % 2 == 0:
            ws = [w[j].astype(BF16) for w in (ffn_w_gate, ffn_w_up, ffn_w_down)]
            xp = _ffn(xp, *ws, ln_g[l, 1], ln_b[l, 1], ROW_TILE)
            xs = _ffn(xs, *ws, ln_g[l, 1], ln_b[l, 1], xs.shape[0])
        else:
            ws = [w[j].astype(BF16) for w in (moe_w_gate, moe_w_up, moe_w_down)]
            xp = _moe(xp, moe_router[j], *ws, ln_g[l, 1], ln_b[l, 1], MOE_BLOCK)
            xs = _moe(xs, moe_router[j], *ws, ln_g[l, 1], ln_b[l, 1], xs.shape[0])
        hp, hs = xp.reshape(bp, tp, D_MODEL), xs.reshape(bs, ts, D_MODEL)
        for i in range(7):
            new_p[i].append(st_p[i])
            new_s[i].append(st_s[i])
    outs_p = [jnp.stack(x) for x in new_p]
    outs_s = [jnp.stack(x) for x in new_s]
    return (hp, hs, *outs_p, *outs_s)
```

```python
import functools
import math

import jax
import jax.numpy as jnp
from jax import lax
from jax.experimental import pallas as pl
from jax.experimental.pallas import tpu as pltpu

F32 = jnp.float32
BF16 = jnp.bfloat16

D_MODEL = 1024
DEPTH = 2
HEAD_DIM = 64
N_HEADS = 4
WIDTH = N_HEADS * HEAD_DIM
RW_COLS = 1024
RW_GN_EPS = 64e-5
POOL_WINDOWS = (2, 4, 8, 16)
POOL_BUF = 15
POOL_HALO = max(POOL_WINDOWS)
RET_CHUNK = 128
ROPE_BASE = 10000.0
DIL_PATTERNS = ((128, 1), (512, 4), (2048, 16))
DIL_GROUPS = 3
DIL_BLOCK = 128
DIL_STEP_BATCH = 2
DIL_SPAN = 2048
DIL_COLS = 3 * DIL_GROUPS * WIDTH
COL_POOL = RW_COLS
COL_RET = COL_POOL + WIDTH
COL_DIL = COL_RET + 4 * WIDTH
COL_GATE = COL_DIL + DIL_COLS
N_EXPERTS = 8
DN_ALPHA = (2 * DEPTH) ** 0.25
LN_EPS = 1e-5
RW_CHUNK = 64
PAST_LEN = 8192
SUBLANES = 8
ROW_TILE = 512
PROJ_COL_TILE = 512
MOE_BLOCK = 1024
MOE_TILE = 288
RW_P = 1
RW_BATCH = 4
RW_CHUNKS_PER_STEP = 2
RW_LANE_GROUPS = 2
NEG_BIG = -1e30

NN = (((1,), (0,)), ((), ()))
NT = (((1,), (1,)), ((), ()))
TN = (((0,), (0,)), ((), ()))

VMEM_LIMIT = 56 * 1024 * 1024
MOE_VMEM_LIMIT = 61 * 1024 * 1024


def _split(x, n):
    if x.dtype == BF16:
        return [x]
    parts, rem = [], x
    for i in range(n):
        p = rem.astype(BF16)
        parts.append(p)
        if i + 1 < n:
            rem = rem - p.astype(F32)
    return parts


def _mm(a, b, dims=NN, pa=1, pb=1):
    a_parts, b_parts = _split(a, pa), _split(b, pb)
    depth = max(len(a_parts), len(b_parts))
    acc = None
    for i, ai in enumerate(a_parts):
        for j, bj in enumerate(b_parts):
            if i + j < depth:
                t = lax.dot_general(ai, bj, dims, preferred_element_type=F32)
                acc = t if acc is None else acc + t
    return acc


def _sigmoid(x):
    return 0.5 * jnp.tanh(0.5 * x) + 0.5


def _silu(x):
    return x * _sigmoid(x)


def _iota(shape, axis):
    return lax.broadcasted_iota(jnp.int32, shape, axis)


def _div(x, d):
    assert d & (d - 1) == 0
    return x >> (d.bit_length() - 1)


def _mod(x, d):
    assert d & (d - 1) == 0
    return x & (d - 1)


def _per_head(head, values):
    out = jnp.full(head.shape, values[-1], F32)
    for h in range(len(values) - 2, -1, -1):
        out = jnp.where(head == h, values[h], out)
    return out


def _head_mask(rows_per_head, n_rows):
    return _div(_iota((n_rows, WIDTH), 0), rows_per_head) == _div(_iota((n_rows, WIDTH), 1), HEAD_DIM)


def _stack_heads(x, mask):
    return jnp.where(mask, jnp.concatenate([x] * N_HEADS, axis=0), 0.0)


def _unstack_heads(x_st, mask, c):
    x_st = jnp.where(mask, x_st, 0.0)
    out = x_st[0:c]
    for h in range(1, N_HEADS):
        out = out + x_st[h * c:(h + 1) * c]
    return out


def _ones_bd():
    return _same_head().astype(BF16)


def _same_head():
    return _div(_iota((WIDTH, WIDTH), 0), HEAD_DIM) == _div(_iota((WIDTH, WIDTH), 1), HEAD_DIM)


def _head_norm(x, ones_bd, g, b, eps):
    mu = _mm(x, ones_bd, pa=2) * (1.0 / HEAD_DIM)
    d = x - mu
    var = _mm(d * d, ones_bd, pa=2) * (1.0 / HEAD_DIM)
    return d * lax.rsqrt(var + eps) * g + b


def _layer_norm(x, g, b):
    mu = jnp.mean(x, axis=-1, keepdims=True)
    d = x - mu
    var = jnp.mean(d * d, axis=-1, keepdims=True)
    return d * lax.rsqrt(var + LN_EPS) * g + b


def _resident(shape):
    nd = len(shape)
    return pl.BlockSpec(shape, lambda *_: (0,) * nd, pipeline_mode=pl.Buffered(1))


def _proj_kernel(x_ref, w_ref, rw_ref, pool_ref, ret_ref, dil_ref):
    xb = x_ref[...].astype(BF16)
    for ref, lo, hi in ((rw_ref, 0, COL_POOL), (pool_ref, COL_POOL, COL_RET),
                        (ret_ref, COL_RET, COL_DIL), (dil_ref, COL_DIL, COL_GATE)):
        for s in range(lo, hi, PROJ_COL_TILE):
            e = min(s + PROJ_COL_TILE, hi)
            ref[:, s - lo:e - lo] = _mm(xb, w_ref[:, s:e])


def _layer_resident(w, l):
    nd = w.ndim - 1
    return pl.BlockSpec((None,) + w.shape[1:], lambda *_: (l,) + (0,) * nd, pipeline_mode=pl.Buffered(1))


def _project(x, w_in_b, l, tm):
    m = x.shape[0]
    widths = (COL_POOL, WIDTH, 4 * WIDTH, DIL_COLS)
    row = lambda w: pl.BlockSpec((tm, w), lambda i: (i, 0))
    return pl.pallas_call(
        _proj_kernel,
        grid=(m // tm,),
        in_specs=[row(D_MODEL), _layer_resident(w_in_b, l)],
        out_specs=[row(w) for w in widths],
        out_shape=[jax.ShapeDtypeStruct((m, w), F32) for w in widths],
        compiler_params=pltpu.CompilerParams(dimension_semantics=("parallel",), vmem_limit_bytes=VMEM_LIMIT),
        name="proj",
    )(x, w_in_b)


BNT = (((2,), (2,)), ((0,), (0,)))
BNN = (((2,), (1,)), ((0,), (0,)))
BTN = (((1,), (1,)), ((0,), (0,)))


def _rwkv_kernel(p_ref, prev_ref, s0_ref, mu_ref, w0_ref, w2_ref, a0_ref, a2_ref, g2_ref, kk_ref, ka_ref,
                 rk_ref, gng_ref, gnb_ref, o_ref, st_ref, s_scr, prev_scr, buf_scr, *, c, t_in, nb, nsub):
    ci = pl.program_id(1)
    rows = nsub * c

    halves = RW_LANE_GROUPS
    hw, hpg = WIDTH // halves, N_HEADS // halves
    n = hpg * c
    same_grp = _div(_iota((hw, hw), 0), HEAD_DIM) == _div(_iota((hw, hw), 1), HEAD_DIM)

    @pl.when(ci == 0)
    def _():
        for g in range(halves):
            part = s0_ref[:, g * hw:(g + 1) * hw, :]
            s_scr[g * nb:(g + 1) * nb] = jnp.where(same_grp, jnp.concatenate([part] * hpg, axis=-1), 0.0)
        for b in range(nb):
            prev_scr[b, 0:1, :] = prev_ref[b]

    first = _iota((rows, RW_COLS), 0) == 0
    ps, shs = [], []
    for b in range(nb):
        if t_in == c:
            pb_ = p_ref[b]
        else:
            buf_scr[b] = jnp.zeros((c, RW_COLS), F32)
            buf_scr[b, 0:t_in, :] = p_ref[b]
            pb_ = buf_scr[b]
        shs.append(jnp.where(first, prev_scr[b, 0:1, :], pltpu.roll(pb_, 1, 0)))
        prev_scr[b, 0:1, :] = pb_[rows - 1:rows, :]
        ps.append(pb_)
    p, shifted = jnp.concatenate(ps, axis=0), jnp.concatenate(shs, axis=0)
    m = nb * rows
    u = p + (shifted - p) * mu_ref[...]
    r, k, v, ul = u[:, 0:WIDTH], u[:, WIDTH:2 * WIDTH], u[:, 2 * WIDTH:3 * WIDTH], u[:, 3 * WIDTH:]

    w_pre = w0_ref[...] + _mm(jnp.tanh(ul), w2_ref[...])
    logw = -math.exp(-0.5) * _sigmoid(w_pre)
    a = _sigmoid(a0_ref[...] + _mm(ul, a2_ref[...]))
    g = _mm(_sigmoid(ul), g2_ref[...])

    ones_bd = _ones_bd()
    kk = k * kk_ref[...]
    kk = kk * lax.rsqrt(jnp.maximum(_mm(kk * kk, ones_bd, pa=2), 1e-24))
    k_mod = k * (1.0 + (a - 1.0) * ka_ref[...])
    a_vec, b_vec = -kk, kk * a
    if t_in < c:
        live = _mod(_iota((m, WIDTH), 0), c) < t_in
        logw = jnp.where(live, logw, 0.0)
        a_vec, b_vec = jnp.where(live, a_vec, 0.0), jnp.where(live, b_vec, 0.0)
        k_mod, v = jnp.where(live, k_mod, 0.0), jnp.where(live, v, 0.0)

    qi, qj = _iota((m, m), 0), _iota((m, m), 1)
    tri = ((_div(qi, c) == _div(qj, c)) & (qi >= qj)).astype(BF16)
    cum2 = _mm(tri, logw, pb=3)
    def chunk_rows(x, s):
        return x if nsub == 1 else x.reshape(nb, nsub, c, x.shape[-1])[:, s].reshape(nb * c, x.shape[-1])

    def seq(x):
        parts = []
        for s in range(nsub):
            x3 = chunk_rows(x, s).reshape(nb, c, WIDTH)
            parts += [x3[:, :, g * hw:(g + 1) * hw] for g in range(halves)]
        return jnp.concatenate(parts, axis=0)

    cum, lw3 = seq(cum2), seq(logw)
    cum_end = cum[:, c - 1:c, :]
    e_neg, e_end = jnp.exp(-cum), jnp.exp(cum_end - cum)
    a_t = seq(a_vec) * jnp.exp(cum - lw3)
    r_t = seq(r) * jnp.exp(cum)
    b_t, k_t = seq(b_vec) * e_neg, seq(k_mod) * e_neg
    b_e, k_e = seq(b_vec) * e_end, seq(k_mod) * e_end
    v3 = seq(v)

    hm = _div(_iota((n, hw), 0), c) == _div(_iota((n, hw), 1), HEAD_DIM)
    stack = lambda x: jnp.where(hm, jnp.concatenate([x] * hpg, axis=1), 0.0)
    a_st, r_st, b_st, v_st = stack(a_t), stack(r_t), stack(b_t), stack(v3)
    ri, rj = _iota((n, n), 0), _iota((n, n), 1)
    same = _div(ri, c) == _div(rj, c)
    strict_bd = same & (_mod(ri, c) > _mod(rj, c))
    incl_bd = same & (_mod(ri, c) >= _mod(rj, c))
    ti, tj = _mod(_iota((n, c), 0), c), _iota((n, c), 1)
    strict_st, incl_st = ti > tj, ti >= tj

    mm = functools.partial(_mm, pa=RW_P, pb=RW_P)
    a_ab = jnp.where(strict_bd, mm(a_st, b_st, BNT), 0.0)
    inv = jnp.where(ri == rj, 1.0, 0.0) + a_ab
    pw = a_ab
    for _ in range(int(math.log2(c)) - 1):
        pw = mm(pw, pw, BNN)
        inv = inv + mm(inv, pw, BNN)
    a_ak = jnp.where(strict_st, mm(a_st, k_t, BNT), 0.0)
    z_st = jnp.where(hm, mm(a_ak, v3, BNN), 0.0)
    wu = mm(inv, jnp.concatenate([a_st, z_st], axis=2), BNN)
    w_st, u0_st = wu[:, :, 0:hw], wu[:, :, hw:]

    rhs = jnp.concatenate([stack(b_e), stack(k_e)], axis=1)
    a_rb = jnp.where(incl_bd, mm(r_st, b_st, BNT), 0.0)
    a_rk = jnp.where(incl_st, mm(r_st, k_t, BNT), 0.0)
    o_in = mm(a_rk, v3, BNN)
    decay_end = jnp.exp(cum_end)

    per = halves * nb
    s0 = s_scr[...]
    for s in range(nsub):
        sl = slice(s * per, (s + 1) * per)
        u_st = mm(w_st[sl], s0, BNT) + u0_st[sl]
        o_st = jnp.where(hm, mm(r_st[sl], s0, BNT) + mm(a_rb[sl], u_st, BNN) + o_in[sl], 0.0)
        s0 = s0 * decay_end[sl] + mm(jnp.concatenate([u_st, v_st[sl]], axis=1), rhs[sl], BTN)
        o3 = o_st[:, 0:c]
        for h in range(1, hpg):
            o3 = o3 + o_st[:, h * c:(h + 1) * c]
        o = jnp.concatenate([o3[g * nb:(g + 1) * nb] for g in range(halves)], axis=2).reshape(nb * c, WIDTH)
        o = _head_norm(o, ones_bd, gng_ref[...], gnb_ref[...], RW_GN_EPS)
        v_s = chunk_rows(v, s)
        o = o + _mm(chunk_rows(r, s) * chunk_rows(k_mod, s) * rk_ref[...], ones_bd, pa=2) * v_s
        o = (o * chunk_rows(g, s)).reshape(nb, c, WIDTH)
        if t_in == c:
            o_ref[:, s * c:(s + 1) * c, :] = o
        else:
            o_ref[...] = o[:, 0:t_in]
    s_scr[...] = s0

    @pl.when(ci == pl.num_programs(1) - 1)
    def _():
        for g in range(halves):
            blk = s_scr[g * nb:(g + 1) * nb]
            part = blk[..., 0:HEAD_DIM]
            for h in range(1, hpg):
                part = part + blk[..., h * HEAD_DIM:(h + 1) * HEAD_DIM]
            st_ref[:, g * hw:(g + 1) * hw, :] = part


def _embed_heads(s):
    return jnp.where(_same_head(), jnp.concatenate([s] * N_HEADS, axis=-1), 0.0)


def _extract_heads(s_bd):
    out = s_bd[..., 0:HEAD_DIM]
    for h in range(1, N_HEADS):
        out = out + s_bd[..., h * HEAD_DIM:(h + 1) * HEAD_DIM]
    return out


def _rwkv_branch(p, p_prev, wkv0, lw):
    b, t, _ = p.shape
    c = RW_CHUNK if t % RW_CHUNK == 0 else RW_CHUNK // 2
    assert t % c == 0 or t <= c
    t_in = c if t % c == 0 else t
    nc = t // c if t % c == 0 else 1
    vec = lambda x: x.reshape(1, -1)
    pad_rows = lambda w, lo: jnp.zeros((WIDTH, WIDTH), F32).at[lo:lo + w.shape[0]].set(w).astype(BF16)
    params = [vec(lw['rw_mu']), vec(lw['rw_w0']), pad_rows(lw['rw_w2'], 0), vec(lw['rw_a0']),
              pad_rows(lw['rw_a2'], 64), pad_rows(lw['rw_g2'], 128), vec(lw['rw_kk']), vec(lw['rw_ka']),
              vec(lw['rw_rk']), vec(lw['rw_gn_g']), vec(lw['rw_gn_b'])]
    nb = 2 * RW_BATCH if b % (2 * RW_BATCH) == 0 else RW_BATCH
    nsub = RW_CHUNKS_PER_STEP if nc % RW_CHUNKS_PER_STEP == 0 else 1
    rows = nsub * t_in
    assert b % nb == 0
    o, st = pl.pallas_call(
        functools.partial(_rwkv_kernel, c=c, t_in=t_in, nb=nb, nsub=nsub),
        grid=(b // nb, nc // nsub),
        in_specs=[pl.BlockSpec((nb, rows, RW_COLS), lambda i, j: (i, j, 0)),
                  pl.BlockSpec((nb, 1, RW_COLS), lambda i, j: (i, 0, 0)),
                  pl.BlockSpec((nb, WIDTH, HEAD_DIM), lambda i, j: (i, 0, 0))]
                 + [_resident(x.shape) for x in params],
        out_specs=[pl.BlockSpec((nb, rows, WIDTH), lambda i, j: (i, j, 0)),
                   pl.BlockSpec((nb, WIDTH, HEAD_DIM), lambda i, j: (i, 0, 0))],
        out_shape=[jax.ShapeDtypeStruct((b, t, WIDTH), F32), jax.ShapeDtypeStruct((b, WIDTH, HEAD_DIM), F32)],
        scratch_shapes=[pltpu.VMEM((RW_LANE_GROUPS * nb, WIDTH // RW_LANE_GROUPS, WIDTH // RW_LANE_GROUPS), F32),
                        pltpu.VMEM((nb, 8, RW_COLS), F32),
                        pltpu.VMEM((nb, c, RW_COLS), F32)],
        compiler_params=pltpu.CompilerParams(dimension_semantics=("parallel", "arbitrary"),
                                             vmem_limit_bytes=VMEM_LIMIT),
        name="rwkv7",
    )(p, p_prev.reshape(b, 1, RW_COLS), wkv0.reshape(b, WIDTH, HEAD_DIM), *params)
    return o, st.reshape(b, N_HEADS, HEAD_DIM, HEAD_DIM)


def _pool_rows(ext_scr, c, pos_first, w, scale):
    x = ext_scr[POOL_HALO:POOL_HALO + c, :]
    sums, acc, off = [], x, 1
    for win in POOL_WINDOWS:
        while off < win:
            acc = acc + ext_scr[POOL_HALO - off:POOL_HALO - off + c, :]
            off += 1
        sums.append(acc)
    pos = pos_first + _iota((c, WIDTH), 0)
    grp = _div(_iota((c, WIDTH), 1), HEAD_DIM)
    mean = jnp.zeros((c, WIDTH), F32)
    for gi, win in enumerate(POOL_WINDOWS):
        cnt = jnp.minimum(win, pos + 1).astype(F32)
        mean = jnp.where(grp == gi, sums[gi] / cnt, mean)
    ext_scr[0:POOL_HALO, :] = ext_scr[c:c + POOL_HALO, :]
    return _mm(mean - x, w) * scale


def _pool_kernel(u_ref, buf_ref, w_ref, scale_ref, o_ref, ext_scr, *, c, t_in, pos0):
    ci = pl.program_id(1)

    @pl.when(ci == 0)
    def _():
        ext_scr[0:POOL_HALO, :] = buf_ref[0]

    if t_in < c:
        ext_scr[POOL_HALO:POOL_HALO + c, :] = jnp.zeros((c, WIDTH), F32)
    ext_scr[POOL_HALO:POOL_HALO + t_in, :] = u_ref[0]
    o_ref[0] = _pool_rows(ext_scr, c, pos0 + ci * c, w_ref[...], scale_ref[...])[0:t_in]


def _pool_weights(lw):
    w_bd = jnp.einsum('gcd,gh->gchd', lw['pool_w'], jnp.eye(N_HEADS, dtype=F32)).reshape(WIDTH, WIDTH)
    return w_bd.astype(BF16), lw['pool_scale'].reshape(1, WIDTH)


def _pool_branch(u, buf, pos0, lw):
    b, t, _ = u.shape
    c = ROW_TILE if t % ROW_TILE == 0 else SUBLANES
    t_in = c if t % c == 0 else t
    nc = t // c if t % c == 0 else 1
    w_bd, scale = _pool_weights(lw)
    buf16 = jnp.pad(buf, ((0, 0), (1, 0), (0, 0)))
    return pl.pallas_call(
        functools.partial(_pool_kernel, c=c, t_in=t_in, pos0=pos0),
        grid=(b, nc),
        in_specs=[pl.BlockSpec((1, t_in, WIDTH), lambda i, j: (i, j, 0)),
                  pl.BlockSpec((1, POOL_HALO, WIDTH), lambda i, j: (i, 0, 0)),
                  _resident((WIDTH, WIDTH)), _resident((1, WIDTH))],
        out_specs=pl.BlockSpec((1, t_in, WIDTH), lambda i, j: (i, j, 0)),
        out_shape=jax.ShapeDtypeStruct((b, t, WIDTH), F32),
        scratch_shapes=[pltpu.VMEM((POOL_HALO + c, WIDTH), F32)],
        compiler_params=pltpu.CompilerParams(dimension_semantics=("parallel", "arbitrary")),
        name="pool",
    )(u, buf16, w_bd, scale)


def _rot_half(x):
    first = _mod(_iota(x.shape, 1), HEAD_DIM) < (HEAD_DIM // 2)
    return jnp.where(first, pltpu.roll(x, WIDTH - HEAD_DIM // 2, 1), pltpu.roll(x, HEAD_DIM // 2, 1))


RET_LOG_DECAY = tuple(math.log(1.0 - 2.0 ** (-5.0 - h)) for h in range(N_HEADS))
ALIBI_SLOPES = tuple(2.0 ** (-8.0 * (i + 1) / (DIL_GROUPS * N_HEADS)) for i in range(DIL_GROUPS * N_HEADS))


def _ret_kernel(p_ref, cos_ref, sin_ref, s0_ref, gng_ref, gnb_ref, o_ref, st_ref, s_scr, buf_scr,
                dmask_scr, qd_scr, kd_scr, *, c, t_in, nb):
    ci = pl.program_id(1)
    n = N_HEADS * c
    lg = _per_head(_div(_iota((1, WIDTH), 1), HEAD_DIM), RET_LOG_DECAY)

    @pl.when(ci == 0)
    def _():
        s_scr[...] = _embed_heads(s0_ref[...])
        idx = _iota((c, WIDTH), 0).astype(F32)
        qd_scr[...] = jnp.exp(lg * (idx + 1.0))
        kd_scr[...] = jnp.exp(lg * (t_in - 1.0 - idx))
        rel = _mod(_iota((n, c), 0), c) - _iota((n, c), 1)
        lg_rows = _per_head(_div(_iota((n, c), 0), c), RET_LOG_DECAY)
        dmask_scr[...] = jnp.where(rel >= 0, jnp.exp(lg_rows * jnp.maximum(rel, 0).astype(F32)), 0.0)

    hm = _head_mask(c, n)
    cos, sin = cos_ref[...], sin_ref[...]
    chunk_decay = jnp.exp(lg * float(t_in))
    same_head, ones_bd = _same_head(), _ones_bd()
    for b in range(nb):
        if t_in == c:
            p = p_ref[b]
        else:
            buf_scr[b] = jnp.zeros((c, 4 * WIDTH), F32)
            buf_scr[b, 0:t_in, :] = p_ref[b]
            p = buf_scr[b]
        q, k, v, g = (p[:, i * WIDTH:(i + 1) * WIDTH] for i in range(4))
        q = q * cos + _rot_half(q) * sin
        k = (k * cos + _rot_half(k) * sin) * HEAD_DIM ** -0.5
        if t_in < c:
            live = _iota((c, WIDTH), 0) < t_in
            k, v = jnp.where(live, k, 0.0), jnp.where(live, v, 0.0)
        inner = _mm(_stack_heads(q, hm), k, NT) * dmask_scr[...]
        s0 = s_scr[b]
        o = _unstack_heads(_mm(inner, v), hm, c) + _mm(q * qd_scr[...], s0)
        s_scr[b] = s0 * chunk_decay + jnp.where(same_head, _mm(k * kd_scr[...], v, TN), 0.0)
        o = _silu(g) * _head_norm(o, ones_bd, gng_ref[...], gnb_ref[...], LN_EPS)
        o_ref[b] = o[0:t_in]

    @pl.when(ci == pl.num_programs(1) - 1)
    def _():
        st_ref[...] = _extract_heads(s_scr[...])


def _rope_tables(pos0, t, rows):
    half = HEAD_DIM // 2
    inv = ROPE_BASE ** (-jnp.arange(half, dtype=F32) / half)
    ang = (pos0 + jnp.arange(t, dtype=jnp.int32)).astype(F32)[:, None] * inv[None, :]
    cos = jnp.tile(jnp.cos(ang), (1, 2 * N_HEADS))
    sin = jnp.tile(jnp.concatenate([-jnp.sin(ang), jnp.sin(ang)], axis=1), (1, N_HEADS))
    pad = ((0, rows - t), (0, 0))
    return jnp.pad(cos, pad), jnp.pad(sin, pad)


def _ret_branch(p, s0, pos0, lw):
    b, t, _ = p.shape
    c = RET_CHUNK
    t_in = c if t % c == 0 else t
    nc = t // c if t % c == 0 else 1
    cos, sin = _rope_tables(pos0, t, nc * c)
    nb = 2 * RW_BATCH if b % (2 * RW_BATCH) == 0 else RW_BATCH
    assert b % nb == 0
    o, st = pl.pallas_call(
        functools.partial(_ret_kernel, c=c, t_in=t_in, nb=nb),
        grid=(b // nb, nc),
        in_specs=[pl.BlockSpec((nb, t_in, 4 * WIDTH), lambda i, j: (i, j, 0)),
                  pl.BlockSpec((c, WIDTH), lambda i, j: (j, 0)),
                  pl.BlockSpec((c, WIDTH), lambda i, j: (j, 0)),
                  pl.BlockSpec((nb, WIDTH, HEAD_DIM), lambda i, j: (i, 0, 0)),
                  _resident((1, WIDTH)), _resident((1, WIDTH))],
        out_specs=[pl.BlockSpec((nb, t_in, WIDTH), lambda i, j: (i, j, 0)),
                   pl.BlockSpec((nb, WIDTH, HEAD_DIM), lambda i, j: (i, 0, 0))],
        out_shape=[jax.ShapeDtypeStruct((b, t, WIDTH), F32), jax.ShapeDtypeStruct((b, WIDTH, HEAD_DIM), F32)],
        scratch_shapes=[pltpu.VMEM((nb, WIDTH, WIDTH), F32), pltpu.VMEM((nb, c, 4 * WIDTH), F32),
                        pltpu.VMEM((N_HEADS * c, c), F32), pltpu.VMEM((c, WIDTH), F32), pltpu.VMEM((c, WIDTH), F32)],
        compiler_params=pltpu.CompilerParams(dimension_semantics=("parallel", "arbitrary")),
        name="retention",
    )(p, cos, sin, s0.reshape(b, WIDTH, HEAD_DIM), lw['ret_gn_g'].reshape(1, WIDTH), lw['ret_gn_b'].reshape(1, WIDTH))
    return o, st.reshape(b, N_HEADS, HEAD_DIM, HEAD_DIM)


def _dil_prompt_kernel(*refs, group, dil, span):
    q_refs, kc_refs, kp_refs, vc_refs, vp_refs = (refs[2 * i:2 * i + 2] for i in range(5))
    o_ref, lse_ref, k_scr, v_scr, o_scr, lse_scr = refs[10:]
    blk = DIL_BLOCK
    tail = blk * dil
    si = pl.program_id(1)
    for half in range(2):
        k_scr[half, 0:tail, :] = kp_refs[half][0]
        k_scr[half, tail:tail + span, :] = kc_refs[half][0]
        v_scr[half, 0:tail, :] = vp_refs[half][0]
        v_scr[half, tail:tail + span, :] = vc_refs[half][0]
    ki = _iota((blk, 2 * blk), 1)
    steps = blk + _iota((blk, 2 * blk), 0) - ki
    band = (steps >= 0) & (steps <= blk)
    masks = [jnp.where(band, (-ALIBI_SLOPES[group * N_HEADS + h] * dil) * steps.astype(F32), NEG_BIG)
             for h in range(N_HEADS)]
    sub_head = _div(_iota((blk, 128), 1), HEAD_DIM)
    for cc in range(span // tail):
        has_prev = (si > 0) | (ki >= blk)
        cc_masks = [jnp.where(has_prev, x, NEG_BIG) for x in masks] if cc == 0 else masks
        for r in range(dil):
            rows_q = pl.ds(cc * tail + r, blk, stride=dil) if dil > 1 else pl.ds(cc * tail, blk)
            rows_kv = pl.ds(cc * tail + r, 2 * blk, stride=dil) if dil > 1 else pl.ds(cc * tail, 2 * blk)
            for half in range(2):
                q2 = q_refs[half][0, rows_q, :] * HEAD_DIM ** -0.5
                k2 = k_scr[half, rows_kv, :].astype(BF16)
                v2 = v_scr[half, rows_kv, :].astype(BF16)
                o2 = lse2 = None
                for sub in range(2):
                    mine = sub_head == sub
                    s = _mm(jnp.where(mine, q2, 0.0), k2, NT) + cc_masks[2 * half + sub]
                    m = jnp.max(s, axis=1, keepdims=True)
                    e = jnp.exp(s - m)
                    l = jnp.sum(e, axis=1, keepdims=True)
                    o_h = _mm(e, v2) / l
                    lse_h = jnp.broadcast_to(m + jnp.log(l), (blk, 128))
                    o2 = o_h if o2 is None else jnp.where(mine, o_h, o2)
                    lse2 = lse_h if lse2 is None else jnp.where(mine, lse_h, lse2)
                o_scr[half, rows_q, :] = o2
                lse_scr[half, rows_q, :] = lse2
    o_ref[0] = jnp.concatenate([o_scr[0], o_scr[1]], axis=1)
    lse_ref[0] = jnp.concatenate([lse_scr[0], lse_scr[1]], axis=1)


def _dil_prompt_group(pd, group):
    b, s, _ = pd.shape
    win, dil = DIL_PATTERNS[group]
    span = DIL_SPAN
    tail = DIL_BLOCK * dil
    assert win // dil == DIL_BLOCK and span % tail == 0 and s % span == 0
    base = group * 3

    def cur(col):
        return [pl.BlockSpec((1, span, 128), lambda i, j, h=h: (i, j, 2 * (base + col) + h)) for h in range(2)]

    def prev(col):
        return [pl.BlockSpec((1, tail, 128),
                             lambda i, j, h=h: (i, jnp.maximum(j * (span // tail) - 1, 0), 2 * (base + col) + h))
                for h in range(2)]

    out_spec = pl.BlockSpec((1, span, WIDTH), lambda i, j: (i, j, 0))
    return pl.pallas_call(
        functools.partial(_dil_prompt_kernel, group=group, dil=dil, span=span),
        grid=(b, s // span),
        in_specs=cur(0) + cur(1) + prev(1) + cur(2) + prev(2),
        out_specs=[out_spec, out_spec],
        out_shape=[jax.ShapeDtypeStruct((b, s, WIDTH), F32)] * 2,
        scratch_shapes=[pltpu.VMEM((2, tail + span, 128), F32)] * 2 + [pltpu.VMEM((2, span, 128), F32)] * 2,
        compiler_params=pltpu.CompilerParams(dimension_semantics=("parallel", "arbitrary"),
                                             vmem_limit_bytes=VMEM_LIMIT),
        name=f"dil_prompt_g{group}",
    )(*([pd] * 10))


def _dil_combine(os_, ls):
    m = jnp.maximum(jnp.maximum(ls[0], ls[1]), ls[2])
    es = [jnp.exp(x - m) for x in ls]
    return (es[0] * os_[0] + es[1] * os_[1] + es[2] * os_[2]) / (es[0] + es[1] + es[2])


def _dil_prompt(pd):
    b, s, _ = pd.shape
    outs = [_dil_prompt_group(pd, g) for g in range(DIL_GROUPS)]
    return [x[0].reshape(b * s, WIDTH) for x in outs] + [x[1].reshape(b * s, WIDTH) for x in outs]


def _dil_step_kernel(pd_ref, c0_ref, c1_ref, c2_ref, o_ref, buf_scr, *, t, tp, nb):
    for b in range(nb):
        _dil_step_one(b, pd_ref, (c0_ref, c1_ref, c2_ref), o_ref, buf_scr, t, tp)


def _dil_step_one(b, pd_ref, cache_refs, o_ref, buf_scr, t, tp):
    buf_scr[b] = jnp.zeros((tp, DIL_COLS), F32)
    buf_scr[b, 0:t, :] = pd_ref[b]
    pd = buf_scr[b]
    qt = _iota((tp, 1), 0)
    d_new = qt - _iota((tp, tp), 1)
    outs = [[None] * DIL_GROUPS for _ in range(N_HEADS)]
    lses = [[None] * DIL_GROUPS for _ in range(N_HEADS)]
    for g, cache_ref in enumerate(cache_refs):
        win, dil = DIL_PATTERNS[g]
        length = cache_ref.shape[-1]
        d_old = length + qt - _iota((tp, length), 1)
        ok_old = (_mod(d_old, dil) == 0) & (d_old <= win)
        ok_new = (d_new >= 0) & (_mod(d_new, dil) == 0)
        for h in range(N_HEADS):
            slope = ALIBI_SLOPES[g * N_HEADS + h]
            lo = g * 3 * WIDTH + h * HEAD_DIM
            q = pd[:, lo:lo + HEAD_DIM] * HEAD_DIM ** -0.5
            k_new = pd[:, lo + WIDTH:lo + WIDTH + HEAD_DIM]
            v_new = pd[:, lo + 2 * WIDTH:lo + 2 * WIDTH + HEAD_DIM]
            k_t, v_t = cache_ref[b, 0, h], cache_ref[b, 1, h]
            s_old = jnp.where(ok_old, _mm(q, k_t) - slope * d_old.astype(F32), NEG_BIG)
            s_new = jnp.where(ok_new, _mm(q, k_new, NT) - slope * d_new.astype(F32), NEG_BIG)
            m = jnp.maximum(jnp.max(s_old, axis=1, keepdims=True), jnp.max(s_new, axis=1, keepdims=True))
            e_old, e_new = jnp.exp(s_old - m), jnp.exp(s_new - m)
            l = jnp.sum(e_old, axis=1, keepdims=True) + jnp.sum(e_new, axis=1, keepdims=True)
            outs[h][g] = (_mm(e_old, v_t, NT) + _mm(e_new, v_new)) / l
            lses[h][g] = m + jnp.log(l)
    o = jnp.concatenate([_dil_combine(outs[h], lses[h]) for h in range(N_HEADS)], axis=1)
    o_ref[b] = o[0:t]


def _dil_step(pd, caches, l):
    b, t, _ = pd.shape
    tp = SUBLANES
    nb = DIL_STEP_BATCH if b % DIL_STEP_BATCH == 0 else 1
    flat = caches
    return pl.pallas_call(
        functools.partial(_dil_step_kernel, t=t, tp=tp, nb=nb),
        grid=(b // nb,),
        in_specs=[pl.BlockSpec((nb, t, DIL_COLS), lambda i: (i, 0, 0))]
                 + [pl.BlockSpec((None, nb) + c.shape[2:], lambda i: (l, i, 0, 0, 0, 0)) for c in flat],
        out_specs=pl.BlockSpec((nb, t, WIDTH), lambda i: (i, 0, 0)),
        out_shape=jax.ShapeDtypeStruct((b, t, WIDTH), F32),
        scratch_shapes=[pltpu.VMEM((nb, tp, DIL_COLS), F32)],
        compiler_params=pltpu.CompilerParams(dimension_semantics=("parallel",), vmem_limit_bytes=VMEM_LIMIT),
        name="dil_step",
    )(pd, *flat)


def _merge_kernel(h_ref, *refs):
    wg_ref, wb_ref, wo_ref, g_ref, b_ref, out_ref = refs[-6:]
    branch_refs = refs[:-6]
    branches = [r[...] for r in branch_refs[:3]]
    if len(branch_refs) == 4:
        branches.append(branch_refs[3][...])
    else:
        branches.append(_dil_combine([r[...] for r in branch_refs[3:6]], [r[...] for r in branch_refs[6:9]]))
    h = h_ref[...]
    hb = h.astype(BF16)
    z = None
    for n, o in enumerate(branches):
        gate = _sigmoid(_mm(hb, wg_ref[:, COL_GATE + n * D_MODEL:COL_GATE + (n + 1) * D_MODEL]))
        term = gate * _mm(o, wb_ref[n])
        z = term if z is None else z + term
    y = _mm(z, wo_ref[...])
    out_ref[...] = _layer_norm(DN_ALPHA * h + y, g_ref[...], b_ref[...])


def _merge(h, branches, w_in_b, w_branch_b, w_out_b, l, ln_g, ln_b, tm):
    m = h.shape[0]
    row = lambda w: pl.BlockSpec((tm, w), lambda i: (i, 0))
    return pl.pallas_call(
        _merge_kernel,
        grid=(m // tm,),
        in_specs=[row(D_MODEL)] + [row(WIDTH)] * len(branches)
                 + [_layer_resident(w_in_b, l), _layer_resident(w_branch_b, l), _layer_resident(w_out_b, l),
                    _resident((1, D_MODEL)), _resident((1, D_MODEL))],
        out_specs=row(D_MODEL),
        out_shape=jax.ShapeDtypeStruct((m, D_MODEL), F32),
        compiler_params=pltpu.CompilerParams(dimension_semantics=("parallel",), vmem_limit_bytes=VMEM_LIMIT),
        name="merge_ln",
    )(h, *branches, w_in_b, w_branch_b, w_out_b, ln_g.reshape(1, -1), ln_b.reshape(1, -1))


def _ffn_kernel(x_ref, wg_ref, wu_ref, wd_ref, g_ref, b_ref, out_ref):
    x = x_ref[...]
    xb = x.astype(BF16)
    act = _silu(_mm(xb, wg_ref[...])) * _mm(xb, wu_ref[...])
    out_ref[...] = _layer_norm(DN_ALPHA * x + _mm(act, wd_ref[...]), g_ref[...], b_ref[...])


def _ffn(x, wg, wu, wd, ln_g, ln_b, tm):
    m = x.shape[0]
    row = pl.BlockSpec((tm, D_MODEL), lambda i: (i, 0))
    return pl.pallas_call(
        _ffn_kernel,
        grid=(m // tm,),
        in_specs=[row, _resident(wg.shape), _resident(wu.shape), _resident(wd.shape),
                  _resident((1, D_MODEL)), _resident((1, D_MODEL))],
        out_specs=row,
        out_shape=jax.ShapeDtypeStruct((m, D_MODEL), F32),
        compiler_params=pltpu.CompilerParams(dimension_semantics=("parallel",), vmem_limit_bytes=VMEM_LIMIT),
        name="ffn_ln",
    )(x, wg, wu, wd, ln_g.reshape(1, -1), ln_b.reshape(1, -1))


def _route_kernel(x_ref, rt_ref, tri_ref, gate_ref, rank_ref, cnt_ref):
    logits = _mm(rt_ref[...], x_ref[...], NT, 2, 2)
    sub = _iota(logits.shape, 0).astype(F32)
    m1 = jnp.max(logits, axis=0, keepdims=True)
    i1 = jnp.min(jnp.where(logits == m1, sub, float(N_EXPERTS)), axis=0, keepdims=True)
    rest = jnp.where(sub == i1, NEG_BIG, logits)
    m2 = jnp.max(rest, axis=0, keepdims=True)
    i2 = jnp.min(jnp.where(rest == m2, sub, float(N_EXPERTS)), axis=0, keepdims=True)
    e2 = jnp.exp(m2 - m1)
    gate_ref[0] = jnp.where(sub == i1, 1.0 / (1.0 + e2), 0.0) + jnp.where(sub == i2, e2 / (1.0 + e2), 0.0)
    chosen = (sub == i1) | (sub == i2)
    sel = jnp.where(chosen, 1.0, 0.0)
    rank_ref[0] = jnp.where(chosen, _mm(sel, tri_ref[...]), -1.0)
    cnt_ref[0] = jnp.broadcast_to(jnp.sum(sel, axis=1, keepdims=True), cnt_ref.shape[1:])


def _moe_kernel(cnt_ref, x_ref, gate_ref, rank_ref, wg_ref, wu_ref, wd_ref, g_ref, b_ref, out_ref,
                xb_scr, xg_scr, yg_scr, *, ts, tb, nsb):
    i, e, f = pl.program_id(0), pl.program_id(1), pl.program_id(2)
    last_f = f == pl.num_programs(2) - 1

    @pl.when((e == 0) & (f == 0))
    def _():
        xb_scr[...] = x_ref[...].astype(BF16)
        out_ref[...] = jnp.zeros_like(out_ref)

    slot = _iota((ts, tb), 0)

    def expert(xg):
        return _mm(_silu(_mm(xg, wg_ref[0])) * _mm(xg, wu_ref[0]), wd_ref[0])

    def pick_of(sb, j):
        rank_row = rank_ref[sb, pl.ds(e, 1), :]
        return jnp.where(rank_row == (slot + j * ts).astype(F32), 1.0, 0.0)

    def scatter(sb, pick, y):
        w_slot = jnp.sum(pick * gate_ref[sb, pl.ds(e, 1), :], axis=1, keepdims=True)
        out_ref[sb * tb:(sb + 1) * tb, :] += _mm(pick, y * w_slot, TN)

    @pl.when(f == 0)
    def _():
        for sb in range(nsb):
            xg_scr[sb] = _mm(pick_of(sb, 0), xb_scr[sb * tb:(sb + 1) * tb, :]).astype(BF16)
        yg_scr[...] = jnp.zeros_like(yg_scr)

    yg_scr[...] += expert(xg_scr[...].reshape(nsb * ts, D_MODEL)).reshape(nsb, ts, D_MODEL)

    @pl.when(last_f)
    def _():
        for sb in range(nsb):
            scatter(sb, pick_of(sb, 0), yg_scr[sb])

    for sb in range(nsb):
        n_tiles = (cnt_ref[(i * nsb + sb) * N_EXPERTS + e] + (ts - 1)) // ts

        def overflow(j, carry, sb=sb):
            pick = pick_of(sb, j)
            scatter(sb, pick, expert(_mm(pick, xb_scr[sb * tb:(sb + 1) * tb, :]).astype(BF16)))
            return carry

        lax.fori_loop(1, n_tiles, overflow, 0)

    @pl.when((e == pl.num_programs(1) - 1) & last_f)
    def _():
        out_ref[...] = _layer_norm(DN_ALPHA * x_ref[...] + out_ref[...], g_ref[...], b_ref[...])


def _moe(x, router, wg, wu, wd, ln_g, ln_b, tm):
    m = x.shape[0]
    nblk = m // tm
    dff = wg.shape[2]
    nf = 2
    tf = dff // nf
    ts = min(MOE_TILE, tm)
    nsb = 2 if nblk % 2 == 0 else 1
    assert m % tm == 0 and tf % 128 == 0
    tri = (jnp.arange(tm)[:, None] < jnp.arange(tm)[None, :]).astype(BF16)
    gate, rank, cnt = pl.pallas_call(
        _route_kernel,
        grid=(nblk,),
        in_specs=[pl.BlockSpec((tm, D_MODEL), lambda i: (i, 0)), _resident((N_EXPERTS, D_MODEL)),
                  _resident((tm, tm))],
        out_specs=[pl.BlockSpec((1, N_EXPERTS, tm), lambda i: (i, 0, 0)),
                   pl.BlockSpec((1, N_EXPERTS, tm), lambda i: (i, 0, 0)),
                   pl.BlockSpec((1, N_EXPERTS, 128), lambda i: (i, 0, 0))],
        out_shape=[jax.ShapeDtypeStruct((nblk, N_EXPERTS, tm), F32), jax.ShapeDtypeStruct((nblk, N_EXPERTS, tm), F32),
                   jax.ShapeDtypeStruct((nblk, N_EXPERTS, 128), F32)],
        compiler_params=pltpu.CompilerParams(dimension_semantics=("parallel",), vmem_limit_bytes=VMEM_LIMIT),
        name="moe_route",
    )(x, router.T, tri)
    counts = cnt[:, :, 0].astype(jnp.int32).reshape(-1)
    rows = nsb * tm
    x_in = pl.BlockSpec((rows, D_MODEL), lambda i, e, f, c: (i, 0), pipeline_mode=pl.Buffered(1))
    meta = pl.BlockSpec((nsb, N_EXPERTS, tm), lambda i, e, f, c: (i, 0, 0))
    vec = pl.BlockSpec((1, D_MODEL), lambda i, e, f, c: (0, 0), pipeline_mode=pl.Buffered(1))
    return pl.pallas_call(
        functools.partial(_moe_kernel, ts=ts, tb=tm, nsb=nsb),
        grid_spec=pltpu.PrefetchScalarGridSpec(
            num_scalar_prefetch=1,
            grid=(nblk // nsb, N_EXPERTS, nf),
            in_specs=[x_in, meta, meta,
                      pl.BlockSpec((1, D_MODEL, tf), lambda i, e, f, c: (e, 0, f)),
                      pl.BlockSpec((1, D_MODEL, tf), lambda i, e, f, c: (e, 0, f)),
                      pl.BlockSpec((1, tf, D_MODEL), lambda i, e, f, c: (e, f, 0)),
                      vec, vec],
            out_specs=pl.BlockSpec((rows, D_MODEL), lambda i, e, f, c: (i, 0), pipeline_mode=pl.Buffered(1)),
            scratch_shapes=[pltpu.VMEM((rows, D_MODEL), BF16),
                            pltpu.VMEM((nsb, ts, D_MODEL), BF16), pltpu.VMEM((nsb, ts, D_MODEL), F32)]),
        out_shape=jax.ShapeDtypeStruct((m, D_MODEL), F32),
        compiler_params=pltpu.CompilerParams(dimension_semantics=("parallel", "arbitrary", "arbitrary"),
                                             vmem_limit_bytes=MOE_VMEM_LIMIT),
        name="moe_ln",
    )(counts, x, gate, rank, wg, wu, wd, ln_g.reshape(1, -1), ln_b.reshape(1, -1))


def _token_mix(h, pos0, rw_prev, wkv0, pool_buf, ret0, kv_bufs, lw, l, w_in_b, w_branch_b, w_out_b, ln_g, ln_b):
    b, t, _ = h.shape
    m = b * t
    tm = ROW_TILE if m % ROW_TILE == 0 else m
    hf = h.reshape(m, D_MODEL)
    p_rw, p_pool, p_ret, p_dil = _project(hf, w_in_b, l, tm)
    p_rw, p_pool = p_rw.reshape(b, t, -1), p_pool.reshape(b, t, -1)
    p_ret, p_dil = p_ret.reshape(b, t, -1), p_dil.reshape(b, t, -1)
    o_a, wkv_new = _rwkv_branch(p_rw, rw_prev, wkv0, lw)
    o_b = _pool_branch(p_pool, pool_buf, pos0, lw)
    pool_new = jnp.concatenate([pool_buf, p_pool], axis=1)[:, -POOL_BUF:]
    o_c, ret_new = _ret_branch(p_ret, ret0, pos0, lw)
    def kv_rows(g, keep):
        lo = (3 * g + 1) * WIDTH
        return p_dil[:, t - keep:, lo:lo + 2 * WIDTH].reshape(b, keep, 2, N_HEADS, HEAD_DIM)

    if kv_bufs is None:
        dil_parts = _dil_prompt(p_dil)
        kv_new = [kv_rows(g, min(win, t)) for g, (win, _) in enumerate(DIL_PATTERNS)]
    else:
        dil_parts = [_dil_step(p_dil, kv_bufs, l).reshape(m, WIDTH)]
        kv_new = [kv_rows(g, t) for g in range(DIL_GROUPS)]
    branches = [x.reshape(m, WIDTH) for x in (o_a, o_b, o_c)] + dil_parts
    x1 = _merge(hf, branches, w_in_b, w_branch_b, w_out_b, l, ln_g, ln_b, tm)
    return x1, (wkv_new, p_rw[:, -1], pool_new, ret_new, kv_new[0], kv_new[1], kv_new[2])


def kernel(x_prompt, x_sample, state_wkv, state_shift, state_pool, state_ret, cache_kv_w128, cache_kv_w512, cache_kv_w2048, w_in, rw_mu, rw_w0, rw_w2, rw_a0, rw_a2, rw_g2, rw_kk, rw_ka, rw_rk, rw_gn_g, rw_gn_b, pool_w, pool_scale, ret_gn_g, ret_gn_b, w_branch, w_out, ln_g, ln_b, ffn_w_gate, ffn_w_up, ffn_w_down, moe_router, moe_w_gate, moe_w_up, moe_w_down):
    hp, hs = x_prompt, x_sample
    bp, tp, _ = hp.shape
    bs, ts, _ = hs.shape
    names = ('rw_mu', 'rw_w0', 'rw_w2', 'rw_a0', 'rw_a2', 'rw_g2', 'rw_kk', 'rw_ka', 'rw_rk', 'rw_gn_g',
             'rw_gn_b', 'pool_w', 'pool_scale', 'ret_gn_g', 'ret_gn_b')
    stacked = (rw_mu, rw_w0, rw_w2, rw_a0, rw_a2, rw_g2, rw_kk, rw_ka, rw_rk, rw_gn_g, rw_gn_b, pool_w,
               pool_scale, ret_gn_g, ret_gn_b)
    new_p = [[] for _ in range(7)]
    new_s = [[] for _ in range(7)]
    zeros = lambda *shape: jnp.zeros(shape, F32)
    w_in_b, wb, wo = w_in.astype(BF16), w_branch.astype(BF16), w_out.astype(BF16)
    caches = [jnp.transpose(c, (0, 1, 3, 4, 5, 2)) for c in (cache_kv_w128, cache_kv_w512, cache_kv_w2048)]
    for l in range(DEPTH):
        lw = {k: v[l] for k, v in zip(names, stacked)}
        xp, st_p = _token_mix(hp, 0, zeros(bp, RW_COLS), zeros(bp, N_HEADS, HEAD_DIM, HEAD_DIM),
                              zeros(bp, POOL_BUF, WIDTH), zeros(bp, N_HEADS, HEAD_DIM, HEAD_DIM), None,
                              lw, l, w_in_b, wb, wo, ln_g[l, 0], ln_b[l, 0])
        xs, st_s = _token_mix(hs, PAST_LEN, state_shift[l], state_wkv[l], state_pool[l], state_ret[l], caches,
                              lw, l, w_in_b, wb, wo, ln_g[l, 0], ln_b[l, 0])
        j = l // 2
        if l % 2 == 0:
            ws = [w[j].astype(BF16) for w in (ffn_w_gate, ffn_w_up, ffn_w_down)]
            xp = _ffn(xp, *ws, ln_g[l, 1], ln_b[l, 1], ROW_TILE)
            xs = _ffn(xs, *ws, ln_g[l, 1], ln_b[l, 1], xs.shape[0])
        else:
            ws = [w[j].astype(BF16) for w in (moe_w_gate, moe_w_up, moe_w_down)]
            xp = _moe(xp, moe_router[j], *ws, ln_g[l, 1], ln_b[l, 1], MOE_BLOCK)
            xs = _moe(xs, moe_router[j], *ws, ln_g[l, 1], ln_b[l, 1], xs.shape[0])
        hp, hs = xp.reshape(bp, tp, D_MODEL), xs.reshape(bs, ts, D_MODEL)
        for i in range(7):
            new_p[i].append(st_p[i])
            new_s[i].append(st_s[i])
    outs_p = [jnp.stack(x) for x in new_p]
    outs_s = [jnp.stack(x) for x in new_s]
    return (hp, hs, *outs_p, *outs_s)
```
